```python
import math
import jax
import jax.numpy as jnp
from jax import lax
import numpy as np

D_MODEL = 1024
BATCH = 8
SEQ = 4096
DEPTH = 2

HEAD_DIM = 64
N_EVEN = (DEPTH + 1) // 2
N_ODD = DEPTH // 2
NUM_BUCKETS = 32
MAX_DISTANCE = 128
N_BIAS_HEADS = 16
ALPHA = (2 * DEPTH) ** 0.25
BETA = (8 * DEPTH) ** -0.25
A_HEADS = 8
A_Q_RANK = 256
A_KV_RANK = 128
IDX_HEADS = 16
IDX_DIM = 64
DSA_TOPK = 256
B_HEADS = 8
B_GROUPS = 2
B_HPG = B_HEADS // B_GROUPS
CMP_LEN = 32
CMP_STRIDE = 16
SLC_BLOCK = 64
SLC_TOPN = 16
WINDOW = 512
C_HEADS = 16
MOBA_BLOCK = 256
MOBA_TOPK = 3
D_FF = 2816
N_EXPERTS = 8
TOP_K = 2
D_FF_EXPERT = 3584
EXPERT_ROWS = 256
Q_BLOCK = 64
MOBA_Q_BLOCK = 16
EVEN_SPLITS = (A_Q_RANK, A_KV_RANK, IDX_DIM, IDX_HEADS, B_HEADS * HEAD_DIM) + (B_GROUPS * HEAD_DIM,) * 6 + (B_HEADS * 3,)
EVEN_IN = sum(EVEN_SPLITS)
ODD_IN = 3 * C_HEADS * HEAD_DIM
MIX_WIDTH_EVEN = (A_HEADS + B_HEADS) * HEAD_DIM
MIX_WIDTH_ODD = C_HEADS * HEAD_DIM

kernel_name = 'hybrid_dsa_nsa_moba_moe_deepnorm'


def split_cols(y, sizes):
    cuts = [int(c) for c in np.cumsum(sizes)[:-1]]
    return jnp.split(y, cuts, axis=-1)


def layer_norm(x, g, b, eps=1e-5):
    xf = x.astype(jnp.float32)
    mu = jnp.mean(xf, axis=-1, keepdims=True)
    var = jnp.mean(jnp.square(xf - mu), axis=-1, keepdims=True)
    return ((xf - mu) * lax.rsqrt(var + eps) * g + b).astype(x.dtype)


def rms_norm(x, g, eps=1e-6):
    xf = x.astype(jnp.float32)
    return (xf * lax.rsqrt(jnp.mean(jnp.square(xf), axis=-1, keepdims=True) + eps) * g).astype(x.dtype)


def swiglu(h, w1, w3, w2):
    return (jax.nn.silu(h @ w1) * (h @ w3)) @ w2


def masked_softmax(logits, mask):
    z = jnp.where(mask, logits.astype(jnp.float32), -jnp.inf)
    m = jnp.max(z, axis=-1, keepdims=True)
    m = jnp.where(jnp.isfinite(m), m, 0.0)
    e = jnp.exp(z - m)
    s = jnp.sum(e, axis=-1, keepdims=True)
    return e / jnp.where(s > 0, s, 1.0)


def t5_bucket(dist):
    n = jnp.maximum(dist, 0)
    max_exact = NUM_BUCKETS // 2
    large = max_exact + (jnp.log(jnp.maximum(n, 1).astype(jnp.float32) / max_exact)
                         / math.log(MAX_DISTANCE / max_exact) * (NUM_BUCKETS - max_exact)).astype(jnp.int32)
    large = jnp.minimum(large, NUM_BUCKETS - 1)
    return jnp.where(n < max_exact, n, large)


def dsa_attention(c_q, c_kv, k_idx, w_idx, q_norm, kv_norm, w_uq, w_uk, w_uv, w_qidx, bias_table):
    bsz, seq = c_q.shape[:2]
    scale = HEAD_DIM ** -0.5
    c_q = rms_norm(c_q, q_norm)
    c_kv = rms_norm(c_kv, kv_norm)
    q = jnp.einsum('bsr,rhd->bshd', c_q, w_uq)
    q_lat = jnp.einsum('bshd,chd->bshc', q, w_uk)
    q_idx = jnp.einsum('bsr,rhd->bshd', c_q, w_qidx)
    w_idx = w_idx * IDX_HEADS ** -0.5
    n_keep = min(DSA_TOPK, seq // 4)
    b_idx = jnp.arange(bsz)[:, None, None]
    s_pos = jnp.arange(seq)

    def chunk(c):
        start = c * Q_BLOCK
        t = start + jnp.arange(Q_BLOCK)
        ql = lax.dynamic_slice_in_dim(q_lat, start, Q_BLOCK, axis=1)
        qi = lax.dynamic_slice_in_dim(q_idx, start, Q_BLOCK, axis=1)
        wi = lax.dynamic_slice_in_dim(w_idx, start, Q_BLOCK, axis=1)
        dots = jax.nn.relu(jnp.einsum('bqhd,bsd->bqhs', qi, k_idx))
        score = jnp.einsum('bqhs,bqh->bqs', dots, wi).astype(jnp.float32)
        score = jnp.where(s_pos[None, None, :] <= t[None, :, None], score, -jnp.inf)
        _, idx = lax.top_k(score, n_keep)
        c_sel = c_kv[b_idx, idx]
        dist = t[None, :, None] - idx
        logits = jnp.einsum('bqhr,bqkr->bhqk', ql, c_sel).astype(jnp.float32) * scale
        logits = logits + jnp.moveaxis(bias_table[t5_bucket(dist)], -1, 1)
        p = masked_softmax(logits, (dist >= 0)[:, None])
        o_lat = jnp.einsum('bhqk,bqkr->bqhr', p.astype(c_sel.dtype), c_sel)
        o = jnp.einsum('bqhr,rhd->bqhd', o_lat, w_uv)
        return o.astype(c_q.dtype).reshape(bsz, Q_BLOCK, A_HEADS * HEAD_DIM)

    out = lax.map(chunk, jnp.arange(seq // Q_BLOCK))
    return out.transpose(1, 0, 2, 3).reshape(bsz, seq, A_HEADS * HEAD_DIM)


def nsa_attention(q, k_cmp, v_cmp, k_slc, v_slc, k_win, v_win, gate_logits,
                  pos_k, pos_v, ck1, ck2, cv1, cv2, bias_table):
    bsz, seq = q.shape[:2]
    scale = HEAD_DIM ** -0.5
    q = q.reshape(bsz, seq, B_GROUPS, B_HPG, HEAD_DIM)
    kv_shape = (bsz, seq, B_GROUPS, HEAD_DIM)
    k_cmp, v_cmp, k_slc, v_slc, k_win, v_win = [a.reshape(kv_shape) for a in (k_cmp, v_cmp, k_slc, v_slc, k_win, v_win)]
    n_cmp = (seq - CMP_LEN) // CMP_STRIDE + 1
    cmp_start = np.arange(n_cmp) * CMP_STRIDE
    cmp_tok = cmp_start[:, None] + np.arange(CMP_LEN)[None, :]

    def compress(a, pos, w1, w2):
        blocks = a[:, cmp_tok] + pos[None, None, :, None, :]
        hid = jax.nn.gelu(jnp.einsum('bnlgd,lde->bnge', blocks, w1))
        return jnp.einsum('bnge,ef->bngf', hid, w2)

    kc = compress(k_cmp, pos_k, ck1, ck2)
    vc = compress(v_cmp, pos_v, cv1, cv2)
    cmp_end = jnp.asarray(cmp_start + CMP_LEN - 1, jnp.int32)
    n_slc = seq // SLC_BLOCK
    n_sel = min(SLC_TOPN, n_slc)
    slc_start = np.arange(n_slc) * SLC_BLOCK
    overlap = jnp.asarray(((cmp_start[:, None] + CMP_LEN - 1 >= slc_start[None, :])
                           & (cmp_start[:, None] <= slc_start[None, :] + SLC_BLOCK - 1)).astype(np.float32))
    ks_blocks = k_slc.reshape(bsz, n_slc, SLC_BLOCK, B_GROUPS, HEAD_DIM).transpose(0, 3, 1, 2, 4)
    vs_blocks = v_slc.reshape(bsz, n_slc, SLC_BLOCK, B_GROUPS, HEAD_DIM).transpose(0, 3, 1, 2, 4)
    kw_pad = jnp.pad(k_win, ((0, 0), (WINDOW, 0), (0, 0), (0, 0)))
    vw_pad = jnp.pad(v_win, ((0, 0), (WINDOW, 0), (0, 0), (0, 0)))
    gates = jax.nn.sigmoid(gate_logits.astype(jnp.float32)).reshape(bsz, seq, B_GROUPS, B_HPG, 3)
    table = bias_table.reshape(NUM_BUCKETS, B_GROUPS, B_HPG).transpose(1, 0, 2)
    b_idx = jnp.arange(bsz)[:, None, None, None]
    g_idx = jnp.arange(B_GROUPS)[None, :, None, None]
    blk = jnp.arange(n_slc)

    def chunk(c):
        start = c * Q_BLOCK
        t = start + jnp.arange(Q_BLOCK)
        qc = lax.dynamic_slice_in_dim(q, start, Q_BLOCK, axis=1)
        dist_c = t[:, None] - cmp_end[None, :]
        lg = jnp.einsum('bqgnd,bigd->bgnqi', qc, kc).astype(jnp.float32) * scale
        lg = lg + table[:, t5_bucket(dist_c)].transpose(0, 3, 1, 2)
        p_c = masked_softmax(lg, dist_c >= 0)
        o_c = jnp.einsum('bgnqi,bigd->bqgnd', p_c.astype(vc.dtype), vc)
        cur = t // SLC_BLOCK
        sc = jnp.einsum('bgnqi,ij->bgqj', p_c, overlap)
        admissible = blk[None, :] <= cur[:, None]
        forced = (blk[None, :] == 0) | (blk[None, :] == cur[:, None]) | (blk[None, :] == cur[:, None] - 1)
        sc = jnp.where(admissible, jnp.where(forced, jnp.inf, sc), -jnp.inf)
        _, sel = lax.top_k(sc, n_sel)
        n_key = n_sel * SLC_BLOCK
        k_s = ks_blocks[b_idx, g_idx, sel].reshape(bsz, B_GROUPS, Q_BLOCK, n_key, HEAD_DIM)
        v_s = vs_blocks[b_idx, g_idx, sel].reshape(bsz, B_GROUPS, Q_BLOCK, n_key, HEAD_DIM)
        pos_s = (sel[..., None] * SLC_BLOCK + jnp.arange(SLC_BLOCK)).reshape(bsz, B_GROUPS, Q_BLOCK, n_key)
        dist_s = t[None, None, :, None] - pos_s
        lg = jnp.einsum('bqgnd,bgqkd->bgnqk', qc, k_s).astype(jnp.float32) * scale
        lg = lg + jnp.moveaxis(table[g_idx, t5_bucket(dist_s)], -1, 2)
        p_s = masked_softmax(lg, (dist_s >= 0)[:, :, None])
        o_s = jnp.einsum('bgnqk,bgqkd->bqgnd', p_s.astype(v_s.dtype), v_s)
        pos_w = start - WINDOW + jnp.arange(Q_BLOCK + WINDOW)
        dist_w = t[:, None] - pos_w[None, :]
        valid_w = (dist_w >= 0) & (dist_w < WINDOW) & (pos_w[None, :] >= 0)
        k_w = lax.dynamic_slice_in_dim(kw_pad, start, Q_BLOCK + WINDOW, axis=1)
        v_w = lax.dynamic_slice_in_dim(vw_pad, start, Q_BLOCK + WINDOW, axis=1)
        lg = jnp.einsum('bqgnd,bkgd->bgnqk', qc, k_w).astype(jnp.float32) * scale
        lg = lg + table[:, t5_bucket(dist_w)].transpose(0, 3, 1, 2)
        p_w = masked_softmax(lg, valid_w)
        o_w = jnp.einsum('bgnqk,bkgd->bqgnd', p_w.astype(v_w.dtype), v_w)
        g = lax.dynamic_slice_in_dim(gates, start, Q_BLOCK, axis=1)
        o = g[..., 0:1] * o_c + g[..., 1:2] * o_s + g[..., 2:3] * o_w
        return o.astype(q.dtype).reshape(bsz, Q_BLOCK, B_HEADS * HEAD_DIM)

    out = lax.map(chunk, jnp.arange(seq // Q_BLOCK))
    return out.transpose(1, 0, 2, 3).reshape(bsz, seq, B_HEADS * HEAD_DIM)


def moba_attention(q, k, v, bias_table):
    bsz, seq = q.shape[:2]
    scale = HEAD_DIM ** -0.5
    n_blk = -(-seq // MOBA_BLOCK)
    pad = n_blk * MOBA_BLOCK - seq
    k_pad = jnp.pad(k, ((0, 0), (0, pad), (0, 0), (0, 0)))
    v_pad = jnp.pad(v, ((0, 0), (0, pad), (0, 0), (0, 0)))
    k_blocks = k_pad.reshape(bsz, n_blk, MOBA_BLOCK, C_HEADS, HEAD_DIM)
    k_mean = jnp.mean(k_blocks, axis=2)
    kb = k_blocks.transpose(0, 3, 1, 2, 4)
    vb = v_pad.reshape(bsz, n_blk, MOBA_BLOCK, C_HEADS, HEAD_DIM).transpose(0, 3, 1, 2, 4)
    n_sel = min(MOBA_TOPK, n_blk - 1)
    table = bias_table.T
    b_idx = jnp.arange(bsz)[:, None, None, None]
    h_idx = jnp.arange(C_HEADS)[None, :, None, None]
    blk = jnp.arange(n_blk)

    def chunk(c):
        start = c * MOBA_Q_BLOCK
        t = start + jnp.arange(MOBA_Q_BLOCK)
        own = start // MOBA_BLOCK
        qc = lax.dynamic_slice_in_dim(q, start, MOBA_Q_BLOCK, axis=1)
        k_o = lax.dynamic_slice_in_dim(k_pad, own * MOBA_BLOCK, MOBA_BLOCK, axis=1)
        v_o = lax.dynamic_slice_in_dim(v_pad, own * MOBA_BLOCK, MOBA_BLOCK, axis=1)
        dist_o = t[:, None] - (own * MOBA_BLOCK + jnp.arange(MOBA_BLOCK))[None, :]
        lg_o = jnp.einsum('bqhd,bkhd->bhqk', qc, k_o).astype(jnp.float32) * scale
        lg_o = lg_o + table[:, t5_bucket(dist_o)]
        mask_o = jnp.broadcast_to(dist_o >= 0, lg_o.shape)
        if n_sel == 0:
            p = masked_softmax(lg_o, mask_o)
            o = jnp.einsum('bhqk,bkhd->bqhd', p.astype(v_o.dtype), v_o)
        else:
            gate = jnp.einsum('bqhd,bjhd->bhqj', qc, k_mean).astype(jnp.float32)
            gate = jnp.where(blk < own, gate, -jnp.inf)
            _, sel = lax.top_k(gate, n_sel)
            n_key = n_sel * MOBA_BLOCK
            k_s = kb[b_idx, h_idx, sel].reshape(bsz, C_HEADS, MOBA_Q_BLOCK, n_key, HEAD_DIM)
            v_s = vb[b_idx, h_idx, sel].reshape(bsz, C_HEADS, MOBA_Q_BLOCK, n_key, HEAD_DIM)
            pos_s = (sel[..., None] * MOBA_BLOCK + jnp.arange(MOBA_BLOCK)).reshape(bsz, C_HEADS, MOBA_Q_BLOCK, n_key)
            lg_s = jnp.einsum('bqhd,bhqkd->bhqk', qc, k_s).astype(jnp.float32) * scale
            lg_s = lg_s + table[h_idx, t5_bucket(t[None, None, :, None] - pos_s)]
            mask_s = jnp.broadcast_to((sel < own)[..., None], sel.shape + (MOBA_BLOCK,)).reshape(lg_s.shape)
            p = masked_softmax(jnp.concatenate([lg_s, lg_o], axis=-1), jnp.concatenate([mask_s, mask_o], axis=-1))
            o = (jnp.einsum('bhqk,bhqkd->bqhd', p[..., :n_key].astype(v_s.dtype), v_s)
                 + jnp.einsum('bhqk,bkhd->bqhd', p[..., n_key:].astype(v_o.dtype), v_o))
        return o.astype(q.dtype).reshape(bsz, MOBA_Q_BLOCK, C_HEADS * HEAD_DIM)

    out = lax.map(chunk, jnp.arange(seq // MOBA_Q_BLOCK))
    return out.transpose(1, 0, 2, 3).reshape(bsz, seq, C_HEADS * HEAD_DIM)


def moe_swiglu(h, router, w1, w3, w2):
    bsz, seq, dm = h.shape
    n_tok = bsz * seq
    xf = h.reshape(n_tok, dm)
    logits = (xf @ router).astype(jnp.float32)
    top_val, top_e = lax.top_k(logits, TOP_K)
    gate = jax.nn.softmax(top_val, axis=-1)
    e_flat = top_e.reshape(-1)
    tok_flat = jnp.repeat(jnp.arange(n_tok), TOP_K)
    g_flat = gate.reshape(-1)
    order = jnp.argsort(e_flat)
    e_s, tok_s, g_s = e_flat[order], tok_flat[order], g_flat[order]
    counts = jnp.bincount(e_flat, length=N_EXPERTS)
    padded = (counts + EXPERT_ROWS - 1) // EXPERT_ROWS * EXPERT_ROWS
    start = jnp.cumsum(counts) - counts
    pend = jnp.cumsum(padded)
    pstart = pend - padded
    n_assign = n_tok * TOP_K
    dest = pstart[e_s] + jnp.arange(n_assign) - start[e_s]
    n_rows = -(-n_assign // EXPERT_ROWS) * EXPERT_ROWS + N_EXPERTS * EXPERT_ROWS
    n_groups = n_rows // EXPERT_ROWS
    row_tok = jnp.full((n_rows,), n_tok, jnp.int32).at[dest].set(tok_s)
    x_pad = jnp.concatenate([xf, jnp.zeros((1, dm), xf.dtype)], axis=0)
    x_rows = x_pad[row_tok].reshape(n_groups, EXPERT_ROWS, dm)
    grp_e = jnp.minimum(jnp.searchsorted(pend, jnp.arange(n_groups) * EXPERT_ROWS, side='right'), N_EXPERTS - 1)

    def expert_group(args):
        xg, e = args
        return swiglu(xg, w1[e], w3[e], w2[e])

    y_rows = lax.map(expert_group, (x_rows, grp_e)).reshape(n_rows, dm)
    y = jax.ops.segment_sum(y_rows[dest] * g_s[:, None], tok_s, num_segments=n_tok)
    return y.astype(h.dtype).reshape(bsz, seq, dm)


def setup_inputs(seed: int = 0) -> dict:
    key = jax.random.key(seed)
    keys = iter(jax.random.split(key, 40))

    def nrm(shape, scale):
        return jax.random.normal(next(keys), shape, jnp.float32) * scale

    def gain(shape):
        return 1.0 + nrm(shape, 0.02)

    d = D_MODEL
    return {
        'x': nrm((BATCH, SEQ, d), 1.0),
        'rel_bias': nrm((NUM_BUCKETS, N_BIAS_HEADS), 0.5),
        'e_w_in': nrm((N_EVEN, d, EVEN_IN), d ** -0.5),
        'e_q_norm': gain((N_EVEN, A_Q_RANK)),
        'e_kv_norm': gain((N_EVEN, A_KV_RANK)),
        'e_w_uq': nrm((N_EVEN, A_Q_RANK, A_HEADS, HEAD_DIM), A_Q_RANK ** -0.5),
        'e_w_uk': nrm((N_EVEN, A_KV_RANK, A_HEADS, HEAD_DIM), A_KV_RANK ** -0.5),
        'e_w_uv': nrm((N_EVEN, A_KV_RANK, A_HEADS, HEAD_DIM), A_KV_RANK ** -0.5),
        'e_w_qidx': nrm((N_EVEN, A_Q_RANK, IDX_HEADS, IDX_DIM), A_Q_RANK ** -0.5),
        'e_pos_k': nrm((N_EVEN, CMP_LEN, HEAD_DIM), 0.1),
        'e_pos_v': nrm((N_EVEN, CMP_LEN, HEAD_DIM), 0.1),
        'e_ck1': nrm((N_EVEN, CMP_LEN, HEAD_DIM, HEAD_DIM), (CMP_LEN * HEAD_DIM) ** -0.5),
        'e_ck2': nrm((N_EVEN, HEAD_DIM, HEAD_DIM), HEAD_DIM ** -0.5),
        'e_cv1': nrm((N_EVEN, CMP_LEN, HEAD_DIM, HEAD_DIM), (CMP_LEN * HEAD_DIM) ** -0.5),
        'e_cv2': nrm((N_EVEN, HEAD_DIM, HEAD_DIM), HEAD_DIM ** -0.5),
        'e_w_out': nrm((N_EVEN, MIX_WIDTH_EVEN, d), MIX_WIDTH_EVEN ** -0.5 * BETA),
        'e_ln1_g': gain((N_EVEN, d)),
        'e_ln1_b': nrm((N_EVEN, d), 0.02),
        'e_ffn_w1': nrm((N_EVEN, d, D_FF), d ** -0.5),
        'e_ffn_w3': nrm((N_EVEN, d, D_FF), d ** -0.5),
        'e_ffn_w2': nrm((N_EVEN, D_FF, d), D_FF ** -0.5 * BETA),
        'e_ln2_g': gain((N_EVEN, d)),
        'e_ln2_b': nrm((N_EVEN, d), 0.02),
        'o_w_in': nrm((N_ODD, d, ODD_IN), d ** -0.5),
        'o_w_out': nrm((N_ODD, MIX_WIDTH_ODD, d), MIX_WIDTH_ODD ** -0.5 * BETA),
        'o_ln1_g': gain((N_ODD, d)),
        'o_ln1_b': nrm((N_ODD, d), 0.02),
        'o_router': nrm((N_ODD, d, N_EXPERTS), d ** -0.5),
        'o_moe_w1': nrm((N_ODD, N_EXPERTS, d, D_FF_EXPERT), d ** -0.5),
        'o_moe_w3': nrm((N_ODD, N_EXPERTS, d, D_FF_EXPERT), d ** -0.5),
        'o_moe_w2': nrm((N_ODD, N_EXPERTS, D_FF_EXPERT, d), D_FF_EXPERT ** -0.5 * BETA),
        'o_ln2_g': gain((N_ODD, d)),
        'o_ln2_b': nrm((N_ODD, d), 0.02),
    }


def reference(x, rel_bias,
              e_w_in, e_q_norm, e_kv_norm, e_w_uq, e_w_uk, e_w_uv, e_w_qidx,
              e_pos_k, e_pos_v, e_ck1, e_ck2, e_cv1, e_cv2, e_w_out,
              e_ln1_g, e_ln1_b, e_ffn_w1, e_ffn_w3, e_ffn_w2, e_ln2_g, e_ln2_b,
              o_w_in, o_w_out, o_ln1_g, o_ln1_b, o_router, o_moe_w1, o_moe_w3, o_moe_w2,
              o_ln2_g, o_ln2_b):
    bsz, seq = x.shape[:2]
    for layer in range(DEPTH):
        i = layer // 2
        if layer % 2 == 0:
            parts = split_cols(x @ e_w_in[i], EVEN_SPLITS)
            c_q, c_kv, k_idx, w_idx, q_b = parts[0], parts[1], parts[2], parts[3], parts[4]
            k_c, v_c, k_sl, v_sl, k_w, v_w = parts[5], parts[6], parts[7], parts[8], parts[9], parts[10]
            gate_logits = parts[11]
            o_a = dsa_attention(c_q, c_kv, k_idx, w_idx, e_q_norm[i], e_kv_norm[i],
                                e_w_uq[i], e_w_uk[i], e_w_uv[i], e_w_qidx[i], rel_bias[:, :A_HEADS])
            o_b = nsa_attention(q_b, k_c, v_c, k_sl, v_sl, k_w, v_w, gate_logits,
                                e_pos_k[i], e_pos_v[i], e_ck1[i], e_ck2[i], e_cv1[i], e_cv2[i],
                                rel_bias[:, A_HEADS:A_HEADS + B_HEADS])
            mix = jnp.concatenate([o_a, o_b], axis=-1) @ e_w_out[i]
            h = layer_norm(ALPHA * x + mix, e_ln1_g[i], e_ln1_b[i])
            ffn = swiglu(h, e_ffn_w1[i], e_ffn_w3[i], e_ffn_w2[i])
            x = layer_norm(ALPHA * h + ffn, e_ln2_g[i], e_ln2_b[i])
        else:
            q, k, v = [a.reshape(bsz, seq, C_HEADS, HEAD_DIM) for a in jnp.split(x @ o_w_in[i], 3, axis=-1)]
            mix = moba_attention(q, k, v, rel_bias[:, :C_HEADS]) @ o_w_out[i]
            h = layer_norm(ALPHA * x + mix, o_ln1_g[i], o_ln1_b[i])
            ffn = moe_swiglu(h, o_router[i], o_moe_w1[i], o_moe_w3[i], o_moe_w2[i])
            x = layer_norm(ALPHA * h + ffn, o_ln2_g[i], o_ln2_b[i])
    return x
```

```python
import functools
import math

import numpy as np
import jax
import jax.numpy as jnp
from jax import lax
from jax.experimental import pallas as pl
from jax.experimental.pallas import tpu as pltpu

F32 = jnp.float32
BF16 = jnp.bfloat16
I32 = jnp.int32

HEAD_DIM = 64
NUM_BUCKETS = 32
MAX_DISTANCE = 128
N_BIAS_HEADS = 16
A_HEADS = 8
A_Q_RANK = 256
A_KV_RANK = 128
IDX_HEADS = 16
IDX_DIM = 64
DSA_TOPK = 256
B_HEADS = 8
B_GROUPS = 2
B_HPG = B_HEADS // B_GROUPS
CMP_LEN = 32
CMP_STRIDE = 16
SLC_BLOCK = 64
SLC_TOPN = 16
WINDOW = 512
C_HEADS = 16
MOBA_BLOCK = 256
MOBA_TOPK = 3
N_EXPERTS = 8
TOP_K = 2
EXPERT_ROWS = 256
DEPTH = 2
ALPHA = (2 * DEPTH) ** 0.25

NEG = -1e30
NEG_HALF = -5e29
INT_MIN = -2 ** 31
VMEM_LIMIT = 56 * 1024 * 1024


def _t5_thresholds():
    def bucket(n):
        if n < NUM_BUCKETS // 2:
            return n
        v = np.log(np.float32(n) / np.float32(NUM_BUCKETS // 2)) / np.float32(math.log(MAX_DISTANCE / (NUM_BUCKETS // 2)))
        return min(NUM_BUCKETS // 2 + int(np.float32(v) * (NUM_BUCKETS - NUM_BUCKETS // 2)), NUM_BUCKETS - 1)
    b = [bucket(i) for i in range(4 * MAX_DISTANCE)]
    return [0] + [min(i for i in range(len(b)) if b[i] >= k) for k in range(1, NUM_BUCKETS)]


T5_THR = _t5_thresholds()
T5_FAR = T5_THR[-1]


def _cparams(sem):
    return pltpu.CompilerParams(dimension_semantics=sem, vmem_limit_bytes=VMEM_LIMIT)


def _bias_kernel(tab_ref, off_ref, o_ref, *, c_row, c_col, h0, causal_neg):
    v = pl.program_id(0)
    h = pl.program_id(1) + h0
    shape = o_ref.shape[2:]
    dist = (c_col * lax.broadcasted_iota(I32, shape, 1) + c_row * lax.broadcasted_iota(I32, shape, 0) + off_ref[v])
    n = jnp.maximum(dist, 0)
    acc = jnp.full(shape, tab_ref[h], F32)
    for k in range(1, NUM_BUCKETS):
        acc = jnp.where(n >= T5_THR[k], tab_ref[k * N_BIAS_HEADS + h], acc)
    if causal_neg:
        acc = jnp.where(dist >= 0, acc, NEG)
    o_ref[0, 0] = acc


def bias_tiles(rel_bias, offs, n_heads, h0, rows, cols, c_row, c_col, causal_neg):
    offs = jnp.asarray(offs, I32)
    nv = offs.shape[0]
    return pl.pallas_call(
        functools.partial(_bias_kernel, c_row=c_row, c_col=c_col, h0=h0, causal_neg=causal_neg),
        grid=(nv, n_heads),
        in_specs=[pl.BlockSpec(memory_space=pltpu.SMEM), pl.BlockSpec(memory_space=pltpu.SMEM)],
        out_specs=pl.BlockSpec((1, 1, rows, cols), lambda v, h: (v, h, 0, 0)),
        out_shape=jax.ShapeDtypeStruct((nv, n_heads, rows, cols), F32),
        compiler_params=_cparams(("arbitrary", "arbitrary")),
        name="t5_bias_tiles",
    )(rel_bias.reshape(-1), offs)


def _mm_kernel(x_ref, w_ref, o_ref):
    o_ref[...] = jnp.dot(x_ref[...].astype(BF16), w_ref[...].astype(BF16),
                         preferred_element_type=F32).astype(o_ref.dtype)


def matmul(x, w, out_dtype, tm=512):
    m, k = x.shape
    n = w.shape[1]
    tm = min(tm, m)
    return pl.pallas_call(
        _mm_kernel,
        grid=(m // tm,),
        in_specs=[pl.BlockSpec((tm, k), lambda i: (i, 0)), pl.BlockSpec((k, n), lambda i: (0, 0))],
        out_specs=pl.BlockSpec((tm, n), lambda i: (i, 0)),
        out_shape=jax.ShapeDtypeStruct((m, n), out_dtype),
        compiler_params=_cparams(("arbitrary",)),
        name="matmul",
    )(x, w)


def _layer_norm_rows(z, g, b):
    mu = jnp.mean(z, axis=-1, keepdims=True)
    zc = z - mu
    var = jnp.mean(zc * zc, axis=-1, keepdims=True)
    return zc * lax.rsqrt(var + 1e-5) * g + b


def _proj_ln_kernel(a_ref, w_ref, x_ref, g_ref, b_ref, o_ref):
    mix = jnp.dot(a_ref[...], w_ref[...], preferred_element_type=F32)
    o_ref[...] = _layer_norm_rows(ALPHA * x_ref[...] + mix, g_ref[...], b_ref[...])


def proj_residual_ln(a, w, x, g, b, tm=512):
    m, k = a.shape
    d = w.shape[1]
    tm = min(tm, m)
    return pl.pallas_call(
        _proj_ln_kernel,
        grid=(m // tm,),
        in_specs=[pl.BlockSpec((tm, k), lambda i: (i, 0)), pl.BlockSpec((k, d), lambda i: (0, 0)),
                  pl.BlockSpec((tm, d), lambda i: (i, 0)),
                  pl.BlockSpec((1, d), lambda i: (0, 0)), pl.BlockSpec((1, d), lambda i: (0, 0))],
        out_specs=pl.BlockSpec((tm, d), lambda i: (i, 0)),
        out_shape=jax.ShapeDtypeStruct((m, d), F32),
        compiler_params=_cparams(("arbitrary",)),
        name="proj_residual_ln",
    )(a, w, x, g.reshape(1, d), b.reshape(1, d))


def _ffn_kernel(ge_ref, x_ref, w1_ref, w3_ref, w2_ref, g_ref, b_ref, o_ref, *, ff_chunk, with_ln):
    del ge_ref
    x = x_ref[...]
    xb = x.astype(BF16)
    d_ff = w1_ref.shape[2]
    acc = jnp.zeros((x.shape[0], w2_ref.shape[2]), F32)
    for c in range(0, d_ff, ff_chunk):
        a = jnp.dot(xb, w1_ref[0, :, c:c + ff_chunk], preferred_element_type=F32)
        u = jnp.dot(xb, w3_ref[0, :, c:c + ff_chunk], preferred_element_type=F32)
        hid = (a * jax.nn.sigmoid(a) * u).astype(BF16)
        acc = acc + jnp.dot(hid, w2_ref[0, c:c + ff_chunk, :], preferred_element_type=F32)
    if with_ln:
        o_ref[...] = _layer_norm_rows(ALPHA * x.astype(F32) + acc, g_ref[...], b_ref[...]).astype(o_ref.dtype)
    else:
        o_ref[...] = acc.astype(o_ref.dtype)


def swiglu_ffn(x_rows, grp_e, w1, w3, w2, ln_g, ln_b, *, with_ln, out_dtype, tm, ff_chunk):
    m, d = x_rows.shape
    d_ff = w1.shape[2]
    once = pl.Buffered(1)
    grid_spec = pltpu.PrefetchScalarGridSpec(
        num_scalar_prefetch=1,
        grid=(m // tm,),
        in_specs=[pl.BlockSpec((tm, d), lambda i, ge: (i, 0)),
                  pl.BlockSpec((1, d, d_ff), lambda i, ge: (ge[i], 0, 0), pipeline_mode=once),
                  pl.BlockSpec((1, d, d_ff), lambda i, ge: (ge[i], 0, 0), pipeline_mode=once),
                  pl.BlockSpec((1, d_ff, d), lambda i, ge: (ge[i], 0, 0), pipeline_mode=once),
                  pl.BlockSpec((1, d), lambda i, ge: (0, 0)), pl.BlockSpec((1, d), lambda i, ge: (0, 0))],
        out_specs=pl.BlockSpec((tm, d), lambda i, ge: (i, 0)),
    )
    return pl.pallas_call(
        functools.partial(_ffn_kernel, ff_chunk=ff_chunk, with_ln=with_ln),
        grid_spec=grid_spec,
        out_shape=jax.ShapeDtypeStruct((m, d), out_dtype),
        compiler_params=_cparams(("arbitrary",)),
        name="swiglu_ffn",
    )(grp_e, x_rows, w1, w3, w2, ln_g.reshape(1, d), ln_b.reshape(1, d))


def _flash_update(s, vT, m, l, acc):
    m_new = jnp.maximum(m, jnp.max(s, axis=0, keepdims=True))
    alpha = jnp.exp(m - m_new)
    p = jnp.exp(s - m_new)
    l_new = alpha * l + jnp.sum(p, axis=0, keepdims=True)
    acc_new = alpha * acc + jnp.dot(vT, p.astype(BF16), preferred_element_type=F32)
    return m_new, l_new, acc_new


def _flash_finish(m, l, acc):
    return jnp.where(m > NEG_HALF, acc / l, 0.0)


def _rms_rows(x, g):
    return x * lax.rsqrt(jnp.mean(x * x, axis=-1, keepdims=True) + 1e-6) * g


def _dsa_prep_kernel(cq_ref, ckv_ref, qn_ref, kvn_ref, wuq_ref, wukt_ref, wqi_ref,
                     qlat_ref, qidx_ref, ckvn_ref):
    cqn = _rms_rows(cq_ref[...], qn_ref[...]).astype(BF16)
    ckvn_ref[...] = _rms_rows(ckv_ref[...], kvn_ref[...]).astype(BF16)
    q = jnp.dot(cqn, wuq_ref[...], preferred_element_type=F32).astype(BF16)
    for h in range(A_HEADS):
        ql = jnp.dot(q[:, h * HEAD_DIM:(h + 1) * HEAD_DIM], wukt_ref[h], preferred_element_type=F32)
        qlat_ref[:, h * A_KV_RANK:(h + 1) * A_KV_RANK] = (ql * HEAD_DIM ** -0.5).astype(BF16)
    qidx_ref[...] = jnp.dot(cqn, wqi_ref[...], preferred_element_type=F32).astype(BF16)


def dsa_prep(c_q, c_kv, q_norm, kv_norm, w_uq, w_uk, w_qidx, tm=512):
    m = c_q.shape[0]
    tm = min(tm, m)
    wuq = w_uq.reshape(A_Q_RANK, A_HEADS * HEAD_DIM).astype(BF16)
    wukt = jnp.transpose(w_uk, (1, 2, 0)).astype(BF16)
    wqi = w_qidx.reshape(A_Q_RANK, IDX_HEADS * IDX_DIM).astype(BF16)
    full = lambda shape: pl.BlockSpec(shape, lambda i: (0,) * len(shape))
    rows = lambda n: pl.BlockSpec((tm, n), lambda i: (i, 0))
    return pl.pallas_call(
        _dsa_prep_kernel,
        grid=(m // tm,),
        in_specs=[rows(A_Q_RANK), rows(A_KV_RANK), full((1, A_Q_RANK)), full((1, A_KV_RANK)),
                  full(wuq.shape), full(wukt.shape), full(wqi.shape)],
        out_specs=[rows(A_HEADS * A_KV_RANK), rows(IDX_HEADS * IDX_DIM), rows(A_KV_RANK)],
        out_shape=[jax.ShapeDtypeStruct((m, A_HEADS * A_KV_RANK), BF16),
                   jax.ShapeDtypeStruct((m, IDX_HEADS * IDX_DIM), BF16),
                   jax.ShapeDtypeStruct((m, A_KV_RANK), BF16)],
        compiler_params=_cparams(("arbitrary",)),
        name="dsa_prep",
    )(c_q, c_kv, q_norm.reshape(1, -1), kv_norm.reshape(1, -1), wuq, wukt, wqi)


DSA_T = 256
SUB = 128


def _dsa_kernel(qidx_ref, wT_ref, qlat_ref, kidx_ref, ckv_ref, ckvT_ref, bias_ref, wuvt_ref, o_ref,
                key_ref, m_ref, l_ref, acc_ref, *, n_keep):
    qi = pl.program_id(1)
    nkb = qi + 1
    T = DSA_T

    def score_block(kb, carry):
        for sub in range(T // SUB):
            k = kidx_ref[0, kb, sub * SUB:(sub + 1) * SUB, :]
            acc = jnp.zeros((SUB, T), F32)
            for h in range(IDX_HEADS):
                d = jnp.dot(k, qidx_ref[0, h], preferred_element_type=F32)
                acc = acc + jnp.maximum(d, 0.0) * wT_ref[0, h:h + 1, :]
            bits = lax.bitcast_convert_type(acc, I32)
            key = bits ^ (lax.shift_right_arithmetic(bits, 31) & 0x7FFFFFFF)
            s_pos = kb * T + sub * SUB + lax.broadcasted_iota(I32, (SUB, T), 0)
            t_pos = qi * T + lax.broadcasted_iota(I32, (SUB, T), 1)
            key = jnp.where(s_pos <= t_pos, key, INT_MIN)
            key_ref[pl.ds(pl.multiple_of(kb * T + sub * SUB, SUB), SUB), :] = key
        return carry

    lax.fori_loop(0, nkb, score_block, 0)

    n_chunks = nkb * (T // SUB)

    def count_ge(cand):
        def body(i, cnt):
            blk = key_ref[pl.ds(pl.multiple_of(i * SUB, SUB), SUB), :]
            ge = jnp.where(blk >= cand, 1, 0).astype(I32)
            return cnt + jnp.sum(ge.reshape(SUB // 8, 8, T), axis=0)
        cnt = lax.fori_loop(0, n_chunks, body, jnp.zeros((8, T), I32))
        return jnp.sum(cnt, axis=0, keepdims=True)

    def bit_step(i, u):
        cand_u = u | lax.shift_left(jnp.int32(1), 31 - i)
        cnt = count_ge(cand_u ^ INT_MIN)
        return jnp.where(cnt >= n_keep, cand_u, u)

    u = lax.fori_loop(0, 32, bit_step, jnp.zeros((1, T), I32))
    thr = jnp.maximum(u ^ INT_MIN, INT_MIN + 1)

    m_ref[...] = jnp.full(m_ref.shape, NEG, F32)
    l_ref[...] = jnp.zeros(l_ref.shape, F32)
    acc_ref[...] = jnp.zeros(acc_ref.shape, F32)

    def attend(kb, carry):
        sel = key_ref[pl.ds(pl.multiple_of(kb * T, T), T), :] >= thr
        ckv = ckv_ref[0, kb]
        ckvT = ckvT_ref[0, kb]
        rel = jnp.minimum(qi - kb, 2)
        for h in range(A_HEADS):
            s = jnp.dot(ckv, qlat_ref[0, h], preferred_element_type=F32) + bias_ref[rel, h]
            s = jnp.where(sel, s, NEG)
            m, l, acc = _flash_update(s, ckvT, m_ref[h], l_ref[h], acc_ref[h])
            m_ref[h] = m
            l_ref[h] = l
            acc_ref[h] = acc
        return carry

    lax.fori_loop(0, nkb, attend, 0)

    for h in range(A_HEADS):
        o_lat = _flash_finish(m_ref[h], l_ref[h], acc_ref[h]).astype(BF16)
        o_ref[0, h * HEAD_DIM:(h + 1) * HEAD_DIM, :] = jnp.dot(
            wuvt_ref[h], o_lat, preferred_element_type=F32).astype(o_ref.dtype)


def dsa_attention(q_lat, q_idx, ckvn, k_idx, w_idx, w_uv, bias3, bsz, seq):
    T = DSA_T
    nq = seq // T
    n_keep = min(DSA_TOPK, seq // 4)
    qidxT = q_idx.reshape(bsz, seq, IDX_HEADS, IDX_DIM).transpose(0, 2, 3, 1)
    qlatT = q_lat.reshape(bsz, seq, A_HEADS, A_KV_RANK).transpose(0, 2, 3, 1)
    wT = w_idx.reshape(bsz, seq, IDX_HEADS).transpose(0, 2, 1)
    kidx = k_idx.astype(BF16).reshape(bsz, nq, T, IDX_DIM)
    ckv = ckvn.reshape(bsz, nq, T, A_KV_RANK)
    ckvT = ckv.transpose(0, 1, 3, 2)
    wuvt = jnp.transpose(w_uv, (1, 2, 0)).astype(BF16)
    return pl.pallas_call(
        functools.partial(_dsa_kernel, n_keep=n_keep),
        grid=(bsz, nq),
        in_specs=[pl.BlockSpec((1, IDX_HEADS, IDX_DIM, T), lambda b, i: (b, 0, 0, i)),
                  pl.BlockSpec((1, IDX_HEADS, T), lambda b, i: (b, 0, i)),
                  pl.BlockSpec((1, A_HEADS, A_KV_RANK, T), lambda b, i: (b, 0, 0, i)),
                  pl.BlockSpec((1, nq, T, IDX_DIM), lambda b, i: (b, 0, 0, 0)),
                  pl.BlockSpec((1, nq, T, A_KV_RANK), lambda b, i: (b, 0, 0, 0)),
                  pl.BlockSpec((1, nq, A_KV_RANK, T), lambda b, i: (b, 0, 0, 0)),
                  pl.BlockSpec((3, A_HEADS, T, T), lambda b, i: (0, 0, 0, 0)),
                  pl.BlockSpec((A_HEADS, HEAD_DIM, A_KV_RANK), lambda b, i: (0, 0, 0))],
        out_specs=pl.BlockSpec((1, A_HEADS * HEAD_DIM, T), lambda b, i: (b, 0, i)),
        out_shape=jax.ShapeDtypeStruct((bsz, A_HEADS * HEAD_DIM, seq), BF16),
        scratch_shapes=[pltpu.VMEM((seq, T), I32),
                        pltpu.VMEM((A_HEADS, 1, T), F32), pltpu.VMEM((A_HEADS, 1, T), F32),
                        pltpu.VMEM((A_HEADS, A_KV_RANK, T), F32)],
        compiler_params=_cparams(("arbitrary", "arbitrary")),
        name="dsa_attention",
    )(qidxT, wT, qlatT, kidx, ckv, ckvT, bias3, wuvt)


def dsa_bias_tiles(rel_bias):
    assert DSA_T + 1 >= T5_FAR
    return bias_tiles(rel_bias, [0, DSA_T, 4 * DSA_T], A_HEADS, 0, DSA_T, DSA_T, -1, 1, False)


N_CMP_PAD = 256


def _compress_kernel(blk_ref, pos_ref, w1_ref, w2_ref, o_ref):
    x = (blk_ref[0].astype(F32) + pos_ref[...]).astype(BF16)
    hid = jax.nn.gelu(jnp.dot(x, w1_ref[...], preferred_element_type=F32))
    o_ref[0] = jnp.dot(hid.astype(BF16), w2_ref[...], preferred_element_type=F32).astype(o_ref.dtype)


def nsa_compress(a, pos, w1, w2, bsz, seq):
    n_chunk = seq // CMP_STRIDE
    assert CMP_LEN == 2 * CMP_STRIDE and n_chunk <= N_CMP_PAD
    width = CMP_STRIDE * HEAD_DIM
    chunks = a.reshape(bsz, n_chunk, CMP_STRIDE, B_GROUPS, HEAD_DIM).transpose(0, 3, 1, 2, 4)
    chunks = chunks.reshape(bsz * B_GROUPS, n_chunk, width)
    blocks = jnp.concatenate([chunks[:, :-1], chunks[:, 1:]], axis=-1)
    blocks = jnp.pad(blocks, ((0, 0), (0, N_CMP_PAD - (n_chunk - 1)), (0, 0)))
    out = pl.pallas_call(
        _compress_kernel,
        grid=(bsz * B_GROUPS,),
        in_specs=[pl.BlockSpec((1, N_CMP_PAD, 2 * width), lambda i: (i, 0, 0)),
                  pl.BlockSpec((1, 2 * width), lambda i: (0, 0)),
                  pl.BlockSpec((2 * width, HEAD_DIM), lambda i: (0, 0)),
                  pl.BlockSpec((HEAD_DIM, HEAD_DIM), lambda i: (0, 0))],
        out_specs=pl.BlockSpec((1, N_CMP_PAD, HEAD_DIM), lambda i: (i, 0, 0)),
        out_shape=jax.ShapeDtypeStruct((bsz * B_GROUPS, N_CMP_PAD, HEAD_DIM), BF16),
        compiler_params=_cparams(("arbitrary",)),
        name="nsa_compress",
    )(blocks, pos.reshape(1, 2 * width), w1.reshape(2 * width, HEAD_DIM).astype(BF16), w2.astype(BF16))
    return out.reshape(bsz, B_GROUPS, N_CMP_PAD, HEAD_DIM)


NSA_TQ = 128
NSA_L = B_HPG * NSA_TQ
NSA_REL = 5


def _nsa_kernel(qT_ref, kc_ref, vcT_ref, biasc_ref, ovl_ref, ks_ref, vsT_ref, kw_ref, vwT_ref,
                toe_ref, gate_ref, o_ref, selb_ref, *, n_cmp, n_sel):
    qi = pl.program_id(2)
    TQ, L = NSA_TQ, NSA_L
    n_slc = ks_ref.shape[2]
    q0 = qi * TQ
    qblk = q0 // SLC_BLOCK
    qT = qT_ref[0, 0, 0]
    t_lane = q0 + (lax.broadcasted_iota(I32, (1, L), 1) & (TQ - 1))

    s = jnp.dot(kc_ref[0, 0], qT, preferred_element_type=F32) + biasc_ref[0, 0]
    i_idx = lax.broadcasted_iota(I32, (N_CMP_PAD, L), 0)
    valid = jnp.where(i_idx < n_cmp, i_idx * CMP_STRIDE + (CMP_LEN - 1), 2 ** 30) <= t_lane
    s = jnp.where(valid, s, NEG)
    m = jnp.max(s, axis=0, keepdims=True)
    p = jnp.where(valid, jnp.exp(s - m), 0.0)
    l = jnp.sum(p, axis=0, keepdims=True)
    p_c = p / jnp.where(l > 0, l, 1.0)
    o_c = jnp.dot(vcT_ref[0, 0], p_c.astype(BF16), preferred_element_type=F32)

    psum = p_c[:, 0:TQ]
    for n in range(1, B_HPG):
        psum = psum + p_c[:, n * TQ:(n + 1) * TQ]
    sc = jnp.dot(ovl_ref[...], psum, preferred_element_type=F32, precision=lax.Precision.HIGHEST)
    j_idx = lax.broadcasted_iota(I32, (n_slc, TQ), 0)
    cur = (q0 + lax.broadcasted_iota(I32, (1, TQ), 1)) // SLC_BLOCK
    adm = j_idx <= cur
    forced = (j_idx == 0) | (j_idx == cur) | (j_idx == cur - 1)
    scv = jnp.where(adm, jnp.where(forced, jnp.inf, sc), -jnp.inf)
    rank = jnp.zeros((n_slc, TQ), I32)
    for jp in range(n_slc):
        row = scv[jp:jp + 1, :]
        beats = jnp.where(row > scv, 1, jnp.where((row == scv) & (jp < j_idx), 1, 0))
        rank = rank + beats
    selb = jnp.where(rank < n_sel, 0.0, NEG).astype(F32)
    selb4 = jnp.concatenate([selb] * B_HPG, axis=1)
    for j in range(n_slc):
        selb_ref[j] = selb4[j:j + 1, :]

    zero_state = (jnp.full((1, L), NEG, F32), jnp.zeros((1, L), F32), jnp.zeros((HEAD_DIM, L), F32))
    n_j = qblk + TQ // SLC_BLOCK

    def slc_body(j, carry):
        rel = jnp.clip(qblk - j + 1, 0, NSA_REL - 1)
        s = jnp.dot(ks_ref[0, 0, j], qT, preferred_element_type=F32) + toe_ref[0, rel] + selb_ref[j]
        return _flash_update(s, vsT_ref[0, 0, j], *carry)

    o_s = _flash_finish(*lax.fori_loop(0, n_j, slc_body, zero_state))

    r_idx = lax.broadcasted_iota(I32, (SLC_BLOCK, L), 0)

    def win_body(j, carry):
        rel = jnp.clip(qblk - j + 1, 0, NSA_REL - 1)
        s = jnp.dot(kw_ref[0, 0, j], qT, preferred_element_type=F32) + toe_ref[0, rel]
        dist = t_lane - j * SLC_BLOCK - r_idx
        s = jnp.where(dist < WINDOW, s, NEG)
        return _flash_update(s, vwT_ref[0, 0, j], *carry)

    j_lo = jnp.maximum(qblk - WINDOW // SLC_BLOCK, 0)
    o_w = _flash_finish(*lax.fori_loop(j_lo, n_j, win_body, zero_state))

    g = jax.nn.sigmoid(gate_ref[0, 0, 0])
    o_ref[0, 0, 0] = (g[0:1] * o_c + g[1:2] * o_s + g[2:3] * o_w).astype(o_ref.dtype)


def nsa_bias_inputs(rel_bias, seq):
    TQ, L = NSA_TQ, NSA_L
    nq = seq // TQ
    bc = bias_tiles(rel_bias, [-(CMP_LEN - 1)], B_HEADS, A_HEADS, N_CMP_PAD, seq, -CMP_STRIDE, 1, False)
    bc = bc.reshape(B_GROUPS, B_HPG, N_CMP_PAD, nq, TQ).transpose(0, 3, 2, 1, 4).reshape(B_GROUPS, nq, N_CMP_PAD, L)
    assert (NSA_REL - 2) * SLC_BLOCK - (SLC_BLOCK - 1) >= T5_FAR
    offs = [(v - 1) * SLC_BLOCK for v in range(NSA_REL - 1)] + [64 * SLC_BLOCK]
    toe = bias_tiles(rel_bias, offs, B_HEADS, A_HEADS, SLC_BLOCK, TQ, -1, 1, True)
    toe = toe.reshape(NSA_REL, B_GROUPS, B_HPG, SLC_BLOCK, TQ).transpose(1, 0, 3, 2, 4).reshape(B_GROUPS, NSA_REL, SLC_BLOCK, L)
    return bc, toe


def nsa_overlap(seq):
    n_cmp = (seq - CMP_LEN) // CMP_STRIDE + 1
    n_slc = seq // SLC_BLOCK
    cs = np.arange(N_CMP_PAD) * CMP_STRIDE
    ss = np.arange(n_slc) * SLC_BLOCK
    ov = ((cs[None, :] + CMP_LEN - 1 >= ss[:, None]) & (cs[None, :] <= ss[:, None] + SLC_BLOCK - 1)
          & (np.arange(N_CMP_PAD)[None, :] < n_cmp))
    return jnp.asarray(ov.astype(np.float32))


def nsa_attention(q, kc, vc, k_slc, v_slc, k_win, v_win, gate_logits, biasc, toe, bsz, seq):
    TQ, L = NSA_TQ, NSA_L
    nq = seq // TQ
    n_slc = seq // SLC_BLOCK
    n_cmp = (seq - CMP_LEN) // CMP_STRIDE + 1
    n_sel = min(SLC_TOPN, n_slc)
    scale = HEAD_DIM ** -0.5
    qT = (q * scale).astype(BF16).reshape(bsz, nq, TQ, B_GROUPS, B_HPG, HEAD_DIM)
    qT = qT.transpose(0, 3, 1, 5, 4, 2).reshape(bsz, B_GROUPS, nq, HEAD_DIM, L)
    vcT = vc.transpose(0, 1, 3, 2)

    def key_blocks(a):
        return a.astype(BF16).reshape(bsz, n_slc, SLC_BLOCK, B_GROUPS, HEAD_DIM).transpose(0, 3, 1, 2, 4)

    def val_blocks(a):
        return a.astype(BF16).reshape(bsz, n_slc, SLC_BLOCK, B_GROUPS, HEAD_DIM).transpose(0, 3, 1, 4, 2)

    gT = gate_logits.reshape(bsz, nq, TQ, B_GROUPS, B_HPG, 3).transpose(0, 3, 1, 5, 4, 2).reshape(bsz, B_GROUPS, nq, 3, L)
    kv_spec = pl.BlockSpec((1, 1, n_slc, SLC_BLOCK, SLC_BLOCK), lambda b, g, i: (b, g, 0, 0, 0))
    out = pl.pallas_call(
        functools.partial(_nsa_kernel, n_cmp=n_cmp, n_sel=n_sel),
        grid=(bsz, B_GROUPS, nq),
        in_specs=[pl.BlockSpec((1, 1, 1, HEAD_DIM, L), lambda b, g, i: (b, g, i, 0, 0)),
                  pl.BlockSpec((1, 1, N_CMP_PAD, HEAD_DIM), lambda b, g, i: (b, g, 0, 0)),
                  pl.BlockSpec((1, 1, HEAD_DIM, N_CMP_PAD), lambda b, g, i: (b, g, 0, 0)),
                  pl.BlockSpec((1, 1, N_CMP_PAD, L), lambda b, g, i: (g, i, 0, 0)),
                  pl.BlockSpec((n_slc, N_CMP_PAD), lambda b, g, i: (0, 0)),
                  kv_spec, kv_spec, kv_spec, kv_spec,
                  pl.BlockSpec((1, NSA_REL, SLC_BLOCK, L), lambda b, g, i: (g, 0, 0, 0)),
                  pl.BlockSpec((1, 1, 1, 3, L), lambda b, g, i: (b, g, i, 0, 0))],
        out_specs=pl.BlockSpec((1, 1, 1, HEAD_DIM, L), lambda b, g, i: (b, g, i, 0, 0)),
        out_shape=jax.ShapeDtypeStruct((bsz, B_GROUPS, nq, HEAD_DIM, L), BF16),
        scratch_shapes=[pltpu.VMEM((n_slc, 1, L), F32)],
        compiler_params=_cparams(("arbitrary", "arbitrary", "arbitrary")),
        name="nsa_attention",
    )(qT, kc, vcT, biasc, nsa_overlap(seq), key_blocks(k_slc), val_blocks(v_slc),
      key_blocks(k_win), val_blocks(v_win), toe, gT)
    out = out.reshape(bsz, B_GROUPS, nq, HEAD_DIM, B_HPG, TQ).transpose(0, 2, 5, 1, 4, 3)
    return out.reshape(bsz * seq, B_HEADS * HEAD_DIM)


MOBA_T = MOBA_BLOCK


def _moba_kernel(qT_ref, k_ref, vT_ref, bias_ref, o_ref, kmean_ref, selb_ref, *, n_sel):
    qi = pl.program_id(2)
    T = MOBA_T
    n_blk = k_ref.shape[2]

    @pl.when(qi == 0)
    def _():
        for j in range(n_blk):
            kmean_ref[j:j + 1, :] = jnp.mean(k_ref[0, 0, j].astype(F32), axis=0, keepdims=True)

    qT = qT_ref[0, 0]
    gate = jnp.dot(kmean_ref[...], qT.astype(F32), preferred_element_type=F32,
                   precision=lax.Precision.HIGHEST)
    j_idx = lax.broadcasted_iota(I32, (n_blk, T), 0)
    gv = jnp.where(j_idx < qi, gate, -jnp.inf)
    rank = jnp.zeros((n_blk, T), I32)
    for jp in range(n_blk):
        row = gv[jp:jp + 1, :]
        rank = rank + jnp.where(row > gv, 1, jnp.where((row == gv) & (jp < j_idx), 1, 0))
    selb = jnp.where(j_idx < qi, jnp.where(rank < n_sel, 0.0, NEG), jnp.where(j_idx == qi, 0.0, NEG)).astype(F32)
    for j in range(n_blk):
        selb_ref[j] = selb[j:j + 1, :]

    def body(kb, carry):
        rel = jnp.minimum(qi - kb, 2)
        s = jnp.dot(k_ref[0, 0, kb], qT, preferred_element_type=F32) + bias_ref[rel, 0] + selb_ref[kb]
        return _flash_update(s, vT_ref[0, 0, kb], *carry)

    state = (jnp.full((1, T), NEG, F32), jnp.zeros((1, T), F32), jnp.zeros((HEAD_DIM, T), F32))
    o_ref[0, 0] = _flash_finish(*lax.fori_loop(0, qi + 1, body, state)).astype(o_ref.dtype)


def moba_bias_tiles(rel_bias):
    assert MOBA_T + 1 >= T5_FAR
    t0 = bias_tiles(rel_bias, [0], C_HEADS, 0, MOBA_T, MOBA_T, -1, 1, True)
    t12 = bias_tiles(rel_bias, [MOBA_T, 4 * MOBA_T], C_HEADS, 0, MOBA_T, MOBA_T, -1, 1, False)
    return jnp.concatenate([t0, t12], axis=0)


def moba_attention(qkv, bias3, bsz, seq):
    T = MOBA_T
    assert seq % T == 0
    n_blk = seq // T
    n_sel = min(MOBA_TOPK, n_blk - 1)
    hd = C_HEADS * HEAD_DIM
    q = (qkv[:, :hd] * HEAD_DIM ** -0.5).astype(BF16).reshape(bsz, seq, C_HEADS, HEAD_DIM)
    k = qkv[:, hd:2 * hd].reshape(bsz, n_blk, T, C_HEADS, HEAD_DIM)
    v = qkv[:, 2 * hd:].reshape(bsz, n_blk, T, C_HEADS, HEAD_DIM)
    qT = q.transpose(0, 2, 3, 1)
    kb = k.transpose(0, 3, 1, 2, 4)
    vT = v.transpose(0, 3, 1, 4, 2)
    out = pl.pallas_call(
        functools.partial(_moba_kernel, n_sel=n_sel),
        grid=(bsz, C_HEADS, n_blk),
        in_specs=[pl.BlockSpec((1, 1, HEAD_DIM, T), lambda b, h, i: (b, h, 0, i)),
                  pl.BlockSpec((1, 1, n_blk, T, HEAD_DIM), lambda b, h, i: (b, h, 0, 0, 0)),
                  pl.BlockSpec((1, 1, n_blk, HEAD_DIM, T), lambda b, h, i: (b, h, 0, 0, 0)),
                  pl.BlockSpec((3, 1, T, T), lambda b, h, i: (0, h, 0, 0))],
        out_specs=pl.BlockSpec((1, 1, HEAD_DIM, T), lambda b, h, i: (b, h, 0, i)),
        out_shape=jax.ShapeDtypeStruct((bsz, C_HEADS, HEAD_DIM, seq), BF16),
        scratch_shapes=[pltpu.VMEM((n_blk, HEAD_DIM), F32), pltpu.VMEM((n_blk, 1, T), F32)],
        compiler_params=_cparams(("arbitrary", "arbitrary", "arbitrary")),
        name="moba_attention",
    )(qT, kb, vT, bias3)
    return out.transpose(0, 3, 1, 2).reshape(bsz * seq, hd)


def _router_kernel(h_ref, w_ref, o_ref):
    o_ref[...] = jnp.dot(h_ref[...], w_ref[...], preferred_element_type=F32, precision=lax.Precision.HIGHEST)


def router_logits(h, router, tm=1024):
    m, d = h.shape
    lanes = 128
    w = jnp.pad(router, ((0, 0), (0, lanes - N_EXPERTS)))
    out = pl.pallas_call(
        _router_kernel,
        grid=(m // tm,),
        in_specs=[pl.BlockSpec((tm, d), lambda i: (i, 0)), pl.BlockSpec((d, lanes), lambda i: (0, 0))],
        out_specs=pl.BlockSpec((tm, lanes), lambda i: (i, 0)),
        out_shape=jax.ShapeDtypeStruct((m, lanes), F32),
        compiler_params=_cparams(("arbitrary",)),
        name="router_logits",
    )(h, w)
    return out[:, :N_EXPERTS]


def _add_ln_kernel(h_ref, y_ref, g_ref, b_ref, o_ref):
    o_ref[...] = _layer_norm_rows(ALPHA * h_ref[...] + y_ref[...].astype(F32), g_ref[...], b_ref[...])


def add_ln(h, y, g, b, tm=512):
    m, d = h.shape
    row = pl.BlockSpec((tm, d), lambda i: (i, 0))
    vec = pl.BlockSpec((1, d), lambda i: (0, 0))
    return pl.pallas_call(
        _add_ln_kernel, grid=(m // tm,), in_specs=[row, row, vec, vec], out_specs=row,
        out_shape=jax.ShapeDtypeStruct((m, d), F32),
        compiler_params=_cparams(("arbitrary",)), name="add_ln",
    )(h, y, g.reshape(1, d), b.reshape(1, d))


def moe_dispatch_plan(logits):
    n_tok = logits.shape[0]
    top_val, top_e = lax.top_k(logits, TOP_K)
    gate = jax.nn.softmax(top_val, axis=-1)
    e_flat = top_e.reshape(-1)
    onehot = (e_flat[:, None] == jnp.arange(N_EXPERTS, dtype=e_flat.dtype)[None, :]).astype(I32)
    rank = jnp.take_along_axis(jnp.cumsum(onehot, axis=0) - onehot, e_flat[:, None], axis=1)[:, 0]
    counts = jnp.sum(onehot, axis=0)
    padded = (counts + EXPERT_ROWS - 1) // EXPERT_ROWS * EXPERT_ROWS
    pend = jnp.cumsum(padded)
    pstart = pend - padded
    dest = pstart[e_flat] + rank
    n_assign = n_tok * TOP_K
    n_rows = -(-n_assign // EXPERT_ROWS) * EXPERT_ROWS + N_EXPERTS * EXPERT_ROWS
    n_groups = n_rows // EXPERT_ROWS
    tok_flat = jnp.repeat(jnp.arange(n_tok, dtype=I32), TOP_K)
    row_tok = jnp.zeros((n_rows,), I32).at[dest].set(tok_flat)
    grp_e = jnp.minimum(jnp.searchsorted(pend, jnp.arange(n_groups, dtype=I32) * EXPERT_ROWS, side='right'),
                        N_EXPERTS - 1).astype(I32)
    return gate, dest.astype(I32), row_tok, grp_e


def kernel(x, rel_bias, e_w_in, e_q_norm, e_kv_norm, e_w_uq, e_w_uk, e_w_uv, e_w_qidx, e_pos_k, e_pos_v, e_ck1, e_ck2, e_cv1, e_cv2, e_w_out, e_ln1_g, e_ln1_b, e_ffn_w1, e_ffn_w3, e_ffn_w2, e_ln2_g, e_ln2_b, o_w_in, o_w_out, o_ln1_g, o_ln1_b, o_router, o_moe_w1, o_moe_w3, o_moe_w2, o_ln2_g, o_ln2_b):
    bsz, seq, d = x.shape
    m = bsz * seq
    xf = x.reshape(m, d)
    dsa_bias = dsa_bias_tiles(rel_bias)
    nsa_bc, nsa_toe = nsa_bias_inputs(rel_bias, seq)
    moba_bias = moba_bias_tiles(rel_bias)
    gd = B_GROUPS * HEAD_DIM
    for layer in range(DEPTH):
        i = layer // 2
        if layer % 2 == 0:
            w_in = e_w_in[i]
            a_cols = A_Q_RANK + A_KV_RANK + IDX_DIM + IDX_HEADS
            b_cols = B_HEADS * HEAD_DIM + 6 * gd
            w_a = jnp.concatenate([w_in[:, :a_cols], w_in[:, a_cols + b_cols:]], axis=1)
            w_a = jnp.pad(w_a, ((0, 0), (0, 512 - w_a.shape[1]))).astype(BF16)
            w_b = w_in[:, a_cols:a_cols + b_cols].astype(BF16)
            ya = matmul(xf, w_a, F32)
            yb = matmul(xf, w_b, BF16)
            c_q, c_kv = ya[:, :A_Q_RANK], ya[:, A_Q_RANK:A_Q_RANK + A_KV_RANK]
            k_idx = ya[:, A_Q_RANK + A_KV_RANK:A_Q_RANK + A_KV_RANK + IDX_DIM]
            w_idx = ya[:, a_cols - IDX_HEADS:a_cols] * IDX_HEADS ** -0.5
            gate_logits = ya[:, a_cols:a_cols + 3 * B_HEADS]
            q_b = yb[:, :B_HEADS * HEAD_DIM]
            k_c, v_c, k_sl, v_sl, k_w, v_w = [yb[:, B_HEADS * HEAD_DIM + j * gd:B_HEADS * HEAD_DIM + (j + 1) * gd]
                                              for j in range(6)]
            q_lat, q_idx, ckvn = dsa_prep(c_q, c_kv, e_q_norm[i], e_kv_norm[i], e_w_uq[i], e_w_uk[i], e_w_qidx[i])
            o_aT = dsa_attention(q_lat, q_idx, ckvn, k_idx, w_idx, e_w_uv[i], dsa_bias, bsz, seq)
            o_a = o_aT.transpose(0, 2, 1).reshape(m, A_HEADS * HEAD_DIM)
            kc = nsa_compress(k_c, e_pos_k[i], e_ck1[i], e_ck2[i], bsz, seq)
            vc = nsa_compress(v_c, e_pos_v[i], e_cv1[i], e_cv2[i], bsz, seq)
            o_b = nsa_attention(q_b, kc, vc, k_sl, v_sl, k_w, v_w, gate_logits, nsa_bc, nsa_toe, bsz, seq)
            h = proj_residual_ln(jnp.concatenate([o_a, o_b], axis=1), e_w_out[i].astype(BF16), xf,
                                 e_ln1_g[i], e_ln1_b[i])
            tm = 512
            xf = swiglu_ffn(h, jnp.zeros((m // tm,), I32), e_ffn_w1[i][None].astype(BF16),
                            e_ffn_w3[i][None].astype(BF16), e_ffn_w2[i][None].astype(BF16),
                            e_ln2_g[i], e_ln2_b[i], with_ln=True, out_dtype=F32, tm=tm, ff_chunk=1408)
        else:
            qkv = matmul(xf, o_w_in[i].astype(BF16), BF16)
            o_c = moba_attention(qkv, moba_bias, bsz, seq)
            h = proj_residual_ln(o_c, o_w_out[i].astype(BF16), xf, o_ln1_g[i], o_ln1_b[i])
            gate, dest, row_tok, grp_e = moe_dispatch_plan(router_logits(h, o_router[i]))
            x_rows = h.astype(BF16)[row_tok]
            y_rows = swiglu_ffn(x_rows, grp_e, o_moe_w1[i].astype(BF16), o_moe_w3[i].astype(BF16),
                                o_moe_w2[i].astype(BF16), o_ln2_g[i], o_ln2_b[i],
                                with_ln=False, out_dtype=F32, tm=EXPERT_ROWS, ff_chunk=512)
            y = jnp.sum((y_rows[dest] * gate.reshape(-1, 1)).reshape(m, TOP_K, d), axis=1)
            xf = add_ln(h, y, o_ln2_g[i], o_ln2_b[i])
    return xf.reshape(bsz, seq, d)
```

```python
import functools
import math

import numpy as np
import jax
import jax.numpy as jnp
from jax import lax
from jax.experimental import pallas as pl
from jax.experimental.pallas import tpu as pltpu

F32 = jnp.float32
BF16 = jnp.bfloat16
I32 = jnp.int32

HEAD_DIM = 64
NUM_BUCKETS = 32
MAX_DISTANCE = 128
N_BIAS_HEADS = 16
A_HEADS = 8
A_Q_RANK = 256
A_KV_RANK = 128
IDX_HEADS = 16
IDX_DIM = 64
DSA_TOPK = 256
B_HEADS = 8
B_GROUPS = 2
B_HPG = B_HEADS // B_GROUPS
CMP_LEN = 32
CMP_STRIDE = 16
SLC_BLOCK = 64
SLC_TOPN = 16
WINDOW = 512
C_HEADS = 16
MOBA_BLOCK = 256
MOBA_TOPK = 3
N_EXPERTS = 8
TOP_K = 2
EXPERT_ROWS = 256
DEPTH = 2
ALPHA = (2 * DEPTH) ** 0.25

NEG = -1e30
NEG_HALF = -5e29
INT_MIN = -2 ** 31
VMEM_LIMIT = 56 * 1024 * 1024


def _t5_thresholds():
    def bucket(n):
        if n < NUM_BUCKETS // 2:
            return n
        v = np.log(np.float32(n) / np.float32(NUM_BUCKETS // 2)) / np.float32(math.log(MAX_DISTANCE / (NUM_BUCKETS // 2)))
        return min(NUM_BUCKETS // 2 + int(np.float32(v) * (NUM_BUCKETS - NUM_BUCKETS // 2)), NUM_BUCKETS - 1)
    b = [bucket(i) for i in range(4 * MAX_DISTANCE)]
    return [0] + [min(i for i in range(len(b)) if b[i] >= k) for k in range(1, NUM_BUCKETS)]


T5_THR = _t5_thresholds()
T5_FAR = T5_THR[-1]


def _cparams(sem):
    return pltpu.CompilerParams(dimension_semantics=sem, vmem_limit_bytes=VMEM_LIMIT)


def _bias_kernel(tab_ref, off_ref, o_ref, *, c_row, c_col, h0, causal_neg, window):
    v = pl.program_id(0)
    h = pl.program_id(1) + h0
    shape = o_ref.shape[2:]
    dist = (c_col * lax.broadcasted_iota(I32, shape, 1) + c_row * lax.broadcasted_iota(I32, shape, 0) + off_ref[v])
    n = jnp.maximum(dist, 0)
    acc = jnp.full(shape, tab_ref[h], F32)
    for k in range(1, NUM_BUCKETS):
        acc = jnp.where(n >= T5_THR[k], tab_ref[k * N_BIAS_HEADS + h], acc)
    if causal_neg:
        acc = jnp.where(dist >= 0, acc, NEG)
    if window:
        acc = jnp.where(dist < window, acc, NEG)
    o_ref[0, 0] = acc


def bias_tiles(rel_bias, offs, n_heads, h0, rows, cols, c_row, c_col, causal_neg, window=0):
    offs = jnp.asarray(offs, I32)
    nv = offs.shape[0]
    return pl.pallas_call(
        functools.partial(_bias_kernel, c_row=c_row, c_col=c_col, h0=h0, causal_neg=causal_neg, window=window),
        grid=(nv, n_heads),
        in_specs=[pl.BlockSpec(memory_space=pltpu.SMEM), pl.BlockSpec(memory_space=pltpu.SMEM)],
        out_specs=pl.BlockSpec((1, 1, rows, cols), lambda v, h: (v, h, 0, 0)),
        out_shape=jax.ShapeDtypeStruct((nv, n_heads, rows, cols), F32),
        compiler_params=_cparams(("arbitrary", "arbitrary")),
        name="t5_bias_tiles",
    )(rel_bias.reshape(-1), offs)


def _mm_kernel(x_ref, w_ref, o_ref):
    o_ref[...] = jnp.dot(x_ref[...].astype(BF16), w_ref[...].astype(BF16),
                         preferred_element_type=F32).astype(o_ref.dtype)


def matmul(x, w, out_dtype, tm=512):
    m, k = x.shape
    n = w.shape[1]
    tm = min(tm, m)
    return pl.pallas_call(
        _mm_kernel,
        grid=(m // tm,),
        in_specs=[pl.BlockSpec((tm, k), lambda i: (i, 0)), pl.BlockSpec((k, n), lambda i: (0, 0))],
        out_specs=pl.BlockSpec((tm, n), lambda i: (i, 0)),
        out_shape=jax.ShapeDtypeStruct((m, n), out_dtype),
        compiler_params=_cparams(("arbitrary",)),
        name="matmul",
    )(x, w)


def _layer_norm_rows(z, g, b):
    mu = jnp.mean(z, axis=-1, keepdims=True)
    zc = z - mu
    var = jnp.mean(zc * zc, axis=-1, keepdims=True)
    return zc * lax.rsqrt(var + 1e-5) * g + b


def _proj_ln_kernel(a_ref, w_ref, x_ref, g_ref, b_ref, o_ref):
    mix = jnp.dot(a_ref[...], w_ref[...], preferred_element_type=F32)
    o_ref[...] = _layer_norm_rows(ALPHA * x_ref[...] + mix, g_ref[...], b_ref[...])


def proj_residual_ln(a, w, x, g, b, tm=512):
    m, k = a.shape
    d = w.shape[1]
    tm = min(tm, m)
    return pl.pallas_call(
        _proj_ln_kernel,
        grid=(m // tm,),
        in_specs=[pl.BlockSpec((tm, k), lambda i: (i, 0)), pl.BlockSpec((k, d), lambda i: (0, 0)),
                  pl.BlockSpec((tm, d), lambda i: (i, 0)),
                  pl.BlockSpec((1, d), lambda i: (0, 0)), pl.BlockSpec((1, d), lambda i: (0, 0))],
        out_specs=pl.BlockSpec((tm, d), lambda i: (i, 0)),
        out_shape=jax.ShapeDtypeStruct((m, d), F32),
        compiler_params=_cparams(("arbitrary",)),
        name="proj_residual_ln",
    )(a, w, x, g.reshape(1, d), b.reshape(1, d))


def _ffn_kernel(ge_ref, x_ref, w1_ref, w3_ref, w2_ref, g_ref, b_ref, o_ref, *, ff_chunk, with_ln):
    del ge_ref
    x = x_ref[...]
    xb = x.astype(BF16)
    d_ff = w1_ref.shape[2]
    acc = jnp.zeros((x.shape[0], w2_ref.shape[2]), F32)
    for c in range(0, d_ff, ff_chunk):
        a = jnp.dot(xb, w1_ref[0, :, c:c + ff_chunk], preferred_element_type=F32)
        u = jnp.dot(xb, w3_ref[0, :, c:c + ff_chunk], preferred_element_type=F32)
        hid = (a * jax.nn.sigmoid(a) * u).astype(BF16)
        acc = acc + jnp.dot(hid, w2_ref[0, c:c + ff_chunk, :], preferred_element_type=F32)
    if with_ln:
        o_ref[...] = _layer_norm_rows(ALPHA * x.astype(F32) + acc, g_ref[...], b_ref[...]).astype(o_ref.dtype)
    else:
        o_ref[...] = acc.astype(o_ref.dtype)


def swiglu_ffn(x_rows, grp_e, w1, w3, w2, ln_g, ln_b, *, with_ln, out_dtype, tm, ff_chunk):
    m, d = x_rows.shape
    d_ff = w1.shape[2]
    once = pl.Buffered(1)
    grid_spec = pltpu.PrefetchScalarGridSpec(
        num_scalar_prefetch=1,
        grid=(m // tm,),
        in_specs=[pl.BlockSpec((tm, d), lambda i, ge: (i, 0)),
                  pl.BlockSpec((1, d, d_ff), lambda i, ge: (ge[i], 0, 0), pipeline_mode=once),
                  pl.BlockSpec((1, d, d_ff), lambda i, ge: (ge[i], 0, 0), pipeline_mode=once),
                  pl.BlockSpec((1, d_ff, d), lambda i, ge: (ge[i], 0, 0), pipeline_mode=once),
                  pl.BlockSpec((1, d), lambda i, ge: (0, 0)), pl.BlockSpec((1, d), lambda i, ge: (0, 0))],
        out_specs=pl.BlockSpec((tm, d), lambda i, ge: (i, 0)),
    )
    return pl.pallas_call(
        functools.partial(_ffn_kernel, ff_chunk=ff_chunk, with_ln=with_ln),
        grid_spec=grid_spec,
        out_shape=jax.ShapeDtypeStruct((m, d), out_dtype),
        compiler_params=_cparams(("arbitrary",)),
        name="swiglu_ffn",
    )(grp_e, x_rows, w1, w3, w2, ln_g.reshape(1, d), ln_b.reshape(1, d))


def _flash_probs(s, m, l):
    m_new = jnp.maximum(m, jnp.max(s, axis=0, keepdims=True))
    alpha = jnp.exp(m - m_new)
    p = jnp.exp(s - m_new)
    l_new = alpha * l + jnp.sum(p, axis=0, keepdims=True)
    return m_new, alpha, l_new, p.astype(BF16)


def _flash_update(s, vT, m, l, acc):
    m_new, alpha, l_new, p = _flash_probs(s, m, l)
    return m_new, l_new, alpha * acc + jnp.dot(vT, p, preferred_element_type=F32)


def _flash_finish(m, l, acc):
    return jnp.where(m > NEG_HALF, acc / l, 0.0)


def _rms_rows(x, g):
    return x * lax.rsqrt(jnp.mean(x * x, axis=-1, keepdims=True) + 1e-6) * g


def _dsa_prep_kernel(cq_ref, ckv_ref, qn_ref, kvn_ref, wuq_ref, wukt_ref, wqi_ref,
                     qlat_ref, qidx_ref, ckvn_ref):
    cqn = _rms_rows(cq_ref[...], qn_ref[...]).astype(BF16)
    ckvn_ref[...] = _rms_rows(ckv_ref[...], kvn_ref[...]).astype(BF16)
    q = jnp.dot(cqn, wuq_ref[...], preferred_element_type=F32).astype(BF16)
    for h in range(A_HEADS):
        ql = jnp.dot(q[:, h * HEAD_DIM:(h + 1) * HEAD_DIM], wukt_ref[h], preferred_element_type=F32)
        qlat_ref[:, h * A_KV_RANK:(h + 1) * A_KV_RANK] = (ql * HEAD_DIM ** -0.5).astype(BF16)
    qidx_ref[...] = jnp.dot(cqn, wqi_ref[...], preferred_element_type=F32).astype(BF16)


def dsa_prep(c_q, c_kv, q_norm, kv_norm, w_uq, w_uk, w_qidx, tm=512):
    m = c_q.shape[0]
    tm = min(tm, m)
    wuq = w_uq.reshape(A_Q_RANK, A_HEADS * HEAD_DIM).astype(BF16)
    wukt = jnp.transpose(w_uk, (1, 2, 0)).astype(BF16)
    wqi = w_qidx.reshape(A_Q_RANK, IDX_HEADS * IDX_DIM).astype(BF16)
    full = lambda shape: pl.BlockSpec(shape, lambda i: (0,) * len(shape))
    rows = lambda n: pl.BlockSpec((tm, n), lambda i: (i, 0))
    return pl.pallas_call(
        _dsa_prep_kernel,
        grid=(m // tm,),
        in_specs=[rows(A_Q_RANK), rows(A_KV_RANK), full((1, A_Q_RANK)), full((1, A_KV_RANK)),
                  full(wuq.shape), full(wukt.shape), full(wqi.shape)],
        out_specs=[rows(A_HEADS * A_KV_RANK), rows(IDX_HEADS * IDX_DIM), rows(A_KV_RANK)],
        out_shape=[jax.ShapeDtypeStruct((m, A_HEADS * A_KV_RANK), BF16),
                   jax.ShapeDtypeStruct((m, IDX_HEADS * IDX_DIM), BF16),
                   jax.ShapeDtypeStruct((m, A_KV_RANK), BF16)],
        compiler_params=_cparams(("arbitrary",)),
        name="dsa_prep",
    )(c_q, c_kv, q_norm.reshape(1, -1), kv_norm.reshape(1, -1), wuq, wukt, wqi)


DSA_T = 256
SUB = 128


def _dsa_kernel(qidx_ref, wT_ref, qlat_ref, kidx_ref, ckv_ref, ckvT_ref, bias_ref, wuvt_ref, o_ref,
                key_ref, selb_ref, *state_refs, n_keep):
    m_refs, l_refs, acc_refs = (state_refs[0:A_HEADS], state_refs[A_HEADS:2 * A_HEADS], state_refs[2 * A_HEADS:])
    qi = pl.program_id(1)
    nkb = qi + 1
    T = DSA_T

    def score_block(kb, carry):
        for sub in range(T // SUB):
            k = kidx_ref[0, kb, sub * SUB:(sub + 1) * SUB, :]
            acc = jnp.zeros((SUB, T), F32)
            for h in range(IDX_HEADS):
                d = jnp.dot(k, qidx_ref[0, h], preferred_element_type=F32)
                acc = acc + jnp.maximum(d, 0.0) * wT_ref[0, h:h + 1, :]
            bits = lax.bitcast_convert_type(acc, I32)
            key = bits ^ (lax.shift_right_arithmetic(bits, 31) & 0x7FFFFFFF)
            s_pos = kb * T + sub * SUB + lax.broadcasted_iota(I32, (SUB, T), 0)
            t_pos = qi * T + lax.broadcasted_iota(I32, (SUB, T), 1)
            key = jnp.where(s_pos <= t_pos, key, INT_MIN)
            key_ref[pl.ds(pl.multiple_of(kb * T + sub * SUB, SUB), SUB), :] = key
        return carry

    lax.fori_loop(0, nkb, score_block, 0)

    n_chunks = nkb * (T // SUB)

    def count_ge(cand):
        def body(i, cnt):
            blk = key_ref[pl.ds(pl.multiple_of(i * SUB, SUB), SUB), :]
            ge = jnp.where(blk >= cand, 1, 0).astype(I32)
            return cnt + jnp.sum(ge.reshape(SUB // 8, 8, T), axis=0)
        cnt = lax.fori_loop(0, n_chunks, body, jnp.zeros((8, T), I32))
        return jnp.sum(cnt, axis=0, keepdims=True)

    def bit_step(i, u):
        cand_u = u | lax.shift_left(jnp.int32(1), 31 - i)
        cnt = count_ge(cand_u ^ INT_MIN)
        return jnp.where(cnt >= n_keep, cand_u, u)

    u = lax.fori_loop(0, 32, bit_step, jnp.zeros((1, T), I32))
    thr = jnp.maximum(u ^ INT_MIN, INT_MIN + 1)

    for h in range(A_HEADS):
        m_refs[h][...] = jnp.full(m_refs[h].shape, NEG, F32)
        l_refs[h][...] = jnp.zeros(l_refs[h].shape, F32)
        acc_refs[h][...] = jnp.zeros(acc_refs[h].shape, F32)

    def attend(kb, carry):
        selb_ref[...] = jnp.where(key_ref[pl.ds(pl.multiple_of(kb * T, T), T), :] >= thr, 0.0, NEG)
        ckv = ckv_ref[0, kb]
        ckvT = ckvT_ref[0, kb]
        rel = jnp.minimum(qi - kb, 2)
        s_all = [jnp.dot(ckv, qlat_ref[0, h], preferred_element_type=F32) + bias_ref[rel, h] + selb_ref[...]
                 for h in range(A_HEADS)]
        p_all = []
        for h in range(A_HEADS):
            m_new, alpha, l_new, p = _flash_probs(s_all[h], m_refs[h][...], l_refs[h][...])
            m_refs[h][...] = m_new
            l_refs[h][...] = l_new
            p_all.append((alpha, p))
        for h in range(A_HEADS):
            alpha, p = p_all[h]
            acc_refs[h][...] = alpha * acc_refs[h][...] + jnp.dot(ckvT, p, preferred_element_type=F32)
        return carry

    lax.fori_loop(0, nkb, attend, 0)

    for h in range(A_HEADS):
        o_lat = _flash_finish(m_refs[h][...], l_refs[h][...], acc_refs[h][...]).astype(BF16)
        o_ref[0, h * HEAD_DIM:(h + 1) * HEAD_DIM, :] = jnp.dot(
            wuvt_ref[h], o_lat, preferred_element_type=F32).astype(o_ref.dtype)


def dsa_attention(q_lat, q_idx, ckvn, k_idx, w_idx, w_uv, bias3, bsz, seq):
    T = DSA_T
    nq = seq // T
    n_keep = min(DSA_TOPK, seq // 4)
    qidxT = q_idx.reshape(bsz, seq, IDX_HEADS, IDX_DIM).transpose(0, 2, 3, 1)
    qlatT = q_lat.reshape(bsz, seq, A_HEADS, A_KV_RANK).transpose(0, 2, 3, 1)
    wT = w_idx.reshape(bsz, seq, IDX_HEADS).transpose(0, 2, 1)
    kidx = k_idx.astype(BF16).reshape(bsz, nq, T, IDX_DIM)
    ckv = ckvn.reshape(bsz, nq, T, A_KV_RANK)
    ckvT = ckv.transpose(0, 1, 3, 2)
    wuvt = jnp.transpose(w_uv, (1, 2, 0)).astype(BF16)
    return pl.pallas_call(
        functools.partial(_dsa_kernel, n_keep=n_keep),
        grid=(bsz, nq),
        in_specs=[pl.BlockSpec((1, IDX_HEADS, IDX_DIM, T), lambda b, i: (b, 0, 0, i)),
                  pl.BlockSpec((1, IDX_HEADS, T), lambda b, i: (b, 0, i)),
                  pl.BlockSpec((1, A_HEADS, A_KV_RANK, T), lambda b, i: (b, 0, 0, i)),
                  pl.BlockSpec((1, nq, T, IDX_DIM), lambda b, i: (b, 0, 0, 0)),
                  pl.BlockSpec((1, nq, T, A_KV_RANK), lambda b, i: (b, 0, 0, 0)),
                  pl.BlockSpec((1, nq, A_KV_RANK, T), lambda b, i: (b, 0, 0, 0)),
                  pl.BlockSpec((3, A_HEADS, T, T), lambda b, i: (0, 0, 0, 0)),
                  pl.BlockSpec((A_HEADS, HEAD_DIM, A_KV_RANK), lambda b, i: (0, 0, 0))],
        out_specs=pl.BlockSpec((1, A_HEADS * HEAD_DIM, T), lambda b, i: (b, 0, i)),
        out_shape=jax.ShapeDtypeStruct((bsz, A_HEADS * HEAD_DIM, seq), BF16),
        scratch_shapes=([pltpu.VMEM((seq, T), I32), pltpu.VMEM((T, T), F32)] + [pltpu.VMEM((1, T), F32)] * (2 * A_HEADS)
                        + [pltpu.VMEM((A_KV_RANK, T), F32)] * A_HEADS),
        compiler_params=_cparams(("arbitrary", "arbitrary")),
        name="dsa_attention",
    )(qidxT, wT, qlatT, kidx, ckv, ckvT, bias3, wuvt)


def dsa_bias_tiles(rel_bias):
    assert DSA_T + 1 >= T5_FAR
    return bias_tiles(rel_bias, [0, DSA_T, 4 * DSA_T], A_HEADS, 0, DSA_T, DSA_T, -1, 1, False)


N_CMP_PAD = 256


def _compress_kernel(blk_ref, pos_ref, w1_ref, w2_ref, o_ref):
    x = (blk_ref[0].astype(F32) + pos_ref[...]).astype(BF16)
    hid = jax.nn.gelu(jnp.dot(x, w1_ref[...], preferred_element_type=F32))
    o_ref[0] = jnp.dot(hid.astype(BF16), w2_ref[...], preferred_element_type=F32).astype(o_ref.dtype)


def nsa_compress(a, pos, w1, w2, bsz, seq):
    n_chunk = seq // CMP_STRIDE
    assert CMP_LEN == 2 * CMP_STRIDE and n_chunk <= N_CMP_PAD
    width = CMP_STRIDE * HEAD_DIM
    chunks = a.reshape(bsz, n_chunk, CMP_STRIDE, B_GROUPS, HEAD_DIM).transpose(0, 3, 1, 2, 4)
    chunks = chunks.reshape(bsz * B_GROUPS, n_chunk, width)
    blocks = jnp.concatenate([chunks[:, :-1], chunks[:, 1:]], axis=-1)
    blocks = jnp.pad(blocks, ((0, 0), (0, N_CMP_PAD - (n_chunk - 1)), (0, 0)))
    out = pl.pallas_call(
        _compress_kernel,
        grid=(bsz * B_GROUPS,),
        in_specs=[pl.BlockSpec((1, N_CMP_PAD, 2 * width), lambda i: (i, 0, 0)),
                  pl.BlockSpec((1, 2 * width), lambda i: (0, 0)),
                  pl.BlockSpec((2 * width, HEAD_DIM), lambda i: (0, 0)),
                  pl.BlockSpec((HEAD_DIM, HEAD_DIM), lambda i: (0, 0))],
        out_specs=pl.BlockSpec((1, N_CMP_PAD, HEAD_DIM), lambda i: (i, 0, 0)),
        out_shape=jax.ShapeDtypeStruct((bsz * B_GROUPS, N_CMP_PAD, HEAD_DIM), BF16),
        compiler_params=_cparams(("arbitrary",)),
        name="nsa_compress",
    )(blocks, pos.reshape(1, 2 * width), w1.reshape(2 * width, HEAD_DIM).astype(BF16), w2.astype(BF16))
    return out.reshape(bsz, B_GROUPS, N_CMP_PAD, HEAD_DIM)


NSA_TQ = 128
NSA_L = B_HPG * NSA_TQ
NSA_KT = 128
NSA_SLC_REL = 3
NSA_WIN_REL = 5


def _flash_step(s_all, vT_all, states):
    probs = []
    for s, (m_ref, l_ref, _) in zip(s_all, states):
        m_new, alpha, l_new, p = _flash_probs(s, m_ref[...], l_ref[...])
        m_ref[...] = m_new
        l_ref[...] = l_new
        probs.append((alpha, p))
    for (alpha, p), vT, (_, _, acc_ref) in zip(probs, vT_all, states):
        acc_ref[...] = alpha * acc_ref[...] + jnp.dot(vT, p, preferred_element_type=F32)


def _nsa_kernel(qT_ref, kc_ref, vcT_ref, biasc_ref, ovl_ref, ks_ref, vsT_ref, kw_ref, vwT_ref,
                toes_ref, toew_ref, gate_ref, o_ref, selb_ref, *st, n_cmp, n_sel, n_slc):
    qi = pl.program_id(1)
    TQ, L = NSA_TQ, NSA_L
    q0 = qi * TQ
    qTs = [qT_ref[0, g, 0] for g in range(B_GROUPS)]
    t_lane = q0 + (lax.broadcasted_iota(I32, (1, L), 1) & (TQ - 1))

    o_cs = []
    for g in range(B_GROUPS):
        s = jnp.dot(kc_ref[0, g], qTs[g], preferred_element_type=F32) + biasc_ref[g, 0]
        i_idx = lax.broadcasted_iota(I32, (N_CMP_PAD, L), 0)
        valid = jnp.where(i_idx < n_cmp, i_idx * CMP_STRIDE + (CMP_LEN - 1), 2 ** 30) <= t_lane
        s = jnp.where(valid, s, NEG)
        m = jnp.max(s, axis=0, keepdims=True)
        p = jnp.where(valid, jnp.exp(s - m), 0.0)
        l = jnp.sum(p, axis=0, keepdims=True)
        p_c = p / jnp.where(l > 0, l, 1.0)
        o_c = jnp.dot(vcT_ref[0, g], p_c.astype(BF16), preferred_element_type=F32)

        psum = p_c[:, 0:TQ]
        for n in range(1, B_HPG):
            psum = psum + p_c[:, n * TQ:(n + 1) * TQ]
        sc = jnp.dot(ovl_ref[...], psum, preferred_element_type=F32, precision=lax.Precision.HIGHEST)
        j_idx = lax.broadcasted_iota(I32, (n_slc, TQ), 0)
        cur = (q0 + lax.broadcasted_iota(I32, (1, TQ), 1)) // SLC_BLOCK
        adm = j_idx <= cur
        forced = (j_idx == 0) | (j_idx == cur) | (j_idx == cur - 1)
        scv = jnp.where(adm, jnp.where(forced, jnp.inf, sc), -jnp.inf)
        rank = jnp.zeros((n_slc, TQ), I32)
        for jp in range(n_slc):
            row = scv[jp:jp + 1, :]
            beats = jnp.where(row > scv, 1, jnp.where((row == scv) & (jp < j_idx), 1, 0))
            rank = rank + beats
        selb = jnp.where(rank < n_sel, 0.0, NEG).astype(F32)
        selb4 = jnp.concatenate([selb] * B_HPG, axis=1)
        for j in range(n_slc):
            selb_ref[g, j] = selb4[j:j + 1, :]
        o_cs.append(o_c)

    for ref in st[0::3]:
        ref[...] = jnp.full(ref.shape, NEG, F32)
    for ref in st[1::3] + st[2::3]:
        ref[...] = jnp.zeros(ref.shape, F32)
    slc_st = [st[6 * g:6 * g + 3] for g in range(B_GROUPS)]
    win_st = [st[6 * g + 3:6 * g + 6] for g in range(B_GROUPS)]
    per_kt = NSA_KT // SLC_BLOCK

    def slc_scores(g, jt):
        rel = jnp.minimum(qi - jt, NSA_SLC_REL - 1)
        selb = jnp.concatenate([jnp.broadcast_to(selb_ref[g, per_kt * jt + r], (SLC_BLOCK, L))
                                for r in range(per_kt)], axis=0)
        return jnp.dot(ks_ref[0, g, jt], qTs[g], preferred_element_type=F32) + toes_ref[g, rel] + selb

    def win_scores(g, jt):
        rel = jnp.minimum(qi - jt, NSA_WIN_REL - 1)
        return jnp.dot(kw_ref[0, g, jt], qTs[g], preferred_element_type=F32) + toew_ref[g, rel]

    def far_body(jt, carry):
        s_all = [slc_scores(g, jt) for g in range(B_GROUPS)]
        _flash_step(s_all, [vsT_ref[0, g, jt] for g in range(B_GROUPS)], slc_st)
        return carry

    def near_body(jt, carry):
        s_all = [slc_scores(g, jt) for g in range(B_GROUPS)] + [win_scores(g, jt) for g in range(B_GROUPS)]
        v_all = [vsT_ref[0, g, jt] for g in range(B_GROUPS)] + [vwT_ref[0, g, jt] for g in range(B_GROUPS)]
        _flash_step(s_all, v_all, slc_st + win_st)
        return carry

    j_lo = jnp.maximum(qi - (NSA_WIN_REL - 1), 0)
    lax.fori_loop(0, j_lo, far_body, 0)
    lax.fori_loop(j_lo, qi + 1, near_body, 0)

    for g in range(B_GROUPS):
        o_s = _flash_finish(*[r[...] for r in slc_st[g]])
        o_w = _flash_finish(*[r[...] for r in win_st[g]])
        gate = jax.nn.sigmoid(gate_ref[0, g, 0])
        o_ref[0, g, 0] = (gate[0:1] * o_cs[g] + gate[1:2] * o_s + gate[2:3] * o_w).astype(o_ref.dtype)


def nsa_bias_inputs(rel_bias, seq):
    TQ, L, KT = NSA_TQ, NSA_L, NSA_KT
    nq = seq // TQ
    bc = bias_tiles(rel_bias, [-(CMP_LEN - 1)], B_HEADS, A_HEADS, N_CMP_PAD, seq, -CMP_STRIDE, 1, False, 0)
    bc = bc.reshape(B_GROUPS, B_HPG, N_CMP_PAD, nq, TQ).transpose(0, 3, 2, 1, 4).reshape(B_GROUPS, nq, N_CMP_PAD, L)

    def lanes(t):
        v = t.shape[0]
        return t.reshape(v, B_GROUPS, B_HPG, KT, TQ).transpose(1, 0, 3, 2, 4).reshape(B_GROUPS, v, KT, L)

    assert KT == TQ and (NSA_SLC_REL - 1) * KT - (KT - 1) >= T5_FAR
    toe_s = bias_tiles(rel_bias, [v * KT for v in range(NSA_SLC_REL - 1)] + [64 * KT],
                       B_HEADS, A_HEADS, KT, TQ, -1, 1, True, 0)
    assert (NSA_WIN_REL - 1) * KT - (KT - 1) < WINDOW <= NSA_WIN_REL * KT - (KT - 1)
    toe_w = bias_tiles(rel_bias, [v * KT for v in range(NSA_WIN_REL)], B_HEADS, A_HEADS, KT, TQ, -1, 1, True, WINDOW)
    return bc, lanes(toe_s), lanes(toe_w)


def nsa_overlap(seq):
    n_cmp = (seq - CMP_LEN) // CMP_STRIDE + 1
    n_slc = seq // SLC_BLOCK
    cs = np.arange(N_CMP_PAD) * CMP_STRIDE
    ss = np.arange(n_slc) * SLC_BLOCK
    ov = ((cs[None, :] + CMP_LEN - 1 >= ss[:, None]) & (cs[None, :] <= ss[:, None] + SLC_BLOCK - 1)
          & (np.arange(N_CMP_PAD)[None, :] < n_cmp))
    return jnp.asarray(ov.astype(np.float32))


def nsa_attention(q, kc, vc, k_slc, v_slc, k_win, v_win, gate_logits, biasc, toe_s, toe_w, bsz, seq):
    TQ, L = NSA_TQ, NSA_L
    nq = seq // TQ
    n_slc = seq // SLC_BLOCK
    n_cmp = (seq - CMP_LEN) // CMP_STRIDE + 1
    n_sel = min(SLC_TOPN, n_slc)
    scale = HEAD_DIM ** -0.5
    qT = (q * scale).astype(BF16).reshape(bsz, nq, TQ, B_GROUPS, B_HPG, HEAD_DIM)
    qT = qT.transpose(0, 3, 1, 5, 4, 2).reshape(bsz, B_GROUPS, nq, HEAD_DIM, L)
    vcT = vc.transpose(0, 1, 3, 2)

    KT = NSA_KT
    n_kt = seq // KT

    def key_blocks(a):
        return a.astype(BF16).reshape(bsz, n_kt, KT, B_GROUPS, HEAD_DIM).transpose(0, 3, 1, 2, 4)

    def val_blocks(a):
        return a.astype(BF16).reshape(bsz, n_kt, KT, B_GROUPS, HEAD_DIM).transpose(0, 3, 1, 4, 2)

    gT = gate_logits.reshape(bsz, nq, TQ, B_GROUPS, B_HPG, 3).transpose(0, 3, 1, 5, 4, 2).reshape(bsz, B_GROUPS, nq, 3, L)
    G = B_GROUPS
    once = pl.Buffered(1)
    k_spec = pl.BlockSpec((1, G, n_kt, KT, HEAD_DIM), lambda b, i: (b, 0, 0, 0, 0))
    v_spec = pl.BlockSpec((1, G, n_kt, HEAD_DIM, KT), lambda b, i: (b, 0, 0, 0, 0))
    n_chain = 2 * G
    out = pl.pallas_call(
        functools.partial(_nsa_kernel, n_cmp=n_cmp, n_sel=n_sel, n_slc=n_slc),
        grid=(bsz, nq),
        in_specs=[pl.BlockSpec((1, G, 1, HEAD_DIM, L), lambda b, i: (b, 0, i, 0, 0)),
                  pl.BlockSpec((1, G, N_CMP_PAD, HEAD_DIM), lambda b, i: (b, 0, 0, 0)),
                  pl.BlockSpec((1, G, HEAD_DIM, N_CMP_PAD), lambda b, i: (b, 0, 0, 0)),
                  pl.BlockSpec((G, 1, N_CMP_PAD, L), lambda b, i: (0, i, 0, 0)),
                  pl.BlockSpec((n_slc, N_CMP_PAD), lambda b, i: (0, 0), pipeline_mode=once),
                  k_spec, v_spec, k_spec, v_spec,
                  pl.BlockSpec((G, NSA_SLC_REL, KT, L), lambda b, i: (0, 0, 0, 0), pipeline_mode=once),
                  pl.BlockSpec((G, NSA_WIN_REL, KT, L), lambda b, i: (0, 0, 0, 0), pipeline_mode=once),
                  pl.BlockSpec((1, G, 1, 3, L), lambda b, i: (b, 0, i, 0, 0))],
        out_specs=pl.BlockSpec((1, G, 1, HEAD_DIM, L), lambda b, i: (b, 0, i, 0, 0)),
        out_shape=jax.ShapeDtypeStruct((bsz, B_GROUPS, nq, HEAD_DIM, L), BF16),
        scratch_shapes=([pltpu.VMEM((G, n_slc, 1, L), F32)]
                        + [pltpu.VMEM((1, L), F32), pltpu.VMEM((1, L), F32), pltpu.VMEM((HEAD_DIM, L), F32)] * n_chain),
        compiler_params=_cparams(("arbitrary", "arbitrary")),
        name="nsa_attention",
    )(qT, kc, vcT, biasc, nsa_overlap(seq), key_blocks(k_slc), val_blocks(v_slc),
      key_blocks(k_win), val_blocks(v_win), toe_s, toe_w, gT)
    out = out.reshape(bsz, B_GROUPS, nq, HEAD_DIM, B_HPG, TQ).transpose(0, 2, 5, 1, 4, 3)
    return out.reshape(bsz * seq, B_HEADS * HEAD_DIM)


MOBA_T = MOBA_BLOCK


MOBA_HB = 4


def _moba_kernel(qT_ref, k_ref, vT_ref, bias_ref, o_ref, kmean_ref, selb_ref, *st, n_sel):
    qi = pl.program_id(2)
    T = MOBA_T
    n_blk = k_ref.shape[2]
    states = [st[3 * hh:3 * hh + 3] for hh in range(MOBA_HB)]

    @pl.when(qi == 0)
    def _():
        for hh in range(MOBA_HB):
            for j in range(n_blk):
                kmean_ref[hh, j:j + 1, :] = jnp.mean(k_ref[0, hh, j].astype(F32), axis=0, keepdims=True)

    qTs = [qT_ref[0, hh] for hh in range(MOBA_HB)]
    j_idx = lax.broadcasted_iota(I32, (n_blk, T), 0)
    for hh in range(MOBA_HB):
        gate = jnp.dot(kmean_ref[hh], qTs[hh].astype(F32), preferred_element_type=F32,
                       precision=lax.Precision.HIGHEST)
        gv = jnp.where(j_idx < qi, gate, -jnp.inf)
        rank = jnp.zeros((n_blk, T), I32)
        for jp in range(n_blk):
            row = gv[jp:jp + 1, :]
            rank = rank + jnp.where(row > gv, 1, jnp.where((row == gv) & (jp < j_idx), 1, 0))
        selb = jnp.where(j_idx < qi, jnp.where(rank < n_sel, 0.0, NEG),
                         jnp.where(j_idx == qi, 0.0, NEG)).astype(F32)
        for j in range(n_blk):
            selb_ref[hh, j] = selb[j:j + 1, :]
        m_ref, l_ref, acc_ref = states[hh]
        m_ref[...] = jnp.full(m_ref.shape, NEG, F32)
        l_ref[...] = jnp.zeros(l_ref.shape, F32)
        acc_ref[...] = jnp.zeros(acc_ref.shape, F32)

    def body(kb, carry):
        rel = jnp.minimum(qi - kb, 2)
        s_all = [jnp.dot(k_ref[0, hh, kb], qTs[hh], preferred_element_type=F32) + bias_ref[rel, hh]
                 + selb_ref[hh, kb] for hh in range(MOBA_HB)]
        _flash_step(s_all, [vT_ref[0, hh, kb] for hh in range(MOBA_HB)], states)
        return carry

    lax.fori_loop(0, qi + 1, body, 0)
    for hh in range(MOBA_HB):
        o_ref[0, hh] = _flash_finish(*[r[...] for r in states[hh]]).astype(o_ref.dtype)


def moba_bias_tiles(rel_bias):
    assert MOBA_T + 1 >= T5_FAR
    t0 = bias_tiles(rel_bias, [0], C_HEADS, 0, MOBA_T, MOBA_T, -1, 1, True)
    t12 = bias_tiles(rel_bias, [MOBA_T, 4 * MOBA_T], C_HEADS, 0, MOBA_T, MOBA_T, -1, 1, False)
    return jnp.concatenate([t0, t12], axis=0)


def moba_attention(qkv, bias3, bsz, seq):
    T = MOBA_T
    assert seq % T == 0
    n_blk = seq // T
    n_sel = min(MOBA_TOPK, n_blk - 1)
    hd = C_HEADS * HEAD_DIM
    q = (qkv[:, :hd] * HEAD_DIM ** -0.5).astype(BF16).reshape(bsz, seq, C_HEADS, HEAD_DIM)
    k = qkv[:, hd:2 * hd].reshape(bsz, n_blk, T, C_HEADS, HEAD_DIM)
    v = qkv[:, 2 * hd:].reshape(bsz, n_blk, T, C_HEADS, HEAD_DIM)
    qT = q.transpose(0, 2, 3, 1)
    kb = k.transpose(0, 3, 1, 2, 4)
    vT = v.transpose(0, 3, 1, 4, 2)
    HB = MOBA_HB
    out = pl.pallas_call(
        functools.partial(_moba_kernel, n_sel=n_sel),
        grid=(bsz, C_HEADS // HB, n_blk),
        in_specs=[pl.BlockSpec((1, HB, HEAD_DIM, T), lambda b, h, i: (b, h, 0, i)),
                  pl.BlockSpec((1, HB, n_blk, T, HEAD_DIM), lambda b, h, i: (b, h, 0, 0, 0)),
                  pl.BlockSpec((1, HB, n_blk, HEAD_DIM, T), lambda b, h, i: (b, h, 0, 0, 0)),
                  pl.BlockSpec((3, HB, T, T), lambda b, h, i: (0, h, 0, 0))],
        out_specs=pl.BlockSpec((1, HB, HEAD_DIM, T), lambda b, h, i: (b, h, 0, i)),
        out_shape=jax.ShapeDtypeStruct((bsz, C_HEADS, HEAD_DIM, seq), BF16),
        scratch_shapes=([pltpu.VMEM((HB, n_blk, HEAD_DIM), F32), pltpu.VMEM((HB, n_blk, 1, T), F32)]
                        + [pltpu.VMEM((1, T), F32), pltpu.VMEM((1, T), F32), pltpu.VMEM((HEAD_DIM, T), F32)] * HB),
        compiler_params=_cparams(("arbitrary", "arbitrary", "arbitrary")),
        name="moba_attention",
    )(qT, kb, vT, bias3)
    return out.transpose(0, 3, 1, 2).reshape(bsz * seq, hd)


def _router_kernel(h_ref, w_ref, o_ref):
    o_ref[...] = jnp.dot(h_ref[...], w_ref[...], preferred_element_type=F32, precision=lax.Precision.HIGHEST)


def router_logits(h, router, tm=1024):
    m, d = h.shape
    lanes = 128
    w = jnp.pad(router, ((0, 0), (0, lanes - N_EXPERTS)))
    out = pl.pallas_call(
        _router_kernel,
        grid=(m // tm,),
        in_specs=[pl.BlockSpec((tm, d), lambda i: (i, 0)), pl.BlockSpec((d, lanes), lambda i: (0, 0))],
        out_specs=pl.BlockSpec((tm, lanes), lambda i: (i, 0)),
        out_shape=jax.ShapeDtypeStruct((m, lanes), F32),
        compiler_params=_cparams(("arbitrary",)),
        name="router_logits",
    )(h, w)
    return out[:, :N_EXPERTS]


def _add_ln_kernel(h_ref, y_ref, g_ref, b_ref, o_ref):
    o_ref[...] = _layer_norm_rows(ALPHA * h_ref[...] + y_ref[...].astype(F32), g_ref[...], b_ref[...])


def add_ln(h, y, g, b, tm=512):
    m, d = h.shape
    row = pl.BlockSpec((tm, d), lambda i: (i, 0))
    vec = pl.BlockSpec((1, d), lambda i: (0, 0))
    return pl.pallas_call(
        _add_ln_kernel, grid=(m // tm,), in_specs=[row, row, vec, vec], out_specs=row,
        out_shape=jax.ShapeDtypeStruct((m, d), F32),
        compiler_params=_cparams(("arbitrary",)), name="add_ln",
    )(h, y, g.reshape(1, d), b.reshape(1, d))


def moe_dispatch_plan(logits):
    n_tok = logits.shape[0]
    top_val, top_e = lax.top_k(logits, TOP_K)
    gate = jax.nn.softmax(top_val, axis=-1)
    e_flat = top_e.reshape(-1)
    onehot = (e_flat[:, None] == jnp.arange(N_EXPERTS, dtype=e_flat.dtype)[None, :]).astype(I32)
    rank = jnp.take_along_axis(jnp.cumsum(onehot, axis=0) - onehot, e_flat[:, None], axis=1)[:, 0]
    counts = jnp.sum(onehot, axis=0)
    padded = (counts + EXPERT_ROWS - 1) // EXPERT_ROWS * EXPERT_ROWS
    pend = jnp.cumsum(padded)
    pstart = pend - padded
    dest = pstart[e_flat] + rank
    n_assign = n_tok * TOP_K
    n_rows = -(-n_assign // EXPERT_ROWS) * EXPERT_ROWS + N_EXPERTS * EXPERT_ROWS
    n_groups = n_rows // EXPERT_ROWS
    tok_flat = jnp.repeat(jnp.arange(n_tok, dtype=I32), TOP_K)
    row_tok = jnp.zeros((n_rows,), I32).at[dest].set(tok_flat)
    grp_e = jnp.minimum(jnp.searchsorted(pend, jnp.arange(n_groups, dtype=I32) * EXPERT_ROWS, side='right'),
                        N_EXPERTS - 1).astype(I32)
    return gate, dest.astype(I32), row_tok, grp_e


def kernel(x, rel_bias, e_w_in, e_q_norm, e_kv_norm, e_w_uq, e_w_uk, e_w_uv, e_w_qidx, e_pos_k, e_pos_v, e_ck1, e_ck2, e_cv1, e_cv2, e_w_out, e_ln1_g, e_ln1_b, e_ffn_w1, e_ffn_w3, e_ffn_w2, e_ln2_g, e_ln2_b, o_w_in, o_w_out, o_ln1_g, o_ln1_b, o_router, o_moe_w1, o_moe_w3, o_moe_w2, o_ln2_g, o_ln2_b):
    bsz, seq, d = x.shape
    m = bsz * seq
    xf = x.reshape(m, d)
    dsa_bias = dsa_bias_tiles(rel_bias)
    nsa_bc, nsa_toe_s, nsa_toe_w = nsa_bias_inputs(rel_bias, seq)
    moba_bias = moba_bias_tiles(rel_bias)
    gd = B_GROUPS * HEAD_DIM
    for layer in range(DEPTH):
        i = layer // 2
        if layer % 2 == 0:
            w_in = e_w_in[i]
            a_cols = A_Q_RANK + A_KV_RANK + IDX_DIM + IDX_HEADS
            b_cols = B_HEADS * HEAD_DIM + 6 * gd
            w_a = jnp.concatenate([w_in[:, :a_cols], w_in[:, a_cols + b_cols:]], axis=1)
            w_a = jnp.pad(w_a, ((0, 0), (0, 512 - w_a.shape[1]))).astype(BF16)
            w_b = w_in[:, a_cols:a_cols + b_cols].astype(BF16)
            ya = matmul(xf, w_a, F32)
            yb = matmul(xf, w_b, BF16)
            c_q, c_kv = ya[:, :A_Q_RANK], ya[:, A_Q_RANK:A_Q_RANK + A_KV_RANK]
            k_idx = ya[:, A_Q_RANK + A_KV_RANK:A_Q_RANK + A_KV_RANK + IDX_DIM]
            w_idx = ya[:, a_cols - IDX_HEADS:a_cols] * IDX_HEADS ** -0.5
            gate_logits = ya[:, a_cols:a_cols + 3 * B_HEADS]
            q_b = yb[:, :B_HEADS * HEAD_DIM]
            k_c, v_c, k_sl, v_sl, k_w, v_w = [yb[:, B_HEADS * HEAD_DIM + j * gd:B_HEADS * HEAD_DIM + (j + 1) * gd]
                                              for j in range(6)]
            q_lat, q_idx, ckvn = dsa_prep(c_q, c_kv, e_q_norm[i], e_kv_norm[i], e_w_uq[i], e_w_uk[i], e_w_qidx[i])
            o_aT = dsa_attention(q_lat, q_idx, ckvn, k_idx, w_idx, e_w_uv[i], dsa_bias, bsz, seq)
            o_a = o_aT.transpose(0, 2, 1).reshape(m, A_HEADS * HEAD_DIM)
            kc = nsa_compress(k_c, e_pos_k[i], e_ck1[i], e_ck2[i], bsz, seq)
            vc = nsa_compress(v_c, e_pos_v[i], e_cv1[i], e_cv2[i], bsz, seq)
            o_b = nsa_attention(q_b, kc, vc, k_sl, v_sl, k_w, v_w, gate_logits, nsa_bc, nsa_toe_s, nsa_toe_w,
                                bsz, seq)
            h = proj_residual_ln(jnp.concatenate([o_a, o_b], axis=1), e_w_out[i].astype(BF16), xf,
                                 e_ln1_g[i], e_ln1_b[i])
            tm = 512
            xf = swiglu_ffn(h, jnp.zeros((m // tm,), I32), e_ffn_w1[i][None].astype(BF16),
                            e_ffn_w3[i][None].astype(BF16), e_ffn_w2[i][None].astype(BF16),
                            e_ln2_g[i], e_ln2_b[i], with_ln=True, out_dtype=F32, tm=tm, ff_chunk=1408)
        else:
            qkv = matmul(xf, o_w_in[i].astype(BF16), BF16)
            o_c = moba_attention(qkv, moba_bias, bsz, seq)
            h = proj_residual_ln(o_c, o_w_out[i].astype(BF16), xf, o_ln1_g[i], o_ln1_b[i])
            gate, dest, row_tok, grp_e = moe_dispatch_plan(router_logits(h, o_router[i]))
            x_rows = h.astype(BF16)[row_tok]
            y_rows = swiglu_ffn(x_rows, grp_e, o_moe_w1[i].astype(BF16), o_moe_w3[i].astype(BF16),
                                o_moe_w2[i].astype(BF16), o_ln2_g[i], o_ln2_b[i],
                                with_ln=False, out_dtype=F32, tm=EXPERT_ROWS, ff_chunk=512)
            y = jnp.sum((y_rows[dest] * gate.reshape(-1, 1)).reshape(m, TOP_K, d), axis=1)
            xf = add_ln(h, y, o_ln2_g[i], o_ln2_b[i])
    return xf.reshape(bsz, seq, d)
```

```python
import functools
import math

import numpy as np
import jax
import jax.numpy as jnp
from jax import lax
from jax.experimental import pallas as pl
from jax.experimental.pallas import tpu as pltpu

F32 = jnp.float32
BF16 = jnp.bfloat16
I32 = jnp.int32

HEAD_DIM = 64
NUM_BUCKETS = 32
MAX_DISTANCE = 128
N_BIAS_HEADS = 16
A_HEADS = 8
A_Q_RANK = 256
A_KV_RANK = 128
IDX_HEADS = 16
IDX_DIM = 64
DSA_TOPK = 256
B_HEADS = 8
B_GROUPS = 2
B_HPG = B_HEADS // B_GROUPS
CMP_LEN = 32
CMP_STRIDE = 16
SLC_BLOCK = 64
SLC_TOPN = 16
WINDOW = 512
C_HEADS = 16
MOBA_BLOCK = 256
MOBA_TOPK = 3
N_EXPERTS = 8
TOP_K = 2
EXPERT_ROWS = 256
DEPTH = 2
ALPHA = (2 * DEPTH) ** 0.25

NEG = -1e30
NEG_HALF = -5e29
INT_MIN = -2 ** 31
VMEM_LIMIT = 56 * 1024 * 1024


def _t5_thresholds():
    def bucket(n):
        if n < NUM_BUCKETS // 2:
            return n
        v = np.log(np.float32(n) / np.float32(NUM_BUCKETS // 2)) / np.float32(math.log(MAX_DISTANCE / (NUM_BUCKETS // 2)))
        return min(NUM_BUCKETS // 2 + int(np.float32(v) * (NUM_BUCKETS - NUM_BUCKETS // 2)), NUM_BUCKETS - 1)
    b = [bucket(i) for i in range(4 * MAX_DISTANCE)]
    return [0] + [min(i for i in range(len(b)) if b[i] >= k) for k in range(1, NUM_BUCKETS)]


T5_THR = _t5_thresholds()
T5_FAR = T5_THR[-1]


def _cparams(sem):
    return pltpu.CompilerParams(dimension_semantics=sem, vmem_limit_bytes=VMEM_LIMIT)


def _bias_kernel(tab_ref, off_ref, o_ref, *, c_row, c_col, h0, causal_neg, window):
    v = pl.program_id(0)
    h = pl.program_id(1) + h0
    shape = o_ref.shape[2:]
    dist = (c_col * lax.broadcasted_iota(I32, shape, 1) + c_row * lax.broadcasted_iota(I32, shape, 0) + off_ref[v])
    n = jnp.maximum(dist, 0)
    acc = jnp.full(shape, tab_ref[h], F32)
    for k in range(1, NUM_BUCKETS):
        acc = jnp.where(n >= T5_THR[k], tab_ref[k * N_BIAS_HEADS + h], acc)
    if causal_neg:
        acc = jnp.where(dist >= 0, acc, NEG)
    if window:
        acc = jnp.where(dist < window, acc, NEG)
    o_ref[0, 0] = acc


def bias_tiles(rel_bias, offs, n_heads, h0, rows, cols, c_row, c_col, causal_neg, window=0):
    offs = jnp.asarray(offs, I32)
    nv = offs.shape[0]
    return pl.pallas_call(
        functools.partial(_bias_kernel, c_row=c_row, c_col=c_col, h0=h0, causal_neg=causal_neg, window=window),
        grid=(nv, n_heads),
        in_specs=[pl.BlockSpec(memory_space=pltpu.SMEM), pl.BlockSpec(memory_space=pltpu.SMEM)],
        out_specs=pl.BlockSpec((1, 1, rows, cols), lambda v, h: (v, h, 0, 0)),
        out_shape=jax.ShapeDtypeStruct((nv, n_heads, rows, cols), F32),
        compiler_params=_cparams(("arbitrary", "arbitrary")),
        name="t5_bias_tiles",
    )(rel_bias.reshape(-1), offs)


def _mm_kernel(x_ref, w_ref, o_ref):
    o_ref[...] = jnp.dot(x_ref[...].astype(BF16), w_ref[...].astype(BF16),
                         preferred_element_type=F32).astype(o_ref.dtype)


def matmul(x, w, out_dtype, tm=512):
    m, k = x.shape
    n = w.shape[1]
    tm = min(tm, m)
    return pl.pallas_call(
        _mm_kernel,
        grid=(m // tm,),
        in_specs=[pl.BlockSpec((tm, k), lambda i: (i, 0)), pl.BlockSpec((k, n), lambda i: (0, 0))],
        out_specs=pl.BlockSpec((tm, n), lambda i: (i, 0)),
        out_shape=jax.ShapeDtypeStruct((m, n), out_dtype),
        compiler_params=_cparams(("arbitrary",)),
        name="matmul",
    )(x, w)


def _layer_norm_rows(z, g, b):
    mu = jnp.mean(z, axis=-1, keepdims=True)
    zc = z - mu
    var = jnp.mean(zc * zc, axis=-1, keepdims=True)
    return zc * lax.rsqrt(var + 1e-5) * g + b


def _proj_ln_kernel(a_ref, w_ref, x_ref, g_ref, b_ref, o_ref):
    mix = jnp.dot(a_ref[...], w_ref[...], preferred_element_type=F32)
    o_ref[...] = _layer_norm_rows(ALPHA * x_ref[...] + mix, g_ref[...], b_ref[...])


def proj_residual_ln(a, w, x, g, b, tm=512):
    m, k = a.shape
    d = w.shape[1]
    tm = min(tm, m)
    return pl.pallas_call(
        _proj_ln_kernel,
        grid=(m // tm,),
        in_specs=[pl.BlockSpec((tm, k), lambda i: (i, 0)), pl.BlockSpec((k, d), lambda i: (0, 0)),
                  pl.BlockSpec((tm, d), lambda i: (i, 0)),
                  pl.BlockSpec((1, d), lambda i: (0, 0)), pl.BlockSpec((1, d), lambda i: (0, 0))],
        out_specs=pl.BlockSpec((tm, d), lambda i: (i, 0)),
        out_shape=jax.ShapeDtypeStruct((m, d), F32),
        compiler_params=_cparams(("arbitrary",)),
        name="proj_residual_ln",
    )(a, w, x, g.reshape(1, d), b.reshape(1, d))


def _ffn_kernel(ge_ref, x_ref, w1_ref, w3_ref, w2_ref, g_ref, b_ref, o_ref, *, ff_chunk, with_ln):
    del ge_ref
    x = x_ref[...]
    xb = x.astype(BF16)
    d_ff = w1_ref.shape[2]
    acc = jnp.zeros((x.shape[0], w2_ref.shape[2]), F32)
    for c in range(0, d_ff, ff_chunk):
        a = jnp.dot(xb, w1_ref[0, :, c:c + ff_chunk], preferred_element_type=F32)
        u = jnp.dot(xb, w3_ref[0, :, c:c + ff_chunk], preferred_element_type=F32)
        hid = (a * jax.nn.sigmoid(a) * u).astype(BF16)
        acc = acc + jnp.dot(hid, w2_ref[0, c:c + ff_chunk, :], preferred_element_type=F32)
    if with_ln:
        o_ref[...] = _layer_norm_rows(ALPHA * x.astype(F32) + acc, g_ref[...], b_ref[...]).astype(o_ref.dtype)
    else:
        o_ref[...] = acc.astype(o_ref.dtype)


def swiglu_ffn(x_rows, grp_e, w1, w3, w2, ln_g, ln_b, *, with_ln, out_dtype, tm, ff_chunk):
    m, d = x_rows.shape
    d_ff = w1.shape[2]
    once = pl.Buffered(1)
    grid_spec = pltpu.PrefetchScalarGridSpec(
        num_scalar_prefetch=1,
        grid=(m // tm,),
        in_specs=[pl.BlockSpec((tm, d), lambda i, ge: (i, 0)),
                  pl.BlockSpec((1, d, d_ff), lambda i, ge: (ge[i], 0, 0), pipeline_mode=once),
                  pl.BlockSpec((1, d, d_ff), lambda i, ge: (ge[i], 0, 0), pipeline_mode=once),
                  pl.BlockSpec((1, d_ff, d), lambda i, ge: (ge[i], 0, 0), pipeline_mode=once),
                  pl.BlockSpec((1, d), lambda i, ge: (0, 0)), pl.BlockSpec((1, d), lambda i, ge: (0, 0))],
        out_specs=pl.BlockSpec((tm, d), lambda i, ge: (i, 0)),
    )
    return pl.pallas_call(
        functools.partial(_ffn_kernel, ff_chunk=ff_chunk, with_ln=with_ln),
        grid_spec=grid_spec,
        out_shape=jax.ShapeDtypeStruct((m, d), out_dtype),
        compiler_params=_cparams(("arbitrary",)),
        name="swiglu_ffn",
    )(grp_e, x_rows, w1, w3, w2, ln_g.reshape(1, d), ln_b.reshape(1, d))


def _flash_probs(s, m, l):
    m_new = jnp.maximum(m, jnp.max(s, axis=0, keepdims=True))
    alpha = jnp.exp(m - m_new)
    p = jnp.exp(s - m_new)
    l_new = alpha * l + jnp.sum(p, axis=0, keepdims=True)
    return m_new, alpha, l_new, p.astype(BF16)


def _flash_update(s, vT, m, l, acc):
    m_new, alpha, l_new, p = _flash_probs(s, m, l)
    return m_new, l_new, alpha * acc + jnp.dot(vT, p, preferred_element_type=F32)


def _flash_finish(m, l, acc):
    return jnp.where(m > NEG_HALF, acc / l, 0.0)


def _rms_rows(x, g):
    return x * lax.rsqrt(jnp.mean(x * x, axis=-1, keepdims=True) + 1e-6) * g


def _dsa_prep_kernel(cq_ref, ckv_ref, qn_ref, kvn_ref, wuq_ref, wukt_ref, wqi_ref,
                     qlat_ref, qidx_ref, ckvn_ref):
    cqn = _rms_rows(cq_ref[...], qn_ref[...]).astype(BF16)
    ckvn_ref[...] = _rms_rows(ckv_ref[...], kvn_ref[...]).astype(BF16)
    q = jnp.dot(cqn, wuq_ref[...], preferred_element_type=F32).astype(BF16)
    for h in range(A_HEADS):
        ql = jnp.dot(q[:, h * HEAD_DIM:(h + 1) * HEAD_DIM], wukt_ref[h], preferred_element_type=F32)
        qlat_ref[:, h * A_KV_RANK:(h + 1) * A_KV_RANK] = (ql * HEAD_DIM ** -0.5).astype(BF16)
    qidx_ref[...] = jnp.dot(cqn, wqi_ref[...], preferred_element_type=F32).astype(BF16)


def dsa_prep(c_q, c_kv, q_norm, kv_norm, w_uq, w_uk, w_qidx, tm=512):
    m = c_q.shape[0]
    tm = min(tm, m)
    wuq = w_uq.reshape(A_Q_RANK, A_HEADS * HEAD_DIM).astype(BF16)
    wukt = jnp.transpose(w_uk, (1, 2, 0)).astype(BF16)
    wqi = w_qidx.reshape(A_Q_RANK, IDX_HEADS * IDX_DIM).astype(BF16)
    full = lambda shape: pl.BlockSpec(shape, lambda i: (0,) * len(shape))
    rows = lambda n: pl.BlockSpec((tm, n), lambda i: (i, 0))
    return pl.pallas_call(
        _dsa_prep_kernel,
        grid=(m // tm,),
        in_specs=[rows(A_Q_RANK), rows(A_KV_RANK), full((1, A_Q_RANK)), full((1, A_KV_RANK)),
                  full(wuq.shape), full(wukt.shape), full(wqi.shape)],
        out_specs=[rows(A_HEADS * A_KV_RANK), rows(IDX_HEADS * IDX_DIM), rows(A_KV_RANK)],
        out_shape=[jax.ShapeDtypeStruct((m, A_HEADS * A_KV_RANK), BF16),
                   jax.ShapeDtypeStruct((m, IDX_HEADS * IDX_DIM), BF16),
                   jax.ShapeDtypeStruct((m, A_KV_RANK), BF16)],
        compiler_params=_cparams(("arbitrary",)),
        name="dsa_prep",
    )(c_q, c_kv, q_norm.reshape(1, -1), kv_norm.reshape(1, -1), wuq, wukt, wqi)


DSA_T = 256
SUB = 128


def _dsa_kernel(qidx_ref, wT_ref, qlat_ref, kidx_ref, ckv_ref, ckvT_ref, bias_ref, wuvt_ref, o_ref,
                key_ref, selb_ref, *state_refs, n_keep):
    m_refs, l_refs, acc_refs = (state_refs[0:A_HEADS], state_refs[A_HEADS:2 * A_HEADS], state_refs[2 * A_HEADS:])
    qi = pl.program_id(1)
    nkb = qi + 1
    T = DSA_T

    def score_block(kb, carry):
        for sub in range(T // SUB):
            k = kidx_ref[0, kb, sub * SUB:(sub + 1) * SUB, :]
            acc = jnp.zeros((SUB, T), F32)
            for h in range(IDX_HEADS):
                d = jnp.dot(k, qidx_ref[0, h], preferred_element_type=F32)
                acc = acc + jnp.maximum(d, 0.0) * wT_ref[0, h:h + 1, :]
            bits = lax.bitcast_convert_type(acc, I32)
            key = bits ^ (lax.shift_right_arithmetic(bits, 31) & 0x7FFFFFFF)
            s_pos = kb * T + sub * SUB + lax.broadcasted_iota(I32, (SUB, T), 0)
            t_pos = qi * T + lax.broadcasted_iota(I32, (SUB, T), 1)
            key = jnp.where(s_pos <= t_pos, key, INT_MIN)
            key_ref[pl.ds(pl.multiple_of(kb * T + sub * SUB, SUB), SUB), :] = key
        return carry

    lax.fori_loop(0, nkb, score_block, 0)

    n_chunks = nkb * (T // SUB)

    def count_ge(cand):
        def body(i, cnt):
            blk = key_ref[pl.ds(pl.multiple_of(i * SUB, SUB), SUB), :]
            ge = jnp.where(blk >= cand, 1, 0).astype(I32)
            return cnt + jnp.sum(ge.reshape(SUB // 8, 8, T), axis=0)
        cnt = lax.fori_loop(0, n_chunks, body, jnp.zeros((8, T), I32))
        return jnp.sum(cnt, axis=0, keepdims=True)

    def bit_step(i, u):
        cand_u = u | lax.shift_left(jnp.int32(1), 31 - i)
        cnt = count_ge(cand_u ^ INT_MIN)
        return jnp.where(cnt >= n_keep, cand_u, u)

    u = lax.fori_loop(0, 32, bit_step, jnp.zeros((1, T), I32))
    thr = jnp.maximum(u ^ INT_MIN, INT_MIN + 1)

    for h in range(A_HEADS):
        m_refs[h][...] = jnp.full(m_refs[h].shape, NEG, F32)
        l_refs[h][...] = jnp.zeros(l_refs[h].shape, F32)
        acc_refs[h][...] = jnp.zeros(acc_refs[h].shape, F32)

    def attend(kb, carry):
        selb_ref[...] = jnp.where(key_ref[pl.ds(pl.multiple_of(kb * T, T), T), :] >= thr, 0.0, NEG)
        ckv = ckv_ref[0, kb]
        ckvT = ckvT_ref[0, kb]
        rel = jnp.minimum(qi - kb, 2)
        s_all = [jnp.dot(ckv, qlat_ref[0, h], preferred_element_type=F32) + bias_ref[rel, h] + selb_ref[...]
                 for h in range(A_HEADS)]
        p_all = []
        for h in range(A_HEADS):
            m_new, alpha, l_new, p = _flash_probs(s_all[h], m_refs[h][...], l_refs[h][...])
            m_refs[h][...] = m_new
            l_refs[h][...] = l_new
            p_all.append((alpha, p))
        for h in range(A_HEADS):
            alpha, p = p_all[h]
            acc_refs[h][...] = alpha * acc_refs[h][...] + jnp.dot(ckvT, p, preferred_element_type=F32)
        return carry

    lax.fori_loop(0, nkb, attend, 0)

    for h in range(A_HEADS):
        o_lat = _flash_finish(m_refs[h][...], l_refs[h][...], acc_refs[h][...]).astype(BF16)
        o_ref[0, h * HEAD_DIM:(h + 1) * HEAD_DIM, :] = jnp.dot(
            wuvt_ref[h], o_lat, preferred_element_type=F32).astype(o_ref.dtype)


def dsa_attention(q_lat, q_idx, ckvn, k_idx, w_idx, w_uv, bias3, bsz, seq):
    T = DSA_T
    nq = seq // T
    n_keep = min(DSA_TOPK, seq // 4)
    qidxT = q_idx.reshape(bsz, seq, IDX_HEADS, IDX_DIM).transpose(0, 2, 3, 1)
    qlatT = q_lat.reshape(bsz, seq, A_HEADS, A_KV_RANK).transpose(0, 2, 3, 1)
    wT = w_idx.reshape(bsz, seq, IDX_HEADS).transpose(0, 2, 1)
    kidx = k_idx.astype(BF16).reshape(bsz, nq, T, IDX_DIM)
    ckv = ckvn.reshape(bsz, nq, T, A_KV_RANK)
    ckvT = ckv.transpose(0, 1, 3, 2)
    wuvt = jnp.transpose(w_uv, (1, 2, 0)).astype(BF16)
    return pl.pallas_call(
        functools.partial(_dsa_kernel, n_keep=n_keep),
        grid=(bsz, nq),
        in_specs=[pl.BlockSpec((1, IDX_HEADS, IDX_DIM, T), lambda b, i: (b, 0, 0, i)),
                  pl.BlockSpec((1, IDX_HEADS, T), lambda b, i: (b, 0, i)),
                  pl.BlockSpec((1, A_HEADS, A_KV_RANK, T), lambda b, i: (b, 0, 0, i)),
                  pl.BlockSpec((1, nq, T, IDX_DIM), lambda b, i: (b, 0, 0, 0)),
                  pl.BlockSpec((1, nq, T, A_KV_RANK), lambda b, i: (b, 0, 0, 0)),
                  pl.BlockSpec((1, nq, A_KV_RANK, T), lambda b, i: (b, 0, 0, 0)),
                  pl.BlockSpec((3, A_HEADS, T, T), lambda b, i: (0, 0, 0, 0)),
                  pl.BlockSpec((A_HEADS, HEAD_DIM, A_KV_RANK), lambda b, i: (0, 0, 0))],
        out_specs=pl.BlockSpec((1, A_HEADS * HEAD_DIM, T), lambda b, i: (b, 0, i)),
        out_shape=jax.ShapeDtypeStruct((bsz, A_HEADS * HEAD_DIM, seq), BF16),
        scratch_shapes=([pltpu.VMEM((seq, T), I32), pltpu.VMEM((T, T), F32)] + [pltpu.VMEM((1, T), F32)] * (2 * A_HEADS)
                        + [pltpu.VMEM((A_KV_RANK, T), F32)] * A_HEADS),
        compiler_params=_cparams(("arbitrary", "arbitrary")),
        name="dsa_attention",
    )(qidxT, wT, qlatT, kidx, ckv, ckvT, bias3, wuvt)


def dsa_bias_tiles(rel_bias):
    assert DSA_T + 1 >= T5_FAR
    return bias_tiles(rel_bias, [0, DSA_T, 4 * DSA_T], A_HEADS, 0, DSA_T, DSA_T, -1, 1, False)


N_CMP_PAD = 256


def _compress_kernel(blk_ref, pos_ref, w1_ref, w2_ref, o_ref):
    x = (blk_ref[0].astype(F32) + pos_ref[...]).astype(BF16)
    hid = jax.nn.gelu(jnp.dot(x, w1_ref[...], preferred_element_type=F32))
    o_ref[0] = jnp.dot(hid.astype(BF16), w2_ref[...], preferred_element_type=F32).astype(o_ref.dtype)


def nsa_compress(a, pos, w1, w2, bsz, seq):
    n_chunk = seq // CMP_STRIDE
    assert CMP_LEN == 2 * CMP_STRIDE and n_chunk <= N_CMP_PAD
    width = CMP_STRIDE * HEAD_DIM
    chunks = a.reshape(bsz, n_chunk, CMP_STRIDE, B_GROUPS, HEAD_DIM).transpose(0, 3, 1, 2, 4)
    chunks = chunks.reshape(bsz * B_GROUPS, n_chunk, width)
    blocks = jnp.concatenate([chunks[:, :-1], chunks[:, 1:]], axis=-1)
    blocks = jnp.pad(blocks, ((0, 0), (0, N_CMP_PAD - (n_chunk - 1)), (0, 0)))
    out = pl.pallas_call(
        _compress_kernel,
        grid=(bsz * B_GROUPS,),
        in_specs=[pl.BlockSpec((1, N_CMP_PAD, 2 * width), lambda i: (i, 0, 0)),
                  pl.BlockSpec((1, 2 * width), lambda i: (0, 0)),
                  pl.BlockSpec((2 * width, HEAD_DIM), lambda i: (0, 0)),
                  pl.BlockSpec((HEAD_DIM, HEAD_DIM), lambda i: (0, 0))],
        out_specs=pl.BlockSpec((1, N_CMP_PAD, HEAD_DIM), lambda i: (i, 0, 0)),
        out_shape=jax.ShapeDtypeStruct((bsz * B_GROUPS, N_CMP_PAD, HEAD_DIM), BF16),
        compiler_params=_cparams(("arbitrary",)),
        name="nsa_compress",
    )(blocks, pos.reshape(1, 2 * width), w1.reshape(2 * width, HEAD_DIM).astype(BF16), w2.astype(BF16))
    return out.reshape(bsz, B_GROUPS, N_CMP_PAD, HEAD_DIM)


NSA_TQ = 128
NSA_L = B_HPG * NSA_TQ
NSA_KT = 128
NSA_SLC_REL = 3
NSA_WIN_REL = 5


def _flash_step(s_all, vT_all, states):
    probs = []
    for s, (m_ref, l_ref, _) in zip(s_all, states):
        m_new, alpha, l_new, p = _flash_probs(s, m_ref[...], l_ref[...])
        m_ref[...] = m_new
        l_ref[...] = l_new
        probs.append((alpha, p))
    for (alpha, p), vT, (_, _, acc_ref) in zip(probs, vT_all, states):
        acc_ref[...] = alpha * acc_ref[...] + jnp.dot(vT, p, preferred_element_type=F32)


def _nsa_kernel(qT_ref, kc_ref, vcT_ref, biasc_ref, ovl_ref, ks_ref, vsT_ref, kw_ref, vwT_ref,
                toes_ref, toew_ref, gate_ref, o_ref, selb_ref, *st, n_cmp, n_sel, n_slc):
    qi = pl.program_id(1)
    TQ, L = NSA_TQ, NSA_L
    q0 = qi * TQ
    qTs = [qT_ref[0, g, 0] for g in range(B_GROUPS)]
    t_lane = q0 + (lax.broadcasted_iota(I32, (1, L), 1) & (TQ - 1))

    o_cs = []
    for g in range(B_GROUPS):
        s = jnp.dot(kc_ref[0, g], qTs[g], preferred_element_type=F32) + biasc_ref[g, 0]
        i_idx = lax.broadcasted_iota(I32, (N_CMP_PAD, L), 0)
        valid = jnp.where(i_idx < n_cmp, i_idx * CMP_STRIDE + (CMP_LEN - 1), 2 ** 30) <= t_lane
        s = jnp.where(valid, s, NEG)
        m = jnp.max(s, axis=0, keepdims=True)
        p = jnp.where(valid, jnp.exp(s - m), 0.0)
        l = jnp.sum(p, axis=0, keepdims=True)
        p_c = p / jnp.where(l > 0, l, 1.0)
        o_c = jnp.dot(vcT_ref[0, g], p_c.astype(BF16), preferred_element_type=F32)

        psum = p_c[:, 0:TQ]
        for n in range(1, B_HPG):
            psum = psum + p_c[:, n * TQ:(n + 1) * TQ]
        sc = jnp.dot(ovl_ref[...], psum, preferred_element_type=F32, precision=lax.Precision.HIGHEST)
        j_idx = lax.broadcasted_iota(I32, (n_slc, TQ), 0)
        cur = (q0 + lax.broadcasted_iota(I32, (1, TQ), 1)) // SLC_BLOCK
        adm = j_idx <= cur
        forced = (j_idx == 0) | (j_idx == cur) | (j_idx == cur - 1)
        scv = jnp.where(adm, jnp.where(forced, jnp.inf, sc), -jnp.inf)
        rank = jnp.zeros((n_slc, TQ), I32)
        for jp in range(n_slc):
            row = scv[jp:jp + 1, :]
            beats = jnp.where(row > scv, 1, jnp.where((row == scv) & (jp < j_idx), 1, 0))
            rank = rank + beats
        selb = jnp.where(rank < n_sel, 0.0, NEG).astype(F32)
        selb4 = jnp.concatenate([selb] * B_HPG, axis=1)
        for j in range(n_slc):
            selb_ref[g, j] = selb4[j:j + 1, :]
        o_cs.append(o_c)

    for ref in st[0::3]:
        ref[...] = jnp.full(ref.shape, NEG, F32)
    for ref in st[1::3] + st[2::3]:
        ref[...] = jnp.zeros(ref.shape, F32)
    slc_st = [st[6 * g:6 * g + 3] for g in range(B_GROUPS)]
    win_st = [st[6 * g + 3:6 * g + 6] for g in range(B_GROUPS)]
    per_kt = NSA_KT // SLC_BLOCK

    def slc_scores(g, jt):
        rel = jnp.minimum(qi - jt, NSA_SLC_REL - 1)
        selb = jnp.concatenate([jnp.broadcast_to(selb_ref[g, per_kt * jt + r], (SLC_BLOCK, L))
                                for r in range(per_kt)], axis=0)
        return jnp.dot(ks_ref[0, g, jt], qTs[g], preferred_element_type=F32) + toes_ref[g, rel] + selb

    def win_scores(g, jt):
        rel = jnp.minimum(qi - jt, NSA_WIN_REL - 1)
        return jnp.dot(kw_ref[0, g, jt], qTs[g], preferred_element_type=F32) + toew_ref[g, rel]

    def far_body(jt, carry):
        s_all = [slc_scores(g, jt) for g in range(B_GROUPS)]
        _flash_step(s_all, [vsT_ref[0, g, jt] for g in range(B_GROUPS)], slc_st)
        return carry

    def near_body(jt, carry):
        s_all = [slc_scores(g, jt) for g in range(B_GROUPS)] + [win_scores(g, jt) for g in range(B_GROUPS)]
        v_all = [vsT_ref[0, g, jt] for g in range(B_GROUPS)] + [vwT_ref[0, g, jt] for g in range(B_GROUPS)]
        _flash_step(s_all, v_all, slc_st + win_st)
        return carry

    j_lo = jnp.maximum(qi - (NSA_WIN_REL - 1), 0)
    lax.fori_loop(0, j_lo, far_body, 0)
    lax.fori_loop(j_lo, qi + 1, near_body, 0)

    for g in range(B_GROUPS):
        o_s = _flash_finish(*[r[...] for r in slc_st[g]])
        o_w = _flash_finish(*[r[...] for r in win_st[g]])
        gate = jax.nn.sigmoid(gate_ref[0, g, 0])
        o_ref[0, g, 0] = (gate[0:1] * o_cs[g] + gate[1:2] * o_s + gate[2:3] * o_w).astype(o_ref.dtype)


def nsa_bias_inputs(rel_bias, seq):
    TQ, L, KT = NSA_TQ, NSA_L, NSA_KT
    nq = seq // TQ
    bc = bias_tiles(rel_bias, [-(CMP_LEN - 1)], B_HEADS, A_HEADS, N_CMP_PAD, seq, -CMP_STRIDE, 1, False, 0)
    bc = bc.reshape(B_GROUPS, B_HPG, N_CMP_PAD, nq, TQ).transpose(0, 3, 2, 1, 4).reshape(B_GROUPS, nq, N_CMP_PAD, L)

    def lanes(t):
        v = t.shape[0]
        return t.reshape(v, B_GROUPS, B_HPG, KT, TQ).transpose(1, 0, 3, 2, 4).reshape(B_GROUPS, v, KT, L)

    assert KT == TQ and (NSA_SLC_REL - 1) * KT - (KT - 1) >= T5_FAR
    toe_s = bias_tiles(rel_bias, [v * KT for v in range(NSA_SLC_REL - 1)] + [64 * KT],
                       B_HEADS, A_HEADS, KT, TQ, -1, 1, True, 0)
    assert (NSA_WIN_REL - 1) * KT - (KT - 1) < WINDOW <= NSA_WIN_REL * KT - (KT - 1)
    toe_w = bias_tiles(rel_bias, [v * KT for v in range(NSA_WIN_REL)], B_HEADS, A_HEADS, KT, TQ, -1, 1, True, WINDOW)
    return bc, lanes(toe_s), lanes(toe_w)


def nsa_overlap(seq):
    n_cmp = (seq - CMP_LEN) // CMP_STRIDE + 1
    n_slc = seq // SLC_BLOCK
    cs = np.arange(N_CMP_PAD) * CMP_STRIDE
    ss = np.arange(n_slc) * SLC_BLOCK
    ov = ((cs[None, :] + CMP_LEN - 1 >= ss[:, None]) & (cs[None, :] <= ss[:, None] + SLC_BLOCK - 1)
          & (np.arange(N_CMP_PAD)[None, :] < n_cmp))
    return jnp.asarray(ov.astype(np.float32))


def nsa_attention(q, kc, vc, k_slc, v_slc, k_win, v_win, gate_logits, biasc, toe_s, toe_w, bsz, seq):
    TQ, L = NSA_TQ, NSA_L
    nq = seq // TQ
    n_slc = seq // SLC_BLOCK
    n_cmp = (seq - CMP_LEN) // CMP_STRIDE + 1
    n_sel = min(SLC_TOPN, n_slc)
    scale = HEAD_DIM ** -0.5
    qT = (q * scale).astype(BF16).reshape(bsz, nq, TQ, B_GROUPS, B_HPG, HEAD_DIM)
    qT = qT.transpose(0, 3, 1, 5, 4, 2).reshape(bsz, B_GROUPS, nq, HEAD_DIM, L)
    vcT = vc.transpose(0, 1, 3, 2)

    KT = NSA_KT
    n_kt = seq // KT

    def key_blocks(a):
        return a.astype(BF16).reshape(bsz, n_kt, KT, B_GROUPS, HEAD_DIM).transpose(0, 3, 1, 2, 4)

    def val_blocks(a):
        return a.astype(BF16).reshape(bsz, n_kt, KT, B_GROUPS, HEAD_DIM).transpose(0, 3, 1, 4, 2)

    gT = gate_logits.reshape(bsz, nq, TQ, B_GROUPS, B_HPG, 3).transpose(0, 3, 1, 5, 4, 2).reshape(bsz, B_GROUPS, nq, 3, L)
    G = B_GROUPS
    once = pl.Buffered(1)
    k_spec = pl.BlockSpec((1, G, n_kt, KT, HEAD_DIM), lambda b, i: (b, 0, 0, 0, 0))
    v_spec = pl.BlockSpec((1, G, n_kt, HEAD_DIM, KT), lambda b, i: (b, 0, 0, 0, 0))
    n_chain = 2 * G
    out = pl.pallas_call(
        functools.partial(_nsa_kernel, n_cmp=n_cmp, n_sel=n_sel, n_slc=n_slc),
        grid=(bsz, nq),
        in_specs=[pl.BlockSpec((1, G, 1, HEAD_DIM, L), lambda b, i: (b, 0, i, 0, 0)),
                  pl.BlockSpec((1, G, N_CMP_PAD, HEAD_DIM), lambda b, i: (b, 0, 0, 0)),
                  pl.BlockSpec((1, G, HEAD_DIM, N_CMP_PAD), lambda b, i: (b, 0, 0, 0)),
                  pl.BlockSpec((G, 1, N_CMP_PAD, L), lambda b, i: (0, i, 0, 0)),
                  pl.BlockSpec((n_slc, N_CMP_PAD), lambda b, i: (0, 0), pipeline_mode=once),
                  k_spec, v_spec, k_spec, v_spec,
                  pl.BlockSpec((G, NSA_SLC_REL, KT, L), lambda b, i: (0, 0, 0, 0), pipeline_mode=once),
                  pl.BlockSpec((G, NSA_WIN_REL, KT, L), lambda b, i: (0, 0, 0, 0), pipeline_mode=once),
                  pl.BlockSpec((1, G, 1, 3, L), lambda b, i: (b, 0, i, 0, 0))],
        out_specs=pl.BlockSpec((1, G, 1, HEAD_DIM, L), lambda b, i: (b, 0, i, 0, 0)),
        out_shape=jax.ShapeDtypeStruct((bsz, B_GROUPS, nq, HEAD_DIM, L), BF16),
        scratch_shapes=([pltpu.VMEM((G, n_slc, 1, L), F32)]
                        + [pltpu.VMEM((1, L), F32), pltpu.VMEM((1, L), F32), pltpu.VMEM((HEAD_DIM, L), F32)] * n_chain),
        compiler_params=_cparams(("arbitrary", "arbitrary")),
        name="nsa_attention",
    )(qT, kc, vcT, biasc, nsa_overlap(seq), key_blocks(k_slc), val_blocks(v_slc),
      key_blocks(k_win), val_blocks(v_win), toe_s, toe_w, gT)
    out = out.reshape(bsz, B_GROUPS, nq, HEAD_DIM, B_HPG, TQ).transpose(0, 2, 5, 1, 4, 3)
    return out.reshape(bsz * seq, B_HEADS * HEAD_DIM)


MOBA_T = MOBA_BLOCK


MOBA_HB = 4


def _moba_kernel(qT_ref, k_ref, vT_ref, bias_ref, o_ref, kmean_ref, selb_ref, *st, n_sel):
    qi = pl.program_id(2)
    T = MOBA_T
    n_blk = k_ref.shape[2]
    states = [st[3 * hh:3 * hh + 3] for hh in range(MOBA_HB)]

    @pl.when(qi == 0)
    def _():
        for hh in range(MOBA_HB):
            for j in range(n_blk):
                kmean_ref[hh, j:j + 1, :] = jnp.mean(k_ref[0, hh, j].astype(F32), axis=0, keepdims=True)

    qTs = [qT_ref[0, hh] for hh in range(MOBA_HB)]
    j_idx = lax.broadcasted_iota(I32, (n_blk, T), 0)
    for hh in range(MOBA_HB):
        gate = jnp.dot(kmean_ref[hh], qTs[hh].astype(F32), preferred_element_type=F32,
                       precision=lax.Precision.HIGHEST)
        gv = jnp.where(j_idx < qi, gate, -jnp.inf)
        rank = jnp.zeros((n_blk, T), I32)
        for jp in range(n_blk):
            row = gv[jp:jp + 1, :]
            rank = rank + jnp.where(row > gv, 1, jnp.where((row == gv) & (jp < j_idx), 1, 0))
        selb = jnp.where(j_idx < qi, jnp.where(rank < n_sel, 0.0, NEG),
                         jnp.where(j_idx == qi, 0.0, NEG)).astype(F32)
        for j in range(n_blk):
            selb_ref[hh, j] = selb[j:j + 1, :]
        m_ref, l_ref, acc_ref = states[hh]
        m_ref[...] = jnp.full(m_ref.shape, NEG, F32)
        l_ref[...] = jnp.zeros(l_ref.shape, F32)
        acc_ref[...] = jnp.zeros(acc_ref.shape, F32)

    def body(kb, carry):
        rel = jnp.minimum(qi - kb, 2)
        s_all = [jnp.dot(k_ref[0, hh, kb], qTs[hh], preferred_element_type=F32) + bias_ref[rel, hh]
                 + selb_ref[hh, kb] for hh in range(MOBA_HB)]
        _flash_step(s_all, [vT_ref[0, hh, kb] for hh in range(MOBA_HB)], states)
        return carry

    lax.fori_loop(0, qi + 1, body, 0)
    for hh in range(MOBA_HB):
        o_ref[0, hh] = _flash_finish(*[r[...] for r in states[hh]]).astype(o_ref.dtype)


def moba_bias_tiles(rel_bias):
    assert MOBA_T + 1 >= T5_FAR
    t0 = bias_tiles(rel_bias, [0], C_HEADS, 0, MOBA_T, MOBA_T, -1, 1, True)
    t12 = bias_tiles(rel_bias, [MOBA_T, 4 * MOBA_T], C_HEADS, 0, MOBA_T, MOBA_T, -1, 1, False)
    return jnp.concatenate([t0, t12], axis=0)


def moba_attention(qkv, bias3, bsz, seq):
    T = MOBA_T
    assert seq % T == 0
    n_blk = seq // T
    n_sel = min(MOBA_TOPK, n_blk - 1)
    hd = C_HEADS * HEAD_DIM
    q = (qkv[:, :hd] * HEAD_DIM ** -0.5).astype(BF16).reshape(bsz, seq, C_HEADS, HEAD_DIM)
    k = qkv[:, hd:2 * hd].reshape(bsz, n_blk, T, C_HEADS, HEAD_DIM)
    v = qkv[:, 2 * hd:].reshape(bsz, n_blk, T, C_HEADS, HEAD_DIM)
    qT = q.transpose(0, 2, 3, 1)
    kb = k.transpose(0, 3, 1, 2, 4)
    vT = v.transpose(0, 3, 1, 4, 2)
    HB = MOBA_HB
    out = pl.pallas_call(
        functools.partial(_moba_kernel, n_sel=n_sel),
        grid=(bsz, C_HEADS // HB, n_blk),
        in_specs=[pl.BlockSpec((1, HB, HEAD_DIM, T), lambda b, h, i: (b, h, 0, i)),
                  pl.BlockSpec((1, HB, n_blk, T, HEAD_DIM), lambda b, h, i: (b, h, 0, 0, 0)),
                  pl.BlockSpec((1, HB, n_blk, HEAD_DIM, T), lambda b, h, i: (b, h, 0, 0, 0)),
                  pl.BlockSpec((3, HB, T, T), lambda b, h, i: (0, h, 0, 0))],
        out_specs=pl.BlockSpec((1, HB, HEAD_DIM, T), lambda b, h, i: (b, h, 0, i)),
        out_shape=jax.ShapeDtypeStruct((bsz, C_HEADS, HEAD_DIM, seq), BF16),
        scratch_shapes=([pltpu.VMEM((HB, n_blk, HEAD_DIM), F32), pltpu.VMEM((HB, n_blk, 1, T), F32)]
                        + [pltpu.VMEM((1, T), F32), pltpu.VMEM((1, T), F32), pltpu.VMEM((HEAD_DIM, T), F32)] * HB),
        compiler_params=_cparams(("arbitrary", "arbitrary", "arbitrary")),
        name="moba_attention",
    )(qT, kb, vT, bias3)
    return out.transpose(0, 3, 1, 2).reshape(bsz * seq, hd)


def _router_kernel(h_ref, w_ref, o_ref):
    o_ref[...] = jnp.dot(h_ref[...], w_ref[...], preferred_element_type=F32, precision=lax.Precision.HIGHEST)


def router_logits(h, router, tm=1024):
    m, d = h.shape
    lanes = 128
    w = jnp.pad(router, ((0, 0), (0, lanes - N_EXPERTS)))
    out = pl.pallas_call(
        _router_kernel,
        grid=(m // tm,),
        in_specs=[pl.BlockSpec((tm, d), lambda i: (i, 0)), pl.BlockSpec((d, lanes), lambda i: (0, 0))],
        out_specs=pl.BlockSpec((tm, lanes), lambda i: (i, 0)),
        out_shape=jax.ShapeDtypeStruct((m, lanes), F32),
        compiler_params=_cparams(("arbitrary",)),
        name="router_logits",
    )(h, w)
    return out[:, :N_EXPERTS]


def _add_ln_kernel(h_ref, y_ref, g_ref, b_ref, o_ref):
    o_ref[...] = _layer_norm_rows(ALPHA * h_ref[...] + y_ref[...].astype(F32), g_ref[...], b_ref[...])


def add_ln(h, y, g, b, tm=512):
    m, d = h.shape
    row = pl.BlockSpec((tm, d), lambda i: (i, 0))
    vec = pl.BlockSpec((1, d), lambda i: (0, 0))
    return pl.pallas_call(
        _add_ln_kernel, grid=(m // tm,), in_specs=[row, row, vec, vec], out_specs=row,
        out_shape=jax.ShapeDtypeStruct((m, d), F32),
        compiler_params=_cparams(("arbitrary",)), name="add_ln",
    )(h, y, g.reshape(1, d), b.reshape(1, d))


IDX_LANES = 128


def _issue_row_gather(idx_vmem_ref, idx_smem, sem_i, src_hbm, dst_slot_ref, sem_slot, n_rows):
    cp = pltpu.make_async_copy(idx_vmem_ref.at[0], idx_smem, sem_i)
    cp.start()
    cp.wait()

    for r in range(n_rows):
        row = idx_smem[r // IDX_LANES, r % IDX_LANES]
        pltpu.make_async_copy(src_hbm.at[pl.ds(row, 1)], dst_slot_ref.at[pl.ds(r, 1)], sem_slot).start()


def _pipelined_gather(idx0_ref, idxn_ref, idx_smem, sem_i, src_hbm, buf, sem_buf, n_rows):
    g = pl.program_id(0)
    slot = lax.rem(g, 2)

    @pl.when(g == 0)
    def _():
        _issue_row_gather(idx0_ref, idx_smem, sem_i, src_hbm, buf.at[0], sem_buf.at[0], n_rows)

    @pl.when(g + 1 < pl.num_programs(0))
    def _():
        _issue_row_gather(idxn_ref, idx_smem, sem_i, src_hbm, buf.at[1 - slot], sem_buf.at[1 - slot], n_rows)

    pltpu.make_async_copy(buf.at[slot], buf.at[slot], sem_buf.at[slot]).wait()
    return slot


def _gather_specs(n_steps, k):
    first = lambda g, *_: (0, 0, 0)
    nxt = lambda g, *_: (jnp.minimum(g + 1, n_steps - 1), 0, 0)
    return pl.BlockSpec((1, k, IDX_LANES), first), pl.BlockSpec((1, k, IDX_LANES), nxt)


def _moe_ffn_kernel(ge_ref, idx0_ref, idxn_ref, h_hbm, w1_ref, w3_ref, w2_ref, o_ref,
                    xbuf, idx_smem, sem_i, sem_x, *, ff_chunk):
    del ge_ref
    slot = _pipelined_gather(idx0_ref, idxn_ref, idx_smem, sem_i, h_hbm, xbuf, sem_x, EXPERT_ROWS)
    xb = xbuf[slot].astype(BF16)
    d_ff = w1_ref.shape[2]
    acc = jnp.zeros((EXPERT_ROWS, w2_ref.shape[2]), F32)
    for c in range(0, d_ff, ff_chunk):
        a = jnp.dot(xb, w1_ref[0, :, c:c + ff_chunk], preferred_element_type=F32)
        u = jnp.dot(xb, w3_ref[0, :, c:c + ff_chunk], preferred_element_type=F32)
        hid = (a * jax.nn.sigmoid(a) * u).astype(BF16)
        acc = acc + jnp.dot(hid, w2_ref[0, c:c + ff_chunk, :], preferred_element_type=F32)
    o_ref[...] = acc


def moe_expert_ffn(h, row_tok, grp_e, w1, w3, w2, ff_chunk=512):
    d = h.shape[1]
    d_ff = w1.shape[2]
    n_groups = grp_e.shape[0]
    k = EXPERT_ROWS // IDX_LANES
    idx = row_tok.reshape(n_groups, k, IDX_LANES)
    once = pl.Buffered(1)
    idx0_spec, idxn_spec = _gather_specs(n_groups, k)
    grid_spec = pltpu.PrefetchScalarGridSpec(
        num_scalar_prefetch=1,
        grid=(n_groups,),
        in_specs=[idx0_spec, idxn_spec, pl.BlockSpec(memory_space=pl.ANY),
                  pl.BlockSpec((1, d, d_ff), lambda g, ge: (ge[g], 0, 0), pipeline_mode=once),
                  pl.BlockSpec((1, d, d_ff), lambda g, ge: (ge[g], 0, 0), pipeline_mode=once),
                  pl.BlockSpec((1, d_ff, d), lambda g, ge: (ge[g], 0, 0), pipeline_mode=once)],
        out_specs=pl.BlockSpec((EXPERT_ROWS, d), lambda g, ge: (g, 0)),
        scratch_shapes=[pltpu.VMEM((2, EXPERT_ROWS, d), F32), pltpu.SMEM((k, IDX_LANES), I32),
                        pltpu.SemaphoreType.DMA(()), pltpu.SemaphoreType.DMA((2,))],
    )
    return pl.pallas_call(
        functools.partial(_moe_ffn_kernel, ff_chunk=ff_chunk),
        grid_spec=grid_spec,
        out_shape=jax.ShapeDtypeStruct((n_groups * EXPERT_ROWS, d), F32),
        compiler_params=_cparams(("arbitrary",)),
        name="moe_expert_ffn",
    )(grp_e, idx, idx, h, w1, w3, w2)


COMBINE_TM = 256


def _moe_combine_ln_kernel(idx0_ref, idxn_ref, y_hbm, h_ref, gate_ref, g_ref, b_ref, o_ref,
                           ybuf, idx_smem, sem_i, sem_y):
    tm = COMBINE_TM
    slot = _pipelined_gather(idx0_ref, idxn_ref, idx_smem, sem_i, y_hbm, ybuf, sem_y, TOP_K * tm)
    y = gate_ref[:, 0:1] * ybuf[slot, 0:tm, :]
    for j in range(1, TOP_K):
        y = y + gate_ref[:, j:j + 1] * ybuf[slot, j * tm:(j + 1) * tm, :]
    o_ref[...] = _layer_norm_rows(ALPHA * h_ref[...] + y, g_ref[...], b_ref[...])


def moe_combine_ln(h, y_rows, dest, gate, g, b):
    m, d = h.shape
    tm = COMBINE_TM
    n_tiles = m // tm
    k = TOP_K * tm // IDX_LANES
    idx = dest.reshape(n_tiles, tm, TOP_K).transpose(0, 2, 1).reshape(n_tiles, k, IDX_LANES)
    idx0_spec, idxn_spec = _gather_specs(n_tiles, k)
    row = pl.BlockSpec((tm, d), lambda i: (i, 0))
    vec = pl.BlockSpec((1, d), lambda i: (0, 0))
    return pl.pallas_call(
        _moe_combine_ln_kernel,
        grid=(n_tiles,),
        in_specs=[idx0_spec, idxn_spec, pl.BlockSpec(memory_space=pl.ANY), row,
                  pl.BlockSpec((tm, TOP_K), lambda i: (i, 0)), vec, vec],
        out_specs=row,
        out_shape=jax.ShapeDtypeStruct((m, d), F32),
        scratch_shapes=[pltpu.VMEM((2, TOP_K * tm, d), F32), pltpu.SMEM((k, IDX_LANES), I32),
                        pltpu.SemaphoreType.DMA(()), pltpu.SemaphoreType.DMA((2,))],
        compiler_params=_cparams(("arbitrary",)),
        name="moe_combine_ln",
    )(idx, idx, y_rows, h, gate, g.reshape(1, d), b.reshape(1, d))


def moe_dispatch_plan(logits):
    n_tok = logits.shape[0]
    top_val, top_e = lax.top_k(logits, TOP_K)
    gate = jax.nn.softmax(top_val, axis=-1)
    e_flat = top_e.reshape(-1)
    onehot = (e_flat[:, None] == jnp.arange(N_EXPERTS, dtype=e_flat.dtype)[None, :]).astype(I32)
    rank = jnp.take_along_axis(jnp.cumsum(onehot, axis=0) - onehot, e_flat[:, None], axis=1)[:, 0]
    counts = jnp.sum(onehot, axis=0)
    padded = (counts + EXPERT_ROWS - 1) // EXPERT_ROWS * EXPERT_ROWS
    pend = jnp.cumsum(padded)
    pstart = pend - padded
    dest = pstart[e_flat] + rank
    n_assign = n_tok * TOP_K
    n_rows = -(-n_assign // EXPERT_ROWS) * EXPERT_ROWS + N_EXPERTS * EXPERT_ROWS
    n_groups = n_rows // EXPERT_ROWS
    tok_flat = jnp.repeat(jnp.arange(n_tok, dtype=I32), TOP_K)
    row_tok = jnp.zeros((n_rows,), I32).at[dest].set(tok_flat)
    grp_e = jnp.minimum(jnp.searchsorted(pend, jnp.arange(n_groups, dtype=I32) * EXPERT_ROWS, side='right'),
                        N_EXPERTS - 1).astype(I32)
    return gate, dest.astype(I32), row_tok, grp_e


def kernel(x, rel_bias, e_w_in, e_q_norm, e_kv_norm, e_w_uq, e_w_uk, e_w_uv, e_w_qidx, e_pos_k, e_pos_v, e_ck1, e_ck2, e_cv1, e_cv2, e_w_out, e_ln1_g, e_ln1_b, e_ffn_w1, e_ffn_w3, e_ffn_w2, e_ln2_g, e_ln2_b, o_w_in, o_w_out, o_ln1_g, o_ln1_b, o_router, o_moe_w1, o_moe_w3, o_moe_w2, o_ln2_g, o_ln2_b):
    bsz, seq, d = x.shape
    m = bsz * seq
    xf = x.reshape(m, d)
    dsa_bias = dsa_bias_tiles(rel_bias)
    nsa_bc, nsa_toe_s, nsa_toe_w = nsa_bias_inputs(rel_bias, seq)
    moba_bias = moba_bias_tiles(rel_bias)
    gd = B_GROUPS * HEAD_DIM
    for layer in range(DEPTH):
        i = layer // 2
        if layer % 2 == 0:
            w_in = e_w_in[i]
            a_cols = A_Q_RANK + A_KV_RANK + IDX_DIM + IDX_HEADS
            b_cols = B_HEADS * HEAD_DIM + 6 * gd
            w_a = jnp.concatenate([w_in[:, :a_cols], w_in[:, a_cols + b_cols:]], axis=1)
            w_a = jnp.pad(w_a, ((0, 0), (0, 512 - w_a.shape[1]))).astype(BF16)
            w_b = w_in[:, a_cols:a_cols + b_cols].astype(BF16)
            ya = matmul(xf, w_a, F32)
            yb = matmul(xf, w_b, BF16)
            c_q, c_kv = ya[:, :A_Q_RANK], ya[:, A_Q_RANK:A_Q_RANK + A_KV_RANK]
            k_idx = ya[:, A_Q_RANK + A_KV_RANK:A_Q_RANK + A_KV_RANK + IDX_DIM]
            w_idx = ya[:, a_cols - IDX_HEADS:a_cols] * IDX_HEADS ** -0.5
            gate_logits = ya[:, a_cols:a_cols + 3 * B_HEADS]
            q_b = yb[:, :B_HEADS * HEAD_DIM]
            k_c, v_c, k_sl, v_sl, k_w, v_w = [yb[:, B_HEADS * HEAD_DIM + j * gd:B_HEADS * HEAD_DIM + (j + 1) * gd]
                                              for j in range(6)]
            q_lat, q_idx, ckvn = dsa_prep(c_q, c_kv, e_q_norm[i], e_kv_norm[i], e_w_uq[i], e_w_uk[i], e_w_qidx[i])
            o_aT = dsa_attention(q_lat, q_idx, ckvn, k_idx, w_idx, e_w_uv[i], dsa_bias, bsz, seq)
            o_a = o_aT.transpose(0, 2, 1).reshape(m, A_HEADS * HEAD_DIM)
            kc = nsa_compress(k_c, e_pos_k[i], e_ck1[i], e_ck2[i], bsz, seq)
            vc = nsa_compress(v_c, e_pos_v[i], e_cv1[i], e_cv2[i], bsz, seq)
            o_b = nsa_attention(q_b, kc, vc, k_sl, v_sl, k_w, v_w, gate_logits, nsa_bc, nsa_toe_s, nsa_toe_w,
                                bsz, seq)
            h = proj_residual_ln(jnp.concatenate([o_a, o_b], axis=1), e_w_out[i].astype(BF16), xf,
                                 e_ln1_g[i], e_ln1_b[i])
            tm = 512
            xf = swiglu_ffn(h, jnp.zeros((m // tm,), I32), e_ffn_w1[i][None].astype(BF16),
                            e_ffn_w3[i][None].astype(BF16), e_ffn_w2[i][None].astype(BF16),
                            e_ln2_g[i], e_ln2_b[i], with_ln=True, out_dtype=F32, tm=tm, ff_chunk=1408)
        else:
            qkv = matmul(xf, o_w_in[i].astype(BF16), BF16)
            o_c = moba_attention(qkv, moba_bias, bsz, seq)
            h = proj_residual_ln(o_c, o_w_out[i].astype(BF16), xf, o_ln1_g[i], o_ln1_b[i])
            gate, dest, row_tok, grp_e = moe_dispatch_plan(router_logits(h, o_router[i]))
            y_rows = moe_expert_ffn(h, row_tok, grp_e, o_moe_w1[i].astype(BF16), o_moe_w3[i].astype(BF16),
                                    o_moe_w2[i].astype(BF16))
            xf = moe_combine_ln(h, y_rows, dest, gate, o_ln2_g[i], o_ln2_b[i])
    return xf.reshape(bsz, seq, d)
```

```python
import functools
import math

import numpy as np
import jax
import jax.numpy as jnp
from jax import lax
from jax.experimental import pallas as pl
from jax.experimental.pallas import tpu as pltpu

F32 = jnp.float32
BF16 = jnp.bfloat16
I32 = jnp.int32

HEAD_DIM = 64
NUM_BUCKETS = 32
MAX_DISTANCE = 128
N_BIAS_HEADS = 16
A_HEADS = 8
A_Q_RANK = 256
A_KV_RANK = 128
IDX_HEADS = 16
IDX_DIM = 64
DSA_TOPK = 256
B_HEADS = 8
B_GROUPS = 2
B_HPG = B_HEADS // B_GROUPS
CMP_LEN = 32
CMP_STRIDE = 16
SLC_BLOCK = 64
SLC_TOPN = 16
WINDOW = 512
C_HEADS = 16
MOBA_BLOCK = 256
MOBA_TOPK = 3
N_EXPERTS = 8
TOP_K = 2
EXPERT_ROWS = 256
DEPTH = 2
ALPHA = (2 * DEPTH) ** 0.25

NEG = -1e30
NEG_HALF = -5e29
INT_MIN = -2 ** 31
VMEM_LIMIT = 56 * 1024 * 1024


def _t5_thresholds():
    def bucket(n):
        if n < NUM_BUCKETS // 2:
            return n
        v = np.log(np.float32(n) / np.float32(NUM_BUCKETS // 2)) / np.float32(math.log(MAX_DISTANCE / (NUM_BUCKETS // 2)))
        return min(NUM_BUCKETS // 2 + int(np.float32(v) * (NUM_BUCKETS - NUM_BUCKETS // 2)), NUM_BUCKETS - 1)
    b = [bucket(i) for i in range(4 * MAX_DISTANCE)]
    return [0] + [min(i for i in range(len(b)) if b[i] >= k) for k in range(1, NUM_BUCKETS)]


T5_THR = _t5_thresholds()
T5_FAR = T5_THR[-1]


def _cparams(sem):
    return pltpu.CompilerParams(dimension_semantics=sem, vmem_limit_bytes=VMEM_LIMIT)


def _bias_kernel(tab_ref, off_ref, o_ref, *, c_row, c_col, h0, causal_neg, window):
    v = pl.program_id(0)
    h = pl.program_id(1) + h0
    shape = o_ref.shape[2:]
    dist = (c_col * lax.broadcasted_iota(I32, shape, 1) + c_row * lax.broadcasted_iota(I32, shape, 0) + off_ref[v])
    n = jnp.maximum(dist, 0)
    acc = jnp.full(shape, tab_ref[h], F32)
    for k in range(1, NUM_BUCKETS):
        acc = jnp.where(n >= T5_THR[k], tab_ref[k * N_BIAS_HEADS + h], acc)
    if causal_neg:
        acc = jnp.where(dist >= 0, acc, NEG)
    if window:
        acc = jnp.where(dist < window, acc, NEG)
    o_ref[0, 0] = acc


def bias_tiles(rel_bias, offs, n_heads, h0, rows, cols, c_row, c_col, causal_neg, window=0):
    offs = jnp.asarray(offs, I32)
    nv = offs.shape[0]
    return pl.pallas_call(
        functools.partial(_bias_kernel, c_row=c_row, c_col=c_col, h0=h0, causal_neg=causal_neg, window=window),
        grid=(nv, n_heads),
        in_specs=[pl.BlockSpec(memory_space=pltpu.SMEM), pl.BlockSpec(memory_space=pltpu.SMEM)],
        out_specs=pl.BlockSpec((1, 1, rows, cols), lambda v, h: (v, h, 0, 0)),
        out_shape=jax.ShapeDtypeStruct((nv, n_heads, rows, cols), F32),
        compiler_params=_cparams(("arbitrary", "arbitrary")),
        name="t5_bias_tiles",
    )(rel_bias.reshape(-1), offs)


def _mm_kernel(x_ref, w_ref, o_ref):
    o_ref[...] = jnp.dot(x_ref[...].astype(BF16), w_ref[...].astype(BF16),
                         preferred_element_type=F32).astype(o_ref.dtype)


def matmul(x, w, out_dtype, tm=512):
    m, k = x.shape
    n = w.shape[1]
    tm = min(tm, m)
    return pl.pallas_call(
        _mm_kernel,
        grid=(m // tm,),
        in_specs=[pl.BlockSpec((tm, k), lambda i: (i, 0)), pl.BlockSpec((k, n), lambda i: (0, 0))],
        out_specs=pl.BlockSpec((tm, n), lambda i: (i, 0)),
        out_shape=jax.ShapeDtypeStruct((m, n), out_dtype),
        compiler_params=_cparams(("arbitrary",)),
        name="matmul",
    )(x, w)


def _layer_norm_rows(z, g, b):
    mu = jnp.mean(z, axis=-1, keepdims=True)
    zc = z - mu
    var = jnp.mean(zc * zc, axis=-1, keepdims=True)
    return zc * lax.rsqrt(var + 1e-5) * g + b


def _proj_ln_kernel(a_ref, w_ref, x_ref, g_ref, b_ref, o_ref):
    mix = jnp.dot(a_ref[...], w_ref[...], preferred_element_type=F32)
    o_ref[...] = _layer_norm_rows(ALPHA * x_ref[...] + mix, g_ref[...], b_ref[...])


def proj_residual_ln(a, w, x, g, b, tm=512):
    m, k = a.shape
    d = w.shape[1]
    tm = min(tm, m)
    return pl.pallas_call(
        _proj_ln_kernel,
        grid=(m // tm,),
        in_specs=[pl.BlockSpec((tm, k), lambda i: (i, 0)), pl.BlockSpec((k, d), lambda i: (0, 0)),
                  pl.BlockSpec((tm, d), lambda i: (i, 0)),
                  pl.BlockSpec((1, d), lambda i: (0, 0)), pl.BlockSpec((1, d), lambda i: (0, 0))],
        out_specs=pl.BlockSpec((tm, d), lambda i: (i, 0)),
        out_shape=jax.ShapeDtypeStruct((m, d), F32),
        compiler_params=_cparams(("arbitrary",)),
        name="proj_residual_ln",
    )(a, w, x, g.reshape(1, d), b.reshape(1, d))


def _ffn_kernel(ge_ref, x_ref, w1_ref, w3_ref, w2_ref, g_ref, b_ref, o_ref, *, ff_chunk, with_ln):
    del ge_ref
    x = x_ref[...]
    xb = x.astype(BF16)
    d_ff = w1_ref.shape[2]
    acc = jnp.zeros((x.shape[0], w2_ref.shape[2]), F32)
    for c in range(0, d_ff, ff_chunk):
        a = jnp.dot(xb, w1_ref[0, :, c:c + ff_chunk], preferred_element_type=F32)
        u = jnp.dot(xb, w3_ref[0, :, c:c + ff_chunk], preferred_element_type=F32)
        hid = (a * jax.nn.sigmoid(a) * u).astype(BF16)
        acc = acc + jnp.dot(hid, w2_ref[0, c:c + ff_chunk, :], preferred_element_type=F32)
    if with_ln:
        o_ref[...] = _layer_norm_rows(ALPHA * x.astype(F32) + acc, g_ref[...], b_ref[...]).astype(o_ref.dtype)
    else:
        o_ref[...] = acc.astype(o_ref.dtype)


def swiglu_ffn(x_rows, grp_e, w1, w3, w2, ln_g, ln_b, *, with_ln, out_dtype, tm, ff_chunk):
    m, d = x_rows.shape
    d_ff = w1.shape[2]
    once = pl.Buffered(1)
    grid_spec = pltpu.PrefetchScalarGridSpec(
        num_scalar_prefetch=1,
        grid=(m // tm,),
        in_specs=[pl.BlockSpec((tm, d), lambda i, ge: (i, 0)),
                  pl.BlockSpec((1, d, d_ff), lambda i, ge: (ge[i], 0, 0), pipeline_mode=once),
                  pl.BlockSpec((1, d, d_ff), lambda i, ge: (ge[i], 0, 0), pipeline_mode=once),
                  pl.BlockSpec((1, d_ff, d), lambda i, ge: (ge[i], 0, 0), pipeline_mode=once),
                  pl.BlockSpec((1, d), lambda i, ge: (0, 0)), pl.BlockSpec((1, d), lambda i, ge: (0, 0))],
        out_specs=pl.BlockSpec((tm, d), lambda i, ge: (i, 0)),
    )
    return pl.pallas_call(
        functools.partial(_ffn_kernel, ff_chunk=ff_chunk, with_ln=with_ln),
        grid_spec=grid_spec,
        out_shape=jax.ShapeDtypeStruct((m, d), out_dtype),
        compiler_params=_cparams(("arbitrary",)),
        name="swiglu_ffn",
    )(grp_e, x_rows, w1, w3, w2, ln_g.reshape(1, d), ln_b.reshape(1, d))


def _flash_probs(s, m, l):
    m_new = jnp.maximum(m, jnp.max(s, axis=0, keepdims=True))
    alpha = jnp.exp(m - m_new)
    p = jnp.exp(s - m_new)
    l_new = alpha * l + jnp.sum(p, axis=0, keepdims=True)
    return m_new, alpha, l_new, p.astype(BF16)


def _flash_update(s, vT, m, l, acc):
    m_new, alpha, l_new, p = _flash_probs(s, m, l)
    return m_new, l_new, alpha * acc + jnp.dot(vT, p, preferred_element_type=F32)


def _flash_finish(m, l, acc):
    return jnp.where(m > NEG_HALF, acc / l, 0.0)


def _rms_rows(x, g):
    return x * lax.rsqrt(jnp.mean(x * x, axis=-1, keepdims=True) + 1e-6) * g


EVEN_T = 256
SMALL_ROWS = 48
GATE_ROW0 = IDX_HEADS
NT_DIMS = (((1,), (1,)), ((), ()))


def _even_inproj_kernel(x_ref, wa_ref, wsT_ref, qn_ref, kvn_ref, wuq_ref, wuk_ref, wqiT_ref, wqbT_ref, wk4_ref, wvT_ref,
                        qidxT_ref, qlatT_ref, sT_ref, kidx_ref, ckv_ref, ckvT_ref,
                        qbT_ref, kcmp_ref, vcmp_ref, kslc_ref, kwin_ref, vT_ref):
    xb = x_ref[0].astype(BF16)
    ya = jnp.dot(xb, wa_ref[...], preferred_element_type=F32)
    cqn = _rms_rows(ya[:, :A_Q_RANK], qn_ref[...]).astype(BF16)
    ckvn = _rms_rows(ya[:, A_Q_RANK:A_Q_RANK + A_KV_RANK], kvn_ref[...])
    kidx_ref[0, 0] = ya[:, A_Q_RANK + A_KV_RANK:A_Q_RANK + A_KV_RANK + IDX_DIM].astype(BF16)
    ckv_ref[0, 0] = ckvn.astype(BF16)
    ckvT_ref[0, 0] = ckvn.T.astype(BF16)
    sT_ref[0] = lax.dot_general(wsT_ref[...], xb, NT_DIMS, preferred_element_type=F32)
    q = jnp.dot(cqn, wuq_ref[...], preferred_element_type=F32).astype(BF16)
    for h in range(A_HEADS):
        qlT = lax.dot_general(wuk_ref[h], q[:, h * HEAD_DIM:(h + 1) * HEAD_DIM], NT_DIMS, preferred_element_type=F32)
        qlatT_ref[0, h * A_KV_RANK:(h + 1) * A_KV_RANK, :] = (qlT * HEAD_DIM ** -0.5).astype(BF16)
    qidxT_ref[0] = lax.dot_general(wqiT_ref[...], cqn, NT_DIMS, preferred_element_type=F32).astype(BF16)
    qbT_ref[0] = lax.dot_general(wqbT_ref[...], xb, NT_DIMS, preferred_element_type=F32).astype(BF16)
    yk = jnp.dot(xb, wk4_ref[...], preferred_element_type=F32).astype(BF16)
    gd = B_GROUPS * HEAD_DIM
    for j, ref in enumerate((kcmp_ref, vcmp_ref, kslc_ref, kwin_ref)):
        ref[0] = yk[:, j * gd:(j + 1) * gd]
    vT = lax.dot_general(wvT_ref[...], xb, NT_DIMS, preferred_element_type=F32).astype(BF16)
    for j in range(EVEN_T // NSA_KT):
        vT_ref[0, j] = vT[:, j * NSA_KT:(j + 1) * NSA_KT]


def even_inproj(x3, w_in, q_norm, kv_norm, w_uq, w_uk, w_qidx):
    bsz, seq, d = x3.shape
    T = EVEN_T
    nq = seq // T
    gd = B_GROUPS * HEAD_DIM
    n_kt = seq // NSA_KT
    o_kidx = A_Q_RANK + A_KV_RANK
    o_widx = o_kidx + IDX_DIM
    o_qb = o_widx + IDX_HEADS
    o_kv = o_qb + B_HEADS * HEAD_DIM
    o_gate = o_kv + 6 * gd
    kv = lambda j: w_in[:, o_kv + j * gd:o_kv + (j + 1) * gd]
    wa = jnp.pad(w_in[:, :o_widx], ((0, 0), (0, 512 - o_widx))).astype(BF16)
    w_gate = w_in[:, o_gate:].reshape(d, B_GROUPS, B_HPG, 3).transpose(0, 1, 3, 2).reshape(d, 3 * B_HEADS)
    wsT = jnp.concatenate([w_in[:, o_widx:o_qb] * IDX_HEADS ** -0.5, w_gate,
                           jnp.zeros((d, SMALL_ROWS - IDX_HEADS - 3 * B_HEADS), w_in.dtype)], axis=1).T.astype(BF16)
    wuq = w_uq.reshape(A_Q_RANK, A_HEADS * HEAD_DIM).astype(BF16)
    wuk = jnp.transpose(w_uk, (1, 0, 2)).astype(BF16)
    wqiT = w_qidx.reshape(A_Q_RANK, IDX_HEADS * IDX_DIM).T.astype(BF16)
    wqbT = (w_in[:, o_qb:o_kv] * HEAD_DIM ** -0.5).T.astype(BF16)
    wk4 = jnp.concatenate([kv(0), kv(1), kv(2), kv(4)], axis=1).astype(BF16)
    wvT = jnp.concatenate([kv(3), kv(5)], axis=1).T.astype(BF16)
    weights = (wa, wsT, q_norm.reshape(1, -1), kv_norm.reshape(1, -1), wuq, wuk, wqiT, wqbT, wk4, wvT)
    once = pl.Buffered(1)
    w_specs = [pl.BlockSpec(w.shape, (lambda b, i, n=w.ndim: (0,) * n), pipeline_mode=once) for w in weights]
    fm = lambda rows: pl.BlockSpec((1, rows, T), lambda b, i: (b, 0, i))
    tok = lambda cols: pl.BlockSpec((1, T, cols), lambda b, i: (b, i, 0))
    blk = lambda r, c: pl.BlockSpec((1, 1, r, c), lambda b, i: (b, i, 0, 0))
    sds = jax.ShapeDtypeStruct
    return pl.pallas_call(
        _even_inproj_kernel,
        grid=(bsz, nq),
        in_specs=[pl.BlockSpec((1, T, d), lambda b, i: (b, i, 0))] + w_specs,
        out_specs=[fm(IDX_HEADS * IDX_DIM), fm(A_HEADS * A_KV_RANK), fm(SMALL_ROWS),
                   blk(T, IDX_DIM), blk(T, A_KV_RANK), blk(A_KV_RANK, T),
                   fm(B_HEADS * HEAD_DIM), tok(gd), tok(gd), tok(gd), tok(gd),
                   pl.BlockSpec((1, T // NSA_KT, 2 * gd, NSA_KT), lambda b, i: (b, i, 0, 0))],
        out_shape=[sds((bsz, IDX_HEADS * IDX_DIM, seq), BF16), sds((bsz, A_HEADS * A_KV_RANK, seq), BF16),
                   sds((bsz, SMALL_ROWS, seq), F32),
                   sds((bsz, nq, T, IDX_DIM), BF16), sds((bsz, nq, T, A_KV_RANK), BF16), sds((bsz, nq, A_KV_RANK, T), BF16),
                   sds((bsz, B_HEADS * HEAD_DIM, seq), BF16),
                   sds((bsz, seq, gd), BF16), sds((bsz, seq, gd), BF16), sds((bsz, seq, gd), BF16), sds((bsz, seq, gd), BF16),
                   sds((bsz, n_kt, 2 * gd, NSA_KT), BF16)],
        compiler_params=_cparams(("arbitrary", "arbitrary")),
        name="even_inproj",
    )(x3, *weights)


DSA_T = 256
SUB = 128


def _dsa_kernel(qidx_ref, wT_ref, qlat_ref, kidx_ref, ckv_ref, ckvT_ref, bias_ref, wuvt_ref, o_ref,
                key_ref, selb_ref, *state_refs, n_keep):
    m_refs, l_refs, acc_refs = (state_refs[0:A_HEADS], state_refs[A_HEADS:2 * A_HEADS], state_refs[2 * A_HEADS:])
    qi = pl.program_id(1)
    nkb = qi + 1
    T = DSA_T

    def score_block(kb, carry):
        for sub in range(T // SUB):
            k = kidx_ref[0, kb, sub * SUB:(sub + 1) * SUB, :]
            acc = jnp.zeros((SUB, T), F32)
            for h in range(IDX_HEADS):
                d = jnp.dot(k, qidx_ref[0, h * IDX_DIM:(h + 1) * IDX_DIM, :], preferred_element_type=F32)
                acc = acc + jnp.maximum(d, 0.0) * wT_ref[0, h:h + 1, :]
            bits = lax.bitcast_convert_type(acc, I32)
            key = bits ^ (lax.shift_right_arithmetic(bits, 31) & 0x7FFFFFFF)
            s_pos = kb * T + sub * SUB + lax.broadcasted_iota(I32, (SUB, T), 0)
            t_pos = qi * T + lax.broadcasted_iota(I32, (SUB, T), 1)
            key = jnp.where(s_pos <= t_pos, key, INT_MIN)
            key_ref[pl.ds(pl.multiple_of(kb * T + sub * SUB, SUB), SUB), :] = key
        return carry

    lax.fori_loop(0, nkb, score_block, 0)

    n_chunks = nkb * (T // SUB)

    def count_ge(cand):
        def body(i, cnt):
            blk = key_ref[pl.ds(pl.multiple_of(i * SUB, SUB), SUB), :]
            ge = jnp.where(blk >= cand, 1, 0).astype(I32)
            return cnt + jnp.sum(ge.reshape(SUB // 8, 8, T), axis=0)
        cnt = lax.fori_loop(0, n_chunks, body, jnp.zeros((8, T), I32))
        return jnp.sum(cnt, axis=0, keepdims=True)

    def bit_step(i, u):
        cand_u = u | lax.shift_left(jnp.int32(1), 31 - i)
        cnt = count_ge(cand_u ^ INT_MIN)
        return jnp.where(cnt >= n_keep, cand_u, u)

    u = lax.fori_loop(0, 32, bit_step, jnp.zeros((1, T), I32))
    thr = jnp.maximum(u ^ INT_MIN, INT_MIN + 1)

    for h in range(A_HEADS):
        m_refs[h][...] = jnp.full(m_refs[h].shape, NEG, F32)
        l_refs[h][...] = jnp.zeros(l_refs[h].shape, F32)
        acc_refs[h][...] = jnp.zeros(acc_refs[h].shape, F32)

    def attend(kb, carry):
        selb_ref[...] = jnp.where(key_ref[pl.ds(pl.multiple_of(kb * T, T), T), :] >= thr, 0.0, NEG)
        ckv = ckv_ref[0, kb]
        ckvT = ckvT_ref[0, kb]
        rel = jnp.minimum(qi - kb, 2)
        s_all = [jnp.dot(ckv, qlat_ref[0, h * A_KV_RANK:(h + 1) * A_KV_RANK, :], preferred_element_type=F32)
                 + bias_ref[rel, h] + selb_ref[...]
                 for h in range(A_HEADS)]
        p_all = []
        for h in range(A_HEADS):
            m_new, alpha, l_new, p = _flash_probs(s_all[h], m_refs[h][...], l_refs[h][...])
            m_refs[h][...] = m_new
            l_refs[h][...] = l_new
            p_all.append((alpha, p))
        for h in range(A_HEADS):
            alpha, p = p_all[h]
            acc_refs[h][...] = alpha * acc_refs[h][...] + jnp.dot(ckvT, p, preferred_element_type=F32)
        return carry

    lax.fori_loop(0, nkb, attend, 0)

    for h in range(A_HEADS):
        o_lat = _flash_finish(m_refs[h][...], l_refs[h][...], acc_refs[h][...]).astype(BF16)
        o_ref[0, h * HEAD_DIM:(h + 1) * HEAD_DIM, :] = jnp.dot(
            wuvt_ref[h], o_lat, preferred_element_type=F32).astype(o_ref.dtype)


def dsa_attention(qidxT, sT, qlatT, kidx, ckv, ckvT, w_uv, bias3):
    T = DSA_T
    assert T == EVEN_T
    bsz, nq = kidx.shape[0], kidx.shape[1]
    seq = nq * T
    n_keep = min(DSA_TOPK, seq // 4)
    wuvt = jnp.transpose(w_uv, (1, 2, 0)).astype(BF16)
    return pl.pallas_call(
        functools.partial(_dsa_kernel, n_keep=n_keep),
        grid=(bsz, nq),
        in_specs=[pl.BlockSpec((1, IDX_HEADS * IDX_DIM, T), lambda b, i: (b, 0, i)),
                  pl.BlockSpec((1, SMALL_ROWS, T), lambda b, i: (b, 0, i)),
                  pl.BlockSpec((1, A_HEADS * A_KV_RANK, T), lambda b, i: (b, 0, i)),
                  pl.BlockSpec((1, nq, T, IDX_DIM), lambda b, i: (b, 0, 0, 0)),
                  pl.BlockSpec((1, nq, T, A_KV_RANK), lambda b, i: (b, 0, 0, 0)),
                  pl.BlockSpec((1, nq, A_KV_RANK, T), lambda b, i: (b, 0, 0, 0)),
                  pl.BlockSpec((3, A_HEADS, T, T), lambda b, i: (0, 0, 0, 0)),
                  pl.BlockSpec((A_HEADS, HEAD_DIM, A_KV_RANK), lambda b, i: (0, 0, 0))],
        out_specs=pl.BlockSpec((1, A_HEADS * HEAD_DIM, T), lambda b, i: (b, 0, i)),
        out_shape=jax.ShapeDtypeStruct((bsz, A_HEADS * HEAD_DIM, seq), BF16),
        scratch_shapes=([pltpu.VMEM((seq, T), I32), pltpu.VMEM((T, T), F32)] + [pltpu.VMEM((1, T), F32)] * (2 * A_HEADS)
                        + [pltpu.VMEM((A_KV_RANK, T), F32)] * A_HEADS),
        compiler_params=_cparams(("arbitrary", "arbitrary")),
        name="dsa_attention",
    )(qidxT, sT, qlatT, kidx, ckv, ckvT, bias3, wuvt)


def dsa_bias_tiles(rel_bias):
    assert DSA_T + 1 >= T5_FAR
    return bias_tiles(rel_bias, [0, DSA_T, 4 * DSA_T], A_HEADS, 0, DSA_T, DSA_T, -1, 1, False)


N_CMP_PAD = 256


def _compress_kernel(blk_ref, pos_ref, w1_ref, w2_ref, o_ref):
    x = (blk_ref[0].astype(F32) + pos_ref[...]).astype(BF16)
    hid = jax.nn.gelu(jnp.dot(x, w1_ref[...], preferred_element_type=F32))
    o_ref[0] = jnp.dot(hid.astype(BF16), w2_ref[...], preferred_element_type=F32).astype(o_ref.dtype)


def nsa_compress(a, pos, w1, w2, bsz, seq):
    n_chunk = seq // CMP_STRIDE
    assert CMP_LEN == 2 * CMP_STRIDE and n_chunk <= N_CMP_PAD
    width = CMP_STRIDE * HEAD_DIM
    chunks = a.reshape(bsz, n_chunk, CMP_STRIDE, B_GROUPS, HEAD_DIM).transpose(0, 3, 1, 2, 4)
    chunks = chunks.reshape(bsz * B_GROUPS, n_chunk, width)
    blocks = jnp.concatenate([chunks[:, :-1], chunks[:, 1:]], axis=-1)
    blocks = jnp.pad(blocks, ((0, 0), (0, N_CMP_PAD - (n_chunk - 1)), (0, 0)))
    out = pl.pallas_call(
        _compress_kernel,
        grid=(bsz * B_GROUPS,),
        in_specs=[pl.BlockSpec((1, N_CMP_PAD, 2 * width), lambda i: (i, 0, 0)),
                  pl.BlockSpec((1, 2 * width), lambda i: (0, 0)),
                  pl.BlockSpec((2 * width, HEAD_DIM), lambda i: (0, 0)),
                  pl.BlockSpec((HEAD_DIM, HEAD_DIM), lambda i: (0, 0))],
        out_specs=pl.BlockSpec((1, N_CMP_PAD, HEAD_DIM), lambda i: (i, 0, 0)),
        out_shape=jax.ShapeDtypeStruct((bsz * B_GROUPS, N_CMP_PAD, HEAD_DIM), BF16),
        compiler_params=_cparams(("arbitrary",)),
        name="nsa_compress",
    )(blocks, pos.reshape(1, 2 * width), w1.reshape(2 * width, HEAD_DIM).astype(BF16), w2.astype(BF16))
    return out.reshape(bsz, B_GROUPS, N_CMP_PAD, HEAD_DIM)


NSA_TQ = 128
NSA_L = B_HPG * NSA_TQ
NSA_KT = 128
NSA_SLC_REL = 3
NSA_WIN_REL = 5


def _flash_step(s_all, vT_all, states):
    probs = []
    for s, (m_ref, l_ref, _) in zip(s_all, states):
        m_new, alpha, l_new, p = _flash_probs(s, m_ref[...], l_ref[...])
        m_ref[...] = m_new
        l_ref[...] = l_new
        probs.append((alpha, p))
    for (alpha, p), vT, (_, _, acc_ref) in zip(probs, vT_all, states):
        acc_ref[...] = alpha * acc_ref[...] + jnp.dot(vT, p, preferred_element_type=F32)


def _nsa_kernel(qT_ref, kc_ref, vcT_ref, biasc_ref, ovl_ref, ks_ref, kw_ref, vT_ref,
                toes_ref, toew_ref, sT_ref, o_ref, selb_ref, *st, n_cmp, n_sel, n_slc):
    qi = pl.program_id(1)
    TQ, L = NSA_TQ, NSA_L
    q0 = qi * TQ
    qTs, qTs_pad = [], []
    for g in range(B_GROUPS):
        q = jnp.concatenate([qT_ref[0, (g * B_HPG + n) * HEAD_DIM:(g * B_HPG + n + 1) * HEAD_DIM, :]
                             for n in range(B_HPG)], axis=1)
        parts = [jnp.zeros_like(q)] * B_GROUPS
        parts[g] = q
        qTs.append(q)
        qTs_pad.append(jnp.concatenate(parts, axis=0))
    t_lane = q0 + (lax.broadcasted_iota(I32, (1, L), 1) & (TQ - 1))

    o_cs = []
    for g in range(B_GROUPS):
        s = jnp.dot(kc_ref[0, g], qTs[g], preferred_element_type=F32) + biasc_ref[g, 0]
        i_idx = lax.broadcasted_iota(I32, (N_CMP_PAD, L), 0)
        valid = jnp.where(i_idx < n_cmp, i_idx * CMP_STRIDE + (CMP_LEN - 1), 2 ** 30) <= t_lane
        s = jnp.where(valid, s, NEG)
        m = jnp.max(s, axis=0, keepdims=True)
        p = jnp.where(valid, jnp.exp(s - m), 0.0)
        l = jnp.sum(p, axis=0, keepdims=True)
        p_c = p / jnp.where(l > 0, l, 1.0)
        o_c = jnp.dot(vcT_ref[0, g], p_c.astype(BF16), preferred_element_type=F32)

        psum = p_c[:, 0:TQ]
        for n in range(1, B_HPG):
            psum = psum + p_c[:, n * TQ:(n + 1) * TQ]
        sc = jnp.dot(ovl_ref[...], psum, preferred_element_type=F32, precision=lax.Precision.HIGHEST)
        j_idx = lax.broadcasted_iota(I32, (n_slc, TQ), 0)
        cur = (q0 + lax.broadcasted_iota(I32, (1, TQ), 1)) // SLC_BLOCK
        adm = j_idx <= cur
        forced = (j_idx == 0) | (j_idx == cur) | (j_idx == cur - 1)
        scv = jnp.where(adm, jnp.where(forced, jnp.inf, sc), -jnp.inf)
        rank = jnp.zeros((n_slc, TQ), I32)
        for jp in range(n_slc):
            row = scv[jp:jp + 1, :]
            beats = jnp.where(row > scv, 1, jnp.where((row == scv) & (jp < j_idx), 1, 0))
            rank = rank + beats
        selb = jnp.where(rank < n_sel, 0.0, NEG).astype(F32)
        selb4 = jnp.concatenate([selb] * B_HPG, axis=1)
        for j in range(n_slc):
            selb_ref[g, j] = selb4[j:j + 1, :]
        o_cs.append(o_c)

    for ref in st[0::3]:
        ref[...] = jnp.full(ref.shape, NEG, F32)
    for ref in st[1::3] + st[2::3]:
        ref[...] = jnp.zeros(ref.shape, F32)
    slc_st = [st[6 * g:6 * g + 3] for g in range(B_GROUPS)]
    win_st = [st[6 * g + 3:6 * g + 6] for g in range(B_GROUPS)]
    per_kt = NSA_KT // SLC_BLOCK

    def slc_scores(g, jt):
        rel = jnp.minimum(qi - jt, NSA_SLC_REL - 1)
        selb = jnp.concatenate([jnp.broadcast_to(selb_ref[g, per_kt * jt + r], (SLC_BLOCK, L))
                                for r in range(per_kt)], axis=0)
        return jnp.dot(ks_ref[0, jt], qTs_pad[g], preferred_element_type=F32) + toes_ref[g, rel] + selb

    def win_scores(g, jt):
        rel = jnp.minimum(qi - jt, NSA_WIN_REL - 1)
        return jnp.dot(kw_ref[0, jt], qTs_pad[g], preferred_element_type=F32) + toew_ref[g, rel]

    gd = B_GROUPS * HEAD_DIM
    v_slc = lambda g, jt: vT_ref[0, jt, g * HEAD_DIM:(g + 1) * HEAD_DIM, :]
    v_win = lambda g, jt: vT_ref[0, jt, gd + g * HEAD_DIM:gd + (g + 1) * HEAD_DIM, :]

    def far_body(jt, carry):
        s_all = [slc_scores(g, jt) for g in range(B_GROUPS)]
        _flash_step(s_all, [v_slc(g, jt) for g in range(B_GROUPS)], slc_st)
        return carry

    def near_body(jt, carry):
        s_all = [slc_scores(g, jt) for g in range(B_GROUPS)] + [win_scores(g, jt) for g in range(B_GROUPS)]
        v_all = [v_slc(g, jt) for g in range(B_GROUPS)] + [v_win(g, jt) for g in range(B_GROUPS)]
        _flash_step(s_all, v_all, slc_st + win_st)
        return carry

    j_lo = jnp.maximum(qi - (NSA_WIN_REL - 1), 0)
    lax.fori_loop(0, j_lo, far_body, 0)
    lax.fori_loop(j_lo, qi + 1, near_body, 0)

    for g in range(B_GROUPS):
        o_s = _flash_finish(*[r[...] for r in slc_st[g]])
        o_w = _flash_finish(*[r[...] for r in win_st[g]])
        row0 = GATE_ROW0 + g * 3 * B_HPG
        gate = [jax.nn.sigmoid(jnp.concatenate([sT_ref[0, row0 + j * B_HPG + n:row0 + j * B_HPG + n + 1, :]
                                                for n in range(B_HPG)], axis=1)) for j in range(3)]
        o = (gate[0] * o_cs[g] + gate[1] * o_s + gate[2] * o_w).astype(o_ref.dtype)
        for n in range(B_HPG):
            o_ref[0, (g * B_HPG + n) * HEAD_DIM:(g * B_HPG + n + 1) * HEAD_DIM, :] = o[:, n * TQ:(n + 1) * TQ]


def nsa_bias_inputs(rel_bias, seq):
    TQ, L, KT = NSA_TQ, NSA_L, NSA_KT
    nq = seq // TQ
    bc = bias_tiles(rel_bias, [-(CMP_LEN - 1)], B_HEADS, A_HEADS, N_CMP_PAD, seq, -CMP_STRIDE, 1, False, 0)
    bc = bc.reshape(B_GROUPS, B_HPG, N_CMP_PAD, nq, TQ).transpose(0, 3, 2, 1, 4).reshape(B_GROUPS, nq, N_CMP_PAD, L)

    def lanes(t):
        v = t.shape[0]
        return t.reshape(v, B_GROUPS, B_HPG, KT, TQ).transpose(1, 0, 3, 2, 4).reshape(B_GROUPS, v, KT, L)

    assert KT == TQ and (NSA_SLC_REL - 1) * KT - (KT - 1) >= T5_FAR
    toe_s = bias_tiles(rel_bias, [v * KT for v in range(NSA_SLC_REL - 1)] + [64 * KT],
                       B_HEADS, A_HEADS, KT, TQ, -1, 1, True, 0)
    assert (NSA_WIN_REL - 1) * KT - (KT - 1) < WINDOW <= NSA_WIN_REL * KT - (KT - 1)
    toe_w = bias_tiles(rel_bias, [v * KT for v in range(NSA_WIN_REL)], B_HEADS, A_HEADS, KT, TQ, -1, 1, True, WINDOW)
    return bc, lanes(toe_s), lanes(toe_w)


def nsa_overlap(seq):
    n_cmp = (seq - CMP_LEN) // CMP_STRIDE + 1
    n_slc = seq // SLC_BLOCK
    cs = np.arange(N_CMP_PAD) * CMP_STRIDE
    ss = np.arange(n_slc) * SLC_BLOCK
    ov = ((cs[None, :] + CMP_LEN - 1 >= ss[:, None]) & (cs[None, :] <= ss[:, None] + SLC_BLOCK - 1)
          & (np.arange(N_CMP_PAD)[None, :] < n_cmp))
    return jnp.asarray(ov.astype(np.float32))


def nsa_attention(qbT, kc, vc, kslc, kwin, vT, sT, biasc, toe_s, toe_w):
    TQ, L, KT, G = NSA_TQ, NSA_L, NSA_KT, B_GROUPS
    bsz, n_kt = vT.shape[0], vT.shape[1]
    seq = n_kt * KT
    nq = seq // TQ
    n_slc = seq // SLC_BLOCK
    n_cmp = (seq - CMP_LEN) // CMP_STRIDE + 1
    n_sel = min(SLC_TOPN, n_slc)
    gd = G * HEAD_DIM
    vcT = vc.transpose(0, 1, 3, 2)
    once = pl.Buffered(1)
    k_spec = pl.BlockSpec((1, n_kt, KT, gd), lambda b, i: (b, 0, 0, 0))
    n_chain = 2 * G
    return pl.pallas_call(
        functools.partial(_nsa_kernel, n_cmp=n_cmp, n_sel=n_sel, n_slc=n_slc),
        grid=(bsz, nq),
        in_specs=[pl.BlockSpec((1, B_HEADS * HEAD_DIM, TQ), lambda b, i: (b, 0, i)),
                  pl.BlockSpec((1, G, N_CMP_PAD, HEAD_DIM), lambda b, i: (b, 0, 0, 0)),
                  pl.BlockSpec((1, G, HEAD_DIM, N_CMP_PAD), lambda b, i: (b, 0, 0, 0)),
                  pl.BlockSpec((G, 1, N_CMP_PAD, L), lambda b, i: (0, i, 0, 0)),
                  pl.BlockSpec((n_slc, N_CMP_PAD), lambda b, i: (0, 0), pipeline_mode=once),
                  k_spec, k_spec,
                  pl.BlockSpec((1, n_kt, 2 * gd, KT), lambda b, i: (b, 0, 0, 0)),
                  pl.BlockSpec((G, NSA_SLC_REL, KT, L), lambda b, i: (0, 0, 0, 0), pipeline_mode=once),
                  pl.BlockSpec((G, NSA_WIN_REL, KT, L), lambda b, i: (0, 0, 0, 0), pipeline_mode=once),
                  pl.BlockSpec((1, SMALL_ROWS, TQ), lambda b, i: (b, 0, i))],
        out_specs=pl.BlockSpec((1, B_HEADS * HEAD_DIM, TQ), lambda b, i: (b, 0, i)),
        out_shape=jax.ShapeDtypeStruct((bsz, B_HEADS * HEAD_DIM, seq), BF16),
        scratch_shapes=([pltpu.VMEM((G, n_slc, 1, L), F32)]
                        + [pltpu.VMEM((1, L), F32), pltpu.VMEM((1, L), F32), pltpu.VMEM((HEAD_DIM, L), F32)] * n_chain),
        compiler_params=_cparams(("arbitrary", "arbitrary")),
        name="nsa_attention",
    )(qbT, kc, vcT, biasc, nsa_overlap(seq), kslc.reshape(bsz, n_kt, KT, gd), kwin.reshape(bsz, n_kt, KT, gd),
      vT, toe_s, toe_w, sT)


MOBA_T = MOBA_BLOCK


MOBA_HB = 4


PAIR = 2 * HEAD_DIM


def _moba_inproj_kernel(x_ref, wqT_ref, wk_ref, wvT_ref, qT_ref, k_ref, vT_ref):
    xb = x_ref[0].astype(BF16)
    nt = (((1,), (1,)), ((), ()))
    qT_ref[0] = lax.dot_general(wqT_ref[...], xb, nt, preferred_element_type=F32).astype(BF16)
    k_ref[0, 0] = jnp.dot(xb, wk_ref[...], preferred_element_type=F32).astype(BF16)
    vT_ref[0, 0] = lax.dot_general(wvT_ref[...], xb, nt, preferred_element_type=F32).astype(BF16)


def _pair_padded_qT(wq):
    n_heads = wq.shape[1] // HEAD_DIM
    wT = wq.T.reshape(n_heads, HEAD_DIM, wq.shape[0])
    z = jnp.zeros_like(wT)
    even = jnp.concatenate([wT, z], axis=1)
    odd = jnp.concatenate([z, wT], axis=1)
    is_even = (jnp.arange(n_heads) % 2 == 0)[:, None, None]
    return jnp.where(is_even, even, odd).reshape(n_heads * PAIR, wq.shape[0])


def moba_inproj(x3, w_in):
    bsz, seq, d = x3.shape
    T = MOBA_T
    n_blk = seq // T
    hd = C_HEADS * HEAD_DIM
    wqT = _pair_padded_qT(w_in[:, :hd] * HEAD_DIM ** -0.5).astype(BF16)
    wk = w_in[:, hd:2 * hd].astype(BF16)
    wvT = w_in[:, 2 * hd:].T.astype(BF16)
    once = pl.Buffered(1)
    return pl.pallas_call(
        _moba_inproj_kernel,
        grid=(bsz, n_blk),
        in_specs=[pl.BlockSpec((1, T, d), lambda b, i: (b, i, 0)),
                  pl.BlockSpec(wqT.shape, lambda b, i: (0, 0), pipeline_mode=once),
                  pl.BlockSpec(wk.shape, lambda b, i: (0, 0), pipeline_mode=once),
                  pl.BlockSpec(wvT.shape, lambda b, i: (0, 0), pipeline_mode=once)],
        out_specs=[pl.BlockSpec((1, C_HEADS * PAIR, T), lambda b, i: (b, 0, i)),
                   pl.BlockSpec((1, 1, T, hd), lambda b, i: (b, i, 0, 0)),
                   pl.BlockSpec((1, 1, hd, T), lambda b, i: (b, i, 0, 0))],
        out_shape=[jax.ShapeDtypeStruct((bsz, C_HEADS * PAIR, seq), BF16),
                   jax.ShapeDtypeStruct((bsz, n_blk, T, hd), BF16),
                   jax.ShapeDtypeStruct((bsz, n_blk, hd, T), BF16)],
        compiler_params=_cparams(("arbitrary", "arbitrary")),
        name="moba_inproj",
    )(x3, wqT, wk, wvT)


def _moba_kernel(qT_ref, k_ref, vT_ref, bias_ref, o_ref, kmean_ref, selb_ref, *st, n_sel):
    qi = pl.program_id(2)
    T = MOBA_T
    n_blk = k_ref.shape[1]
    states = [st[3 * hh:3 * hh + 3] for hh in range(MOBA_HB)]

    @pl.when(qi == 0)
    def _():
        for j in range(n_blk):
            kmean_ref[j:j + 1, :] = jnp.mean(k_ref[0, j].astype(F32), axis=0, keepdims=True)

    qTs = [qT_ref[0, hh * PAIR:(hh + 1) * PAIR, :] for hh in range(MOBA_HB)]
    pair = lambda hh: slice((hh // 2) * PAIR, (hh // 2 + 1) * PAIR)
    j_idx = lax.broadcasted_iota(I32, (n_blk, T), 0)
    for hh in range(MOBA_HB):
        gate = jnp.dot(kmean_ref[:, pair(hh)], qTs[hh].astype(F32), preferred_element_type=F32,
                       precision=lax.Precision.HIGHEST)
        gv = jnp.where(j_idx < qi, gate, -jnp.inf)
        rank = jnp.zeros((n_blk, T), I32)
        for jp in range(n_blk):
            row = gv[jp:jp + 1, :]
            rank = rank + jnp.where(row > gv, 1, jnp.where((row == gv) & (jp < j_idx), 1, 0))
        selb = jnp.where(j_idx < qi, jnp.where(rank < n_sel, 0.0, NEG),
                         jnp.where(j_idx == qi, 0.0, NEG)).astype(F32)
        for j in range(n_blk):
            selb_ref[hh, j] = selb[j:j + 1, :]
        m_ref, l_ref, acc_ref = states[hh]
        m_ref[...] = jnp.full(m_ref.shape, NEG, F32)
        l_ref[...] = jnp.zeros(l_ref.shape, F32)
        acc_ref[...] = jnp.zeros(acc_ref.shape, F32)

    def body(kb, carry):
        rel = jnp.minimum(qi - kb, 2)
        s_all = [jnp.dot(k_ref[0, kb, :, pair(hh)], qTs[hh], preferred_element_type=F32) + bias_ref[rel, hh]
                 + selb_ref[hh, kb] for hh in range(MOBA_HB)]
        v_all = [vT_ref[0, kb, hh * HEAD_DIM:(hh + 1) * HEAD_DIM, :] for hh in range(MOBA_HB)]
        _flash_step(s_all, v_all, states)
        return carry

    lax.fori_loop(0, qi + 1, body, 0)
    for hh in range(MOBA_HB):
        o_ref[0, hh * HEAD_DIM:(hh + 1) * HEAD_DIM, :] = _flash_finish(*[r[...] for r in states[hh]]).astype(o_ref.dtype)


def moba_bias_tiles(rel_bias):
    assert MOBA_T + 1 >= T5_FAR
    t0 = bias_tiles(rel_bias, [0], C_HEADS, 0, MOBA_T, MOBA_T, -1, 1, True)
    t12 = bias_tiles(rel_bias, [MOBA_T, 4 * MOBA_T], C_HEADS, 0, MOBA_T, MOBA_T, -1, 1, False)
    return jnp.concatenate([t0, t12], axis=0)


def moba_attention(qT, k, vT, bias3):
    T = MOBA_T
    bsz, n_blk = k.shape[0], k.shape[1]
    seq = n_blk * T
    n_sel = min(MOBA_TOPK, n_blk - 1)
    HB = MOBA_HB
    assert HB % 2 == 0
    out = pl.pallas_call(
        functools.partial(_moba_kernel, n_sel=n_sel),
        grid=(bsz, C_HEADS // HB, n_blk),
        in_specs=[pl.BlockSpec((1, HB * PAIR, T), lambda b, h, i: (b, h, i)),
                  pl.BlockSpec((1, n_blk, T, HB * HEAD_DIM), lambda b, h, i: (b, 0, 0, h)),
                  pl.BlockSpec((1, n_blk, HB * HEAD_DIM, T), lambda b, h, i: (b, 0, h, 0)),
                  pl.BlockSpec((3, HB, T, T), lambda b, h, i: (0, h, 0, 0))],
        out_specs=pl.BlockSpec((1, HB * HEAD_DIM, T), lambda b, h, i: (b, h, i)),
        out_shape=jax.ShapeDtypeStruct((bsz, C_HEADS * HEAD_DIM, seq), BF16),
        scratch_shapes=([pltpu.VMEM((n_blk, HB * HEAD_DIM), F32), pltpu.VMEM((HB, n_blk, 1, T), F32)]
                        + [pltpu.VMEM((1, T), F32), pltpu.VMEM((1, T), F32), pltpu.VMEM((HEAD_DIM, T), F32)] * HB),
        compiler_params=_cparams(("arbitrary", "arbitrary", "arbitrary")),
        name="moba_attention",
    )(qT, k, vT, bias3)
    return out


def _projT_ln_kernel(*refs, n_in):
    aT_refs, w_refs = refs[:n_in], refs[n_in:2 * n_in]
    x_ref, g_ref, b_ref, o_ref = refs[2 * n_in:]
    tn = (((0,), (0,)), ((), ()))
    mix = lax.dot_general(aT_refs[0][0], w_refs[0][...], tn, preferred_element_type=F32)
    for aT_ref, w_ref in zip(aT_refs[1:], w_refs[1:]):
        mix = mix + lax.dot_general(aT_ref[0], w_ref[...], tn, preferred_element_type=F32)
    o_ref[0] = _layer_norm_rows(ALPHA * x_ref[0] + mix, g_ref[...], b_ref[...])


def projT_residual_ln(aTs, w, x3, g, b, tm=256):
    bsz, seq, d = x3.shape
    ws, k0 = [], 0
    for aT in aTs:
        ws.append(w[k0:k0 + aT.shape[1]])
        k0 += aT.shape[1]
    n_in = len(aTs)
    once = pl.Buffered(1)
    vec = pl.BlockSpec((1, d), lambda bb, i: (0, 0))
    return pl.pallas_call(
        functools.partial(_projT_ln_kernel, n_in=n_in),
        grid=(bsz, seq // tm),
        in_specs=([pl.BlockSpec((1, aT.shape[1], tm), lambda bb, i: (bb, 0, i)) for aT in aTs]
                  + [pl.BlockSpec(wi.shape, lambda bb, i: (0, 0), pipeline_mode=once) for wi in ws]
                  + [pl.BlockSpec((1, tm, d), lambda bb, i: (bb, i, 0)), vec, vec]),
        out_specs=pl.BlockSpec((1, tm, d), lambda bb, i: (bb, i, 0)),
        out_shape=jax.ShapeDtypeStruct((bsz, seq, d), F32),
        compiler_params=_cparams(("arbitrary", "arbitrary")),
        name="projT_residual_ln",
    )(*aTs, *ws, x3, g.reshape(1, d), b.reshape(1, d))


def _router_kernel(h_ref, w_ref, o_ref):
    o_ref[...] = jnp.dot(h_ref[...], w_ref[...], preferred_element_type=F32, precision=lax.Precision.HIGHEST)


def router_logits(h, router, tm=1024):
    m, d = h.shape
    lanes = 128
    w = jnp.pad(router, ((0, 0), (0, lanes - N_EXPERTS)))
    out = pl.pallas_call(
        _router_kernel,
        grid=(m // tm,),
        in_specs=[pl.BlockSpec((tm, d), lambda i: (i, 0)), pl.BlockSpec((d, lanes), lambda i: (0, 0))],
        out_specs=pl.BlockSpec((tm, lanes), lambda i: (i, 0)),
        out_shape=jax.ShapeDtypeStruct((m, lanes), F32),
        compiler_params=_cparams(("arbitrary",)),
        name="router_logits",
    )(h, w)
    return out[:, :N_EXPERTS]


def _add_ln_kernel(h_ref, y_ref, g_ref, b_ref, o_ref):
    o_ref[...] = _layer_norm_rows(ALPHA * h_ref[...] + y_ref[...].astype(F32), g_ref[...], b_ref[...])


def add_ln(h, y, g, b, tm=512):
    m, d = h.shape
    row = pl.BlockSpec((tm, d), lambda i: (i, 0))
    vec = pl.BlockSpec((1, d), lambda i: (0, 0))
    return pl.pallas_call(
        _add_ln_kernel, grid=(m // tm,), in_specs=[row, row, vec, vec], out_specs=row,
        out_shape=jax.ShapeDtypeStruct((m, d), F32),
        compiler_params=_cparams(("arbitrary",)), name="add_ln",
    )(h, y, g.reshape(1, d), b.reshape(1, d))


IDX_LANES = 128


def _issue_row_gather(idx_vmem_ref, idx_smem, sem_i, src_hbm, dst_slot_ref, sem_slot, n_rows):
    cp = pltpu.make_async_copy(idx_vmem_ref.at[0], idx_smem, sem_i)
    cp.start()
    cp.wait()

    for r in range(n_rows):
        row = idx_smem[r // IDX_LANES, r % IDX_LANES]
        pltpu.make_async_copy(src_hbm.at[pl.ds(row, 1)], dst_slot_ref.at[pl.ds(r, 1)], sem_slot).start()


def _pipelined_gather(idx0_ref, idxn_ref, idx_smem, sem_i, src_hbm, buf, sem_buf, n_rows):
    g = pl.program_id(0)
    slot = lax.rem(g, 2)

    @pl.when(g == 0)
    def _():
        _issue_row_gather(idx0_ref, idx_smem, sem_i, src_hbm, buf.at[0], sem_buf.at[0], n_rows)

    @pl.when(g + 1 < pl.num_programs(0))
    def _():
        _issue_row_gather(idxn_ref, idx_smem, sem_i, src_hbm, buf.at[1 - slot], sem_buf.at[1 - slot], n_rows)

    pltpu.make_async_copy(buf.at[slot], buf.at[slot], sem_buf.at[slot]).wait()
    return slot


def _gather_specs(n_steps, k):
    first = lambda g, *_: (0, 0, 0)
    nxt = lambda g, *_: (jnp.minimum(g + 1, n_steps - 1), 0, 0)
    return pl.BlockSpec((1, k, IDX_LANES), first), pl.BlockSpec((1, k, IDX_LANES), nxt)


def _moe_ffn_kernel(ge_ref, idx0_ref, idxn_ref, h_hbm, w1_ref, w3_ref, w2_ref, o_ref,
                    xbuf, idx_smem, sem_i, sem_x, *, ff_chunk):
    del ge_ref
    slot = _pipelined_gather(idx0_ref, idxn_ref, idx_smem, sem_i, h_hbm, xbuf, sem_x, EXPERT_ROWS)
    xb = xbuf[slot].astype(BF16)
    d_ff = w1_ref.shape[2]
    acc = jnp.zeros((EXPERT_ROWS, w2_ref.shape[2]), F32)
    for c in range(0, d_ff, ff_chunk):
        a = jnp.dot(xb, w1_ref[0, :, c:c + ff_chunk], preferred_element_type=F32)
        u = jnp.dot(xb, w3_ref[0, :, c:c + ff_chunk], preferred_element_type=F32)
        hid = (a * jax.nn.sigmoid(a) * u).astype(BF16)
        acc = acc + jnp.dot(hid, w2_ref[0, c:c + ff_chunk, :], preferred_element_type=F32)
    o_ref[...] = acc


def moe_expert_ffn(h, row_tok, grp_e, w1, w3, w2, ff_chunk=512):
    d = h.shape[1]
    d_ff = w1.shape[2]
    n_groups = grp_e.shape[0]
    k = EXPERT_ROWS // IDX_LANES
    idx = row_tok.reshape(n_groups, k, IDX_LANES)
    once = pl.Buffered(1)
    idx0_spec, idxn_spec = _gather_specs(n_groups, k)
    grid_spec = pltpu.PrefetchScalarGridSpec(
        num_scalar_prefetch=1,
        grid=(n_groups,),
        in_specs=[idx0_spec, idxn_spec, pl.BlockSpec(memory_space=pl.ANY),
                  pl.BlockSpec((1, d, d_ff), lambda g, ge: (ge[g], 0, 0), pipeline_mode=once),
                  pl.BlockSpec((1, d, d_ff), lambda g, ge: (ge[g], 0, 0), pipeline_mode=once),
                  pl.BlockSpec((1, d_ff, d), lambda g, ge: (ge[g], 0, 0), pipeline_mode=once)],
        out_specs=pl.BlockSpec((EXPERT_ROWS, d), lambda g, ge: (g, 0)),
        scratch_shapes=[pltpu.VMEM((2, EXPERT_ROWS, d), F32), pltpu.SMEM((k, IDX_LANES), I32),
                        pltpu.SemaphoreType.DMA(()), pltpu.SemaphoreType.DMA((2,))],
    )
    return pl.pallas_call(
        functools.partial(_moe_ffn_kernel, ff_chunk=ff_chunk),
        grid_spec=grid_spec,
        out_shape=jax.ShapeDtypeStruct((n_groups * EXPERT_ROWS, d), F32),
        compiler_params=_cparams(("arbitrary",)),
        name="moe_expert_ffn",
    )(grp_e, idx, idx, h, w1, w3, w2)


COMBINE_TM = 256


def _moe_combine_ln_kernel(idx0_ref, idxn_ref, y_hbm, h_ref, gate_ref, g_ref, b_ref, o_ref,
                           ybuf, idx_smem, sem_i, sem_y):
    tm = COMBINE_TM
    slot = _pipelined_gather(idx0_ref, idxn_ref, idx_smem, sem_i, y_hbm, ybuf, sem_y, TOP_K * tm)
    y = gate_ref[:, 0:1] * ybuf[slot, 0:tm, :]
    for j in range(1, TOP_K):
        y = y + gate_ref[:, j:j + 1] * ybuf[slot, j * tm:(j + 1) * tm, :]
    o_ref[...] = _layer_norm_rows(ALPHA * h_ref[...] + y, g_ref[...], b_ref[...])


def moe_combine_ln(h, y_rows, dest, gate, g, b):
    m, d = h.shape
    tm = COMBINE_TM
    n_tiles = m // tm
    k = TOP_K * tm // IDX_LANES
    idx = dest.reshape(n_tiles, tm, TOP_K).transpose(0, 2, 1).reshape(n_tiles, k, IDX_LANES)
    idx0_spec, idxn_spec = _gather_specs(n_tiles, k)
    row = pl.BlockSpec((tm, d), lambda i: (i, 0))
    vec = pl.BlockSpec((1, d), lambda i: (0, 0))
    return pl.pallas_call(
        _moe_combine_ln_kernel,
        grid=(n_tiles,),
        in_specs=[idx0_spec, idxn_spec, pl.BlockSpec(memory_space=pl.ANY), row,
                  pl.BlockSpec((tm, TOP_K), lambda i: (i, 0)), vec, vec],
        out_specs=row,
        out_shape=jax.ShapeDtypeStruct((m, d), F32),
        scratch_shapes=[pltpu.VMEM((2, TOP_K * tm, d), F32), pltpu.SMEM((k, IDX_LANES), I32),
                        pltpu.SemaphoreType.DMA(()), pltpu.SemaphoreType.DMA((2,))],
        compiler_params=_cparams(("arbitrary",)),
        name="moe_combine_ln",
    )(idx, idx, y_rows, h, gate, g.reshape(1, d), b.reshape(1, d))


def moe_dispatch_plan(logits):
    n_tok = logits.shape[0]
    top_val, top_e = lax.top_k(logits, TOP_K)
    gate = jax.nn.softmax(top_val, axis=-1)
    e_flat = top_e.reshape(-1)
    onehot = (e_flat[:, None] == jnp.arange(N_EXPERTS, dtype=e_flat.dtype)[None, :]).astype(I32)
    rank = jnp.take_along_axis(jnp.cumsum(onehot, axis=0) - onehot, e_flat[:, None], axis=1)[:, 0]
    counts = jnp.sum(onehot, axis=0)
    padded = (counts + EXPERT_ROWS - 1) // EXPERT_ROWS * EXPERT_ROWS
    pend = jnp.cumsum(padded)
    pstart = pend - padded
    dest = pstart[e_flat] + rank
    n_assign = n_tok * TOP_K
    n_rows = -(-n_assign // EXPERT_ROWS) * EXPERT_ROWS + N_EXPERTS * EXPERT_ROWS
    n_groups = n_rows // EXPERT_ROWS
    tok_flat = jnp.repeat(jnp.arange(n_tok, dtype=I32), TOP_K)
    row_tok = jnp.zeros((n_rows,), I32).at[dest].set(tok_flat)
    grp_e = jnp.minimum(jnp.searchsorted(pend, jnp.arange(n_groups, dtype=I32) * EXPERT_ROWS, side='right'),
                        N_EXPERTS - 1).astype(I32)
    return gate, dest.astype(I32), row_tok, grp_e


def kernel(x, rel_bias, e_w_in, e_q_norm, e_kv_norm, e_w_uq, e_w_uk, e_w_uv, e_w_qidx, e_pos_k, e_pos_v, e_ck1, e_ck2, e_cv1, e_cv2, e_w_out, e_ln1_g, e_ln1_b, e_ffn_w1, e_ffn_w3, e_ffn_w2, e_ln2_g, e_ln2_b, o_w_in, o_w_out, o_ln1_g, o_ln1_b, o_router, o_moe_w1, o_moe_w3, o_moe_w2, o_ln2_g, o_ln2_b):
    bsz, seq, d = x.shape
    m = bsz * seq
    xf = x.reshape(m, d)
    dsa_bias = dsa_bias_tiles(rel_bias)
    nsa_bc, nsa_toe_s, nsa_toe_w = nsa_bias_inputs(rel_bias, seq)
    moba_bias = moba_bias_tiles(rel_bias)
    gd = B_GROUPS * HEAD_DIM
    for layer in range(DEPTH):
        i = layer // 2
        if layer % 2 == 0:
            x3 = xf.reshape(bsz, seq, d)
            (qidxT, qlatT, sT, kidx, ckv, ckvT, qbT, kcmp, vcmp, kslc, kwin, vT) = even_inproj(
                x3, e_w_in[i], e_q_norm[i], e_kv_norm[i], e_w_uq[i], e_w_uk[i], e_w_qidx[i])
            o_aT = dsa_attention(qidxT, sT, qlatT, kidx, ckv, ckvT, e_w_uv[i], dsa_bias)
            kc = nsa_compress(kcmp.reshape(m, gd), e_pos_k[i], e_ck1[i], e_ck2[i], bsz, seq)
            vc = nsa_compress(vcmp.reshape(m, gd), e_pos_v[i], e_cv1[i], e_cv2[i], bsz, seq)
            o_bT = nsa_attention(qbT, kc, vc, kslc, kwin, vT, sT, nsa_bc, nsa_toe_s, nsa_toe_w)
            h = projT_residual_ln([o_aT, o_bT], e_w_out[i].astype(BF16), x3, e_ln1_g[i], e_ln1_b[i]).reshape(m, d)
            tm = 512
            xf = swiglu_ffn(h, jnp.zeros((m // tm,), I32), e_ffn_w1[i][None].astype(BF16),
                            e_ffn_w3[i][None].astype(BF16), e_ffn_w2[i][None].astype(BF16),
                            e_ln2_g[i], e_ln2_b[i], with_ln=True, out_dtype=F32, tm=tm, ff_chunk=1408)
        else:
            x3 = xf.reshape(bsz, seq, d)
            o_cT = moba_attention(*moba_inproj(x3, o_w_in[i]), moba_bias)
            h = projT_residual_ln([o_cT], o_w_out[i].astype(BF16), x3, o_ln1_g[i], o_ln1_b[i]).reshape(m, d)
            gate, dest, row_tok, grp_e = moe_dispatch_plan(router_logits(h, o_router[i]))
            y_rows = moe_expert_ffn(h, row_tok, grp_e, o_moe_w1[i].astype(BF16), o_moe_w3[i].astype(BF16),
                                    o_moe_w2[i].astype(BF16))
            xf = moe_combine_ln(h, y_rows, dest, gate, o_ln2_g[i], o_ln2_b[i])
    return xf.reshape(bsz, seq, d)
```

```python
import functools
import math

import numpy as np
import jax
import jax.numpy as jnp
from jax import lax
from jax.experimental import pallas as pl
from jax.experimental.pallas import tpu as pltpu

F32 = jnp.float32
BF16 = jnp.bfloat16
I32 = jnp.int32

HEAD_DIM = 64
NUM_BUCKETS = 32
MAX_DISTANCE = 128
N_BIAS_HEADS = 16
A_HEADS = 8
A_Q_RANK = 256
A_KV_RANK = 128
IDX_HEADS = 16
IDX_DIM = 64
DSA_TOPK = 256
B_HEADS = 8
B_GROUPS = 2
B_HPG = B_HEADS // B_GROUPS
CMP_LEN = 32
CMP_STRIDE = 16
SLC_BLOCK = 64
SLC_TOPN = 16
WINDOW = 512
C_HEADS = 16
MOBA_BLOCK = 256
MOBA_TOPK = 3
N_EXPERTS = 8
TOP_K = 2
EXPERT_ROWS = 256
DEPTH = 2
ALPHA = (2 * DEPTH) ** 0.25

LOG2E = 1.4426950408889634
QK_SCALE = HEAD_DIM ** -0.5 * LOG2E
NEG = -1e30
NEG_HALF = -5e29
INT_MIN = -2 ** 31
VMEM_LIMIT = 56 * 1024 * 1024


def _t5_thresholds():
    def bucket(n):
        if n < NUM_BUCKETS // 2:
            return n
        v = np.log(np.float32(n) / np.float32(NUM_BUCKETS // 2)) / np.float32(math.log(MAX_DISTANCE / (NUM_BUCKETS // 2)))
        return min(NUM_BUCKETS // 2 + int(np.float32(v) * (NUM_BUCKETS - NUM_BUCKETS // 2)), NUM_BUCKETS - 1)
    b = [bucket(i) for i in range(4 * MAX_DISTANCE)]
    return [0] + [min(i for i in range(len(b)) if b[i] >= k) for k in range(1, NUM_BUCKETS)]


T5_THR = _t5_thresholds()
T5_FAR = T5_THR[-1]


def _cparams(sem):
    return pltpu.CompilerParams(dimension_semantics=sem, vmem_limit_bytes=VMEM_LIMIT)


def _bias_kernel(tab_ref, off_ref, o_ref, *, c_row, c_col, h0, causal_neg, window):
    v = pl.program_id(0)
    h = pl.program_id(1) + h0
    shape = o_ref.shape[2:]
    dist = (c_col * lax.broadcasted_iota(I32, shape, 1) + c_row * lax.broadcasted_iota(I32, shape, 0) + off_ref[v])
    n = jnp.maximum(dist, 0)
    acc = jnp.full(shape, tab_ref[h] * LOG2E, F32)
    for k in range(1, NUM_BUCKETS):
        acc = jnp.where(n >= T5_THR[k], tab_ref[k * N_BIAS_HEADS + h] * LOG2E, acc)
    if causal_neg:
        acc = jnp.where(dist >= 0, acc, NEG)
    if window:
        acc = jnp.where(dist < window, acc, NEG)
    o_ref[0, 0] = acc


def bias_tiles(rel_bias, offs, n_heads, h0, rows, cols, c_row, c_col, causal_neg, window=0):
    offs = jnp.asarray(offs, I32)
    nv = offs.shape[0]
    return pl.pallas_call(
        functools.partial(_bias_kernel, c_row=c_row, c_col=c_col, h0=h0, causal_neg=causal_neg, window=window),
        grid=(nv, n_heads),
        in_specs=[pl.BlockSpec(memory_space=pltpu.SMEM), pl.BlockSpec(memory_space=pltpu.SMEM)],
        out_specs=pl.BlockSpec((1, 1, rows, cols), lambda v, h: (v, h, 0, 0)),
        out_shape=jax.ShapeDtypeStruct((nv, n_heads, rows, cols), F32),
        compiler_params=_cparams(("arbitrary", "arbitrary")),
        name="t5_bias_tiles",
    )(rel_bias.reshape(-1), offs)


def _mm_kernel(x_ref, w_ref, o_ref):
    o_ref[...] = jnp.dot(x_ref[...].astype(BF16), w_ref[...].astype(BF16),
                         preferred_element_type=F32).astype(o_ref.dtype)


def matmul(x, w, out_dtype, tm=512):
    m, k = x.shape
    n = w.shape[1]
    tm = min(tm, m)
    return pl.pallas_call(
        _mm_kernel,
        grid=(m // tm,),
        in_specs=[pl.BlockSpec((tm, k), lambda i: (i, 0)), pl.BlockSpec((k, n), lambda i: (0, 0))],
        out_specs=pl.BlockSpec((tm, n), lambda i: (i, 0)),
        out_shape=jax.ShapeDtypeStruct((m, n), out_dtype),
        compiler_params=_cparams(("arbitrary",)),
        name="matmul",
    )(x, w)


def _layer_norm_rows(z, g, b):
    mu = jnp.mean(z, axis=-1, keepdims=True)
    zc = z - mu
    var = jnp.mean(zc * zc, axis=-1, keepdims=True)
    return zc * lax.rsqrt(var + 1e-5) * g + b


def _proj_ln_kernel(a_ref, w_ref, x_ref, g_ref, b_ref, o_ref):
    mix = jnp.dot(a_ref[...], w_ref[...], preferred_element_type=F32)
    o_ref[...] = _layer_norm_rows(ALPHA * x_ref[...] + mix, g_ref[...], b_ref[...])


def proj_residual_ln(a, w, x, g, b, tm=512):
    m, k = a.shape
    d = w.shape[1]
    tm = min(tm, m)
    return pl.pallas_call(
        _proj_ln_kernel,
        grid=(m // tm,),
        in_specs=[pl.BlockSpec((tm, k), lambda i: (i, 0)), pl.BlockSpec((k, d), lambda i: (0, 0)),
                  pl.BlockSpec((tm, d), lambda i: (i, 0)),
                  pl.BlockSpec((1, d), lambda i: (0, 0)), pl.BlockSpec((1, d), lambda i: (0, 0))],
        out_specs=pl.BlockSpec((tm, d), lambda i: (i, 0)),
        out_shape=jax.ShapeDtypeStruct((m, d), F32),
        compiler_params=_cparams(("arbitrary",)),
        name="proj_residual_ln",
    )(a, w, x, g.reshape(1, d), b.reshape(1, d))


def _ffn_kernel(ge_ref, x_ref, w1_ref, w3_ref, w2_ref, g_ref, b_ref, o_ref, *, ff_chunk, with_ln):
    del ge_ref
    x = x_ref[...]
    xb = x.astype(BF16)
    d_ff = w1_ref.shape[2]
    acc = jnp.zeros((x.shape[0], w2_ref.shape[2]), F32)
    for c in range(0, d_ff, ff_chunk):
        a = jnp.dot(xb, w1_ref[0, :, c:c + ff_chunk], preferred_element_type=F32)
        u = jnp.dot(xb, w3_ref[0, :, c:c + ff_chunk], preferred_element_type=F32)
        hid = (a * jax.nn.sigmoid(a) * u).astype(BF16)
        acc = acc + jnp.dot(hid, w2_ref[0, c:c + ff_chunk, :], preferred_element_type=F32)
    if with_ln:
        o_ref[...] = _layer_norm_rows(ALPHA * x.astype(F32) + acc, g_ref[...], b_ref[...]).astype(o_ref.dtype)
    else:
        o_ref[...] = acc.astype(o_ref.dtype)


def swiglu_ffn(x_rows, grp_e, w1, w3, w2, ln_g, ln_b, *, with_ln, out_dtype, tm, ff_chunk):
    m, d = x_rows.shape
    d_ff = w1.shape[2]
    once = pl.Buffered(1)
    grid_spec = pltpu.PrefetchScalarGridSpec(
        num_scalar_prefetch=1,
        grid=(m // tm,),
        in_specs=[pl.BlockSpec((tm, d), lambda i, ge: (i, 0)),
                  pl.BlockSpec((1, d, d_ff), lambda i, ge: (ge[i], 0, 0), pipeline_mode=once),
                  pl.BlockSpec((1, d, d_ff), lambda i, ge: (ge[i], 0, 0), pipeline_mode=once),
                  pl.BlockSpec((1, d_ff, d), lambda i, ge: (ge[i], 0, 0), pipeline_mode=once),
                  pl.BlockSpec((1, d), lambda i, ge: (0, 0)), pl.BlockSpec((1, d), lambda i, ge: (0, 0))],
        out_specs=pl.BlockSpec((tm, d), lambda i, ge: (i, 0)),
    )
    return pl.pallas_call(
        functools.partial(_ffn_kernel, ff_chunk=ff_chunk, with_ln=with_ln),
        grid_spec=grid_spec,
        out_shape=jax.ShapeDtypeStruct((m, d), out_dtype),
        compiler_params=_cparams(("arbitrary",)),
        name="swiglu_ffn",
    )(grp_e, x_rows, w1, w3, w2, ln_g.reshape(1, d), ln_b.reshape(1, d))


def _flash_probs(s, m, l, segs=None):
    if segs is None:
        m_new = jnp.maximum(m, jnp.max(s, axis=0, keepdims=True))
        p = jnp.exp2(s - m_new)
    else:
        m_new, r0 = m, 0
        for n, c in segs:
            seg_max = jnp.max(s[r0:r0 + n], axis=0, keepdims=True)
            m_new = jnp.maximum(m_new, jnp.where(c > NEG_HALF, seg_max + c, NEG))
            r0 += n
        parts, r0 = [], 0
        for n, c in segs:
            shift = jnp.where(c > NEG_HALF, m_new - c, -NEG)
            parts.append(jnp.exp2(s[r0:r0 + n] - shift))
            r0 += n
        p = parts[0] if len(parts) == 1 else jnp.concatenate(parts, axis=0)
    alpha = jnp.exp2(m - m_new)
    l_new = alpha * l + jnp.sum(p, axis=0, keepdims=True)
    return m_new, alpha, l_new, p.astype(BF16)


def _flash_merge(states):
    ms = [st[0][...] for st in states]
    m = functools.reduce(jnp.maximum, ms)
    ws = [jnp.exp2(mi - m) for mi in ms]
    l = sum(w * st[1][...] for w, st in zip(ws, states))
    acc = sum(w * st[2][...] for w, st in zip(ws, states))
    return m, l, acc


def _flash_finish(m, l, acc):
    return jnp.where(m > NEG_HALF, acc / l, 0.0)


def _rms_rows(x, g):
    return x * lax.rsqrt(jnp.mean(x * x, axis=-1, keepdims=True) + 1e-6) * g


EVEN_T = 256
SMALL_ROWS = 48
GATE_ROW0 = IDX_HEADS
NT_DIMS = (((1,), (1,)), ((), ()))


def _even_inproj_kernel(x_ref, wa_ref, wsT_ref, qn_ref, kvn_ref, wuq_ref, wuk_ref, wqiT_ref, wqbT_ref, wk4_ref, wvT_ref,
                        qidxT_ref, qlatT_ref, sT_ref, kidx_ref, ckv_ref, ckvT_ref,
                        qbT_ref, kcmp_ref, vcmp_ref, kslc_ref, kwin_ref, vT_ref):
    xb = x_ref[0].astype(BF16)
    ya = jnp.dot(xb, wa_ref[...], preferred_element_type=F32)
    cqn = _rms_rows(ya[:, :A_Q_RANK], qn_ref[...]).astype(BF16)
    ckvn = _rms_rows(ya[:, A_Q_RANK:A_Q_RANK + A_KV_RANK], kvn_ref[...])
    kidx_ref[0, 0] = ya[:, A_Q_RANK + A_KV_RANK:A_Q_RANK + A_KV_RANK + IDX_DIM].astype(BF16)
    ckv_ref[0, 0] = ckvn.astype(BF16)
    ckvT_ref[0, 0] = ckvn.T.astype(BF16)
    sT_ref[0] = lax.dot_general(wsT_ref[...], xb, NT_DIMS, preferred_element_type=F32)
    q = jnp.dot(cqn, wuq_ref[...], preferred_element_type=F32).astype(BF16)
    for h in range(A_HEADS):
        qlT = lax.dot_general(wuk_ref[h], q[:, h * HEAD_DIM:(h + 1) * HEAD_DIM], NT_DIMS, preferred_element_type=F32)
        qlatT_ref[0, h * A_KV_RANK:(h + 1) * A_KV_RANK, :] = (qlT * QK_SCALE).astype(BF16)
    qidxT_ref[0] = lax.dot_general(wqiT_ref[...], cqn, NT_DIMS, preferred_element_type=F32).astype(BF16)
    qbT_ref[0] = lax.dot_general(wqbT_ref[...], xb, NT_DIMS, preferred_element_type=F32).astype(BF16)
    yk = jnp.dot(xb, wk4_ref[...], preferred_element_type=F32).astype(BF16)
    gd = B_GROUPS * HEAD_DIM
    for j, ref in enumerate((kcmp_ref, vcmp_ref, kslc_ref, kwin_ref)):
        ref[0] = yk[:, j * gd:(j + 1) * gd]
    vT = lax.dot_general(wvT_ref[...], xb, NT_DIMS, preferred_element_type=F32).astype(BF16)
    for j in range(EVEN_T // NSA_KT):
        vT_ref[0, j] = vT[:, j * NSA_KT:(j + 1) * NSA_KT]


def even_inproj(x3, w_in, q_norm, kv_norm, w_uq, w_uk, w_qidx):
    bsz, seq, d = x3.shape
    T = EVEN_T
    nq = seq // T
    gd = B_GROUPS * HEAD_DIM
    n_kt = seq // NSA_KT
    o_kidx = A_Q_RANK + A_KV_RANK
    o_widx = o_kidx + IDX_DIM
    o_qb = o_widx + IDX_HEADS
    o_kv = o_qb + B_HEADS * HEAD_DIM
    o_gate = o_kv + 6 * gd
    kv = lambda j: w_in[:, o_kv + j * gd:o_kv + (j + 1) * gd]
    wa = jnp.pad(w_in[:, :o_widx], ((0, 0), (0, 512 - o_widx))).astype(BF16)
    w_gate = w_in[:, o_gate:].reshape(d, B_GROUPS, B_HPG, 3).transpose(0, 1, 3, 2).reshape(d, 3 * B_HEADS)
    wsT = jnp.concatenate([w_in[:, o_widx:o_qb] * IDX_HEADS ** -0.5, w_gate,
                           jnp.zeros((d, SMALL_ROWS - IDX_HEADS - 3 * B_HEADS), w_in.dtype)], axis=1).T.astype(BF16)
    wuq = w_uq.reshape(A_Q_RANK, A_HEADS * HEAD_DIM).astype(BF16)
    wuk = jnp.transpose(w_uk, (1, 0, 2)).astype(BF16)
    wqiT = w_qidx.reshape(A_Q_RANK, IDX_HEADS * IDX_DIM).T.astype(BF16)
    wqbT = (w_in[:, o_qb:o_kv] * QK_SCALE).T.astype(BF16)
    wk4 = jnp.concatenate([kv(0), kv(1), kv(2), kv(4)], axis=1).astype(BF16)
    wvT = jnp.concatenate([kv(3), kv(5)], axis=1).T.astype(BF16)
    weights = (wa, wsT, q_norm.reshape(1, -1), kv_norm.reshape(1, -1), wuq, wuk, wqiT, wqbT, wk4, wvT)
    once = pl.Buffered(1)
    w_specs = [pl.BlockSpec(w.shape, (lambda b, i, n=w.ndim: (0,) * n), pipeline_mode=once) for w in weights]
    fm = lambda rows: pl.BlockSpec((1, rows, T), lambda b, i: (b, 0, i))
    tok = lambda cols: pl.BlockSpec((1, T, cols), lambda b, i: (b, i, 0))
    blk = lambda r, c: pl.BlockSpec((1, 1, r, c), lambda b, i: (b, i, 0, 0))
    sds = jax.ShapeDtypeStruct
    return pl.pallas_call(
        _even_inproj_kernel,
        grid=(bsz, nq),
        in_specs=[pl.BlockSpec((1, T, d), lambda b, i: (b, i, 0))] + w_specs,
        out_specs=[fm(IDX_HEADS * IDX_DIM), fm(A_HEADS * A_KV_RANK), fm(SMALL_ROWS),
                   blk(T, IDX_DIM), blk(T, A_KV_RANK), blk(A_KV_RANK, T),
                   fm(B_HEADS * HEAD_DIM), tok(gd), tok(gd), tok(gd), tok(gd),
                   pl.BlockSpec((1, T // NSA_KT, 2 * gd, NSA_KT), lambda b, i: (b, i, 0, 0))],
        out_shape=[sds((bsz, IDX_HEADS * IDX_DIM, seq), BF16), sds((bsz, A_HEADS * A_KV_RANK, seq), BF16),
                   sds((bsz, SMALL_ROWS, seq), F32),
                   sds((bsz, nq, T, IDX_DIM), BF16), sds((bsz, nq, T, A_KV_RANK), BF16), sds((bsz, nq, A_KV_RANK, T), BF16),
                   sds((bsz, B_HEADS * HEAD_DIM, seq), BF16),
                   sds((bsz, seq, gd), BF16), sds((bsz, seq, gd), BF16), sds((bsz, seq, gd), BF16), sds((bsz, seq, gd), BF16),
                   sds((bsz, n_kt, 2 * gd, NSA_KT), BF16)],
        compiler_params=_cparams(("arbitrary", "arbitrary")),
        name="even_inproj",
    )(x3, *weights)


DSA_T = 256
SUB = 128


def _dsa_kernel(qidx_ref, wT_ref, qlat_ref, kidx_ref, ckv_ref, ckvT_ref, bias_ref, wuvt_ref, o_ref,
                key_ref, selb_ref, *state_refs, n_keep):
    m_refs, l_refs, acc_refs = (state_refs[0:A_HEADS], state_refs[A_HEADS:2 * A_HEADS], state_refs[2 * A_HEADS:])
    qi = pl.program_id(1)
    nkb = qi + 1
    T = DSA_T

    def score_block(kb, carry):
        for sub in range(T // SUB):
            k = kidx_ref[0, kb, sub * SUB:(sub + 1) * SUB, :]
            acc = jnp.zeros((SUB, T), F32)
            for h in range(IDX_HEADS):
                d = jnp.dot(k, qidx_ref[0, h * IDX_DIM:(h + 1) * IDX_DIM, :], preferred_element_type=F32)
                acc = acc + jnp.maximum(d, 0.0) * wT_ref[0, h:h + 1, :]
            bits = lax.bitcast_convert_type(acc, I32)
            key = bits ^ (lax.shift_right_arithmetic(bits, 31) & 0x7FFFFFFF)
            s_pos = kb * T + sub * SUB + lax.broadcasted_iota(I32, (SUB, T), 0)
            t_pos = qi * T + lax.broadcasted_iota(I32, (SUB, T), 1)
            key = jnp.where(s_pos <= t_pos, key, INT_MIN)
            key_ref[pl.ds(pl.multiple_of(kb * T + sub * SUB, SUB), SUB), :] = key
        return carry

    lax.fori_loop(0, nkb, score_block, 0)

    n_chunks = nkb * (T // SUB)

    def count_ge(cand):
        def body(i, cnt):
            blk = key_ref[pl.ds(pl.multiple_of(i * SUB, SUB), SUB), :]
            ge = jnp.where(blk >= cand, 1, 0).astype(I32)
            return cnt + jnp.sum(ge.reshape(SUB // 8, 8, T), axis=0)
        cnt = lax.fori_loop(0, n_chunks, body, jnp.zeros((8, T), I32))
        return jnp.sum(cnt, axis=0, keepdims=True)

    def bit_step(i, u):
        cand_u = u | lax.shift_left(jnp.int32(1), 31 - i)
        cnt = count_ge(cand_u ^ INT_MIN)
        return jnp.where(cnt >= n_keep, cand_u, u)

    u = lax.fori_loop(0, 32, bit_step, jnp.zeros((1, T), I32))
    thr = jnp.maximum(u ^ INT_MIN, INT_MIN + 1)

    for h in range(A_HEADS):
        m_refs[h][...] = jnp.full(m_refs[h].shape, NEG, F32)
        l_refs[h][...] = jnp.zeros(l_refs[h].shape, F32)
        acc_refs[h][...] = jnp.zeros(acc_refs[h].shape, F32)

    states = list(zip(m_refs, l_refs, acc_refs))
    far_bias = [bias_ref[2, h, 0:1, :] for h in range(A_HEADS)]

    def attend(kb, near):
        selb_ref[...] = jnp.where(key_ref[pl.ds(pl.multiple_of(kb * T, T), T), :] >= thr, 0.0, NEG)
        ckv = ckv_ref[0, kb]
        ckvT = ckvT_ref[0, kb]
        s_all = []
        for h in range(A_HEADS):
            s = jnp.dot(ckv, qlat_ref[0, h * A_KV_RANK:(h + 1) * A_KV_RANK, :], preferred_element_type=F32) + selb_ref[...]
            s_all.append(s + bias_ref[qi - kb, h] if near else (s, [(T, far_bias[h])]))
        _flash_step(s_all, [ckvT] * A_HEADS, states)

    def far_body(kb, carry):
        attend(kb, False)
        return carry

    def near_body(kb, carry):
        attend(kb, True)
        return carry

    n_far = jnp.maximum(qi - 1, 0)
    lax.fori_loop(0, n_far, far_body, 0)
    lax.fori_loop(n_far, nkb, near_body, 0)

    for h in range(A_HEADS):
        o_lat = _flash_finish(m_refs[h][...], l_refs[h][...], acc_refs[h][...]).astype(BF16)
        o_ref[0, h * HEAD_DIM:(h + 1) * HEAD_DIM, :] = jnp.dot(
            wuvt_ref[h], o_lat, preferred_element_type=F32).astype(o_ref.dtype)


def dsa_attention(qidxT, sT, qlatT, kidx, ckv, ckvT, w_uv, bias3):
    T = DSA_T
    assert T == EVEN_T
    bsz, nq = kidx.shape[0], kidx.shape[1]
    seq = nq * T
    n_keep = min(DSA_TOPK, seq // 4)
    wuvt = jnp.transpose(w_uv, (1, 2, 0)).astype(BF16)
    return pl.pallas_call(
        functools.partial(_dsa_kernel, n_keep=n_keep),
        grid=(bsz, nq),
        in_specs=[pl.BlockSpec((1, IDX_HEADS * IDX_DIM, T), lambda b, i: (b, 0, i)),
                  pl.BlockSpec((1, SMALL_ROWS, T), lambda b, i: (b, 0, i)),
                  pl.BlockSpec((1, A_HEADS * A_KV_RANK, T), lambda b, i: (b, 0, i)),
                  pl.BlockSpec((1, nq, T, IDX_DIM), lambda b, i: (b, 0, 0, 0)),
                  pl.BlockSpec((1, nq, T, A_KV_RANK), lambda b, i: (b, 0, 0, 0)),
                  pl.BlockSpec((1, nq, A_KV_RANK, T), lambda b, i: (b, 0, 0, 0)),
                  pl.BlockSpec((3, A_HEADS, T, T), lambda b, i: (0, 0, 0, 0)),
                  pl.BlockSpec((A_HEADS, HEAD_DIM, A_KV_RANK), lambda b, i: (0, 0, 0))],
        out_specs=pl.BlockSpec((1, A_HEADS * HEAD_DIM, T), lambda b, i: (b, 0, i)),
        out_shape=jax.ShapeDtypeStruct((bsz, A_HEADS * HEAD_DIM, seq), BF16),
        scratch_shapes=([pltpu.VMEM((seq, T), I32), pltpu.VMEM((T, T), F32)] + [pltpu.VMEM((1, T), F32)] * (2 * A_HEADS)
                        + [pltpu.VMEM((A_KV_RANK, T), F32)] * A_HEADS),
        compiler_params=_cparams(("arbitrary", "arbitrary")),
        name="dsa_attention",
    )(qidxT, sT, qlatT, kidx, ckv, ckvT, bias3, wuvt)


def dsa_bias_tiles(rel_bias):
    assert DSA_T + 1 >= T5_FAR
    return bias_tiles(rel_bias, [0, DSA_T, 4 * DSA_T], A_HEADS, 0, DSA_T, DSA_T, -1, 1, False)


N_CMP_PAD = 256


def _compress_kernel(blk_ref, pos_ref, w1_ref, w2_ref, o_ref):
    x = (blk_ref[0].astype(F32) + pos_ref[...]).astype(BF16)
    hid = jax.nn.gelu(jnp.dot(x, w1_ref[...], preferred_element_type=F32))
    o_ref[0] = jnp.dot(hid.astype(BF16), w2_ref[...], preferred_element_type=F32).astype(o_ref.dtype)


def nsa_compress(a, pos, w1, w2, bsz, seq):
    n_chunk = seq // CMP_STRIDE
    assert CMP_LEN == 2 * CMP_STRIDE and n_chunk <= N_CMP_PAD
    width = CMP_STRIDE * HEAD_DIM
    chunks = a.reshape(bsz, n_chunk, CMP_STRIDE, B_GROUPS, HEAD_DIM).transpose(0, 3, 1, 2, 4)
    chunks = chunks.reshape(bsz * B_GROUPS, n_chunk, width)
    blocks = jnp.concatenate([chunks[:, :-1], chunks[:, 1:]], axis=-1)
    blocks = jnp.pad(blocks, ((0, 0), (0, N_CMP_PAD - (n_chunk - 1)), (0, 0)))
    out = pl.pallas_call(
        _compress_kernel,
        grid=(bsz * B_GROUPS,),
        in_specs=[pl.BlockSpec((1, N_CMP_PAD, 2 * width), lambda i: (i, 0, 0)),
                  pl.BlockSpec((1, 2 * width), lambda i: (0, 0)),
                  pl.BlockSpec((2 * width, HEAD_DIM), lambda i: (0, 0)),
                  pl.BlockSpec((HEAD_DIM, HEAD_DIM), lambda i: (0, 0))],
        out_specs=pl.BlockSpec((1, N_CMP_PAD, HEAD_DIM), lambda i: (i, 0, 0)),
        out_shape=jax.ShapeDtypeStruct((bsz * B_GROUPS, N_CMP_PAD, HEAD_DIM), BF16),
        compiler_params=_cparams(("arbitrary",)),
        name="nsa_compress",
    )(blocks, pos.reshape(1, 2 * width), w1.reshape(2 * width, HEAD_DIM).astype(BF16), w2.astype(BF16))
    return out.reshape(bsz, B_GROUPS, N_CMP_PAD, HEAD_DIM)


NSA_TQ = 128
NSA_L = B_HPG * NSA_TQ
NSA_KT = 128
NSA_SLC_REL = 3
NSA_WIN_REL = 5
NSA_FAR_SPLIT = 4


def _flash_step(s_all, vT_all, states):
    probs = []
    for item, (m_ref, l_ref, _) in zip(s_all, states):
        s, segs = item if isinstance(item, tuple) else (item, None)
        m_new, alpha, l_new, p = _flash_probs(s, m_ref[...], l_ref[...], segs)
        m_ref[...] = m_new
        l_ref[...] = l_new
        probs.append((alpha, p))
    for (alpha, p), vT, (_, _, acc_ref) in zip(probs, vT_all, states):
        acc_ref[...] = alpha * acc_ref[...] + jnp.dot(vT, p, preferred_element_type=F32)


def _flash_loop(lo, hi, scores, values, states, segs=None):
    def body(j, carry):
        s_all = scores(j)
        if segs is not None:
            s_all = list(zip(s_all, segs(j)))
        _flash_step(s_all, values(j), states)
        return carry

    lax.fori_loop(lo, hi, body, 0)


def _nsa_kernel(qT_ref, kc_ref, vcT_ref, biasc_ref, ovl_ref, ks_ref, kw_ref, vT_ref,
                toes_ref, toew_ref, sT_ref, o_ref, selb_ref, *st, n_cmp, n_sel, n_slc):
    qi = pl.program_id(1)
    TQ, L = NSA_TQ, NSA_L
    q0 = qi * TQ
    qTs, qTs_pad = [], []
    for g in range(B_GROUPS):
        q = jnp.concatenate([qT_ref[0, (g * B_HPG + n) * HEAD_DIM:(g * B_HPG + n + 1) * HEAD_DIM, :]
                             for n in range(B_HPG)], axis=1)
        parts = [jnp.zeros_like(q)] * B_GROUPS
        parts[g] = q
        qTs.append(q)
        qTs_pad.append(jnp.concatenate(parts, axis=0))
    t_lane = q0 + (lax.broadcasted_iota(I32, (1, L), 1) & (TQ - 1))

    o_cs = []
    for g in range(B_GROUPS):
        s = jnp.dot(kc_ref[0, g], qTs[g], preferred_element_type=F32) + biasc_ref[g, 0]
        i_idx = lax.broadcasted_iota(I32, (N_CMP_PAD, L), 0)
        valid = jnp.where(i_idx < n_cmp, i_idx * CMP_STRIDE + (CMP_LEN - 1), 2 ** 30) <= t_lane
        s = jnp.where(valid, s, NEG)
        m = jnp.max(s, axis=0, keepdims=True)
        p = jnp.where(valid, jnp.exp2(s - m), 0.0)
        l = jnp.sum(p, axis=0, keepdims=True)
        p_c = p / jnp.where(l > 0, l, 1.0)
        o_c = jnp.dot(vcT_ref[0, g], p_c.astype(BF16), preferred_element_type=F32)

        psum = p_c[:, 0:TQ]
        for n in range(1, B_HPG):
            psum = psum + p_c[:, n * TQ:(n + 1) * TQ]
        sc = jnp.dot(ovl_ref[...], psum, preferred_element_type=F32, precision=lax.Precision.HIGHEST)
        j_idx = lax.broadcasted_iota(I32, (n_slc, TQ), 0)
        cur = (q0 + lax.broadcasted_iota(I32, (1, TQ), 1)) // SLC_BLOCK
        adm = j_idx <= cur
        forced = (j_idx == 0) | (j_idx == cur) | (j_idx == cur - 1)
        scv = jnp.where(adm, jnp.where(forced, jnp.inf, sc), -jnp.inf)
        rank = jnp.zeros((n_slc, TQ), I32)
        for jp in range(n_slc):
            row = scv[jp:jp + 1, :]
            beats = jnp.where(row > scv, 1, jnp.where((row == scv) & (jp < j_idx), 1, 0))
            rank = rank + beats
        selb = jnp.where(rank < n_sel, 0.0, NEG).astype(F32)
        selb4 = jnp.concatenate([selb] * B_HPG, axis=1)
        for j in range(n_slc):
            selb_ref[g, j] = selb4[j:j + 1, :]
        o_cs.append(o_c)

    for ref in st[0::3]:
        ref[...] = jnp.full(ref.shape, NEG, F32)
    for ref in st[1::3] + st[2::3]:
        ref[...] = jnp.zeros(ref.shape, F32)
    slc_st = [st[6 * g:6 * g + 3] for g in range(B_GROUPS)]
    win_st = [st[6 * g + 3:6 * g + 6] for g in range(B_GROUPS)]
    n_main = 6 * B_GROUPS
    xtr_st = [[st[n_main + 3 * (g * (NSA_FAR_SPLIT - 1) + r):n_main + 3 * (g * (NSA_FAR_SPLIT - 1) + r) + 3]
               for r in range(NSA_FAR_SPLIT - 1)] for g in range(B_GROUPS)]
    per_kt = NSA_KT // SLC_BLOCK

    groups = range(B_GROUPS)
    far_bias = [toes_ref[g, NSA_SLC_REL - 1, 0:1, :] for g in groups]

    def slc_scores(g, jt, near):
        s = jnp.dot(ks_ref[0, jt], qTs_pad[g], preferred_element_type=F32)
        return s + toes_ref[g, jnp.minimum(qi - jt, NSA_SLC_REL - 1)] if near else s

    def slc_segs(g, jt, near):
        return [(SLC_BLOCK, selb_ref[g, per_kt * jt + r] + (0.0 if near else far_bias[g])) for r in range(per_kt)]

    def win_scores(g, jt):
        rel = jnp.minimum(qi - jt, NSA_WIN_REL - 1)
        return jnp.dot(kw_ref[0, jt], qTs_pad[g], preferred_element_type=F32) + toew_ref[g, rel]

    gd = B_GROUPS * HEAD_DIM
    v_slc = lambda g, jt: vT_ref[0, jt, g * HEAD_DIM:(g + 1) * HEAD_DIM, :]
    v_win = lambda g, jt: vT_ref[0, jt, gd + g * HEAD_DIM:gd + (g + 1) * HEAD_DIM, :]

    assert NSA_WIN_REL >= NSA_SLC_REL
    j_lo = jnp.maximum(qi - (NSA_WIN_REL - 1), 0)

    def far_body(i, carry):
        s_all, v_all, chains = [], [], []
        for r in range(NSA_FAR_SPLIT):
            jt_raw = NSA_FAR_SPLIT * i + r
            live = jt_raw < j_lo
            jt = jnp.minimum(jt_raw, j_lo - 1)
            for g in groups:
                segs = [(n, jnp.where(live, c, NEG)) for n, c in slc_segs(g, jt, False)]
                s_all.append((slc_scores(g, jt, False), segs))
                v_all.append(v_slc(g, jt))
                chains.append(slc_st[g] if r == 0 else xtr_st[g][r - 1])
        _flash_step(s_all, v_all, chains)
        return carry

    def near_body(jt, carry):
        s_all = ([(slc_scores(g, jt, True), slc_segs(g, jt, True)) for g in groups]
                 + [win_scores(g, jt) for g in groups])
        v_all = [v_slc(g, jt) for g in groups] + [v_win(g, jt) for g in groups]
        _flash_step(s_all, v_all, slc_st + win_st)
        return carry

    lax.fori_loop(0, (j_lo + NSA_FAR_SPLIT - 1) // NSA_FAR_SPLIT, far_body, 0)
    lax.fori_loop(j_lo, qi + 1, near_body, 0)

    for g in range(B_GROUPS):
        o_s = _flash_finish(*_flash_merge([slc_st[g]] + xtr_st[g]))
        o_w = _flash_finish(*[r[...] for r in win_st[g]])
        row0 = GATE_ROW0 + g * 3 * B_HPG
        gate = [jax.nn.sigmoid(jnp.concatenate([sT_ref[0, row0 + j * B_HPG + n:row0 + j * B_HPG + n + 1, :]
                                                for n in range(B_HPG)], axis=1)) for j in range(3)]
        o = (gate[0] * o_cs[g] + gate[1] * o_s + gate[2] * o_w).astype(o_ref.dtype)
        for n in range(B_HPG):
            o_ref[0, (g * B_HPG + n) * HEAD_DIM:(g * B_HPG + n + 1) * HEAD_DIM, :] = o[:, n * TQ:(n + 1) * TQ]


def nsa_bias_inputs(rel_bias, seq):
    TQ, L, KT = NSA_TQ, NSA_L, NSA_KT
    nq = seq // TQ
    bc = bias_tiles(rel_bias, [-(CMP_LEN - 1)], B_HEADS, A_HEADS, N_CMP_PAD, seq, -CMP_STRIDE, 1, False, 0)
    bc = bc.reshape(B_GROUPS, B_HPG, N_CMP_PAD, nq, TQ).transpose(0, 3, 2, 1, 4).reshape(B_GROUPS, nq, N_CMP_PAD, L)

    def lanes(t):
        v = t.shape[0]
        return t.reshape(v, B_GROUPS, B_HPG, KT, TQ).transpose(1, 0, 3, 2, 4).reshape(B_GROUPS, v, KT, L)

    assert KT == TQ and (NSA_SLC_REL - 1) * KT - (KT - 1) >= T5_FAR
    toe_s = bias_tiles(rel_bias, [v * KT for v in range(NSA_SLC_REL - 1)] + [64 * KT],
                       B_HEADS, A_HEADS, KT, TQ, -1, 1, True, 0)
    assert (NSA_WIN_REL - 1) * KT - (KT - 1) < WINDOW <= NSA_WIN_REL * KT - (KT - 1)
    toe_w = bias_tiles(rel_bias, [v * KT for v in range(NSA_WIN_REL)], B_HEADS, A_HEADS, KT, TQ, -1, 1, True, WINDOW)
    return bc, lanes(toe_s), lanes(toe_w)


def nsa_overlap(seq):
    n_cmp = (seq - CMP_LEN) // CMP_STRIDE + 1
    n_slc = seq // SLC_BLOCK
    cs = np.arange(N_CMP_PAD) * CMP_STRIDE
    ss = np.arange(n_slc) * SLC_BLOCK
    ov = ((cs[None, :] + CMP_LEN - 1 >= ss[:, None]) & (cs[None, :] <= ss[:, None] + SLC_BLOCK - 1)
          & (np.arange(N_CMP_PAD)[None, :] < n_cmp))
    return jnp.asarray(ov.astype(np.float32))


def nsa_attention(qbT, kc, vc, kslc, kwin, vT, sT, biasc, toe_s, toe_w):
    TQ, L, KT, G = NSA_TQ, NSA_L, NSA_KT, B_GROUPS
    bsz, n_kt = vT.shape[0], vT.shape[1]
    seq = n_kt * KT
    nq = seq // TQ
    n_slc = seq // SLC_BLOCK
    n_cmp = (seq - CMP_LEN) // CMP_STRIDE + 1
    n_sel = min(SLC_TOPN, n_slc)
    gd = G * HEAD_DIM
    vcT = vc.transpose(0, 1, 3, 2)
    once = pl.Buffered(1)
    k_spec = pl.BlockSpec((1, n_kt, KT, gd), lambda b, i: (b, 0, 0, 0))
    n_chain = 2 * G + G * (NSA_FAR_SPLIT - 1)
    return pl.pallas_call(
        functools.partial(_nsa_kernel, n_cmp=n_cmp, n_sel=n_sel, n_slc=n_slc),
        grid=(bsz, nq),
        in_specs=[pl.BlockSpec((1, B_HEADS * HEAD_DIM, TQ), lambda b, i: (b, 0, i)),
                  pl.BlockSpec((1, G, N_CMP_PAD, HEAD_DIM), lambda b, i: (b, 0, 0, 0)),
                  pl.BlockSpec((1, G, HEAD_DIM, N_CMP_PAD), lambda b, i: (b, 0, 0, 0)),
                  pl.BlockSpec((G, 1, N_CMP_PAD, L), lambda b, i: (0, i, 0, 0)),
                  pl.BlockSpec((n_slc, N_CMP_PAD), lambda b, i: (0, 0), pipeline_mode=once),
                  k_spec, k_spec,
                  pl.BlockSpec((1, n_kt, 2 * gd, KT), lambda b, i: (b, 0, 0, 0)),
                  pl.BlockSpec((G, NSA_SLC_REL, KT, L), lambda b, i: (0, 0, 0, 0), pipeline_mode=once),
                  pl.BlockSpec((G, NSA_WIN_REL, KT, L), lambda b, i: (0, 0, 0, 0), pipeline_mode=once),
                  pl.BlockSpec((1, SMALL_ROWS, TQ), lambda b, i: (b, 0, i))],
        out_specs=pl.BlockSpec((1, B_HEADS * HEAD_DIM, TQ), lambda b, i: (b, 0, i)),
        out_shape=jax.ShapeDtypeStruct((bsz, B_HEADS * HEAD_DIM, seq), BF16),
        scratch_shapes=([pltpu.VMEM((G, n_slc, 1, L), F32)]
                        + [pltpu.VMEM((1, L), F32), pltpu.VMEM((1, L), F32), pltpu.VMEM((HEAD_DIM, L), F32)] * n_chain),
        compiler_params=_cparams(("arbitrary", "arbitrary")),
        name="nsa_attention",
    )(qbT, kc, vcT, biasc, nsa_overlap(seq), kslc.reshape(bsz, n_kt, KT, gd), kwin.reshape(bsz, n_kt, KT, gd),
      vT, toe_s, toe_w, sT)


MOBA_T = MOBA_BLOCK


MOBA_HB = 8


PAIR = 2 * HEAD_DIM


def _moba_inproj_kernel(x_ref, wqT_ref, wk_ref, wvT_ref, qT_ref, k_ref, vT_ref):
    xb = x_ref[0].astype(BF16)
    nt = (((1,), (1,)), ((), ()))
    qT_ref[0] = lax.dot_general(wqT_ref[...], xb, nt, preferred_element_type=F32).astype(BF16)
    k_ref[0, 0] = jnp.dot(xb, wk_ref[...], preferred_element_type=F32).astype(BF16)
    vT_ref[0, 0] = lax.dot_general(wvT_ref[...], xb, nt, preferred_element_type=F32).astype(BF16)


def _pair_padded_qT(wq):
    n_heads = wq.shape[1] // HEAD_DIM
    wT = wq.T.reshape(n_heads, HEAD_DIM, wq.shape[0])
    z = jnp.zeros_like(wT)
    even = jnp.concatenate([wT, z], axis=1)
    odd = jnp.concatenate([z, wT], axis=1)
    is_even = (jnp.arange(n_heads) % 2 == 0)[:, None, None]
    return jnp.where(is_even, even, odd).reshape(n_heads * PAIR, wq.shape[0])


def moba_inproj(x3, w_in):
    bsz, seq, d = x3.shape
    T = MOBA_T
    n_blk = seq // T
    hd = C_HEADS * HEAD_DIM
    wqT = _pair_padded_qT(w_in[:, :hd] * QK_SCALE).astype(BF16)
    wk = w_in[:, hd:2 * hd].astype(BF16)
    wvT = w_in[:, 2 * hd:].T.astype(BF16)
    once = pl.Buffered(1)
    return pl.pallas_call(
        _moba_inproj_kernel,
        grid=(bsz, n_blk),
        in_specs=[pl.BlockSpec((1, T, d), lambda b, i: (b, i, 0)),
                  pl.BlockSpec(wqT.shape, lambda b, i: (0, 0), pipeline_mode=once),
                  pl.BlockSpec(wk.shape, lambda b, i: (0, 0), pipeline_mode=once),
                  pl.BlockSpec(wvT.shape, lambda b, i: (0, 0), pipeline_mode=once)],
        out_specs=[pl.BlockSpec((1, C_HEADS * PAIR, T), lambda b, i: (b, 0, i)),
                   pl.BlockSpec((1, 1, T, hd), lambda b, i: (b, i, 0, 0)),
                   pl.BlockSpec((1, 1, hd, T), lambda b, i: (b, i, 0, 0))],
        out_shape=[jax.ShapeDtypeStruct((bsz, C_HEADS * PAIR, seq), BF16),
                   jax.ShapeDtypeStruct((bsz, n_blk, T, hd), BF16),
                   jax.ShapeDtypeStruct((bsz, n_blk, hd, T), BF16)],
        compiler_params=_cparams(("arbitrary", "arbitrary")),
        name="moba_inproj",
    )(x3, wqT, wk, wvT)


def _moba_kernel(qT_ref, k_ref, vT_ref, bias_ref, o_ref, kmean_ref, selb_ref, *st, n_sel):
    qi = pl.program_id(2)
    T = MOBA_T
    n_blk = k_ref.shape[1]
    states = [st[3 * hh:3 * hh + 3] for hh in range(MOBA_HB)]

    @pl.when(qi == 0)
    def _():
        for j in range(n_blk):
            kmean_ref[j:j + 1, :] = jnp.mean(k_ref[0, j].astype(F32), axis=0, keepdims=True)

    qTs = [qT_ref[0, hh * PAIR:(hh + 1) * PAIR, :] for hh in range(MOBA_HB)]
    pair = lambda hh: slice((hh // 2) * PAIR, (hh // 2 + 1) * PAIR)
    j_idx = lax.broadcasted_iota(I32, (n_blk, T), 0)
    for hh in range(MOBA_HB):
        gate = jnp.dot(kmean_ref[:, pair(hh)], qTs[hh].astype(F32), preferred_element_type=F32,
                       precision=lax.Precision.HIGHEST)
        gv = jnp.where(j_idx < qi, gate, -jnp.inf)
        rank = jnp.zeros((n_blk, T), I32)
        for jp in range(n_blk):
            row = gv[jp:jp + 1, :]
            rank = rank + jnp.where(row > gv, 1, jnp.where((row == gv) & (jp < j_idx), 1, 0))
        selb = jnp.where(j_idx < qi, jnp.where(rank < n_sel, 0.0, NEG),
                         jnp.where(j_idx == qi, 0.0, NEG)).astype(F32)
        for j in range(n_blk):
            selb_ref[hh, j] = selb[j:j + 1, :]
        m_ref, l_ref, acc_ref = states[hh]
        m_ref[...] = jnp.full(m_ref.shape, NEG, F32)
        l_ref[...] = jnp.zeros(l_ref.shape, F32)
        acc_ref[...] = jnp.zeros(acc_ref.shape, F32)

    heads = range(MOBA_HB)
    far_bias = [bias_ref[2, hh, 0:1, :] for hh in heads]

    def far_scores(kb):
        return [jnp.dot(k_ref[0, kb, :, pair(hh)], qTs[hh], preferred_element_type=F32) for hh in heads]

    def far_segs(kb):
        return [[(T, selb_ref[hh, kb] + far_bias[hh])] for hh in heads]

    def near_scores(kb):
        return [jnp.dot(k_ref[0, kb, :, pair(hh)], qTs[hh], preferred_element_type=F32) + bias_ref[qi - kb, hh]
                for hh in heads]

    def near_segs(kb):
        return [[(T, selb_ref[hh, kb])] for hh in heads]

    def values(kb):
        return [vT_ref[0, kb, hh * HEAD_DIM:(hh + 1) * HEAD_DIM, :] for hh in heads]

    n_far = jnp.maximum(qi - 1, 0)
    _flash_loop(0, n_far, far_scores, values, states, far_segs)
    _flash_loop(n_far, qi + 1, near_scores, values, states, near_segs)
    for hh in range(MOBA_HB):
        o_ref[0, hh * HEAD_DIM:(hh + 1) * HEAD_DIM, :] = _flash_finish(*[r[...] for r in states[hh]]).astype(o_ref.dtype)


def moba_bias_tiles(rel_bias):
    assert MOBA_T + 1 >= T5_FAR
    t0 = bias_tiles(rel_bias, [0], C_HEADS, 0, MOBA_T, MOBA_T, -1, 1, True)
    t12 = bias_tiles(rel_bias, [MOBA_T, 4 * MOBA_T], C_HEADS, 0, MOBA_T, MOBA_T, -1, 1, False)
    return jnp.concatenate([t0, t12], axis=0)


def moba_attention(qT, k, vT, bias3):
    T = MOBA_T
    bsz, n_blk = k.shape[0], k.shape[1]
    seq = n_blk * T
    n_sel = min(MOBA_TOPK, n_blk - 1)
    HB = MOBA_HB
    assert HB % 2 == 0
    out = pl.pallas_call(
        functools.partial(_moba_kernel, n_sel=n_sel),
        grid=(bsz, C_HEADS // HB, n_blk),
        in_specs=[pl.BlockSpec((1, HB * PAIR, T), lambda b, h, i: (b, h, i)),
                  pl.BlockSpec((1, n_blk, T, HB * HEAD_DIM), lambda b, h, i: (b, 0, 0, h)),
                  pl.BlockSpec((1, n_blk, HB * HEAD_DIM, T), lambda b, h, i: (b, 0, h, 0)),
                  pl.BlockSpec((3, HB, T, T), lambda b, h, i: (0, h, 0, 0))],
        out_specs=pl.BlockSpec((1, HB * HEAD_DIM, T), lambda b, h, i: (b, h, i)),
        out_shape=jax.ShapeDtypeStruct((bsz, C_HEADS * HEAD_DIM, seq), BF16),
        scratch_shapes=([pltpu.VMEM((n_blk, HB * HEAD_DIM), F32), pltpu.VMEM((HB, n_blk, 1, T), F32)]
                        + [pltpu.VMEM((1, T), F32), pltpu.VMEM((1, T), F32), pltpu.VMEM((HEAD_DIM, T), F32)] * HB),
        compiler_params=_cparams(("arbitrary", "arbitrary", "arbitrary")),
        name="moba_attention",
    )(qT, k, vT, bias3)
    return out


def _projT_ln_kernel(*refs, n_in):
    aT_refs, w_refs = refs[:n_in], refs[n_in:2 * n_in]
    x_ref, g_ref, b_ref, o_ref = refs[2 * n_in:]
    tn = (((0,), (0,)), ((), ()))
    mix = lax.dot_general(aT_refs[0][0], w_refs[0][...], tn, preferred_element_type=F32)
    for aT_ref, w_ref in zip(aT_refs[1:], w_refs[1:]):
        mix = mix + lax.dot_general(aT_ref[0], w_ref[...], tn, preferred_element_type=F32)
    o_ref[0] = _layer_norm_rows(ALPHA * x_ref[0] + mix, g_ref[...], b_ref[...])


def projT_residual_ln(aTs, w, x3, g, b, tm=256):
    bsz, seq, d = x3.shape
    ws, k0 = [], 0
    for aT in aTs:
        ws.append(w[k0:k0 + aT.shape[1]])
        k0 += aT.shape[1]
    n_in = len(aTs)
    once = pl.Buffered(1)
    vec = pl.BlockSpec((1, d), lambda bb, i: (0, 0))
    return pl.pallas_call(
        functools.partial(_projT_ln_kernel, n_in=n_in),
        grid=(bsz, seq // tm),
        in_specs=([pl.BlockSpec((1, aT.shape[1], tm), lambda bb, i: (bb, 0, i)) for aT in aTs]
                  + [pl.BlockSpec(wi.shape, lambda bb, i: (0, 0), pipeline_mode=once) for wi in ws]
                  + [pl.BlockSpec((1, tm, d), lambda bb, i: (bb, i, 0)), vec, vec]),
        out_specs=pl.BlockSpec((1, tm, d), lambda bb, i: (bb, i, 0)),
        out_shape=jax.ShapeDtypeStruct((bsz, seq, d), F32),
        compiler_params=_cparams(("arbitrary", "arbitrary")),
        name="projT_residual_ln",
    )(*aTs, *ws, x3, g.reshape(1, d), b.reshape(1, d))


def _router_kernel(h_ref, w_ref, o_ref):
    o_ref[...] = jnp.dot(h_ref[...], w_ref[...], preferred_element_type=F32, precision=lax.Precision.HIGHEST)


def router_logits(h, router, tm=1024):
    m, d = h.shape
    lanes = 128
    w = jnp.pad(router, ((0, 0), (0, lanes - N_EXPERTS)))
    out = pl.pallas_call(
        _router_kernel,
        grid=(m // tm,),
        in_specs=[pl.BlockSpec((tm, d), lambda i: (i, 0)), pl.BlockSpec((d, lanes), lambda i: (0, 0))],
        out_specs=pl.BlockSpec((tm, lanes), lambda i: (i, 0)),
        out_shape=jax.ShapeDtypeStruct((m, lanes), F32),
        compiler_params=_cparams(("arbitrary",)),
        name="router_logits",
    )(h, w)
    return out[:, :N_EXPERTS]


def _add_ln_kernel(h_ref, y_ref, g_ref, b_ref, o_ref):
    o_ref[...] = _layer_norm_rows(ALPHA * h_ref[...] + y_ref[...].astype(F32), g_ref[...], b_ref[...])


def add_ln(h, y, g, b, tm=512):
    m, d = h.shape
    row = pl.BlockSpec((tm, d), lambda i: (i, 0))
    vec = pl.BlockSpec((1, d), lambda i: (0, 0))
    return pl.pallas_call(
        _add_ln_kernel, grid=(m // tm,), in_specs=[row, row, vec, vec], out_specs=row,
        out_shape=jax.ShapeDtypeStruct((m, d), F32),
        compiler_params=_cparams(("arbitrary",)), name="add_ln",
    )(h, y, g.reshape(1, d), b.reshape(1, d))


IDX_LANES = 128


def _issue_row_gather(idx_vmem_ref, idx_smem, sem_i, src_hbm, dst_slot_ref, sem_slot, n_rows):
    cp = pltpu.make_async_copy(idx_vmem_ref.at[0], idx_smem, sem_i)
    cp.start()
    cp.wait()

    for r in range(n_rows):
        row = idx_smem[r // IDX_LANES, r % IDX_LANES]
        pltpu.make_async_copy(src_hbm.at[pl.ds(row, 1)], dst_slot_ref.at[pl.ds(r, 1)], sem_slot).start()


def _pipelined_gather(idx0_ref, idxn_ref, idx_smem, sem_i, src_hbm, buf, sem_buf, n_rows):
    g = pl.program_id(0)
    slot = lax.rem(g, 2)

    @pl.when(g == 0)
    def _():
        _issue_row_gather(idx0_ref, idx_smem, sem_i, src_hbm, buf.at[0], sem_buf.at[0], n_rows)

    @pl.when(g + 1 < pl.num_programs(0))
    def _():
        _issue_row_gather(idxn_ref, idx_smem, sem_i, src_hbm, buf.at[1 - slot], sem_buf.at[1 - slot], n_rows)

    pltpu.make_async_copy(buf.at[slot], buf.at[slot], sem_buf.at[slot]).wait()
    return slot


def _gather_specs(n_steps, k):
    first = lambda g, *_: (0, 0, 0)
    nxt = lambda g, *_: (jnp.minimum(g + 1, n_steps - 1), 0, 0)
    return pl.BlockSpec((1, k, IDX_LANES), first), pl.BlockSpec((1, k, IDX_LANES), nxt)


def _moe_ffn_kernel(ge_ref, idx0_ref, idxn_ref, h_hbm, w1_ref, w3_ref, w2_ref, o_ref,
                    xbuf, idx_smem, sem_i, sem_x, *, ff_chunk):
    del ge_ref
    slot = _pipelined_gather(idx0_ref, idxn_ref, idx_smem, sem_i, h_hbm, xbuf, sem_x, EXPERT_ROWS)
    xb = xbuf[slot].astype(BF16)
    d_ff = w1_ref.shape[2]
    acc = jnp.zeros((EXPERT_ROWS, w2_ref.shape[2]), F32)
    for c in range(0, d_ff, ff_chunk):
        a = jnp.dot(xb, w1_ref[0, :, c:c + ff_chunk], preferred_element_type=F32)
        u = jnp.dot(xb, w3_ref[0, :, c:c + ff_chunk], preferred_element_type=F32)
        hid = (a * jax.nn.sigmoid(a) * u).astype(BF16)
        acc = acc + jnp.dot(hid, w2_ref[0, c:c + ff_chunk, :], preferred_element_type=F32)
    o_ref[...] = acc


def moe_expert_ffn(h, row_tok, grp_e, w1, w3, w2, ff_chunk=512):
    d = h.shape[1]
    d_ff = w1.shape[2]
    n_groups = grp_e.shape[0]
    k = EXPERT_ROWS // IDX_LANES
    idx = row_tok.reshape(n_groups, k, IDX_LANES)
    once = pl.Buffered(1)
    idx0_spec, idxn_spec = _gather_specs(n_groups, k)
    grid_spec = pltpu.PrefetchScalarGridSpec(
        num_scalar_prefetch=1,
        grid=(n_groups,),
        in_specs=[idx0_spec, idxn_spec, pl.BlockSpec(memory_space=pl.ANY),
                  pl.BlockSpec((1, d, d_ff), lambda g, ge: (ge[g], 0, 0), pipeline_mode=once),
                  pl.BlockSpec((1, d, d_ff), lambda g, ge: (ge[g], 0, 0), pipeline_mode=once),
                  pl.BlockSpec((1, d_ff, d), lambda g, ge: (ge[g], 0, 0), pipeline_mode=once)],
        out_specs=pl.BlockSpec((EXPERT_ROWS, d), lambda g, ge: (g, 0)),
        scratch_shapes=[pltpu.VMEM((2, EXPERT_ROWS, d), F32), pltpu.SMEM((k, IDX_LANES), I32),
                        pltpu.SemaphoreType.DMA(()), pltpu.SemaphoreType.DMA((2,))],
    )
    return pl.pallas_call(
        functools.partial(_moe_ffn_kernel, ff_chunk=ff_chunk),
        grid_spec=grid_spec,
        out_shape=jax.ShapeDtypeStruct((n_groups * EXPERT_ROWS, d), F32),
        compiler_params=_cparams(("arbitrary",)),
        name="moe_expert_ffn",
    )(grp_e, idx, idx, h, w1, w3, w2)


COMBINE_TM = 256


def _moe_combine_ln_kernel(idx0_ref, idxn_ref, y_hbm, h_ref, gate_ref, g_ref, b_ref, o_ref,
                           ybuf, idx_smem, sem_i, sem_y):
    tm = COMBINE_TM
    slot = _pipelined_gather(idx0_ref, idxn_ref, idx_smem, sem_i, y_hbm, ybuf, sem_y, TOP_K * tm)
    y = gate_ref[:, 0:1] * ybuf[slot, 0:tm, :]
    for j in range(1, TOP_K):
        y = y + gate_ref[:, j:j + 1] * ybuf[slot, j * tm:(j + 1) * tm, :]
    o_ref[...] = _layer_norm_rows(ALPHA * h_ref[...] + y, g_ref[...], b_ref[...])


def moe_combine_ln(h, y_rows, dest, gate, g, b):
    m, d = h.shape
    tm = COMBINE_TM
    n_tiles = m // tm
    k = TOP_K * tm // IDX_LANES
    idx = dest.reshape(n_tiles, tm, TOP_K).transpose(0, 2, 1).reshape(n_tiles, k, IDX_LANES)
    idx0_spec, idxn_spec = _gather_specs(n_tiles, k)
    row = pl.BlockSpec((tm, d), lambda i: (i, 0))
    vec = pl.BlockSpec((1, d), lambda i: (0, 0))
    return pl.pallas_call(
        _moe_combine_ln_kernel,
        grid=(n_tiles,),
        in_specs=[idx0_spec, idxn_spec, pl.BlockSpec(memory_space=pl.ANY), row,
                  pl.BlockSpec((tm, TOP_K), lambda i: (i, 0)), vec, vec],
        out_specs=row,
        out_shape=jax.ShapeDtypeStruct((m, d), F32),
        scratch_shapes=[pltpu.VMEM((2, TOP_K * tm, d), F32), pltpu.SMEM((k, IDX_LANES), I32),
                        pltpu.SemaphoreType.DMA(()), pltpu.SemaphoreType.DMA((2,))],
        compiler_params=_cparams(("arbitrary",)),
        name="moe_combine_ln",
    )(idx, idx, y_rows, h, gate, g.reshape(1, d), b.reshape(1, d))


def moe_dispatch_plan(logits):
    n_tok = logits.shape[0]
    top_val, top_e = lax.top_k(logits, TOP_K)
    gate = jax.nn.softmax(top_val, axis=-1)
    e_flat = top_e.reshape(-1)
    onehot = (e_flat[:, None] == jnp.arange(N_EXPERTS, dtype=e_flat.dtype)[None, :]).astype(I32)
    rank = jnp.take_along_axis(jnp.cumsum(onehot, axis=0) - onehot, e_flat[:, None], axis=1)[:, 0]
    counts = jnp.sum(onehot, axis=0)
    padded = (counts + EXPERT_ROWS - 1) // EXPERT_ROWS * EXPERT_ROWS
    pend = jnp.cumsum(padded)
    pstart = pend - padded
    dest = pstart[e_flat] + rank
    n_assign = n_tok * TOP_K
    n_rows = -(-n_assign // EXPERT_ROWS) * EXPERT_ROWS + N_EXPERTS * EXPERT_ROWS
    n_groups = n_rows // EXPERT_ROWS
    tok_flat = jnp.repeat(jnp.arange(n_tok, dtype=I32), TOP_K)
    row_tok = jnp.zeros((n_rows,), I32).at[dest].set(tok_flat)
    grp_e = jnp.minimum(jnp.searchsorted(pend, jnp.arange(n_groups, dtype=I32) * EXPERT_ROWS, side='right'),
                        N_EXPERTS - 1).astype(I32)
    return gate, dest.astype(I32), row_tok, grp_e


def kernel(x, rel_bias, e_w_in, e_q_norm, e_kv_norm, e_w_uq, e_w_uk, e_w_uv, e_w_qidx, e_pos_k, e_pos_v, e_ck1, e_ck2, e_cv1, e_cv2, e_w_out, e_ln1_g, e_ln1_b, e_ffn_w1, e_ffn_w3, e_ffn_w2, e_ln2_g, e_ln2_b, o_w_in, o_w_out, o_ln1_g, o_ln1_b, o_router, o_moe_w1, o_moe_w3, o_moe_w2, o_ln2_g, o_ln2_b):
    bsz, seq, d = x.shape
    m = bsz * seq
    xf = x.reshape(m, d)
    dsa_bias = dsa_bias_tiles(rel_bias)
    nsa_bc, nsa_toe_s, nsa_toe_w = nsa_bias_inputs(rel_bias, seq)
    moba_bias = moba_bias_tiles(rel_bias)
    gd = B_GROUPS * HEAD_DIM
    for layer in range(DEPTH):
        i = layer // 2
        if layer % 2 == 0:
            x3 = xf.reshape(bsz, seq, d)
            (qidxT, qlatT, sT, kidx, ckv, ckvT, qbT, kcmp, vcmp, kslc, kwin, vT) = even_inproj(
                x3, e_w_in[i], e_q_norm[i], e_kv_norm[i], e_w_uq[i], e_w_uk[i], e_w_qidx[i])
            o_aT = dsa_attention(qidxT, sT, qlatT, kidx, ckv, ckvT, e_w_uv[i], dsa_bias)
            kc = nsa_compress(kcmp.reshape(m, gd), e_pos_k[i], e_ck1[i], e_ck2[i], bsz, seq)
            vc = nsa_compress(vcmp.reshape(m, gd), e_pos_v[i], e_cv1[i], e_cv2[i], bsz, seq)
            o_bT = nsa_attention(qbT, kc, vc, kslc, kwin, vT, sT, nsa_bc, nsa_toe_s, nsa_toe_w)
            h = projT_residual_ln([o_aT, o_bT], e_w_out[i].astype(BF16), x3, e_ln1_g[i], e_ln1_b[i]).reshape(m, d)
            tm = 512
            xf = swiglu_ffn(h, jnp.zeros((m // tm,), I32), e_ffn_w1[i][None].astype(BF16),
                            e_ffn_w3[i][None].astype(BF16), e_ffn_w2[i][None].astype(BF16),
                            e_ln2_g[i], e_ln2_b[i], with_ln=True, out_dtype=F32, tm=tm, ff_chunk=1408)
        else:
            x3 = xf.reshape(bsz, seq, d)
            o_cT = moba_attention(*moba_inproj(x3, o_w_in[i]), moba_bias)
            h = projT_residual_ln([o_cT], o_w_out[i].astype(BF16), x3, o_ln1_g[i], o_ln1_b[i]).reshape(m, d)
            gate, dest, row_tok, grp_e = moe_dispatch_plan(router_logits(h, o_router[i]))
            y_rows = moe_expert_ffn(h, row_tok, grp_e, o_moe_w1[i].astype(BF16), o_moe_w3[i].astype(BF16),
                                    o_moe_w2[i].astype(BF16))
            xf = moe_combine_ln(h, y_rows, dest, gate, o_ln2_g[i], o_ln2_b[i])
    return xf.reshape(bsz, seq, d)
```

```python
import functools
import math

import numpy as np
import jax
import jax.numpy as jnp
from jax import lax
from jax.experimental import pallas as pl
from jax.experimental.pallas import tpu as pltpu

F32 = jnp.float32
BF16 = jnp.bfloat16
I32 = jnp.int32
BF16_ROWS = 16

HEAD_DIM = 64
NUM_BUCKETS = 32
MAX_DISTANCE = 128
N_BIAS_HEADS = 16
A_HEADS = 8
A_Q_RANK = 256
A_KV_RANK = 128
IDX_HEADS = 16
IDX_DIM = 64
DSA_TOPK = 256
B_HEADS = 8
B_GROUPS = 2
B_HPG = B_HEADS // B_GROUPS
CMP_LEN = 32
CMP_STRIDE = 16
SLC_BLOCK = 64
SLC_TOPN = 16
WINDOW = 512
C_HEADS = 16
MOBA_BLOCK = 256
MOBA_TOPK = 3
N_EXPERTS = 8
TOP_K = 2
EXPERT_ROWS = 256
DEPTH = 2
ALPHA = (2 * DEPTH) ** 0.25

LOG2E = 1.4426950408889634
QK_SCALE = HEAD_DIM ** -0.5 * LOG2E
NEG = -1e30
NEG_HALF = -5e29
INT_MIN = -2 ** 31
VMEM_LIMIT = 56 * 1024 * 1024


def _t5_thresholds():
    def bucket(n):
        if n < NUM_BUCKETS // 2:
            return n
        v = np.log(np.float32(n) / np.float32(NUM_BUCKETS // 2)) / np.float32(math.log(MAX_DISTANCE / (NUM_BUCKETS // 2)))
        return min(NUM_BUCKETS // 2 + int(np.float32(v) * (NUM_BUCKETS - NUM_BUCKETS // 2)), NUM_BUCKETS - 1)
    b = [bucket(i) for i in range(4 * MAX_DISTANCE)]
    return [0] + [min(i for i in range(len(b)) if b[i] >= k) for k in range(1, NUM_BUCKETS)]


T5_THR = _t5_thresholds()
T5_FAR = T5_THR[-1]


def _cparams(sem):
    return pltpu.CompilerParams(dimension_semantics=sem, vmem_limit_bytes=VMEM_LIMIT)


def _bias_kernel(tab_ref, off_ref, o_ref, *, c_row, c_col, h0, causal_neg, window):
    v = pl.program_id(0)
    h = pl.program_id(1) + h0
    shape = o_ref.shape[2:]
    dist = (c_col * lax.broadcasted_iota(I32, shape, 1) + c_row * lax.broadcasted_iota(I32, shape, 0) + off_ref[v])
    n = jnp.maximum(dist, 0)
    acc = jnp.full(shape, tab_ref[h] * LOG2E, F32)
    for k in range(1, NUM_BUCKETS):
        acc = jnp.where(n >= T5_THR[k], tab_ref[k * N_BIAS_HEADS + h] * LOG2E, acc)
    if causal_neg:
        acc = jnp.where(dist >= 0, acc, NEG)
    if window:
        acc = jnp.where(dist < window, acc, NEG)
    o_ref[0, 0] = acc


def bias_tiles(rel_bias, offs, n_heads, h0, rows, cols, c_row, c_col, causal_neg, window=0):
    offs = jnp.asarray(offs, I32)
    nv = offs.shape[0]
    return pl.pallas_call(
        functools.partial(_bias_kernel, c_row=c_row, c_col=c_col, h0=h0, causal_neg=causal_neg, window=window),
        grid=(nv, n_heads),
        in_specs=[pl.BlockSpec(memory_space=pltpu.SMEM), pl.BlockSpec(memory_space=pltpu.SMEM)],
        out_specs=pl.BlockSpec((1, 1, rows, cols), lambda v, h: (v, h, 0, 0)),
        out_shape=jax.ShapeDtypeStruct((nv, n_heads, rows, cols), F32),
        compiler_params=_cparams(("arbitrary", "arbitrary")),
        name="t5_bias_tiles",
    )(rel_bias.reshape(-1), offs)


def _mm_kernel(x_ref, w_ref, o_ref):
    o_ref[...] = jnp.dot(x_ref[...].astype(BF16), w_ref[...].astype(BF16),
                         preferred_element_type=F32).astype(o_ref.dtype)


def matmul(x, w, out_dtype, tm=512):
    m, k = x.shape
    n = w.shape[1]
    tm = min(tm, m)
    return pl.pallas_call(
        _mm_kernel,
        grid=(m // tm,),
        in_specs=[pl.BlockSpec((tm, k), lambda i: (i, 0)), pl.BlockSpec((k, n), lambda i: (0, 0))],
        out_specs=pl.BlockSpec((tm, n), lambda i: (i, 0)),
        out_shape=jax.ShapeDtypeStruct((m, n), out_dtype),
        compiler_params=_cparams(("arbitrary",)),
        name="matmul",
    )(x, w)


def _layer_norm_rows(z, g, b):
    mu = jnp.mean(z, axis=-1, keepdims=True)
    zc = z - mu
    var = jnp.mean(zc * zc, axis=-1, keepdims=True)
    return zc * lax.rsqrt(var + 1e-5) * g + b


def _proj_ln_kernel(a_ref, w_ref, x_ref, g_ref, b_ref, o_ref):
    mix = jnp.dot(a_ref[...], w_ref[...], preferred_element_type=F32)
    o_ref[...] = _layer_norm_rows(ALPHA * x_ref[...] + mix, g_ref[...], b_ref[...])


def proj_residual_ln(a, w, x, g, b, tm=512):
    m, k = a.shape
    d = w.shape[1]
    tm = min(tm, m)
    return pl.pallas_call(
        _proj_ln_kernel,
        grid=(m // tm,),
        in_specs=[pl.BlockSpec((tm, k), lambda i: (i, 0)), pl.BlockSpec((k, d), lambda i: (0, 0)),
                  pl.BlockSpec((tm, d), lambda i: (i, 0)),
                  pl.BlockSpec((1, d), lambda i: (0, 0)), pl.BlockSpec((1, d), lambda i: (0, 0))],
        out_specs=pl.BlockSpec((tm, d), lambda i: (i, 0)),
        out_shape=jax.ShapeDtypeStruct((m, d), F32),
        compiler_params=_cparams(("arbitrary",)),
        name="proj_residual_ln",
    )(a, w, x, g.reshape(1, d), b.reshape(1, d))


def _ffn_kernel(ge_ref, x_ref, w1_ref, w3_ref, w2_ref, g_ref, b_ref, o_ref, *, ff_chunk, with_ln):
    del ge_ref
    x = x_ref[...]
    xb = x.astype(BF16)
    d_ff = w1_ref.shape[2]
    acc = jnp.zeros((x.shape[0], w2_ref.shape[2]), F32)
    for c in range(0, d_ff, ff_chunk):
        a = jnp.dot(xb, w1_ref[0, :, c:c + ff_chunk], preferred_element_type=F32)
        u = jnp.dot(xb, w3_ref[0, :, c:c + ff_chunk], preferred_element_type=F32)
        hid = (a * jax.nn.sigmoid(a) * u).astype(BF16)
        acc = acc + jnp.dot(hid, w2_ref[0, c:c + ff_chunk, :], preferred_element_type=F32)
    if with_ln:
        o_ref[...] = _layer_norm_rows(ALPHA * x.astype(F32) + acc, g_ref[...], b_ref[...]).astype(o_ref.dtype)
    else:
        o_ref[...] = acc.astype(o_ref.dtype)


def swiglu_ffn(x_rows, grp_e, w1, w3, w2, ln_g, ln_b, *, with_ln, out_dtype, tm, ff_chunk):
    m, d = x_rows.shape
    d_ff = w1.shape[2]
    once = pl.Buffered(1)
    grid_spec = pltpu.PrefetchScalarGridSpec(
        num_scalar_prefetch=1,
        grid=(m // tm,),
        in_specs=[pl.BlockSpec((tm, d), lambda i, ge: (i, 0)),
                  pl.BlockSpec((1, d, d_ff), lambda i, ge: (ge[i], 0, 0), pipeline_mode=once),
                  pl.BlockSpec((1, d, d_ff), lambda i, ge: (ge[i], 0, 0), pipeline_mode=once),
                  pl.BlockSpec((1, d_ff, d), lambda i, ge: (ge[i], 0, 0), pipeline_mode=once),
                  pl.BlockSpec((1, d), lambda i, ge: (0, 0)), pl.BlockSpec((1, d), lambda i, ge: (0, 0))],
        out_specs=pl.BlockSpec((tm, d), lambda i, ge: (i, 0)),
    )
    return pl.pallas_call(
        functools.partial(_ffn_kernel, ff_chunk=ff_chunk, with_ln=with_ln),
        grid_spec=grid_spec,
        out_shape=jax.ShapeDtypeStruct((m, d), out_dtype),
        compiler_params=_cparams(("arbitrary",)),
        name="swiglu_ffn",
    )(grp_e, x_rows, w1, w3, w2, ln_g.reshape(1, d), ln_b.reshape(1, d))


def _flash_probs(s, m, segs=None):
    if segs is None:
        m_new = jnp.maximum(m, jnp.max(s, axis=0, keepdims=True))
        p = jnp.exp2(s - m_new)
    else:
        m_new, r0 = m, 0
        for n, c in segs:
            seg_max = jnp.max(s[r0:r0 + n], axis=0, keepdims=True)
            m_new = jnp.maximum(m_new, jnp.where(c > NEG_HALF, seg_max + c, NEG))
            r0 += n
        parts, r0 = [], 0
        for n, c in segs:
            shift = jnp.where(c > NEG_HALF, m_new - c, -NEG)
            parts.append(jnp.exp2(s[r0:r0 + n] - shift))
            r0 += n
        p = parts[0] if len(parts) == 1 else jnp.concatenate(parts, axis=0)
    return m_new, jnp.exp2(m - m_new), p.astype(BF16)


def _flash_merge(states):
    ms = [st[0][...] for st in states]
    m = functools.reduce(jnp.maximum, ms)
    ws = [jnp.exp2(mi - m) for mi in ms]
    l = sum(w * st[1][...] for w, st in zip(ws, states))
    acc = sum(w * st[2][...] for w, st in zip(ws, states))
    return m, l, acc


def _flash_finish(m, l, acc):
    return jnp.where(m > NEG_HALF, acc / l, 0.0)


def _rms_rows(x, g):
    return x * lax.rsqrt(jnp.mean(x * x, axis=-1, keepdims=True) + 1e-6) * g


EVEN_T = 256
SMALL_ROWS = 48
GATE_ROW0 = IDX_HEADS
NT_DIMS = (((1,), (1,)), ((), ()))


def _even_inproj_kernel(x_ref, wa_ref, wsT_ref, qn_ref, kvn_ref, wuq_ref, wuk_ref, wqiT_ref, wqbT_ref, wk4_ref, wvT_ref,
                        qidxT_ref, qlatT_ref, sT_ref, kidx_ref, ckv_ref, ckvT_ref,
                        qbT_ref, kcmp_ref, vcmp_ref, kslc_ref, kwin_ref, vT_ref):
    xb = x_ref[0].astype(BF16)
    ya = jnp.dot(xb, wa_ref[...], preferred_element_type=F32)
    cqn = _rms_rows(ya[:, :A_Q_RANK], qn_ref[...]).astype(BF16)
    ckvn = _rms_rows(ya[:, A_Q_RANK:A_Q_RANK + A_KV_RANK], kvn_ref[...])
    kidx_ref[0, 0] = ya[:, A_Q_RANK + A_KV_RANK:A_Q_RANK + A_KV_RANK + IDX_DIM].astype(BF16)
    ckv_ref[0, 0] = ckvn.astype(BF16)
    ckvT_ref[0, 0] = ckvn.T.astype(BF16)
    sT_ref[0] = lax.dot_general(wsT_ref[...], xb, NT_DIMS, preferred_element_type=F32)
    q = jnp.dot(cqn, wuq_ref[...], preferred_element_type=F32).astype(BF16)
    for h in range(A_HEADS):
        qlT = lax.dot_general(wuk_ref[h], q[:, h * HEAD_DIM:(h + 1) * HEAD_DIM], NT_DIMS, preferred_element_type=F32)
        qlatT_ref[0, h * A_KV_RANK:(h + 1) * A_KV_RANK, :] = (qlT * QK_SCALE).astype(BF16)
    qidxT_ref[0] = lax.dot_general(wqiT_ref[...], cqn, NT_DIMS, preferred_element_type=F32).astype(BF16)
    qbT_ref[0] = lax.dot_general(wqbT_ref[...], xb, NT_DIMS, preferred_element_type=F32).astype(BF16)
    yk = jnp.dot(xb, wk4_ref[...], preferred_element_type=F32).astype(BF16)
    gd = B_GROUPS * HEAD_DIM
    for j, ref in enumerate((kcmp_ref, vcmp_ref, kslc_ref, kwin_ref)):
        ref[0] = yk[:, j * gd:(j + 1) * gd]
    vT = lax.dot_general(wvT_ref[...], xb, NT_DIMS, preferred_element_type=F32).astype(BF16)
    for j in range(EVEN_T // NSA_KT):
        vT_ref[0, j] = vT[:, j * NSA_KT:(j + 1) * NSA_KT]


def even_inproj(x3, w_in, q_norm, kv_norm, w_uq, w_uk, w_qidx):
    bsz, seq, d = x3.shape
    T = EVEN_T
    nq = seq // T
    gd = B_GROUPS * HEAD_DIM
    n_kt = seq // NSA_KT
    o_kidx = A_Q_RANK + A_KV_RANK
    o_widx = o_kidx + IDX_DIM
    o_qb = o_widx + IDX_HEADS
    o_kv = o_qb + B_HEADS * HEAD_DIM
    o_gate = o_kv + 6 * gd
    kv = lambda j: w_in[:, o_kv + j * gd:o_kv + (j + 1) * gd]
    wa = jnp.pad(w_in[:, :o_widx], ((0, 0), (0, 512 - o_widx))).astype(BF16)
    w_gate = w_in[:, o_gate:].reshape(d, B_GROUPS, B_HPG, 3).transpose(0, 1, 3, 2).reshape(d, 3 * B_HEADS)
    wsT = jnp.concatenate([w_in[:, o_widx:o_qb] * IDX_HEADS ** -0.5, w_gate,
                           jnp.zeros((d, SMALL_ROWS - IDX_HEADS - 3 * B_HEADS), w_in.dtype)], axis=1).T.astype(BF16)
    wuq = w_uq.reshape(A_Q_RANK, A_HEADS * HEAD_DIM).astype(BF16)
    wuk = jnp.transpose(w_uk, (1, 0, 2)).astype(BF16)
    wqiT = w_qidx.reshape(A_Q_RANK, IDX_HEADS * IDX_DIM).T.astype(BF16)
    wqbT = (w_in[:, o_qb:o_kv] * QK_SCALE).T.astype(BF16)
    wk4 = jnp.concatenate([kv(0), kv(1), kv(2), kv(4)], axis=1).astype(BF16)
    wvT = jnp.concatenate([kv(3), kv(5)], axis=1).T.astype(BF16)
    weights = (wa, wsT, q_norm.reshape(1, -1), kv_norm.reshape(1, -1), wuq, wuk, wqiT, wqbT, wk4, wvT)
    once = pl.Buffered(1)
    w_specs = [pl.BlockSpec(w.shape, (lambda b, i, n=w.ndim: (0,) * n), pipeline_mode=once) for w in weights]
    fm = lambda rows: pl.BlockSpec((1, rows, T), lambda b, i: (b, 0, i))
    tok = lambda cols: pl.BlockSpec((1, T, cols), lambda b, i: (b, i, 0))
    blk = lambda r, c: pl.BlockSpec((1, 1, r, c), lambda b, i: (b, i, 0, 0))
    sds = jax.ShapeDtypeStruct
    return pl.pallas_call(
        _even_inproj_kernel,
        grid=(bsz, nq),
        in_specs=[pl.BlockSpec((1, T, d), lambda b, i: (b, i, 0))] + w_specs,
        out_specs=[fm(IDX_HEADS * IDX_DIM), fm(A_HEADS * A_KV_RANK), fm(SMALL_ROWS),
                   blk(T, IDX_DIM), blk(T, A_KV_RANK), blk(A_KV_RANK, T),
                   fm(B_HEADS * HEAD_DIM), tok(gd), tok(gd), tok(gd), tok(gd),
                   pl.BlockSpec((1, T // NSA_KT, 2 * gd, NSA_KT), lambda b, i: (b, i, 0, 0))],
        out_shape=[sds((bsz, IDX_HEADS * IDX_DIM, seq), BF16), sds((bsz, A_HEADS * A_KV_RANK, seq), BF16),
                   sds((bsz, SMALL_ROWS, seq), F32),
                   sds((bsz, nq, T, IDX_DIM), BF16), sds((bsz, nq, T, A_KV_RANK), BF16), sds((bsz, nq, A_KV_RANK, T), BF16),
                   sds((bsz, B_HEADS * HEAD_DIM, seq), BF16),
                   sds((bsz, seq, gd), BF16), sds((bsz, seq, gd), BF16), sds((bsz, seq, gd), BF16), sds((bsz, seq, gd), BF16),
                   sds((bsz, n_kt, 2 * gd, NSA_KT), BF16)],
        compiler_params=_cparams(("arbitrary", "arbitrary")),
        name="even_inproj",
    )(x3, *weights)


DSA_T = 256
SUB = 128


def _dsa_kernel(qidx_ref, wT_ref, qlat_ref, kidx_ref, ckv_ref, ckvT_ref, bias_ref, wuvt_ref, o_ref,
                key_ref, selb0_ref, selb1_ref, *state_refs, n_keep):
    selb_refs = (selb0_ref, selb1_ref)
    m_refs, l_refs, acc_refs = (state_refs[0:A_HEADS], state_refs[A_HEADS:2 * A_HEADS], state_refs[2 * A_HEADS:])
    qi = pl.program_id(1)
    nkb = qi + 1
    T = DSA_T

    def score_block(kb, carry):
        for sub in range(T // SUB):
            k = kidx_ref[0, kb, sub * SUB:(sub + 1) * SUB, :]
            acc = jnp.zeros((SUB, T), F32)
            for h in range(IDX_HEADS):
                d = jnp.dot(k, qidx_ref[0, h * IDX_DIM:(h + 1) * IDX_DIM, :], preferred_element_type=F32)
                acc = acc + jnp.maximum(d, 0.0) * wT_ref[0, h:h + 1, :]
            bits = lax.bitcast_convert_type(acc, I32)
            key = bits ^ (lax.shift_right_arithmetic(bits, 31) & 0x7FFFFFFF)
            s_pos = kb * T + sub * SUB + lax.broadcasted_iota(I32, (SUB, T), 0)
            t_pos = qi * T + lax.broadcasted_iota(I32, (SUB, T), 1)
            key = jnp.where(s_pos <= t_pos, key, INT_MIN)
            key_ref[pl.ds(pl.multiple_of(kb * T + sub * SUB, SUB), SUB), :] = key
        return carry

    lax.fori_loop(0, nkb, score_block, 0)

    def count_ge(cand):
        def body(kb, cnt):
            blk = key_ref[pl.ds(pl.multiple_of(kb * T, T), T), :]
            ge = jnp.where(blk >= cand, 1, 0).astype(I32)
            return cnt + jnp.sum(ge.reshape(T // 8, 8, T), axis=0)
        cnt = lax.fori_loop(0, nkb, body, jnp.zeros((8, T), I32))
        return jnp.sum(cnt, axis=0, keepdims=True)

    def bit_step(i, u):
        cand_u = u | lax.shift_left(jnp.int32(1), 31 - i)
        cnt = count_ge(cand_u ^ INT_MIN)
        return jnp.where(cnt >= n_keep, cand_u, u)

    u = lax.fori_loop(0, 32, bit_step, jnp.zeros((1, T), I32))
    thr = jnp.maximum(u ^ INT_MIN, INT_MIN + 1)

    for h in range(A_HEADS):
        m_refs[h][...] = jnp.full(m_refs[h].shape, NEG, F32)
        l_refs[h][...] = jnp.zeros(l_refs[h].shape, F32)
        acc_refs[h][...] = jnp.zeros(acc_refs[h].shape, F32)

    states = list(zip(m_refs, l_refs, acc_refs))
    far_bias = [bias_ref[2, h, 0:1, :] for h in range(A_HEADS)]

    def masked_scores(kb, selb):
        selb[...] = jnp.where(key_ref[pl.ds(pl.multiple_of(kb * T, T), T), :] >= thr, 0.0, NEG)
        ckv = ckv_ref[0, kb]
        return [jnp.dot(ckv, qlat_ref[0, h * A_KV_RANK:(h + 1) * A_KV_RANK, :], preferred_element_type=F32) + selb[...]
                for h in range(A_HEADS)]

    def far_body(i, carry):
        tiles = []
        for u, selb in enumerate(selb_refs):
            kb_raw = len(selb_refs) * i + u
            live = kb_raw < n_far
            kb = jnp.minimum(kb_raw, n_far - 1)
            segs = [[(T, jnp.where(live, far_bias[h], NEG))] for h in range(A_HEADS)]
            tiles.append((list(zip(masked_scores(kb, selb), segs)), [ckvT_ref[0, kb]] * A_HEADS))
        for s_all, v_all in tiles:
            _flash_step(s_all, v_all, states)
        return carry

    def near_body(kb, carry):
        s_all = [s + bias_ref[qi - kb, h] for h, s in enumerate(masked_scores(kb, selb_refs[0]))]
        _flash_step(s_all, [ckvT_ref[0, kb]] * A_HEADS, states)
        return carry

    n_far = jnp.maximum(qi - 1, 0)
    lax.fori_loop(0, lax.div(n_far + (len(selb_refs) - 1), jnp.int32(len(selb_refs))), far_body, 0)
    lax.fori_loop(n_far, nkb, near_body, 0)

    for h in range(A_HEADS):
        o_lat = _flash_finish(m_refs[h][...], l_refs[h][...], acc_refs[h][...]).astype(BF16)
        o_ref[0, h * HEAD_DIM:(h + 1) * HEAD_DIM, :] = jnp.dot(
            wuvt_ref[h], o_lat, preferred_element_type=F32).astype(o_ref.dtype)


def dsa_attention(qidxT, sT, qlatT, kidx, ckv, ckvT, w_uv, bias3):
    T = DSA_T
    assert T == EVEN_T
    bsz, nq = kidx.shape[0], kidx.shape[1]
    seq = nq * T
    n_keep = min(DSA_TOPK, seq // 4)
    wuvt = jnp.transpose(w_uv, (1, 2, 0)).astype(BF16)
    return pl.pallas_call(
        functools.partial(_dsa_kernel, n_keep=n_keep),
        grid=(bsz, nq),
        in_specs=[pl.BlockSpec((1, IDX_HEADS * IDX_DIM, T), lambda b, i: (b, 0, i)),
                  pl.BlockSpec((1, SMALL_ROWS, T), lambda b, i: (b, 0, i)),
                  pl.BlockSpec((1, A_HEADS * A_KV_RANK, T), lambda b, i: (b, 0, i)),
                  pl.BlockSpec((1, nq, T, IDX_DIM), lambda b, i: (b, 0, 0, 0)),
                  pl.BlockSpec((1, nq, T, A_KV_RANK), lambda b, i: (b, 0, 0, 0)),
                  pl.BlockSpec((1, nq, A_KV_RANK, T), lambda b, i: (b, 0, 0, 0)),
                  pl.BlockSpec((3, A_HEADS, T, T), lambda b, i: (0, 0, 0, 0)),
                  pl.BlockSpec((A_HEADS, HEAD_DIM, A_KV_RANK), lambda b, i: (0, 0, 0))],
        out_specs=pl.BlockSpec((1, A_HEADS * HEAD_DIM, T), lambda b, i: (b, 0, i)),
        out_shape=jax.ShapeDtypeStruct((bsz, A_HEADS * HEAD_DIM, seq), BF16),
        scratch_shapes=([pltpu.VMEM((seq, T), I32), pltpu.VMEM((T, T), F32), pltpu.VMEM((T, T), F32)]
                        + [pltpu.VMEM((1, T), F32)] * (2 * A_HEADS)
                        + [pltpu.VMEM((A_KV_RANK, T), F32)] * A_HEADS),
        compiler_params=_cparams(("arbitrary", "arbitrary")),
        name="dsa_attention",
    )(qidxT, sT, qlatT, kidx, ckv, ckvT, bias3, wuvt)


def dsa_bias_tiles(rel_bias):
    assert DSA_T + 1 >= T5_FAR
    return bias_tiles(rel_bias, [0, DSA_T, 4 * DSA_T], A_HEADS, 0, DSA_T, DSA_T, -1, 1, False)


N_CMP_PAD = 256


def _compress_kernel(blk_ref, pos_ref, w1_ref, w2_ref, o_ref):
    x = (blk_ref[0].astype(F32) + pos_ref[...]).astype(BF16)
    hid = jax.nn.gelu(jnp.dot(x, w1_ref[...], preferred_element_type=F32))
    o_ref[0] = jnp.dot(hid.astype(BF16), w2_ref[...], preferred_element_type=F32).astype(o_ref.dtype)


def nsa_compress(a, pos, w1, w2, bsz, seq):
    n_chunk = seq // CMP_STRIDE
    assert CMP_LEN == 2 * CMP_STRIDE and n_chunk <= N_CMP_PAD
    width = CMP_STRIDE * HEAD_DIM
    chunks = a.reshape(bsz, n_chunk, CMP_STRIDE, B_GROUPS, HEAD_DIM).transpose(0, 3, 1, 2, 4)
    chunks = chunks.reshape(bsz * B_GROUPS, n_chunk, width)
    blocks = jnp.concatenate([chunks[:, :-1], chunks[:, 1:]], axis=-1)
    blocks = jnp.pad(blocks, ((0, 0), (0, N_CMP_PAD - (n_chunk - 1)), (0, 0)))
    out = pl.pallas_call(
        _compress_kernel,
        grid=(bsz * B_GROUPS,),
        in_specs=[pl.BlockSpec((1, N_CMP_PAD, 2 * width), lambda i: (i, 0, 0)),
                  pl.BlockSpec((1, 2 * width), lambda i: (0, 0)),
                  pl.BlockSpec((2 * width, HEAD_DIM), lambda i: (0, 0)),
                  pl.BlockSpec((HEAD_DIM, HEAD_DIM), lambda i: (0, 0))],
        out_specs=pl.BlockSpec((1, N_CMP_PAD, HEAD_DIM), lambda i: (i, 0, 0)),
        out_shape=jax.ShapeDtypeStruct((bsz * B_GROUPS, N_CMP_PAD, HEAD_DIM), BF16),
        compiler_params=_cparams(("arbitrary",)),
        name="nsa_compress",
    )(blocks, pos.reshape(1, 2 * width), w1.reshape(2 * width, HEAD_DIM).astype(BF16), w2.astype(BF16))
    return out.reshape(bsz, B_GROUPS, N_CMP_PAD, HEAD_DIM)


NSA_TQ = 128
NSA_L = B_HPG * NSA_TQ
NSA_KT = 128
NSA_SLC_REL = 3
NSA_WIN_REL = 5
NSA_FAR_SPLIT = 4


def _flash_step(s_all, vT_all, states):
    probs = []
    for item, (m_ref, _, _) in zip(s_all, states):
        s, segs = item if isinstance(item, tuple) else (item, None)
        m_new, alpha, p = _flash_probs(s, m_ref[...], segs)
        m_ref[...] = m_new
        probs.append((alpha, p))
    for (alpha, p), vT, (_, l_ref, acc_ref) in zip(probs, vT_all, states):
        d = vT.shape[0]
        ones = jnp.ones((BF16_ROWS, vT.shape[1]), BF16)
        pv = jnp.dot(jnp.concatenate([vT, ones], axis=0), p, preferred_element_type=F32)
        acc_ref[...] = alpha * acc_ref[...] + pv[:d]
        l_ref[...] = alpha * l_ref[...] + pv[d:d + 1]


def _flash_loop(lo, hi, scores, values, states, segs, unroll=2):
    def body(i, carry):
        tiles = []
        for u in range(unroll):
            j_raw = lo + unroll * i + u
            live = j_raw < hi
            j = jnp.minimum(j_raw, hi - 1)
            sg = [[(n, jnp.where(live, c, NEG)) for n, c in chain] for chain in segs(j)]
            tiles.append((list(zip(scores(j), sg)), values(j)))
        for s_all, v_all in tiles:
            _flash_step(s_all, v_all, states)
        return carry

    lax.fori_loop(0, lax.div(hi - lo + (unroll - 1), jnp.int32(unroll)), body, 0)


def _nsa_kernel(qT_ref, kc_ref, vcT_ref, biasc_ref, ovl_ref, ks_ref, kw_ref, vT_ref,
                toes_ref, toew_ref, sT_ref, o_ref, selb_ref, *st, n_cmp, n_sel, n_slc):
    qi = pl.program_id(1)
    TQ, L = NSA_TQ, NSA_L
    q0 = qi * TQ
    qTs, qTs_pad = [], []
    for g in range(B_GROUPS):
        q = jnp.concatenate([qT_ref[0, (g * B_HPG + n) * HEAD_DIM:(g * B_HPG + n + 1) * HEAD_DIM, :]
                             for n in range(B_HPG)], axis=1)
        parts = [jnp.zeros_like(q)] * B_GROUPS
        parts[g] = q
        qTs.append(q)
        qTs_pad.append(jnp.concatenate(parts, axis=0))
    t_lane = q0 + (lax.broadcasted_iota(I32, (1, L), 1) & (TQ - 1))

    o_cs = []
    for g in range(B_GROUPS):
        s = jnp.dot(kc_ref[0, g], qTs[g], preferred_element_type=F32) + biasc_ref[g, 0]
        i_idx = lax.broadcasted_iota(I32, (N_CMP_PAD, L), 0)
        valid = jnp.where(i_idx < n_cmp, i_idx * CMP_STRIDE + (CMP_LEN - 1), 2 ** 30) <= t_lane
        s = jnp.where(valid, s, NEG)
        m = jnp.max(s, axis=0, keepdims=True)
        p = jnp.where(valid, jnp.exp2(s - m), 0.0)
        l = jnp.sum(p, axis=0, keepdims=True)
        p_c = p / jnp.where(l > 0, l, 1.0)
        o_c = jnp.dot(vcT_ref[0, g], p_c.astype(BF16), preferred_element_type=F32)

        psum = p_c[:, 0:TQ]
        for n in range(1, B_HPG):
            psum = psum + p_c[:, n * TQ:(n + 1) * TQ]
        sc = jnp.dot(ovl_ref[...], psum, preferred_element_type=F32, precision=lax.Precision.HIGHEST)
        j_idx = lax.broadcasted_iota(I32, (n_slc, TQ), 0)
        cur = (q0 + lax.broadcasted_iota(I32, (1, TQ), 1)) // SLC_BLOCK
        adm = j_idx <= cur
        forced = (j_idx == 0) | (j_idx == cur) | (j_idx == cur - 1)
        scv = jnp.where(adm, jnp.where(forced, jnp.inf, sc), -jnp.inf)
        rank = jnp.zeros((n_slc, TQ), I32)
        for jp in range(n_slc):
            row = scv[jp:jp + 1, :]
            beats = jnp.where(row > scv, 1, jnp.where((row == scv) & (jp < j_idx), 1, 0))
            rank = rank + beats
        selb = jnp.where(rank < n_sel, 0.0, NEG).astype(F32)
        selb4 = jnp.concatenate([selb] * B_HPG, axis=1)
        for j in range(n_slc):
            selb_ref[g, j] = selb4[j:j + 1, :]
        o_cs.append(o_c)

    for ref in st[0::3]:
        ref[...] = jnp.full(ref.shape, NEG, F32)
    for ref in st[1::3] + st[2::3]:
        ref[...] = jnp.zeros(ref.shape, F32)
    slc_st = [st[6 * g:6 * g + 3] for g in range(B_GROUPS)]
    win_st = [st[6 * g + 3:6 * g + 6] for g in range(B_GROUPS)]
    n_main = 6 * B_GROUPS
    xtr_st = [[st[n_main + 3 * (g * (NSA_FAR_SPLIT - 1) + r):n_main + 3 * (g * (NSA_FAR_SPLIT - 1) + r) + 3]
               for r in range(NSA_FAR_SPLIT - 1)] for g in range(B_GROUPS)]
    per_kt = NSA_KT // SLC_BLOCK

    groups = range(B_GROUPS)
    far_bias = [toes_ref[g, NSA_SLC_REL - 1, 0:1, :] for g in groups]

    def slc_scores(g, jt, near):
        s = jnp.dot(ks_ref[0, jt], qTs_pad[g], preferred_element_type=F32)
        return s + toes_ref[g, jnp.minimum(qi - jt, NSA_SLC_REL - 1)] if near else s

    def slc_segs(g, jt, near):
        return [(SLC_BLOCK, selb_ref[g, per_kt * jt + r] + (0.0 if near else far_bias[g])) for r in range(per_kt)]

    def win_scores(g, jt):
        rel = jnp.minimum(qi - jt, NSA_WIN_REL - 1)
        return jnp.dot(kw_ref[0, jt], qTs_pad[g], preferred_element_type=F32) + toew_ref[g, rel]

    gd = B_GROUPS * HEAD_DIM
    v_slc = lambda g, jt: vT_ref[0, jt, g * HEAD_DIM:(g + 1) * HEAD_DIM, :]
    v_win = lambda g, jt: vT_ref[0, jt, gd + g * HEAD_DIM:gd + (g + 1) * HEAD_DIM, :]

    assert NSA_WIN_REL >= NSA_SLC_REL
    j_lo = jnp.maximum(qi - (NSA_WIN_REL - 1), 0)

    def far_body(i, carry):
        s_all, v_all, chains = [], [], []
        for r in range(NSA_FAR_SPLIT):
            jt_raw = NSA_FAR_SPLIT * i + r
            live = jt_raw < j_lo
            jt = jnp.minimum(jt_raw, j_lo - 1)
            for g in groups:
                segs = [(n, jnp.where(live, c, NEG)) for n, c in slc_segs(g, jt, False)]
                s_all.append((slc_scores(g, jt, False), segs))
                v_all.append(v_slc(g, jt))
                chains.append(slc_st[g] if r == 0 else xtr_st[g][r - 1])
        _flash_step(s_all, v_all, chains)
        return carry

    def near_body(jt, carry):
        s_all = ([(slc_scores(g, jt, True), slc_segs(g, jt, True)) for g in groups]
                 + [win_scores(g, jt) for g in groups])
        v_all = [v_slc(g, jt) for g in groups] + [v_win(g, jt) for g in groups]
        _flash_step(s_all, v_all, slc_st + win_st)
        return carry

    lax.fori_loop(0, (j_lo + NSA_FAR_SPLIT - 1) // NSA_FAR_SPLIT, far_body, 0)
    lax.fori_loop(j_lo, qi + 1, near_body, 0)

    for g in range(B_GROUPS):
        o_s = _flash_finish(*_flash_merge([slc_st[g]] + xtr_st[g]))
        o_w = _flash_finish(*[r[...] for r in win_st[g]])
        row0 = GATE_ROW0 + g * 3 * B_HPG
        gate = [jax.nn.sigmoid(jnp.concatenate([sT_ref[0, row0 + j * B_HPG + n:row0 + j * B_HPG + n + 1, :]
                                                for n in range(B_HPG)], axis=1)) for j in range(3)]
        o = (gate[0] * o_cs[g] + gate[1] * o_s + gate[2] * o_w).astype(o_ref.dtype)
        for n in range(B_HPG):
            o_ref[0, (g * B_HPG + n) * HEAD_DIM:(g * B_HPG + n + 1) * HEAD_DIM, :] = o[:, n * TQ:(n + 1) * TQ]


def nsa_bias_inputs(rel_bias, seq):
    TQ, L, KT = NSA_TQ, NSA_L, NSA_KT
    nq = seq // TQ
    bc = bias_tiles(rel_bias, [-(CMP_LEN - 1)], B_HEADS, A_HEADS, N_CMP_PAD, seq, -CMP_STRIDE, 1, False, 0)
    bc = bc.reshape(B_GROUPS, B_HPG, N_CMP_PAD, nq, TQ).transpose(0, 3, 2, 1, 4).reshape(B_GROUPS, nq, N_CMP_PAD, L)

    def lanes(t):
        v = t.shape[0]
        return t.reshape(v, B_GROUPS, B_HPG, KT, TQ).transpose(1, 0, 3, 2, 4).reshape(B_GROUPS, v, KT, L)

    assert KT == TQ and (NSA_SLC_REL - 1) * KT - (KT - 1) >= T5_FAR
    toe_s = bias_tiles(rel_bias, [v * KT for v in range(NSA_SLC_REL - 1)] + [64 * KT],
                       B_HEADS, A_HEADS, KT, TQ, -1, 1, True, 0)
    assert (NSA_WIN_REL - 1) * KT - (KT - 1) < WINDOW <= NSA_WIN_REL * KT - (KT - 1)
    toe_w = bias_tiles(rel_bias, [v * KT for v in range(NSA_WIN_REL)], B_HEADS, A_HEADS, KT, TQ, -1, 1, True, WINDOW)
    return bc, lanes(toe_s), lanes(toe_w)


def nsa_overlap(seq):
    n_cmp = (seq - CMP_LEN) // CMP_STRIDE + 1
    n_slc = seq // SLC_BLOCK
    cs = np.arange(N_CMP_PAD) * CMP_STRIDE
    ss = np.arange(n_slc) * SLC_BLOCK
    ov = ((cs[None, :] + CMP_LEN - 1 >= ss[:, None]) & (cs[None, :] <= ss[:, None] + SLC_BLOCK - 1)
          & (np.arange(N_CMP_PAD)[None, :] < n_cmp))
    return jnp.asarray(ov.astype(np.float32))


def nsa_attention(qbT, kc, vc, kslc, kwin, vT, sT, biasc, toe_s, toe_w):
    TQ, L, KT, G = NSA_TQ, NSA_L, NSA_KT, B_GROUPS
    bsz, n_kt = vT.shape[0], vT.shape[1]
    seq = n_kt * KT
    nq = seq // TQ
    n_slc = seq // SLC_BLOCK
    n_cmp = (seq - CMP_LEN) // CMP_STRIDE + 1
    n_sel = min(SLC_TOPN, n_slc)
    gd = G * HEAD_DIM
    vcT = vc.transpose(0, 1, 3, 2)
    once = pl.Buffered(1)
    k_spec = pl.BlockSpec((1, n_kt, KT, gd), lambda b, i: (b, 0, 0, 0))
    n_chain = 2 * G + G * (NSA_FAR_SPLIT - 1)
    return pl.pallas_call(
        functools.partial(_nsa_kernel, n_cmp=n_cmp, n_sel=n_sel, n_slc=n_slc),
        grid=(bsz, nq),
        in_specs=[pl.BlockSpec((1, B_HEADS * HEAD_DIM, TQ), lambda b, i: (b, 0, i)),
                  pl.BlockSpec((1, G, N_CMP_PAD, HEAD_DIM), lambda b, i: (b, 0, 0, 0)),
                  pl.BlockSpec((1, G, HEAD_DIM, N_CMP_PAD), lambda b, i: (b, 0, 0, 0)),
                  pl.BlockSpec((G, 1, N_CMP_PAD, L), lambda b, i: (0, i, 0, 0)),
                  pl.BlockSpec((n_slc, N_CMP_PAD), lambda b, i: (0, 0), pipeline_mode=once),
                  k_spec, k_spec,
                  pl.BlockSpec((1, n_kt, 2 * gd, KT), lambda b, i: (b, 0, 0, 0)),
                  pl.BlockSpec((G, NSA_SLC_REL, KT, L), lambda b, i: (0, 0, 0, 0), pipeline_mode=once),
                  pl.BlockSpec((G, NSA_WIN_REL, KT, L), lambda b, i: (0, 0, 0, 0), pipeline_mode=once),
                  pl.BlockSpec((1, SMALL_ROWS, TQ), lambda b, i: (b, 0, i))],
        out_specs=pl.BlockSpec((1, B_HEADS * HEAD_DIM, TQ), lambda b, i: (b, 0, i)),
        out_shape=jax.ShapeDtypeStruct((bsz, B_HEADS * HEAD_DIM, seq), BF16),
        scratch_shapes=([pltpu.VMEM((G, n_slc, 1, L), F32)]
                        + [pltpu.VMEM((1, L), F32), pltpu.VMEM((1, L), F32), pltpu.VMEM((HEAD_DIM, L), F32)] * n_chain),
        compiler_params=_cparams(("arbitrary", "arbitrary")),
        name="nsa_attention",
    )(qbT, kc, vcT, biasc, nsa_overlap(seq), kslc.reshape(bsz, n_kt, KT, gd), kwin.reshape(bsz, n_kt, KT, gd),
      vT, toe_s, toe_w, sT)


MOBA_T = MOBA_BLOCK


MOBA_HB = 8


PAIR = 2 * HEAD_DIM


def _moba_inproj_kernel(x_ref, wqT_ref, wk_ref, wvT_ref, qT_ref, k_ref, vT_ref):
    xb = x_ref[0].astype(BF16)
    nt = (((1,), (1,)), ((), ()))
    qT_ref[0] = lax.dot_general(wqT_ref[...], xb, nt, preferred_element_type=F32).astype(BF16)
    k_ref[0, 0] = jnp.dot(xb, wk_ref[...], preferred_element_type=F32).astype(BF16)
    vT_ref[0, 0] = lax.dot_general(wvT_ref[...], xb, nt, preferred_element_type=F32).astype(BF16)


def _pair_padded_qT(wq):
    n_heads = wq.shape[1] // HEAD_DIM
    wT = wq.T.reshape(n_heads, HEAD_DIM, wq.shape[0])
    z = jnp.zeros_like(wT)
    even = jnp.concatenate([wT, z], axis=1)
    odd = jnp.concatenate([z, wT], axis=1)
    is_even = (jnp.arange(n_heads) % 2 == 0)[:, None, None]
    return jnp.where(is_even, even, odd).reshape(n_heads * PAIR, wq.shape[0])


def moba_inproj(x3, w_in):
    bsz, seq, d = x3.shape
    T = MOBA_T
    n_blk = seq // T
    hd = C_HEADS * HEAD_DIM
    wqT = _pair_padded_qT(w_in[:, :hd] * QK_SCALE).astype(BF16)
    wk = w_in[:, hd:2 * hd].astype(BF16)
    wvT = w_in[:, 2 * hd:].T.astype(BF16)
    once = pl.Buffered(1)
    return pl.pallas_call(
        _moba_inproj_kernel,
        grid=(bsz, n_blk),
        in_specs=[pl.BlockSpec((1, T, d), lambda b, i: (b, i, 0)),
                  pl.BlockSpec(wqT.shape, lambda b, i: (0, 0), pipeline_mode=once),
                  pl.BlockSpec(wk.shape, lambda b, i: (0, 0), pipeline_mode=once),
                  pl.BlockSpec(wvT.shape, lambda b, i: (0, 0), pipeline_mode=once)],
        out_specs=[pl.BlockSpec((1, C_HEADS * PAIR, T), lambda b, i: (b, 0, i)),
                   pl.BlockSpec((1, 1, T, hd), lambda b, i: (b, i, 0, 0)),
                   pl.BlockSpec((1, 1, hd, T), lambda b, i: (b, i, 0, 0))],
        out_shape=[jax.ShapeDtypeStruct((bsz, C_HEADS * PAIR, seq), BF16),
                   jax.ShapeDtypeStruct((bsz, n_blk, T, hd), BF16),
                   jax.ShapeDtypeStruct((bsz, n_blk, hd, T), BF16)],
        compiler_params=_cparams(("arbitrary", "arbitrary")),
        name="moba_inproj",
    )(x3, wqT, wk, wvT)


def _moba_kernel(qT_ref, k_ref, vT_ref, bias_ref, o_ref, kmean_ref, selb_ref, *st, n_sel):
    qi = pl.program_id(2)
    T = MOBA_T
    n_blk = k_ref.shape[1]
    states = [st[3 * hh:3 * hh + 3] for hh in range(MOBA_HB)]

    @pl.when(qi == 0)
    def _():
        for j in range(n_blk):
            kmean_ref[j:j + 1, :] = jnp.mean(k_ref[0, j].astype(F32), axis=0, keepdims=True)

    qTs = [qT_ref[0, hh * PAIR:(hh + 1) * PAIR, :] for hh in range(MOBA_HB)]
    pair = lambda hh: slice((hh // 2) * PAIR, (hh // 2 + 1) * PAIR)
    j_idx = lax.broadcasted_iota(I32, (n_blk, T), 0)
    for hh in range(MOBA_HB):
        gate = jnp.dot(kmean_ref[:, pair(hh)], qTs[hh].astype(F32), preferred_element_type=F32,
                       precision=lax.Precision.HIGHEST)
        gv = jnp.where(j_idx < qi, gate, -jnp.inf)
        rank = jnp.zeros((n_blk, T), I32)
        for jp in range(n_blk):
            row = gv[jp:jp + 1, :]
            rank = rank + jnp.where(row > gv, 1, jnp.where((row == gv) & (jp < j_idx), 1, 0))
        selb = jnp.where(j_idx < qi, jnp.where(rank < n_sel, 0.0, NEG),
                         jnp.where(j_idx == qi, 0.0, NEG)).astype(F32)
        for j in range(n_blk):
            selb_ref[hh, j] = selb[j:j + 1, :]
        m_ref, l_ref, acc_ref = states[hh]
        m_ref[...] = jnp.full(m_ref.shape, NEG, F32)
        l_ref[...] = jnp.zeros(l_ref.shape, F32)
        acc_ref[...] = jnp.zeros(acc_ref.shape, F32)

    heads = range(MOBA_HB)
    far_bias = [bias_ref[2, hh, 0:1, :] for hh in heads]

    def far_scores(kb):
        return [jnp.dot(k_ref[0, kb, :, pair(hh)], qTs[hh], preferred_element_type=F32) for hh in heads]

    def far_segs(kb):
        return [[(T, selb_ref[hh, kb] + far_bias[hh])] for hh in heads]

    def near_scores(kb):
        return [jnp.dot(k_ref[0, kb, :, pair(hh)], qTs[hh], preferred_element_type=F32) + bias_ref[qi - kb, hh]
                for hh in heads]

    def near_segs(kb):
        return [[(T, selb_ref[hh, kb])] for hh in heads]

    def values(kb):
        return [vT_ref[0, kb, hh * HEAD_DIM:(hh + 1) * HEAD_DIM, :] for hh in heads]

    n_far = jnp.maximum(qi - 1, 0)
    _flash_loop(0, n_far, far_scores, values, states, far_segs)
    _flash_loop(n_far, qi + 1, near_scores, values, states, near_segs)
    for hh in range(MOBA_HB):
        o_ref[0, hh * HEAD_DIM:(hh + 1) * HEAD_DIM, :] = _flash_finish(*[r[...] for r in states[hh]]).astype(o_ref.dtype)


def moba_bias_tiles(rel_bias):
    assert MOBA_T + 1 >= T5_FAR
    t0 = bias_tiles(rel_bias, [0], C_HEADS, 0, MOBA_T, MOBA_T, -1, 1, True)
    t12 = bias_tiles(rel_bias, [MOBA_T, 4 * MOBA_T], C_HEADS, 0, MOBA_T, MOBA_T, -1, 1, False)
    return jnp.concatenate([t0, t12], axis=0)


def moba_attention(qT, k, vT, bias3):
    T = MOBA_T
    bsz, n_blk = k.shape[0], k.shape[1]
    seq = n_blk * T
    n_sel = min(MOBA_TOPK, n_blk - 1)
    HB = MOBA_HB
    assert HB % 2 == 0
    out = pl.pallas_call(
        functools.partial(_moba_kernel, n_sel=n_sel),
        grid=(bsz, C_HEADS // HB, n_blk),
        in_specs=[pl.BlockSpec((1, HB * PAIR, T), lambda b, h, i: (b, h, i)),
                  pl.BlockSpec((1, n_blk, T, HB * HEAD_DIM), lambda b, h, i: (b, 0, 0, h)),
                  pl.BlockSpec((1, n_blk, HB * HEAD_DIM, T), lambda b, h, i: (b, 0, h, 0)),
                  pl.BlockSpec((3, HB, T, T), lambda b, h, i: (0, h, 0, 0))],
        out_specs=pl.BlockSpec((1, HB * HEAD_DIM, T), lambda b, h, i: (b, h, i)),
        out_shape=jax.ShapeDtypeStruct((bsz, C_HEADS * HEAD_DIM, seq), BF16),
        scratch_shapes=([pltpu.VMEM((n_blk, HB * HEAD_DIM), F32), pltpu.VMEM((HB, n_blk, 1, T), F32)]
                        + [pltpu.VMEM((1, T), F32), pltpu.VMEM((1, T), F32), pltpu.VMEM((HEAD_DIM, T), F32)] * HB),
        compiler_params=_cparams(("arbitrary", "arbitrary", "arbitrary")),
        name="moba_attention",
    )(qT, k, vT, bias3)
    return out


def _projT_ln_kernel(*refs, n_in):
    aT_refs, w_refs = refs[:n_in], refs[n_in:2 * n_in]
    x_ref, g_ref, b_ref, o_ref = refs[2 * n_in:]
    tn = (((0,), (0,)), ((), ()))
    mix = lax.dot_general(aT_refs[0][0], w_refs[0][...], tn, preferred_element_type=F32)
    for aT_ref, w_ref in zip(aT_refs[1:], w_refs[1:]):
        mix = mix + lax.dot_general(aT_ref[0], w_ref[...], tn, preferred_element_type=F32)
    o_ref[0] = _layer_norm_rows(ALPHA * x_ref[0] + mix, g_ref[...], b_ref[...])


def projT_residual_ln(aTs, w, x3, g, b, tm=256):
    bsz, seq, d = x3.shape
    ws, k0 = [], 0
    for aT in aTs:
        ws.append(w[k0:k0 + aT.shape[1]])
        k0 += aT.shape[1]
    n_in = len(aTs)
    once = pl.Buffered(1)
    vec = pl.BlockSpec((1, d), lambda bb, i: (0, 0))
    return pl.pallas_call(
        functools.partial(_projT_ln_kernel, n_in=n_in),
        grid=(bsz, seq // tm),
        in_specs=([pl.BlockSpec((1, aT.shape[1], tm), lambda bb, i: (bb, 0, i)) for aT in aTs]
                  + [pl.BlockSpec(wi.shape, lambda bb, i: (0, 0), pipeline_mode=once) for wi in ws]
                  + [pl.BlockSpec((1, tm, d), lambda bb, i: (bb, i, 0)), vec, vec]),
        out_specs=pl.BlockSpec((1, tm, d), lambda bb, i: (bb, i, 0)),
        out_shape=jax.ShapeDtypeStruct((bsz, seq, d), F32),
        compiler_params=_cparams(("arbitrary", "arbitrary")),
        name="projT_residual_ln",
    )(*aTs, *ws, x3, g.reshape(1, d), b.reshape(1, d))


def _router_kernel(h_ref, w_ref, o_ref):
    o_ref[...] = jnp.dot(h_ref[...], w_ref[...], preferred_element_type=F32, precision=lax.Precision.HIGHEST)


def router_logits(h, router, tm=1024):
    m, d = h.shape
    lanes = 128
    w = jnp.pad(router, ((0, 0), (0, lanes - N_EXPERTS)))
    out = pl.pallas_call(
        _router_kernel,
        grid=(m // tm,),
        in_specs=[pl.BlockSpec((tm, d), lambda i: (i, 0)), pl.BlockSpec((d, lanes), lambda i: (0, 0))],
        out_specs=pl.BlockSpec((tm, lanes), lambda i: (i, 0)),
        out_shape=jax.ShapeDtypeStruct((m, lanes), F32),
        compiler_params=_cparams(("arbitrary",)),
        name="router_logits",
    )(h, w)
    return out[:, :N_EXPERTS]


def _add_ln_kernel(h_ref, y_ref, g_ref, b_ref, o_ref):
    o_ref[...] = _layer_norm_rows(ALPHA * h_ref[...] + y_ref[...].astype(F32), g_ref[...], b_ref[...])


def add_ln(h, y, g, b, tm=512):
    m, d = h.shape
    row = pl.BlockSpec((tm, d), lambda i: (i, 0))
    vec = pl.BlockSpec((1, d), lambda i: (0, 0))
    return pl.pallas_call(
        _add_ln_kernel, grid=(m // tm,), in_specs=[row, row, vec, vec], out_specs=row,
        out_shape=jax.ShapeDtypeStruct((m, d), F32),
        compiler_params=_cparams(("arbitrary",)), name="add_ln",
    )(h, y, g.reshape(1, d), b.reshape(1, d))


IDX_LANES = 128


def _issue_row_gather(idx_vmem_ref, idx_smem, sem_i, src_hbm, dst_slot_ref, sem_slot, n_rows):
    cp = pltpu.make_async_copy(idx_vmem_ref.at[0], idx_smem, sem_i)
    cp.start()
    cp.wait()

    for r in range(n_rows):
        row = idx_smem[r // IDX_LANES, r % IDX_LANES]
        pltpu.make_async_copy(src_hbm.at[pl.ds(row, 1)], dst_slot_ref.at[pl.ds(r, 1)], sem_slot).start()


def _pipelined_gather(idx0_ref, idxn_ref, idx_smem, sem_i, src_hbm, buf, sem_buf, n_rows):
    g = pl.program_id(0)
    slot = lax.rem(g, 2)

    @pl.when(g == 0)
    def _():
        _issue_row_gather(idx0_ref, idx_smem, sem_i, src_hbm, buf.at[0], sem_buf.at[0], n_rows)

    @pl.when(g + 1 < pl.num_programs(0))
    def _():
        _issue_row_gather(idxn_ref, idx_smem, sem_i, src_hbm, buf.at[1 - slot], sem_buf.at[1 - slot], n_rows)

    pltpu.make_async_copy(buf.at[slot], buf.at[slot], sem_buf.at[slot]).wait()
    return slot


def _gather_specs(n_steps, k):
    first = lambda g, *_: (0, 0, 0)
    nxt = lambda g, *_: (jnp.minimum(g + 1, n_steps - 1), 0, 0)
    return pl.BlockSpec((1, k, IDX_LANES), first), pl.BlockSpec((1, k, IDX_LANES), nxt)


def _moe_ffn_kernel(ge_ref, idx0_ref, idxn_ref, h_hbm, w1_ref, w3_ref, w2_ref, o_ref,
                    xbuf, idx_smem, sem_i, sem_x, *, ff_chunk):
    del ge_ref
    slot = _pipelined_gather(idx0_ref, idxn_ref, idx_smem, sem_i, h_hbm, xbuf, sem_x, EXPERT_ROWS)
    xb = xbuf[slot].astype(BF16)
    d_ff = w1_ref.shape[2]
    acc = jnp.zeros((EXPERT_ROWS, w2_ref.shape[2]), F32)
    for c in range(0, d_ff, ff_chunk):
        a = jnp.dot(xb, w1_ref[0, :, c:c + ff_chunk], preferred_element_type=F32)
        u = jnp.dot(xb, w3_ref[0, :, c:c + ff_chunk], preferred_element_type=F32)
        hid = (a * jax.nn.sigmoid(a) * u).astype(BF16)
        acc = acc + jnp.dot(hid, w2_ref[0, c:c + ff_chunk, :], preferred_element_type=F32)
    o_ref[...] = acc


def moe_expert_ffn(h, row_tok, grp_e, w1, w3, w2, ff_chunk=512):
    d = h.shape[1]
    d_ff = w1.shape[2]
    n_groups = grp_e.shape[0]
    k = EXPERT_ROWS // IDX_LANES
    idx = row_tok.reshape(n_groups, k, IDX_LANES)
    once = pl.Buffered(1)
    idx0_spec, idxn_spec = _gather_specs(n_groups, k)
    grid_spec = pltpu.PrefetchScalarGridSpec(
        num_scalar_prefetch=1,
        grid=(n_groups,),
        in_specs=[idx0_spec, idxn_spec, pl.BlockSpec(memory_space=pl.ANY),
                  pl.BlockSpec((1, d, d_ff), lambda g, ge: (ge[g], 0, 0), pipeline_mode=once),
                  pl.BlockSpec((1, d, d_ff), lambda g, ge: (ge[g], 0, 0), pipeline_mode=once),
                  pl.BlockSpec((1, d_ff, d), lambda g, ge: (ge[g], 0, 0), pipeline_mode=once)],
        out_specs=pl.BlockSpec((EXPERT_ROWS, d), lambda g, ge: (g, 0)),
        scratch_shapes=[pltpu.VMEM((2, EXPERT_ROWS, d), F32), pltpu.SMEM((k, IDX_LANES), I32),
                        pltpu.SemaphoreType.DMA(()), pltpu.SemaphoreType.DMA((2,))],
    )
    return pl.pallas_call(
        functools.partial(_moe_ffn_kernel, ff_chunk=ff_chunk),
        grid_spec=grid_spec,
        out_shape=jax.ShapeDtypeStruct((n_groups * EXPERT_ROWS, d), F32),
        compiler_params=_cparams(("arbitrary",)),
        name="moe_expert_ffn",
    )(grp_e, idx, idx, h, w1, w3, w2)


COMBINE_TM = 256


def _moe_combine_ln_kernel(idx0_ref, idxn_ref, y_hbm, h_ref, gate_ref, g_ref, b_ref, o_ref,
                           ybuf, idx_smem, sem_i, sem_y):
    tm = COMBINE_TM
    slot = _pipelined_gather(idx0_ref, idxn_ref, idx_smem, sem_i, y_hbm, ybuf, sem_y, TOP_K * tm)
    y = gate_ref[:, 0:1] * ybuf[slot, 0:tm, :]
    for j in range(1, TOP_K):
        y = y + gate_ref[:, j:j + 1] * ybuf[slot, j * tm:(j + 1) * tm, :]
    o_ref[...] = _layer_norm_rows(ALPHA * h_ref[...] + y, g_ref[...], b_ref[...])


def moe_combine_ln(h, y_rows, dest, gate, g, b):
    m, d = h.shape
    tm = COMBINE_TM
    n_tiles = m // tm
    k = TOP_K * tm // IDX_LANES
    idx = dest.reshape(n_tiles, tm, TOP_K).transpose(0, 2, 1).reshape(n_tiles, k, IDX_LANES)
    idx0_spec, idxn_spec = _gather_specs(n_tiles, k)
    row = pl.BlockSpec((tm, d), lambda i: (i, 0))
    vec = pl.BlockSpec((1, d), lambda i: (0, 0))
    return pl.pallas_call(
        _moe_combine_ln_kernel,
        grid=(n_tiles,),
        in_specs=[idx0_spec, idxn_spec, pl.BlockSpec(memory_space=pl.ANY), row,
                  pl.BlockSpec((tm, TOP_K), lambda i: (i, 0)), vec, vec],
        out_specs=row,
        out_shape=jax.ShapeDtypeStruct((m, d), F32),
        scratch_shapes=[pltpu.VMEM((2, TOP_K * tm, d), F32), pltpu.SMEM((k, IDX_LANES), I32),
                        pltpu.SemaphoreType.DMA(()), pltpu.SemaphoreType.DMA((2,))],
        compiler_params=_cparams(("arbitrary",)),
        name="moe_combine_ln",
    )(idx, idx, y_rows, h, gate, g.reshape(1, d), b.reshape(1, d))


def moe_dispatch_plan(logits):
    n_tok = logits.shape[0]
    top_val, top_e = lax.top_k(logits, TOP_K)
    gate = jax.nn.softmax(top_val, axis=-1)
    e_flat = top_e.reshape(-1)
    onehot = (e_flat[:, None] == jnp.arange(N_EXPERTS, dtype=e_flat.dtype)[None, :]).astype(I32)
    rank = jnp.take_along_axis(jnp.cumsum(onehot, axis=0) - onehot, e_flat[:, None], axis=1)[:, 0]
    counts = jnp.sum(onehot, axis=0)
    padded = (counts + EXPERT_ROWS - 1) // EXPERT_ROWS * EXPERT_ROWS
    pend = jnp.cumsum(padded)
    pstart = pend - padded
    dest = pstart[e_flat] + rank
    n_assign = n_tok * TOP_K
    n_rows = -(-n_assign // EXPERT_ROWS) * EXPERT_ROWS + N_EXPERTS * EXPERT_ROWS
    n_groups = n_rows // EXPERT_ROWS
    tok_flat = jnp.repeat(jnp.arange(n_tok, dtype=I32), TOP_K)
    row_tok = jnp.zeros((n_rows,), I32).at[dest].set(tok_flat)
    grp_e = jnp.minimum(jnp.searchsorted(pend, jnp.arange(n_groups, dtype=I32) * EXPERT_ROWS, side='right'),
                        N_EXPERTS - 1).astype(I32)
    return gate, dest.astype(I32), row_tok, grp_e


def kernel(x, rel_bias, e_w_in, e_q_norm, e_kv_norm, e_w_uq, e_w_uk, e_w_uv, e_w_qidx, e_pos_k, e_pos_v, e_ck1, e_ck2, e_cv1, e_cv2, e_w_out, e_ln1_g, e_ln1_b, e_ffn_w1, e_ffn_w3, e_ffn_w2, e_ln2_g, e_ln2_b, o_w_in, o_w_out, o_ln1_g, o_ln1_b, o_router, o_moe_w1, o_moe_w3, o_moe_w2, o_ln2_g, o_ln2_b):
    bsz, seq, d = x.shape
    m = bsz * seq
    xf = x.reshape(m, d)
    dsa_bias = dsa_bias_tiles(rel_bias)
    nsa_bc, nsa_toe_s, nsa_toe_w = nsa_bias_inputs(rel_bias, seq)
    moba_bias = moba_bias_tiles(rel_bias)
    gd = B_GROUPS * HEAD_DIM
    for layer in range(DEPTH):
        i = layer // 2
        if layer % 2 == 0:
            x3 = xf.reshape(bsz, seq, d)
            (qidxT, qlatT, sT, kidx, ckv, ckvT, qbT, kcmp, vcmp, kslc, kwin, vT) = even_inproj(
                x3, e_w_in[i], e_q_norm[i], e_kv_norm[i], e_w_uq[i], e_w_uk[i], e_w_qidx[i])
            o_aT = dsa_attention(qidxT, sT, qlatT, kidx, ckv, ckvT, e_w_uv[i], dsa_bias)
            kc = nsa_compress(kcmp.reshape(m, gd), e_pos_k[i], e_ck1[i], e_ck2[i], bsz, seq)
            vc = nsa_compress(vcmp.reshape(m, gd), e_pos_v[i], e_cv1[i], e_cv2[i], bsz, seq)
            o_bT = nsa_attention(qbT, kc, vc, kslc, kwin, vT, sT, nsa_bc, nsa_toe_s, nsa_toe_w)
            h = projT_residual_ln([o_aT, o_bT], e_w_out[i].astype(BF16), x3, e_ln1_g[i], e_ln1_b[i]).reshape(m, d)
            tm = 512
            xf = swiglu_ffn(h, jnp.zeros((m // tm,), I32), e_ffn_w1[i][None].astype(BF16),
                            e_ffn_w3[i][None].astype(BF16), e_ffn_w2[i][None].astype(BF16),
                            e_ln2_g[i], e_ln2_b[i], with_ln=True, out_dtype=F32, tm=tm, ff_chunk=1408)
        else:
            x3 = xf.reshape(bsz, seq, d)
            o_cT = moba_attention(*moba_inproj(x3, o_w_in[i]), moba_bias)
            h = projT_residual_ln([o_cT], o_w_out[i].astype(BF16), x3, o_ln1_g[i], o_ln1_b[i]).reshape(m, d)
            gate, dest, row_tok, grp_e = moe_dispatch_plan(router_logits(h, o_router[i]))
            y_rows = moe_expert_ffn(h, row_tok, grp_e, o_moe_w1[i].astype(BF16), o_moe_w3[i].astype(BF16),
                                    o_moe_w2[i].astype(BF16))
            xf = moe_combine_ln(h, y_rows, dest, gate, o_ln2_g[i], o_ln2_b[i])
    return xf.reshape(bsz, seq, d)
```

```python
import functools
import math

import numpy as np
import jax
import jax.numpy as jnp
from jax import lax
from jax.experimental import pallas as pl
from jax.experimental.pallas import tpu as pltpu

F32 = jnp.float32
BF16 = jnp.bfloat16
I32 = jnp.int32
BF16_ROWS = 16

HEAD_DIM = 64
NUM_BUCKETS = 32
MAX_DISTANCE = 128
N_BIAS_HEADS = 16
A_HEADS = 8
A_Q_RANK = 256
A_KV_RANK = 128
IDX_HEADS = 16
IDX_DIM = 64
DSA_TOPK = 256
B_HEADS = 8
B_GROUPS = 2
B_HPG = B_HEADS // B_GROUPS
CMP_LEN = 32
CMP_STRIDE = 16
SLC_BLOCK = 64
SLC_TOPN = 16
WINDOW = 512
C_HEADS = 16
MOBA_BLOCK = 256
MOBA_TOPK = 3
N_EXPERTS = 8
TOP_K = 2
EXPERT_ROWS = 256
DEPTH = 2
ALPHA = (2 * DEPTH) ** 0.25

LOG2E = 1.4426950408889634
QK_SCALE = HEAD_DIM ** -0.5 * LOG2E
NEG = -1e30
NEG_HALF = -5e29
INT_MIN = -2 ** 31
VMEM_LIMIT = 56 * 1024 * 1024


def _t5_thresholds():
    def bucket(n):
        if n < NUM_BUCKETS // 2:
            return n
        v = np.log(np.float32(n) / np.float32(NUM_BUCKETS // 2)) / np.float32(math.log(MAX_DISTANCE / (NUM_BUCKETS // 2)))
        return min(NUM_BUCKETS // 2 + int(np.float32(v) * (NUM_BUCKETS - NUM_BUCKETS // 2)), NUM_BUCKETS - 1)
    b = [bucket(i) for i in range(4 * MAX_DISTANCE)]
    return [0] + [min(i for i in range(len(b)) if b[i] >= k) for k in range(1, NUM_BUCKETS)]


T5_THR = _t5_thresholds()
T5_FAR = T5_THR[-1]


def _cparams(sem):
    return pltpu.CompilerParams(dimension_semantics=sem, vmem_limit_bytes=VMEM_LIMIT)


def _bias_kernel(tab_ref, off_ref, o_ref, *, c_row, c_col, h0, causal_neg, window):
    v = pl.program_id(0)
    h = pl.program_id(1) + h0
    shape = o_ref.shape[2:]
    dist = (c_col * lax.broadcasted_iota(I32, shape, 1) + c_row * lax.broadcasted_iota(I32, shape, 0) + off_ref[v])
    n = jnp.maximum(dist, 0)
    acc = jnp.full(shape, tab_ref[h] * LOG2E, F32)
    for k in range(1, NUM_BUCKETS):
        acc = jnp.where(n >= T5_THR[k], tab_ref[k * N_BIAS_HEADS + h] * LOG2E, acc)
    if causal_neg:
        acc = jnp.where(dist >= 0, acc, NEG)
    if window:
        acc = jnp.where(dist < window, acc, NEG)
    o_ref[0, 0] = acc


def bias_tiles(rel_bias, offs, n_heads, h0, rows, cols, c_row, c_col, causal_neg, window=0):
    offs = jnp.asarray(offs, I32)
    nv = offs.shape[0]
    return pl.pallas_call(
        functools.partial(_bias_kernel, c_row=c_row, c_col=c_col, h0=h0, causal_neg=causal_neg, window=window),
        grid=(nv, n_heads),
        in_specs=[pl.BlockSpec(memory_space=pltpu.SMEM), pl.BlockSpec(memory_space=pltpu.SMEM)],
        out_specs=pl.BlockSpec((1, 1, rows, cols), lambda v, h: (v, h, 0, 0)),
        out_shape=jax.ShapeDtypeStruct((nv, n_heads, rows, cols), F32),
        compiler_params=_cparams(("arbitrary", "arbitrary")),
        name="t5_bias_tiles",
    )(rel_bias.reshape(-1), offs)


def _mm_kernel(x_ref, w_ref, o_ref):
    o_ref[...] = jnp.dot(x_ref[...].astype(BF16), w_ref[...].astype(BF16),
                         preferred_element_type=F32).astype(o_ref.dtype)


def matmul(x, w, out_dtype, tm=512):
    m, k = x.shape
    n = w.shape[1]
    tm = min(tm, m)
    return pl.pallas_call(
        _mm_kernel,
        grid=(m // tm,),
        in_specs=[pl.BlockSpec((tm, k), lambda i: (i, 0)), pl.BlockSpec((k, n), lambda i: (0, 0))],
        out_specs=pl.BlockSpec((tm, n), lambda i: (i, 0)),
        out_shape=jax.ShapeDtypeStruct((m, n), out_dtype),
        compiler_params=_cparams(("arbitrary",)),
        name="matmul",
    )(x, w)


def _layer_norm_rows(z, g, b):
    mu = jnp.mean(z, axis=-1, keepdims=True)
    zc = z - mu
    var = jnp.mean(zc * zc, axis=-1, keepdims=True)
    return zc * lax.rsqrt(var + 1e-5) * g + b


def _proj_ln_kernel(a_ref, w_ref, x_ref, g_ref, b_ref, o_ref):
    mix = jnp.dot(a_ref[...], w_ref[...], preferred_element_type=F32)
    o_ref[...] = _layer_norm_rows(ALPHA * x_ref[...] + mix, g_ref[...], b_ref[...])


def proj_residual_ln(a, w, x, g, b, tm=512):
    m, k = a.shape
    d = w.shape[1]
    tm = min(tm, m)
    return pl.pallas_call(
        _proj_ln_kernel,
        grid=(m // tm,),
        in_specs=[pl.BlockSpec((tm, k), lambda i: (i, 0)), pl.BlockSpec((k, d), lambda i: (0, 0)),
                  pl.BlockSpec((tm, d), lambda i: (i, 0)),
                  pl.BlockSpec((1, d), lambda i: (0, 0)), pl.BlockSpec((1, d), lambda i: (0, 0))],
        out_specs=pl.BlockSpec((tm, d), lambda i: (i, 0)),
        out_shape=jax.ShapeDtypeStruct((m, d), F32),
        compiler_params=_cparams(("arbitrary",)),
        name="proj_residual_ln",
    )(a, w, x, g.reshape(1, d), b.reshape(1, d))


def _ffn_kernel(ge_ref, x_ref, w1_ref, w3_ref, w2_ref, g_ref, b_ref, o_ref, *, ff_chunk, with_ln):
    del ge_ref
    x = x_ref[...]
    xb = x.astype(BF16)
    d_ff = w1_ref.shape[2]
    acc = jnp.zeros((x.shape[0], w2_ref.shape[2]), F32)
    for c in range(0, d_ff, ff_chunk):
        a = jnp.dot(xb, w1_ref[0, :, c:c + ff_chunk], preferred_element_type=F32)
        u = jnp.dot(xb, w3_ref[0, :, c:c + ff_chunk], preferred_element_type=F32)
        hid = (a * jax.nn.sigmoid(a) * u).astype(BF16)
        acc = acc + jnp.dot(hid, w2_ref[0, c:c + ff_chunk, :], preferred_element_type=F32)
    if with_ln:
        o_ref[...] = _layer_norm_rows(ALPHA * x.astype(F32) + acc, g_ref[...], b_ref[...]).astype(o_ref.dtype)
    else:
        o_ref[...] = acc.astype(o_ref.dtype)


def swiglu_ffn(x_rows, grp_e, w1, w3, w2, ln_g, ln_b, *, with_ln, out_dtype, tm, ff_chunk):
    m, d = x_rows.shape
    d_ff = w1.shape[2]
    once = pl.Buffered(1)
    grid_spec = pltpu.PrefetchScalarGridSpec(
        num_scalar_prefetch=1,
        grid=(m // tm,),
        in_specs=[pl.BlockSpec((tm, d), lambda i, ge: (i, 0)),
                  pl.BlockSpec((1, d, d_ff), lambda i, ge: (ge[i], 0, 0), pipeline_mode=once),
                  pl.BlockSpec((1, d, d_ff), lambda i, ge: (ge[i], 0, 0), pipeline_mode=once),
                  pl.BlockSpec((1, d_ff, d), lambda i, ge: (ge[i], 0, 0), pipeline_mode=once),
                  pl.BlockSpec((1, d), lambda i, ge: (0, 0)), pl.BlockSpec((1, d), lambda i, ge: (0, 0))],
        out_specs=pl.BlockSpec((tm, d), lambda i, ge: (i, 0)),
    )
    return pl.pallas_call(
        functools.partial(_ffn_kernel, ff_chunk=ff_chunk, with_ln=with_ln),
        grid_spec=grid_spec,
        out_shape=jax.ShapeDtypeStruct((m, d), out_dtype),
        compiler_params=_cparams(("arbitrary",)),
        name="swiglu_ffn",
    )(grp_e, x_rows, w1, w3, w2, ln_g.reshape(1, d), ln_b.reshape(1, d))


def _flash_probs(s, m, segs=None):
    if segs is None:
        m_new = jnp.maximum(m, jnp.max(s, axis=0, keepdims=True))
        p = jnp.exp2(s - m_new)
    else:
        m_new, r0 = m, 0
        for n, c in segs:
            seg_max = jnp.max(s[r0:r0 + n], axis=0, keepdims=True)
            m_new = jnp.maximum(m_new, jnp.where(c > NEG_HALF, seg_max + c, NEG))
            r0 += n
        parts, r0 = [], 0
        for n, c in segs:
            shift = jnp.where(c > NEG_HALF, m_new - c, -NEG)
            parts.append(jnp.exp2(s[r0:r0 + n] - shift))
            r0 += n
        p = parts[0] if len(parts) == 1 else jnp.concatenate(parts, axis=0)
    return m_new, jnp.exp2(m - m_new), p.astype(BF16)


def _flash_merge(states):
    ms = [st[0][...] for st in states]
    m = functools.reduce(jnp.maximum, ms)
    ws = [jnp.exp2(mi - m) for mi in ms]
    l = sum(w * st[1][...] for w, st in zip(ws, states))
    acc = sum(w * st[2][...] for w, st in zip(ws, states))
    return m, l, acc


def _flash_finish(m, l, acc):
    return jnp.where(m > NEG_HALF, acc / l, 0.0)


def _rms_rows(x, g):
    return x * lax.rsqrt(jnp.mean(x * x, axis=-1, keepdims=True) + 1e-6) * g


EVEN_T = 256
SMALL_ROWS = 48
GATE_ROW0 = IDX_HEADS
NT_DIMS = (((1,), (1,)), ((), ()))


def _even_inproj_kernel(x_ref, wa_ref, wsT_ref, qn_ref, kvn_ref, wuq_ref, wuk_ref, wqiT_ref, wqbT_ref, wk4_ref, wvT_ref,
                        qidxT_ref, qlatT_ref, sT_ref, kidx_ref, ckv_ref, ckvT_ref,
                        qbT_ref, kcmp_ref, vcmp_ref, kslc_ref, kwin_ref, vT_ref):
    xb = x_ref[0].astype(BF16)
    ya = jnp.dot(xb, wa_ref[...], preferred_element_type=F32)
    cqn = _rms_rows(ya[:, :A_Q_RANK], qn_ref[...]).astype(BF16)
    ckvn = _rms_rows(ya[:, A_Q_RANK:A_Q_RANK + A_KV_RANK], kvn_ref[...])
    kidx_ref[0, 0] = ya[:, A_Q_RANK + A_KV_RANK:A_Q_RANK + A_KV_RANK + IDX_DIM].astype(BF16)
    ckv_ref[0, 0] = ckvn.astype(BF16)
    ckvT_ref[0, 0] = ckvn.T.astype(BF16)
    sT_ref[0] = lax.dot_general(wsT_ref[...], xb, NT_DIMS, preferred_element_type=F32)
    q = jnp.dot(cqn, wuq_ref[...], preferred_element_type=F32).astype(BF16)
    for h in range(A_HEADS):
        qlT = lax.dot_general(wuk_ref[h], q[:, h * HEAD_DIM:(h + 1) * HEAD_DIM], NT_DIMS, preferred_element_type=F32)
        qlatT_ref[0, h * A_KV_RANK:(h + 1) * A_KV_RANK, :] = (qlT * QK_SCALE).astype(BF16)
    qidxT_ref[0] = lax.dot_general(wqiT_ref[...], cqn, NT_DIMS, preferred_element_type=F32).astype(BF16)
    qbT_ref[0] = lax.dot_general(wqbT_ref[...], xb, NT_DIMS, preferred_element_type=F32).astype(BF16)
    yk = jnp.dot(xb, wk4_ref[...], preferred_element_type=F32).astype(BF16)
    gd = B_GROUPS * HEAD_DIM
    for j, ref in enumerate((kcmp_ref, vcmp_ref, kslc_ref, kwin_ref)):
        ref[0] = yk[:, j * gd:(j + 1) * gd]
    vT = lax.dot_general(wvT_ref[...], xb, NT_DIMS, preferred_element_type=F32).astype(BF16)
    for j in range(EVEN_T // NSA_KT):
        vT_ref[0, j] = vT[:, j * NSA_KT:(j + 1) * NSA_KT]


def even_inproj(x3, w_in, q_norm, kv_norm, w_uq, w_uk, w_qidx):
    bsz, seq, d = x3.shape
    T = EVEN_T
    nq = seq // T
    gd = B_GROUPS * HEAD_DIM
    n_kt = seq // NSA_KT
    o_kidx = A_Q_RANK + A_KV_RANK
    o_widx = o_kidx + IDX_DIM
    o_qb = o_widx + IDX_HEADS
    o_kv = o_qb + B_HEADS * HEAD_DIM
    o_gate = o_kv + 6 * gd
    kv = lambda j: w_in[:, o_kv + j * gd:o_kv + (j + 1) * gd]
    wa = jnp.pad(w_in[:, :o_widx], ((0, 0), (0, 512 - o_widx))).astype(BF16)
    w_gate = w_in[:, o_gate:].reshape(d, B_GROUPS, B_HPG, 3).transpose(0, 1, 3, 2).reshape(d, 3 * B_HEADS)
    wsT = jnp.concatenate([w_in[:, o_widx:o_qb] * IDX_HEADS ** -0.5, w_gate,
                           jnp.zeros((d, SMALL_ROWS - IDX_HEADS - 3 * B_HEADS), w_in.dtype)], axis=1).T.astype(BF16)
    wuq = w_uq.reshape(A_Q_RANK, A_HEADS * HEAD_DIM).astype(BF16)
    wuk = jnp.transpose(w_uk, (1, 0, 2)).astype(BF16)
    wqiT = w_qidx.reshape(A_Q_RANK, IDX_HEADS * IDX_DIM).T.astype(BF16)
    wqbT = (w_in[:, o_qb:o_kv] * QK_SCALE).T.astype(BF16)
    wk4 = jnp.concatenate([kv(0), kv(1), kv(2), kv(4)], axis=1).astype(BF16)
    wvT = jnp.concatenate([kv(3), kv(5)], axis=1).T.astype(BF16)
    weights = (wa, wsT, q_norm.reshape(1, -1), kv_norm.reshape(1, -1), wuq, wuk, wqiT, wqbT, wk4, wvT)
    once = pl.Buffered(1)
    w_specs = [pl.BlockSpec(w.shape, (lambda b, i, n=w.ndim: (0,) * n), pipeline_mode=once) for w in weights]
    fm = lambda rows: pl.BlockSpec((1, rows, T), lambda b, i: (b, 0, i))
    tok = lambda cols: pl.BlockSpec((1, T, cols), lambda b, i: (b, i, 0))
    blk = lambda r, c: pl.BlockSpec((1, 1, r, c), lambda b, i: (b, i, 0, 0))
    sds = jax.ShapeDtypeStruct
    return pl.pallas_call(
        _even_inproj_kernel,
        grid=(bsz, nq),
        in_specs=[pl.BlockSpec((1, T, d), lambda b, i: (b, i, 0))] + w_specs,
        out_specs=[fm(IDX_HEADS * IDX_DIM), fm(A_HEADS * A_KV_RANK), fm(SMALL_ROWS),
                   blk(T, IDX_DIM), blk(T, A_KV_RANK), blk(A_KV_RANK, T),
                   fm(B_HEADS * HEAD_DIM), tok(gd), tok(gd), tok(gd), tok(gd),
                   pl.BlockSpec((1, T // NSA_KT, 2 * gd, NSA_KT), lambda b, i: (b, i, 0, 0))],
        out_shape=[sds((bsz, IDX_HEADS * IDX_DIM, seq), BF16), sds((bsz, A_HEADS * A_KV_RANK, seq), BF16),
                   sds((bsz, SMALL_ROWS, seq), F32),
                   sds((bsz, nq, T, IDX_DIM), BF16), sds((bsz, nq, T, A_KV_RANK), BF16), sds((bsz, nq, A_KV_RANK, T), BF16),
                   sds((bsz, B_HEADS * HEAD_DIM, seq), BF16),
                   sds((bsz, seq, gd), BF16), sds((bsz, seq, gd), BF16), sds((bsz, seq, gd), BF16), sds((bsz, seq, gd), BF16),
                   sds((bsz, n_kt, 2 * gd, NSA_KT), BF16)],
        compiler_params=_cparams(("arbitrary", "arbitrary")),
        name="even_inproj",
    )(x3, *weights)


DSA_T = 256
SUB = 128


def _dsa_kernel(qidx_ref, wT_ref, qlat_ref, kidx_ref, ckv_ref, ckvT_ref, bias_ref, wuvt_ref, o_ref,
                key_ref, selb0_ref, selb1_ref, *state_refs, n_keep):
    selb_refs = (selb0_ref, selb1_ref)
    m_refs, l_refs, acc_refs = (state_refs[0:A_HEADS], state_refs[A_HEADS:2 * A_HEADS], state_refs[2 * A_HEADS:])
    qi = pl.program_id(1)
    nkb = qi + 1
    T = DSA_T

    def score_block(kb, carry):
        for sub in range(T // SUB):
            k = kidx_ref[0, kb, sub * SUB:(sub + 1) * SUB, :]
            acc = jnp.zeros((SUB, T), F32)
            for h in range(IDX_HEADS):
                d = jnp.dot(k, qidx_ref[0, h * IDX_DIM:(h + 1) * IDX_DIM, :], preferred_element_type=F32)
                acc = acc + jnp.maximum(d, 0.0) * wT_ref[0, h:h + 1, :]
            bits = lax.bitcast_convert_type(acc, I32)
            key = bits ^ (lax.shift_right_arithmetic(bits, 31) & 0x7FFFFFFF)
            s_pos = kb * T + sub * SUB + lax.broadcasted_iota(I32, (SUB, T), 0)
            t_pos = qi * T + lax.broadcasted_iota(I32, (SUB, T), 1)
            key = jnp.where(s_pos <= t_pos, key, INT_MIN)
            key_ref[pl.ds(pl.multiple_of(kb * T + sub * SUB, SUB), SUB), :] = key
        return carry

    lax.fori_loop(0, nkb, score_block, 0)

    def count_ge(cand):
        def body(kb, cnt):
            blk = key_ref[pl.ds(pl.multiple_of(kb * T, T), T), :]
            ge = jnp.where(blk >= cand, 1, 0).astype(I32)
            return cnt + jnp.sum(ge.reshape(T // 8, 8, T), axis=0)
        cnt = lax.fori_loop(0, nkb, body, jnp.zeros((8, T), I32))
        return jnp.sum(cnt, axis=0, keepdims=True)

    def bit_step(i, u):
        cand_u = u | lax.shift_left(jnp.int32(1), 31 - i)
        cnt = count_ge(cand_u ^ INT_MIN)
        return jnp.where(cnt >= n_keep, cand_u, u)

    u = lax.fori_loop(0, 32, bit_step, jnp.zeros((1, T), I32))
    thr = jnp.maximum(u ^ INT_MIN, INT_MIN + 1)

    for h in range(A_HEADS):
        m_refs[h][...] = jnp.full(m_refs[h].shape, NEG, F32)
        l_refs[h][...] = jnp.zeros(l_refs[h].shape, F32)
        acc_refs[h][...] = jnp.zeros(acc_refs[h].shape, F32)

    states = list(zip(m_refs, l_refs, acc_refs))
    far_bias = [bias_ref[2, h, 0:1, :] for h in range(A_HEADS)]

    def masked_scores(kb, selb):
        selb[...] = jnp.where(key_ref[pl.ds(pl.multiple_of(kb * T, T), T), :] >= thr, 0.0, NEG)
        ckv = ckv_ref[0, kb]
        return [jnp.dot(ckv, qlat_ref[0, h * A_KV_RANK:(h + 1) * A_KV_RANK, :], preferred_element_type=F32) + selb[...]
                for h in range(A_HEADS)]

    def far_body(i, carry):
        tiles = []
        for u, selb in enumerate(selb_refs):
            kb_raw = len(selb_refs) * i + u
            live = kb_raw < n_far
            kb = jnp.minimum(kb_raw, n_far - 1)
            segs = [[(T, jnp.where(live, far_bias[h], NEG))] for h in range(A_HEADS)]
            tiles.append((list(zip(masked_scores(kb, selb), segs)), [ckvT_ref[0, kb]] * A_HEADS))
        for s_all, v_all in tiles:
            _flash_step(s_all, v_all, states)
        return carry

    def near_body(kb, carry):
        s_all = [s + bias_ref[qi - kb, h] for h, s in enumerate(masked_scores(kb, selb_refs[0]))]
        _flash_step(s_all, [ckvT_ref[0, kb]] * A_HEADS, states)
        return carry

    n_far = jnp.maximum(qi - 1, 0)
    lax.fori_loop(0, lax.div(n_far + (len(selb_refs) - 1), jnp.int32(len(selb_refs))), far_body, 0)
    lax.fori_loop(n_far, nkb, near_body, 0)

    for h in range(A_HEADS):
        o_lat = _flash_finish(m_refs[h][...], l_refs[h][...], acc_refs[h][...]).astype(BF16)
        o_ref[0, h * HEAD_DIM:(h + 1) * HEAD_DIM, :] = jnp.dot(
            wuvt_ref[h], o_lat, preferred_element_type=F32).astype(o_ref.dtype)


def dsa_attention(qidxT, sT, qlatT, kidx, ckv, ckvT, w_uv, bias3):
    T = DSA_T
    assert T == EVEN_T
    bsz, nq = kidx.shape[0], kidx.shape[1]
    seq = nq * T
    n_keep = min(DSA_TOPK, seq // 4)
    wuvt = jnp.transpose(w_uv, (1, 2, 0)).astype(BF16)
    return pl.pallas_call(
        functools.partial(_dsa_kernel, n_keep=n_keep),
        grid=(bsz, nq),
        in_specs=[pl.BlockSpec((1, IDX_HEADS * IDX_DIM, T), lambda b, i: (b, 0, i)),
                  pl.BlockSpec((1, SMALL_ROWS, T), lambda b, i: (b, 0, i)),
                  pl.BlockSpec((1, A_HEADS * A_KV_RANK, T), lambda b, i: (b, 0, i)),
                  pl.BlockSpec((1, nq, T, IDX_DIM), lambda b, i: (b, 0, 0, 0)),
                  pl.BlockSpec((1, nq, T, A_KV_RANK), lambda b, i: (b, 0, 0, 0)),
                  pl.BlockSpec((1, nq, A_KV_RANK, T), lambda b, i: (b, 0, 0, 0)),
                  pl.BlockSpec((3, A_HEADS, T, T), lambda b, i: (0, 0, 0, 0)),
                  pl.BlockSpec((A_HEADS, HEAD_DIM, A_KV_RANK), lambda b, i: (0, 0, 0))],
        out_specs=pl.BlockSpec((1, A_HEADS * HEAD_DIM, T), lambda b, i: (b, 0, i)),
        out_shape=jax.ShapeDtypeStruct((bsz, A_HEADS * HEAD_DIM, seq), BF16),
        scratch_shapes=([pltpu.VMEM((seq, T), I32), pltpu.VMEM((T, T), F32), pltpu.VMEM((T, T), F32)]
                        + [pltpu.VMEM((1, T), F32)] * (2 * A_HEADS)
                        + [pltpu.VMEM((A_KV_RANK, T), F32)] * A_HEADS),
        compiler_params=_cparams(("arbitrary", "arbitrary")),
        name="dsa_attention",
    )(qidxT, sT, qlatT, kidx, ckv, ckvT, bias3, wuvt)


def dsa_bias_tiles(rel_bias):
    assert DSA_T + 1 >= T5_FAR
    return bias_tiles(rel_bias, [0, DSA_T, 4 * DSA_T], A_HEADS, 0, DSA_T, DSA_T, -1, 1, False)


N_CMP_PAD = 256


def _compress_kernel(blk_ref, pos_ref, w1_ref, w2_ref, o_ref):
    x = (blk_ref[0].astype(F32) + pos_ref[...]).astype(BF16)
    hid = jax.nn.gelu(jnp.dot(x, w1_ref[...], preferred_element_type=F32))
    o_ref[0] = jnp.dot(hid.astype(BF16), w2_ref[...], preferred_element_type=F32).astype(o_ref.dtype)


def nsa_compress(a, pos, w1, w2, bsz, seq):
    n_chunk = seq // CMP_STRIDE
    assert CMP_LEN == 2 * CMP_STRIDE and n_chunk <= N_CMP_PAD
    width = CMP_STRIDE * HEAD_DIM
    chunks = a.reshape(bsz, n_chunk, CMP_STRIDE, B_GROUPS, HEAD_DIM).transpose(0, 3, 1, 2, 4)
    chunks = chunks.reshape(bsz * B_GROUPS, n_chunk, width)
    blocks = jnp.concatenate([chunks[:, :-1], chunks[:, 1:]], axis=-1)
    blocks = jnp.pad(blocks, ((0, 0), (0, N_CMP_PAD - (n_chunk - 1)), (0, 0)))
    out = pl.pallas_call(
        _compress_kernel,
        grid=(bsz * B_GROUPS,),
        in_specs=[pl.BlockSpec((1, N_CMP_PAD, 2 * width), lambda i: (i, 0, 0)),
                  pl.BlockSpec((1, 2 * width), lambda i: (0, 0)),
                  pl.BlockSpec((2 * width, HEAD_DIM), lambda i: (0, 0)),
                  pl.BlockSpec((HEAD_DIM, HEAD_DIM), lambda i: (0, 0))],
        out_specs=pl.BlockSpec((1, N_CMP_PAD, HEAD_DIM), lambda i: (i, 0, 0)),
        out_shape=jax.ShapeDtypeStruct((bsz * B_GROUPS, N_CMP_PAD, HEAD_DIM), BF16),
        compiler_params=_cparams(("arbitrary",)),
        name="nsa_compress",
    )(blocks, pos.reshape(1, 2 * width), w1.reshape(2 * width, HEAD_DIM).astype(BF16), w2.astype(BF16))
    return out.reshape(bsz, B_GROUPS, N_CMP_PAD, HEAD_DIM)


NSA_TQ = 128
NSA_L = B_HPG * NSA_TQ
NSA_KT = 128
NSA_SLC_REL = 3
NSA_WIN_REL = 5
NSA_FAR_SPLIT = 4


def _flash_step(s_all, vT_all, states):
    probs = []
    for item, (m_ref, _, _) in zip(s_all, states):
        s, segs = item if isinstance(item, tuple) else (item, None)
        m_new, alpha, p = _flash_probs(s, m_ref[...], segs)
        m_ref[...] = m_new
        probs.append((alpha, p))
    for (alpha, p), vT, (_, l_ref, acc_ref) in zip(probs, vT_all, states):
        d = vT.shape[0]
        ones = jnp.ones((BF16_ROWS, vT.shape[1]), BF16)
        pv = jnp.dot(jnp.concatenate([vT, ones], axis=0), p, preferred_element_type=F32)
        acc_ref[...] = alpha * acc_ref[...] + pv[:d]
        l_ref[...] = alpha * l_ref[...] + pv[d:d + 1]


def _flash_loop(lo, hi, scores, values, states, segs, unroll=2):
    def body(i, carry):
        tiles = []
        for u in range(unroll):
            j_raw = lo + unroll * i + u
            live = j_raw < hi
            j = jnp.minimum(j_raw, hi - 1)
            sg = [[(n, jnp.where(live, c, NEG)) for n, c in chain] for chain in segs(j)]
            tiles.append((list(zip(scores(j), sg)), values(j)))
        for s_all, v_all in tiles:
            _flash_step(s_all, v_all, states)
        return carry

    lax.fori_loop(0, lax.div(hi - lo + (unroll - 1), jnp.int32(unroll)), body, 0)


def _nsa_kernel(qT_ref, kc_ref, vcT_ref, biasc_ref, ovl_ref, ks_ref, kw_ref, vT_ref,
                toes_ref, toew_ref, sT_ref, o_ref, selb_ref, *st, n_cmp, n_sel, n_slc):
    qi = pl.program_id(1)
    TQ, L = NSA_TQ, NSA_L
    q0 = qi * TQ
    qTs, qTs_pad = [], []
    for g in range(B_GROUPS):
        q = jnp.concatenate([qT_ref[0, (g * B_HPG + n) * HEAD_DIM:(g * B_HPG + n + 1) * HEAD_DIM, :]
                             for n in range(B_HPG)], axis=1)
        parts = [jnp.zeros_like(q)] * B_GROUPS
        parts[g] = q
        qTs.append(q)
        qTs_pad.append(jnp.concatenate(parts, axis=0))
    t_lane = q0 + (lax.broadcasted_iota(I32, (1, L), 1) & (TQ - 1))

    o_cs = []
    for g in range(B_GROUPS):
        s = jnp.dot(kc_ref[0, g], qTs[g], preferred_element_type=F32) + biasc_ref[g, 0]
        i_idx = lax.broadcasted_iota(I32, (N_CMP_PAD, L), 0)
        valid = jnp.where(i_idx < n_cmp, i_idx * CMP_STRIDE + (CMP_LEN - 1), 2 ** 30) <= t_lane
        s = jnp.where(valid, s, NEG)
        m = jnp.max(s, axis=0, keepdims=True)
        p = jnp.where(valid, jnp.exp2(s - m), 0.0)
        l = jnp.sum(p, axis=0, keepdims=True)
        p_c = p / jnp.where(l > 0, l, 1.0)
        o_c = jnp.dot(vcT_ref[0, g], p_c.astype(BF16), preferred_element_type=F32)

        psum = p_c[:, 0:TQ]
        for n in range(1, B_HPG):
            psum = psum + p_c[:, n * TQ:(n + 1) * TQ]
        sc = jnp.dot(ovl_ref[...], psum, preferred_element_type=F32, precision=lax.Precision.HIGHEST)
        j_idx = lax.broadcasted_iota(I32, (n_slc, TQ), 0)
        cur = (q0 + lax.broadcasted_iota(I32, (1, TQ), 1)) // SLC_BLOCK
        adm = j_idx <= cur
        forced = (j_idx == 0) | (j_idx == cur) | (j_idx == cur - 1)
        scv = jnp.where(adm, jnp.where(forced, jnp.inf, sc), -jnp.inf)
        rank = jnp.zeros((n_slc, TQ), I32)
        for jp in range(n_slc):
            row = scv[jp:jp + 1, :]
            beats = jnp.where(row > scv, 1, jnp.where((row == scv) & (jp < j_idx), 1, 0))
            rank = rank + beats
        selb = jnp.where(rank < n_sel, 0.0, NEG).astype(F32)
        selb4 = jnp.concatenate([selb] * B_HPG, axis=1)
        for j in range(n_slc):
            selb_ref[g, j] = selb4[j:j + 1, :]
        o_cs.append(o_c)

    for ref in st[0::3]:
        ref[...] = jnp.full(ref.shape, NEG, F32)
    for ref in st[1::3] + st[2::3]:
        ref[...] = jnp.zeros(ref.shape, F32)
    slc_st = [st[6 * g:6 * g + 3] for g in range(B_GROUPS)]
    win_st = [st[6 * g + 3:6 * g + 6] for g in range(B_GROUPS)]
    n_main = 6 * B_GROUPS
    xtr_st = [[st[n_main + 3 * (g * (NSA_FAR_SPLIT - 1) + r):n_main + 3 * (g * (NSA_FAR_SPLIT - 1) + r) + 3]
               for r in range(NSA_FAR_SPLIT - 1)] for g in range(B_GROUPS)]
    per_kt = NSA_KT // SLC_BLOCK

    groups = range(B_GROUPS)
    far_bias = [toes_ref[g, NSA_SLC_REL - 1, 0:1, :] for g in groups]

    def slc_scores(g, jt, near):
        s = jnp.dot(ks_ref[0, jt], qTs_pad[g], preferred_element_type=F32)
        return s + toes_ref[g, jnp.minimum(qi - jt, NSA_SLC_REL - 1)] if near else s

    def slc_segs(g, jt, near):
        return [(SLC_BLOCK, selb_ref[g, per_kt * jt + r] + (0.0 if near else far_bias[g])) for r in range(per_kt)]

    def win_scores(g, jt):
        rel = jnp.minimum(qi - jt, NSA_WIN_REL - 1)
        return jnp.dot(kw_ref[0, jt], qTs_pad[g], preferred_element_type=F32) + toew_ref[g, rel]

    gd = B_GROUPS * HEAD_DIM
    v_slc = lambda g, jt: vT_ref[0, jt, g * HEAD_DIM:(g + 1) * HEAD_DIM, :]
    v_win = lambda g, jt: vT_ref[0, jt, gd + g * HEAD_DIM:gd + (g + 1) * HEAD_DIM, :]

    assert NSA_WIN_REL >= NSA_SLC_REL
    j_lo = jnp.maximum(qi - (NSA_WIN_REL - 1), 0)

    def far_body(i, carry):
        s_all, v_all, chains = [], [], []
        for r in range(NSA_FAR_SPLIT):
            jt_raw = NSA_FAR_SPLIT * i + r
            live = jt_raw < j_lo
            jt = jnp.minimum(jt_raw, j_lo - 1)
            for g in groups:
                segs = [(n, jnp.where(live, c, NEG)) for n, c in slc_segs(g, jt, False)]
                s_all.append((slc_scores(g, jt, False), segs))
                v_all.append(v_slc(g, jt))
                chains.append(slc_st[g] if r == 0 else xtr_st[g][r - 1])
        _flash_step(s_all, v_all, chains)
        return carry

    zero_row = jnp.zeros((1, L), F32)
    lax.fori_loop(0, lax.div(j_lo + (NSA_FAR_SPLIT - 1), jnp.int32(NSA_FAR_SPLIT)), far_body, 0)
    _flash_loop(j_lo, qi + 1,
                lambda jt: [slc_scores(g, jt, True) for g in groups] + [win_scores(g, jt) for g in groups],
                lambda jt: [v_slc(g, jt) for g in groups] + [v_win(g, jt) for g in groups],
                slc_st + win_st,
                lambda jt: [slc_segs(g, jt, True) for g in groups] + [[(NSA_KT, zero_row)] for g in groups])

    for g in range(B_GROUPS):
        o_s = _flash_finish(*_flash_merge([slc_st[g]] + xtr_st[g]))
        o_w = _flash_finish(*[r[...] for r in win_st[g]])
        row0 = GATE_ROW0 + g * 3 * B_HPG
        gate = [jax.nn.sigmoid(jnp.concatenate([sT_ref[0, row0 + j * B_HPG + n:row0 + j * B_HPG + n + 1, :]
                                                for n in range(B_HPG)], axis=1)) for j in range(3)]
        o = (gate[0] * o_cs[g] + gate[1] * o_s + gate[2] * o_w).astype(o_ref.dtype)
        for n in range(B_HPG):
            o_ref[0, (g * B_HPG + n) * HEAD_DIM:(g * B_HPG + n + 1) * HEAD_DIM, :] = o[:, n * TQ:(n + 1) * TQ]


def nsa_bias_inputs(rel_bias, seq):
    TQ, L, KT = NSA_TQ, NSA_L, NSA_KT
    nq = seq // TQ
    bc = bias_tiles(rel_bias, [-(CMP_LEN - 1)], B_HEADS, A_HEADS, N_CMP_PAD, seq, -CMP_STRIDE, 1, False, 0)
    bc = bc.reshape(B_GROUPS, B_HPG, N_CMP_PAD, nq, TQ).transpose(0, 3, 2, 1, 4).reshape(B_GROUPS, nq, N_CMP_PAD, L)

    def lanes(t):
        v = t.shape[0]
        return t.reshape(v, B_GROUPS, B_HPG, KT, TQ).transpose(1, 0, 3, 2, 4).reshape(B_GROUPS, v, KT, L)

    assert KT == TQ and (NSA_SLC_REL - 1) * KT - (KT - 1) >= T5_FAR
    toe_s = bias_tiles(rel_bias, [v * KT for v in range(NSA_SLC_REL - 1)] + [64 * KT],
                       B_HEADS, A_HEADS, KT, TQ, -1, 1, True, 0)
    assert (NSA_WIN_REL - 1) * KT - (KT - 1) < WINDOW <= NSA_WIN_REL * KT - (KT - 1)
    toe_w = bias_tiles(rel_bias, [v * KT for v in range(NSA_WIN_REL)], B_HEADS, A_HEADS, KT, TQ, -1, 1, True, WINDOW)
    return bc, lanes(toe_s), lanes(toe_w)


def nsa_overlap(seq):
    n_cmp = (seq - CMP_LEN) // CMP_STRIDE + 1
    n_slc = seq // SLC_BLOCK
    cs = np.arange(N_CMP_PAD) * CMP_STRIDE
    ss = np.arange(n_slc) * SLC_BLOCK
    ov = ((cs[None, :] + CMP_LEN - 1 >= ss[:, None]) & (cs[None, :] <= ss[:, None] + SLC_BLOCK - 1)
          & (np.arange(N_CMP_PAD)[None, :] < n_cmp))
    return jnp.asarray(ov.astype(np.float32))


def nsa_attention(qbT, kc, vc, kslc, kwin, vT, sT, biasc, toe_s, toe_w):
    TQ, L, KT, G = NSA_TQ, NSA_L, NSA_KT, B_GROUPS
    bsz, n_kt = vT.shape[0], vT.shape[1]
    seq = n_kt * KT
    nq = seq // TQ
    n_slc = seq // SLC_BLOCK
    n_cmp = (seq - CMP_LEN) // CMP_STRIDE + 1
    n_sel = min(SLC_TOPN, n_slc)
    gd = G * HEAD_DIM
    vcT = vc.transpose(0, 1, 3, 2)
    once = pl.Buffered(1)
    k_spec = pl.BlockSpec((1, n_kt, KT, gd), lambda b, i: (b, 0, 0, 0))
    n_chain = 2 * G + G * (NSA_FAR_SPLIT - 1)
    return pl.pallas_call(
        functools.partial(_nsa_kernel, n_cmp=n_cmp, n_sel=n_sel, n_slc=n_slc),
        grid=(bsz, nq),
        in_specs=[pl.BlockSpec((1, B_HEADS * HEAD_DIM, TQ), lambda b, i: (b, 0, i)),
                  pl.BlockSpec((1, G, N_CMP_PAD, HEAD_DIM), lambda b, i: (b, 0, 0, 0)),
                  pl.BlockSpec((1, G, HEAD_DIM, N_CMP_PAD), lambda b, i: (b, 0, 0, 0)),
                  pl.BlockSpec((G, 1, N_CMP_PAD, L), lambda b, i: (0, i, 0, 0)),
                  pl.BlockSpec((n_slc, N_CMP_PAD), lambda b, i: (0, 0), pipeline_mode=once),
                  k_spec, k_spec,
                  pl.BlockSpec((1, n_kt, 2 * gd, KT), lambda b, i: (b, 0, 0, 0)),
                  pl.BlockSpec((G, NSA_SLC_REL, KT, L), lambda b, i: (0, 0, 0, 0), pipeline_mode=once),
                  pl.BlockSpec((G, NSA_WIN_REL, KT, L), lambda b, i: (0, 0, 0, 0), pipeline_mode=once),
                  pl.BlockSpec((1, SMALL_ROWS, TQ), lambda b, i: (b, 0, i))],
        out_specs=pl.BlockSpec((1, B_HEADS * HEAD_DIM, TQ), lambda b, i: (b, 0, i)),
        out_shape=jax.ShapeDtypeStruct((bsz, B_HEADS * HEAD_DIM, seq), BF16),
        scratch_shapes=([pltpu.VMEM((G, n_slc, 1, L), F32)]
                        + [pltpu.VMEM((1, L), F32), pltpu.VMEM((1, L), F32), pltpu.VMEM((HEAD_DIM, L), F32)] * n_chain),
        compiler_params=_cparams(("arbitrary", "arbitrary")),
        name="nsa_attention",
    )(qbT, kc, vcT, biasc, nsa_overlap(seq), kslc.reshape(bsz, n_kt, KT, gd), kwin.reshape(bsz, n_kt, KT, gd),
      vT, toe_s, toe_w, sT)


MOBA_T = MOBA_BLOCK


MOBA_HB = 8


PAIR = 2 * HEAD_DIM


def _moba_inproj_kernel(x_ref, wqT_ref, wk_ref, wvT_ref, qT_ref, k_ref, vT_ref):
    xb = x_ref[0].astype(BF16)
    nt = (((1,), (1,)), ((), ()))
    qT_ref[0] = lax.dot_general(wqT_ref[...], xb, nt, preferred_element_type=F32).astype(BF16)
    k_ref[0, 0] = jnp.dot(xb, wk_ref[...], preferred_element_type=F32).astype(BF16)
    vT_ref[0, 0] = lax.dot_general(wvT_ref[...], xb, nt, preferred_element_type=F32).astype(BF16)


def _pair_padded_qT(wq):
    n_heads = wq.shape[1] // HEAD_DIM
    wT = wq.T.reshape(n_heads, HEAD_DIM, wq.shape[0])
    z = jnp.zeros_like(wT)
    even = jnp.concatenate([wT, z], axis=1)
    odd = jnp.concatenate([z, wT], axis=1)
    is_even = (jnp.arange(n_heads) % 2 == 0)[:, None, None]
    return jnp.where(is_even, even, odd).reshape(n_heads * PAIR, wq.shape[0])


def moba_inproj(x3, w_in):
    bsz, seq, d = x3.shape
    T = MOBA_T
    n_blk = seq // T
    hd = C_HEADS * HEAD_DIM
    wqT = _pair_padded_qT(w_in[:, :hd] * QK_SCALE).astype(BF16)
    wk = w_in[:, hd:2 * hd].astype(BF16)
    wvT = w_in[:, 2 * hd:].T.astype(BF16)
    once = pl.Buffered(1)
    return pl.pallas_call(
        _moba_inproj_kernel,
        grid=(bsz, n_blk),
        in_specs=[pl.BlockSpec((1, T, d), lambda b, i: (b, i, 0)),
                  pl.BlockSpec(wqT.shape, lambda b, i: (0, 0), pipeline_mode=once),
                  pl.BlockSpec(wk.shape, lambda b, i: (0, 0), pipeline_mode=once),
                  pl.BlockSpec(wvT.shape, lambda b, i: (0, 0), pipeline_mode=once)],
        out_specs=[pl.BlockSpec((1, C_HEADS * PAIR, T), lambda b, i: (b, 0, i)),
                   pl.BlockSpec((1, 1, T, hd), lambda b, i: (b, i, 0, 0)),
                   pl.BlockSpec((1, 1, hd, T), lambda b, i: (b, i, 0, 0))],
        out_shape=[jax.ShapeDtypeStruct((bsz, C_HEADS * PAIR, seq), BF16),
                   jax.ShapeDtypeStruct((bsz, n_blk, T, hd), BF16),
                   jax.ShapeDtypeStruct((bsz, n_blk, hd, T), BF16)],
        compiler_params=_cparams(("arbitrary", "arbitrary")),
        name="moba_inproj",
    )(x3, wqT, wk, wvT)


def _moba_kernel(qT_ref, k_ref, vT_ref, bias_ref, o_ref, kmean_ref, selb_ref, *st, n_sel):
    qi = pl.program_id(2)
    T = MOBA_T
    n_blk = k_ref.shape[1]
    states = [st[3 * hh:3 * hh + 3] for hh in range(MOBA_HB)]

    @pl.when(qi == 0)
    def _():
        for j in range(n_blk):
            kmean_ref[j:j + 1, :] = jnp.mean(k_ref[0, j].astype(F32), axis=0, keepdims=True)

    qTs = [qT_ref[0, hh * PAIR:(hh + 1) * PAIR, :] for hh in range(MOBA_HB)]
    pair = lambda hh: slice((hh // 2) * PAIR, (hh // 2 + 1) * PAIR)
    j_idx = lax.broadcasted_iota(I32, (n_blk, T), 0)
    for hh in range(MOBA_HB):
        gate = jnp.dot(kmean_ref[:, pair(hh)], qTs[hh].astype(F32), preferred_element_type=F32,
                       precision=lax.Precision.HIGHEST)
        gv = jnp.where(j_idx < qi, gate, -jnp.inf)
        rank = jnp.zeros((n_blk, T), I32)
        for jp in range(n_blk):
            row = gv[jp:jp + 1, :]
            rank = rank + jnp.where(row > gv, 1, jnp.where((row == gv) & (jp < j_idx), 1, 0))
        selb = jnp.where(j_idx < qi, jnp.where(rank < n_sel, 0.0, NEG),
                         jnp.where(j_idx == qi, 0.0, NEG)).astype(F32)
        for j in range(n_blk):
            selb_ref[hh, j] = selb[j:j + 1, :]
        m_ref, l_ref, acc_ref = states[hh]
        m_ref[...] = jnp.full(m_ref.shape, NEG, F32)
        l_ref[...] = jnp.zeros(l_ref.shape, F32)
        acc_ref[...] = jnp.zeros(acc_ref.shape, F32)

    heads = range(MOBA_HB)
    far_bias = [bias_ref[2, hh, 0:1, :] for hh in heads]

    def far_scores(kb):
        return [jnp.dot(k_ref[0, kb, :, pair(hh)], qTs[hh], preferred_element_type=F32) for hh in heads]

    def far_segs(kb):
        return [[(T, selb_ref[hh, kb] + far_bias[hh])] for hh in heads]

    def near_scores(kb):
        return [jnp.dot(k_ref[0, kb, :, pair(hh)], qTs[hh], preferred_element_type=F32) + bias_ref[qi - kb, hh]
                for hh in heads]

    def near_segs(kb):
        return [[(T, selb_ref[hh, kb])] for hh in heads]

    def values(kb):
        return [vT_ref[0, kb, hh * HEAD_DIM:(hh + 1) * HEAD_DIM, :] for hh in heads]

    n_far = jnp.maximum(qi - 1, 0)
    _flash_loop(0, n_far, far_scores, values, states, far_segs)
    _flash_loop(n_far, qi + 1, near_scores, values, states, near_segs)
    for hh in range(MOBA_HB):
        o_ref[0, hh * HEAD_DIM:(hh + 1) * HEAD_DIM, :] = _flash_finish(*[r[...] for r in states[hh]]).astype(o_ref.dtype)


def moba_bias_tiles(rel_bias):
    assert MOBA_T + 1 >= T5_FAR
    t0 = bias_tiles(rel_bias, [0], C_HEADS, 0, MOBA_T, MOBA_T, -1, 1, True)
    t12 = bias_tiles(rel_bias, [MOBA_T, 4 * MOBA_T], C_HEADS, 0, MOBA_T, MOBA_T, -1, 1, False)
    return jnp.concatenate([t0, t12], axis=0)


def moba_attention(qT, k, vT, bias3):
    T = MOBA_T
    bsz, n_blk = k.shape[0], k.shape[1]
    seq = n_blk * T
    n_sel = min(MOBA_TOPK, n_blk - 1)
    HB = MOBA_HB
    assert HB % 2 == 0
    out = pl.pallas_call(
        functools.partial(_moba_kernel, n_sel=n_sel),
        grid=(bsz, C_HEADS // HB, n_blk),
        in_specs=[pl.BlockSpec((1, HB * PAIR, T), lambda b, h, i: (b, h, i)),
                  pl.BlockSpec((1, n_blk, T, HB * HEAD_DIM), lambda b, h, i: (b, 0, 0, h)),
                  pl.BlockSpec((1, n_blk, HB * HEAD_DIM, T), lambda b, h, i: (b, 0, h, 0)),
                  pl.BlockSpec((3, HB, T, T), lambda b, h, i: (0, h, 0, 0))],
        out_specs=pl.BlockSpec((1, HB * HEAD_DIM, T), lambda b, h, i: (b, h, i)),
        out_shape=jax.ShapeDtypeStruct((bsz, C_HEADS * HEAD_DIM, seq), BF16),
        scratch_shapes=([pltpu.VMEM((n_blk, HB * HEAD_DIM), F32), pltpu.VMEM((HB, n_blk, 1, T), F32)]
                        + [pltpu.VMEM((1, T), F32), pltpu.VMEM((1, T), F32), pltpu.VMEM((HEAD_DIM, T), F32)] * HB),
        compiler_params=_cparams(("arbitrary", "arbitrary", "arbitrary")),
        name="moba_attention",
    )(qT, k, vT, bias3)
    return out


def _projT_ln_kernel(*refs, n_in):
    aT_refs, w_refs = refs[:n_in], refs[n_in:2 * n_in]
    x_ref, g_ref, b_ref, o_ref = refs[2 * n_in:]
    tn = (((0,), (0,)), ((), ()))
    mix = lax.dot_general(aT_refs[0][0], w_refs[0][...], tn, preferred_element_type=F32)
    for aT_ref, w_ref in zip(aT_refs[1:], w_refs[1:]):
        mix = mix + lax.dot_general(aT_ref[0], w_ref[...], tn, preferred_element_type=F32)
    o_ref[0] = _layer_norm_rows(ALPHA * x_ref[0] + mix, g_ref[...], b_ref[...])


def projT_residual_ln(aTs, w, x3, g, b, tm=256):
    bsz, seq, d = x3.shape
    ws, k0 = [], 0
    for aT in aTs:
        ws.append(w[k0:k0 + aT.shape[1]])
        k0 += aT.shape[1]
    n_in = len(aTs)
    once = pl.Buffered(1)
    vec = pl.BlockSpec((1, d), lambda bb, i: (0, 0))
    return pl.pallas_call(
        functools.partial(_projT_ln_kernel, n_in=n_in),
        grid=(bsz, seq // tm),
        in_specs=([pl.BlockSpec((1, aT.shape[1], tm), lambda bb, i: (bb, 0, i)) for aT in aTs]
                  + [pl.BlockSpec(wi.shape, lambda bb, i: (0, 0), pipeline_mode=once) for wi in ws]
                  + [pl.BlockSpec((1, tm, d), lambda bb, i: (bb, i, 0)), vec, vec]),
        out_specs=pl.BlockSpec((1, tm, d), lambda bb, i: (bb, i, 0)),
        out_shape=jax.ShapeDtypeStruct((bsz, seq, d), F32),
        compiler_params=_cparams(("arbitrary", "arbitrary")),
        name="projT_residual_ln",
    )(*aTs, *ws, x3, g.reshape(1, d), b.reshape(1, d))


def _router_kernel(h_ref, w_ref, o_ref):
    o_ref[...] = jnp.dot(h_ref[...], w_ref[...], preferred_element_type=F32, precision=lax.Precision.HIGHEST)


def router_logits(h, router, tm=1024):
    m, d = h.shape
    lanes = 128
    w = jnp.pad(router, ((0, 0), (0, lanes - N_EXPERTS)))
    out = pl.pallas_call(
        _router_kernel,
        grid=(m // tm,),
        in_specs=[pl.BlockSpec((tm, d), lambda i: (i, 0)), pl.BlockSpec((d, lanes), lambda i: (0, 0))],
        out_specs=pl.BlockSpec((tm, lanes), lambda i: (i, 0)),
        out_shape=jax.ShapeDtypeStruct((m, lanes), F32),
        compiler_params=_cparams(("arbitrary",)),
        name="router_logits",
    )(h, w)
    return out[:, :N_EXPERTS]


def _add_ln_kernel(h_ref, y_ref, g_ref, b_ref, o_ref):
    o_ref[...] = _layer_norm_rows(ALPHA * h_ref[...] + y_ref[...].astype(F32), g_ref[...], b_ref[...])


def add_ln(h, y, g, b, tm=512):
    m, d = h.shape
    row = pl.BlockSpec((tm, d), lambda i: (i, 0))
    vec = pl.BlockSpec((1, d), lambda i: (0, 0))
    return pl.pallas_call(
        _add_ln_kernel, grid=(m // tm,), in_specs=[row, row, vec, vec], out_specs=row,
        out_shape=jax.ShapeDtypeStruct((m, d), F32),
        compiler_params=_cparams(("arbitrary",)), name="add_ln",
    )(h, y, g.reshape(1, d), b.reshape(1, d))


IDX_LANES = 128


def _issue_row_gather(idx_vmem_ref, idx_smem, sem_i, src_hbm, dst_slot_ref, sem_slot, n_rows):
    cp = pltpu.make_async_copy(idx_vmem_ref.at[0], idx_smem, sem_i)
    cp.start()
    cp.wait()

    for r in range(n_rows):
        row = idx_smem[r // IDX_LANES, r % IDX_LANES]
        pltpu.make_async_copy(src_hbm.at[pl.ds(row, 1)], dst_slot_ref.at[pl.ds(r, 1)], sem_slot).start()


def _pipelined_gather(idx0_ref, idxn_ref, idx_smem, sem_i, src_hbm, buf, sem_buf, n_rows):
    g = pl.program_id(0)
    slot = lax.rem(g, 2)

    @pl.when(g == 0)
    def _():
        _issue_row_gather(idx0_ref, idx_smem, sem_i, src_hbm, buf.at[0], sem_buf.at[0], n_rows)

    @pl.when(g + 1 < pl.num_programs(0))
    def _():
        _issue_row_gather(idxn_ref, idx_smem, sem_i, src_hbm, buf.at[1 - slot], sem_buf.at[1 - slot], n_rows)

    pltpu.make_async_copy(buf.at[slot], buf.at[slot], sem_buf.at[slot]).wait()
    return slot


def _gather_specs(n_steps, k):
    first = lambda g, *_: (0, 0, 0)
    nxt = lambda g, *_: (jnp.minimum(g + 1, n_steps - 1), 0, 0)
    return pl.BlockSpec((1, k, IDX_LANES), first), pl.BlockSpec((1, k, IDX_LANES), nxt)


def _moe_ffn_kernel(ge_ref, idx0_ref, idxn_ref, h_hbm, w1_ref, w3_ref, w2_ref, o_ref,
                    xbuf, idx_smem, sem_i, sem_x, *, ff_chunk):
    del ge_ref
    g = pl.program_id(0)
    slot = lax.rem(g, 2)
    wait_slot = lambda s: pltpu.make_async_copy(xbuf.at[s], xbuf.at[s], sem_x.at[s]).wait()

    @pl.when(g == 0)
    def _():
        _issue_row_gather(idx0_ref, idx_smem, sem_i, h_hbm, xbuf.at[0], sem_x.at[0], EXPERT_ROWS)

    wait_slot(slot)
    cp = pltpu.make_async_copy(idxn_ref.at[0], idx_smem, sem_i)
    cp.start()
    cp.wait()
    xb = xbuf[slot].astype(BF16)
    d_ff = w1_ref.shape[2]
    n_chunks = d_ff // ff_chunk
    per_chunk = -(-EXPERT_ROWS // n_chunks)
    acc = jnp.zeros((EXPERT_ROWS, w2_ref.shape[2]), F32)
    for ci in range(n_chunks):
        c = ci * ff_chunk
        a = jnp.dot(xb, w1_ref[0, :, c:c + ff_chunk], preferred_element_type=F32)
        u = jnp.dot(xb, w3_ref[0, :, c:c + ff_chunk], preferred_element_type=F32)
        hid = (a * jax.nn.sigmoid(a) * u).astype(BF16)
        acc = acc + jnp.dot(hid, w2_ref[0, c:c + ff_chunk, :], preferred_element_type=F32)
        for r in range(ci * per_chunk, min((ci + 1) * per_chunk, EXPERT_ROWS)):
            row = idx_smem[r // IDX_LANES, r % IDX_LANES]
            pltpu.make_async_copy(h_hbm.at[pl.ds(row, 1)], xbuf.at[1 - slot, pl.ds(r, 1)], sem_x.at[1 - slot]).start()
    o_ref[...] = acc

    @pl.when(g == pl.num_programs(0) - 1)
    def _():
        wait_slot(1 - slot)


def moe_expert_ffn(h, row_tok, grp_e, w1, w3, w2, ff_chunk=512):
    d = h.shape[1]
    d_ff = w1.shape[2]
    n_groups = grp_e.shape[0]
    k = EXPERT_ROWS // IDX_LANES
    idx = row_tok.reshape(n_groups, k, IDX_LANES)
    once = pl.Buffered(1)
    idx0_spec, idxn_spec = _gather_specs(n_groups, k)
    grid_spec = pltpu.PrefetchScalarGridSpec(
        num_scalar_prefetch=1,
        grid=(n_groups,),
        in_specs=[idx0_spec, idxn_spec, pl.BlockSpec(memory_space=pl.ANY),
                  pl.BlockSpec((1, d, d_ff), lambda g, ge: (ge[g], 0, 0), pipeline_mode=once),
                  pl.BlockSpec((1, d, d_ff), lambda g, ge: (ge[g], 0, 0), pipeline_mode=once),
                  pl.BlockSpec((1, d_ff, d), lambda g, ge: (ge[g], 0, 0), pipeline_mode=once)],
        out_specs=pl.BlockSpec((EXPERT_ROWS, d), lambda g, ge: (g, 0)),
        scratch_shapes=[pltpu.VMEM((2, EXPERT_ROWS, d), F32), pltpu.SMEM((k, IDX_LANES), I32),
                        pltpu.SemaphoreType.DMA(()), pltpu.SemaphoreType.DMA((2,))],
    )
    return pl.pallas_call(
        functools.partial(_moe_ffn_kernel, ff_chunk=ff_chunk),
        grid_spec=grid_spec,
        out_shape=jax.ShapeDtypeStruct((n_groups * EXPERT_ROWS, d), F32),
        compiler_params=_cparams(("arbitrary",)),
        name="moe_expert_ffn",
    )(grp_e, idx, idx, h, w1, w3, w2)


COMBINE_TM = 256


def _moe_combine_ln_kernel(idx0_ref, idxn_ref, y_hbm, h_ref, gate_ref, g_ref, b_ref, o_ref,
                           ybuf, idx_smem, sem_i, sem_y):
    tm = COMBINE_TM
    slot = _pipelined_gather(idx0_ref, idxn_ref, idx_smem, sem_i, y_hbm, ybuf, sem_y, TOP_K * tm)
    y = gate_ref[:, 0:1] * ybuf[slot, 0:tm, :]
    for j in range(1, TOP_K):
        y = y + gate_ref[:, j:j + 1] * ybuf[slot, j * tm:(j + 1) * tm, :]
    o_ref[...] = _layer_norm_rows(ALPHA * h_ref[...] + y, g_ref[...], b_ref[...])


def moe_combine_ln(h, y_rows, dest, gate, g, b):
    m, d = h.shape
    tm = COMBINE_TM
    n_tiles = m // tm
    k = TOP_K * tm // IDX_LANES
    idx = dest.reshape(n_tiles, tm, TOP_K).transpose(0, 2, 1).reshape(n_tiles, k, IDX_LANES)
    idx0_spec, idxn_spec = _gather_specs(n_tiles, k)
    row = pl.BlockSpec((tm, d), lambda i: (i, 0))
    vec = pl.BlockSpec((1, d), lambda i: (0, 0))
    return pl.pallas_call(
        _moe_combine_ln_kernel,
        grid=(n_tiles,),
        in_specs=[idx0_spec, idxn_spec, pl.BlockSpec(memory_space=pl.ANY), row,
                  pl.BlockSpec((tm, TOP_K), lambda i: (i, 0)), vec, vec],
        out_specs=row,
        out_shape=jax.ShapeDtypeStruct((m, d), F32),
        scratch_shapes=[pltpu.VMEM((2, TOP_K * tm, d), F32), pltpu.SMEM((k, IDX_LANES), I32),
                        pltpu.SemaphoreType.DMA(()), pltpu.SemaphoreType.DMA((2,))],
        compiler_params=_cparams(("arbitrary",)),
        name="moe_combine_ln",
    )(idx, idx, y_rows, h, gate, g.reshape(1, d), b.reshape(1, d))


def moe_dispatch_plan(logits):
    n_tok = logits.shape[0]
    top_val, top_e = lax.top_k(logits, TOP_K)
    gate = jax.nn.softmax(top_val, axis=-1)
    e_flat = top_e.reshape(-1)
    onehot = (e_flat[:, None] == jnp.arange(N_EXPERTS, dtype=e_flat.dtype)[None, :]).astype(I32)
    rank = jnp.take_along_axis(jnp.cumsum(onehot, axis=0) - onehot, e_flat[:, None], axis=1)[:, 0]
    counts = jnp.sum(onehot, axis=0)
    padded = (counts + EXPERT_ROWS - 1) // EXPERT_ROWS * EXPERT_ROWS
    pend = jnp.cumsum(padded)
    pstart = pend - padded
    dest = pstart[e_flat] + rank
    n_assign = n_tok * TOP_K
    n_rows = -(-n_assign // EXPERT_ROWS) * EXPERT_ROWS + N_EXPERTS * EXPERT_ROWS
    n_groups = n_rows // EXPERT_ROWS
    tok_flat = jnp.repeat(jnp.arange(n_tok, dtype=I32), TOP_K)
    row_tok = jnp.zeros((n_rows,), I32).at[dest].set(tok_flat)
    grp_e = jnp.minimum(jnp.searchsorted(pend, jnp.arange(n_groups, dtype=I32) * EXPERT_ROWS, side='right'),
                        N_EXPERTS - 1).astype(I32)
    return gate, dest.astype(I32), row_tok, grp_e


def kernel(x, rel_bias, e_w_in, e_q_norm, e_kv_norm, e_w_uq, e_w_uk, e_w_uv, e_w_qidx, e_pos_k, e_pos_v, e_ck1, e_ck2, e_cv1, e_cv2, e_w_out, e_ln1_g, e_ln1_b, e_ffn_w1, e_ffn_w3, e_ffn_w2, e_ln2_g, e_ln2_b, o_w_in, o_w_out, o_ln1_g, o_ln1_b, o_router, o_moe_w1, o_moe_w3, o_moe_w2, o_ln2_g, o_ln2_b):
    bsz, seq, d = x.shape
    m = bsz * seq
    xf = x.reshape(m, d)
    dsa_bias = dsa_bias_tiles(rel_bias)
    nsa_bc, nsa_toe_s, nsa_toe_w = nsa_bias_inputs(rel_bias, seq)
    moba_bias = moba_bias_tiles(rel_bias)
    gd = B_GROUPS * HEAD_DIM
    for layer in range(DEPTH):
        i = layer // 2
        if layer % 2 == 0:
            x3 = xf.reshape(bsz, seq, d)
            (qidxT, qlatT, sT, kidx, ckv, ckvT, qbT, kcmp, vcmp, kslc, kwin, vT) = even_inproj(
                x3, e_w_in[i], e_q_norm[i], e_kv_norm[i], e_w_uq[i], e_w_uk[i], e_w_qidx[i])
            o_aT = dsa_attention(qidxT, sT, qlatT, kidx, ckv, ckvT, e_w_uv[i], dsa_bias)
            kc = nsa_compress(kcmp.reshape(m, gd), e_pos_k[i], e_ck1[i], e_ck2[i], bsz, seq)
            vc = nsa_compress(vcmp.reshape(m, gd), e_pos_v[i], e_cv1[i], e_cv2[i], bsz, seq)
            o_bT = nsa_attention(qbT, kc, vc, kslc, kwin, vT, sT, nsa_bc, nsa_toe_s, nsa_toe_w)
            h = projT_residual_ln([o_aT, o_bT], e_w_out[i].astype(BF16), x3, e_ln1_g[i], e_ln1_b[i]).reshape(m, d)
            tm = 512
            xf = swiglu_ffn(h, jnp.zeros((m // tm,), I32), e_ffn_w1[i][None].astype(BF16),
                            e_ffn_w3[i][None].astype(BF16), e_ffn_w2[i][None].astype(BF16),
                            e_ln2_g[i], e_ln2_b[i], with_ln=True, out_dtype=F32, tm=tm, ff_chunk=1408)
        else:
            x3 = xf.reshape(bsz, seq, d)
            o_cT = moba_attention(*moba_inproj(x3, o_w_in[i]), moba_bias)
            h = projT_residual_ln([o_cT], o_w_out[i].astype(BF16), x3, o_ln1_g[i], o_ln1_b[i]).reshape(m, d)
            gate, dest, row_tok, grp_e = moe_dispatch_plan(router_logits(h, o_router[i]))
            y_rows = moe_expert_ffn(h, row_tok, grp_e, o_moe_w1[i].astype(BF16), o_moe_w3[i].astype(BF16),
                                    o_moe_w2[i].astype(BF16))
            xf = moe_combine_ln(h, y_rows, dest, gate, o_ln2_g[i], o_ln2_b[i])
    return xf.reshape(bsz, seq, d)
```

```python
import functools
import math

import numpy as np
import jax
import jax.numpy as jnp
from jax import lax
from jax.experimental import pallas as pl
from jax.experimental.pallas import tpu as pltpu

F32 = jnp.float32
BF16 = jnp.bfloat16
I32 = jnp.int32
BF16_ROWS = 16

HEAD_DIM = 64
NUM_BUCKETS = 32
MAX_DISTANCE = 128
N_BIAS_HEADS = 16
A_HEADS = 8
A_Q_RANK = 256
A_KV_RANK = 128
IDX_HEADS = 16
IDX_DIM = 64
DSA_TOPK = 256
B_HEADS = 8
B_GROUPS = 2
B_HPG = B_HEADS // B_GROUPS
CMP_LEN = 32
CMP_STRIDE = 16
SLC_BLOCK = 64
SLC_TOPN = 16
WINDOW = 512
C_HEADS = 16
MOBA_BLOCK = 256
MOBA_TOPK = 3
N_EXPERTS = 8
TOP_K = 2
EXPERT_ROWS = 256
DEPTH = 2
ALPHA = (2 * DEPTH) ** 0.25

LOG2E = 1.4426950408889634
QK_SCALE = HEAD_DIM ** -0.5 * LOG2E
NEG = -1e30
NEG_HALF = -5e29
INT_MIN = -2 ** 31
VMEM_LIMIT = 56 * 1024 * 1024


def _t5_thresholds():
    def bucket(n):
        if n < NUM_BUCKETS // 2:
            return n
        v = np.log(np.float32(n) / np.float32(NUM_BUCKETS // 2)) / np.float32(math.log(MAX_DISTANCE / (NUM_BUCKETS // 2)))
        return min(NUM_BUCKETS // 2 + int(np.float32(v) * (NUM_BUCKETS - NUM_BUCKETS // 2)), NUM_BUCKETS - 1)
    b = [bucket(i) for i in range(4 * MAX_DISTANCE)]
    return [0] + [min(i for i in range(len(b)) if b[i] >= k) for k in range(1, NUM_BUCKETS)]


T5_THR = _t5_thresholds()
T5_FAR = T5_THR[-1]


def _cparams(sem):
    return pltpu.CompilerParams(dimension_semantics=sem, vmem_limit_bytes=VMEM_LIMIT)


def _bias_kernel(tab_ref, off_ref, o_ref, *, c_row, c_col, h0, causal_neg, window):
    v = pl.program_id(0)
    h = pl.program_id(1) + h0
    shape = o_ref.shape[2:]
    dist = (c_col * lax.broadcasted_iota(I32, shape, 1) + c_row * lax.broadcasted_iota(I32, shape, 0) + off_ref[v])
    n = jnp.maximum(dist, 0)
    acc = jnp.full(shape, tab_ref[h] * LOG2E, F32)
    for k in range(1, NUM_BUCKETS):
        acc = jnp.where(n >= T5_THR[k], tab_ref[k * N_BIAS_HEADS + h] * LOG2E, acc)
    if causal_neg:
        acc = jnp.where(dist >= 0, acc, NEG)
    if window:
        acc = jnp.where(dist < window, acc, NEG)
    o_ref[0, 0] = acc


def bias_tiles(rel_bias, offs, n_heads, h0, rows, cols, c_row, c_col, causal_neg, window=0):
    offs = jnp.asarray(offs, I32)
    nv = offs.shape[0]
    return pl.pallas_call(
        functools.partial(_bias_kernel, c_row=c_row, c_col=c_col, h0=h0, causal_neg=causal_neg, window=window),
        grid=(nv, n_heads),
        in_specs=[pl.BlockSpec(memory_space=pltpu.SMEM), pl.BlockSpec(memory_space=pltpu.SMEM)],
        out_specs=pl.BlockSpec((1, 1, rows, cols), lambda v, h: (v, h, 0, 0)),
        out_shape=jax.ShapeDtypeStruct((nv, n_heads, rows, cols), F32),
        compiler_params=_cparams(("arbitrary", "arbitrary")),
        name="t5_bias_tiles",
    )(rel_bias.reshape(-1), offs)


def _layer_norm_rows(z, g, b):
    mu = jnp.mean(z, axis=-1, keepdims=True)
    zc = z - mu
    var = jnp.mean(zc * zc, axis=-1, keepdims=True)
    return zc * lax.rsqrt(var + 1e-5) * g + b


def _ffn_kernel(ge_ref, x_ref, w1_ref, w3_ref, w2_ref, g_ref, b_ref, o_ref, *, ff_chunk, with_ln):
    del ge_ref
    x = x_ref[...]
    xb = x.astype(BF16)
    d_ff = w1_ref.shape[2]
    acc = jnp.zeros((x.shape[0], w2_ref.shape[2]), F32)
    for c in range(0, d_ff, ff_chunk):
        a = jnp.dot(xb, w1_ref[0, :, c:c + ff_chunk], preferred_element_type=F32)
        u = jnp.dot(xb, w3_ref[0, :, c:c + ff_chunk], preferred_element_type=F32)
        hid = (a * jax.nn.sigmoid(a) * u).astype(BF16)
        acc = acc + jnp.dot(hid, w2_ref[0, c:c + ff_chunk, :], preferred_element_type=F32)
    if with_ln:
        o_ref[...] = _layer_norm_rows(ALPHA * x.astype(F32) + acc, g_ref[...], b_ref[...]).astype(o_ref.dtype)
    else:
        o_ref[...] = acc.astype(o_ref.dtype)


def swiglu_ffn(x_rows, grp_e, w1, w3, w2, ln_g, ln_b, *, with_ln, out_dtype, tm, ff_chunk):
    m, d = x_rows.shape
    d_ff = w1.shape[2]
    once = pl.Buffered(1)
    grid_spec = pltpu.PrefetchScalarGridSpec(
        num_scalar_prefetch=1,
        grid=(m // tm,),
        in_specs=[pl.BlockSpec((tm, d), lambda i, ge: (i, 0)),
                  pl.BlockSpec((1, d, d_ff), lambda i, ge: (ge[i], 0, 0), pipeline_mode=once),
                  pl.BlockSpec((1, d, d_ff), lambda i, ge: (ge[i], 0, 0), pipeline_mode=once),
                  pl.BlockSpec((1, d_ff, d), lambda i, ge: (ge[i], 0, 0), pipeline_mode=once),
                  pl.BlockSpec((1, d), lambda i, ge: (0, 0)), pl.BlockSpec((1, d), lambda i, ge: (0, 0))],
        out_specs=pl.BlockSpec((tm, d), lambda i, ge: (i, 0)),
    )
    return pl.pallas_call(
        functools.partial(_ffn_kernel, ff_chunk=ff_chunk, with_ln=with_ln),
        grid_spec=grid_spec,
        out_shape=jax.ShapeDtypeStruct((m, d), out_dtype),
        compiler_params=_cparams(("arbitrary",)),
        name="swiglu_ffn",
    )(grp_e, x_rows, w1, w3, w2, ln_g.reshape(1, d), ln_b.reshape(1, d))


def _flash_probs(s, m, segs=None):
    if segs is None:
        m_new = jnp.maximum(m, jnp.max(s, axis=0, keepdims=True))
        p = jnp.exp2(s - m_new)
    else:
        m_new, r0 = m, 0
        for n, c in segs:
            seg_max = jnp.max(s[r0:r0 + n], axis=0, keepdims=True)
            m_new = jnp.maximum(m_new, jnp.where(c > NEG_HALF, seg_max + c, NEG))
            r0 += n
        parts, r0 = [], 0
        for n, c in segs:
            shift = jnp.where(c > NEG_HALF, m_new - c, -NEG)
            parts.append(jnp.exp2(s[r0:r0 + n] - shift))
            r0 += n
        p = parts[0] if len(parts) == 1 else jnp.concatenate(parts, axis=0)
    return m_new, jnp.exp2(m - m_new), p.astype(BF16)


def _flash_merge(states):
    ms = [st[0][...] for st in states]
    m = functools.reduce(jnp.maximum, ms)
    ws = [jnp.exp2(mi - m) for mi in ms]
    l = sum(w * st[1][...] for w, st in zip(ws, states))
    acc = sum(w * st[2][...] for w, st in zip(ws, states))
    return m, l, acc


def _flash_finish(m, l, acc):
    return jnp.where(m > NEG_HALF, acc / l, 0.0)


def _rms_rows(x, g):
    return x * lax.rsqrt(jnp.mean(x * x, axis=-1, keepdims=True) + 1e-6) * g


EVEN_T = 256
SMALL_ROWS = 48
GATE_ROW0 = IDX_HEADS
NT_DIMS = (((1,), (1,)), ((), ()))


def _even_inproj_kernel(x_ref, wa_ref, wsT_ref, qn_ref, kvn_ref, wuq_ref, wuk_ref, wqiT_ref, wqbT_ref, wk4_ref, wvT_ref,
                        qidxT_ref, qlatT_ref, sT_ref, kidx_ref, ckv_ref, ckvT_ref,
                        qbT_ref, kcmp_ref, vcmp_ref, kslc_ref, kwin_ref, vT_ref):
    xb = x_ref[0].astype(BF16)
    ya = jnp.dot(xb, wa_ref[...], preferred_element_type=F32)
    cqn = _rms_rows(ya[:, :A_Q_RANK], qn_ref[...]).astype(BF16)
    ckvn = _rms_rows(ya[:, A_Q_RANK:A_Q_RANK + A_KV_RANK], kvn_ref[...])
    kidx_ref[0, 0] = ya[:, A_Q_RANK + A_KV_RANK:A_Q_RANK + A_KV_RANK + IDX_DIM].astype(BF16)
    ckv_ref[0, 0] = ckvn.astype(BF16)
    ckvT_ref[0, 0] = ckvn.T.astype(BF16)
    sT_ref[0] = lax.dot_general(wsT_ref[...], xb, NT_DIMS, preferred_element_type=F32)
    q = jnp.dot(cqn, wuq_ref[...], preferred_element_type=F32).astype(BF16)
    for h in range(A_HEADS):
        qlT = lax.dot_general(wuk_ref[h], q[:, h * HEAD_DIM:(h + 1) * HEAD_DIM], NT_DIMS, preferred_element_type=F32)
        qlatT_ref[0, h * A_KV_RANK:(h + 1) * A_KV_RANK, :] = (qlT * QK_SCALE).astype(BF16)
    qidxT_ref[0] = lax.dot_general(wqiT_ref[...], cqn, NT_DIMS, preferred_element_type=F32).astype(BF16)
    qbT_ref[0] = lax.dot_general(wqbT_ref[...], xb, NT_DIMS, preferred_element_type=F32).astype(BF16)
    yk = jnp.dot(xb, wk4_ref[...], preferred_element_type=F32).astype(BF16)
    gd = B_GROUPS * HEAD_DIM
    for j, ref in enumerate((kcmp_ref, vcmp_ref, kslc_ref, kwin_ref)):
        ref[0] = yk[:, j * gd:(j + 1) * gd]
    vT = lax.dot_general(wvT_ref[...], xb, NT_DIMS, preferred_element_type=F32).astype(BF16)
    for j in range(EVEN_T // NSA_KT):
        vT_ref[0, j] = vT[:, j * NSA_KT:(j + 1) * NSA_KT]


def even_inproj(x3, w_in, q_norm, kv_norm, w_uq, w_uk, w_qidx):
    bsz, seq, d = x3.shape
    T = EVEN_T
    nq = seq // T
    gd = B_GROUPS * HEAD_DIM
    n_kt = seq // NSA_KT
    o_kidx = A_Q_RANK + A_KV_RANK
    o_widx = o_kidx + IDX_DIM
    o_qb = o_widx + IDX_HEADS
    o_kv = o_qb + B_HEADS * HEAD_DIM
    o_gate = o_kv + 6 * gd
    kv = lambda j: w_in[:, o_kv + j * gd:o_kv + (j + 1) * gd]
    wa = jnp.pad(w_in[:, :o_widx], ((0, 0), (0, 512 - o_widx))).astype(BF16)
    w_gate = w_in[:, o_gate:].reshape(d, B_GROUPS, B_HPG, 3).transpose(0, 1, 3, 2).reshape(d, 3 * B_HEADS)
    wsT = jnp.concatenate([w_in[:, o_widx:o_qb] * IDX_HEADS ** -0.5, w_gate,
                           jnp.zeros((d, SMALL_ROWS - IDX_HEADS - 3 * B_HEADS), w_in.dtype)], axis=1).T.astype(BF16)
    wuq = w_uq.reshape(A_Q_RANK, A_HEADS * HEAD_DIM).astype(BF16)
    wuk = jnp.transpose(w_uk, (1, 0, 2)).astype(BF16)
    wqiT = w_qidx.reshape(A_Q_RANK, IDX_HEADS * IDX_DIM).T.astype(BF16)
    wqbT = (w_in[:, o_qb:o_kv] * QK_SCALE).T.astype(BF16)
    wk4 = jnp.concatenate([kv(0), kv(1), kv(2), kv(4)], axis=1).astype(BF16)
    wvT = jnp.concatenate([kv(3), kv(5)], axis=1).T.astype(BF16)
    weights = (wa, wsT, q_norm.reshape(1, -1), kv_norm.reshape(1, -1), wuq, wuk, wqiT, wqbT, wk4, wvT)
    once = pl.Buffered(1)
    w_specs = [pl.BlockSpec(w.shape, (lambda b, i, n=w.ndim: (0,) * n), pipeline_mode=once) for w in weights]
    fm = lambda rows: pl.BlockSpec((1, rows, T), lambda b, i: (b, 0, i))
    tok = lambda cols: pl.BlockSpec((1, T, cols), lambda b, i: (b, i, 0))
    blk = lambda r, c: pl.BlockSpec((1, 1, r, c), lambda b, i: (b, i, 0, 0))
    sds = jax.ShapeDtypeStruct
    return pl.pallas_call(
        _even_inproj_kernel,
        grid=(bsz, nq),
        in_specs=[pl.BlockSpec((1, T, d), lambda b, i: (b, i, 0))] + w_specs,
        out_specs=[fm(IDX_HEADS * IDX_DIM), fm(A_HEADS * A_KV_RANK), fm(SMALL_ROWS),
                   blk(T, IDX_DIM), blk(T, A_KV_RANK), blk(A_KV_RANK, T),
                   fm(B_HEADS * HEAD_DIM), tok(gd), tok(gd), tok(gd), tok(gd),
                   pl.BlockSpec((1, T // NSA_KT, 2 * gd, NSA_KT), lambda b, i: (b, i, 0, 0))],
        out_shape=[sds((bsz, IDX_HEADS * IDX_DIM, seq), BF16), sds((bsz, A_HEADS * A_KV_RANK, seq), BF16),
                   sds((bsz, SMALL_ROWS, seq), F32),
                   sds((bsz, nq, T, IDX_DIM), BF16), sds((bsz, nq, T, A_KV_RANK), BF16), sds((bsz, nq, A_KV_RANK, T), BF16),
                   sds((bsz, B_HEADS * HEAD_DIM, seq), BF16),
                   sds((bsz, seq, gd), BF16), sds((bsz, seq, gd), BF16), sds((bsz, seq, gd), BF16), sds((bsz, seq, gd), BF16),
                   sds((bsz, n_kt, 2 * gd, NSA_KT), BF16)],
        compiler_params=_cparams(("arbitrary", "arbitrary")),
        name="even_inproj",
    )(x3, *weights)


DSA_T = 256
SUB = 128


def _dsa_kernel(qidx_ref, wT_ref, qlat_ref, kidx_ref, ckv_ref, ckvT_ref, bias_ref, wuvt_ref, o_ref,
                key_ref, selb0_ref, selb1_ref, *state_refs, n_keep):
    selb_refs = (selb0_ref, selb1_ref)
    m_refs, l_refs, acc_refs = (state_refs[0:A_HEADS], state_refs[A_HEADS:2 * A_HEADS], state_refs[2 * A_HEADS:])
    qi = pl.program_id(1)
    nkb = qi + 1
    T = DSA_T

    def score_block(kb, carry):
        for sub in range(T // SUB):
            k = kidx_ref[0, kb, sub * SUB:(sub + 1) * SUB, :]
            acc = jnp.zeros((SUB, T), F32)
            for h in range(IDX_HEADS):
                d = jnp.dot(k, qidx_ref[0, h * IDX_DIM:(h + 1) * IDX_DIM, :], preferred_element_type=F32)
                acc = acc + jnp.maximum(d, 0.0) * wT_ref[0, h:h + 1, :]
            bits = lax.bitcast_convert_type(acc, I32)
            key = bits ^ (lax.shift_right_arithmetic(bits, 31) & 0x7FFFFFFF)
            s_pos = kb * T + sub * SUB + lax.broadcasted_iota(I32, (SUB, T), 0)
            t_pos = qi * T + lax.broadcasted_iota(I32, (SUB, T), 1)
            key = jnp.where(s_pos <= t_pos, key, INT_MIN)
            key_ref[pl.ds(pl.multiple_of(kb * T + sub * SUB, SUB), SUB), :] = key
        return carry

    lax.fori_loop(0, nkb, score_block, 0)

    def count_ge(cand):
        def body(kb, cnt):
            blk = key_ref[pl.ds(pl.multiple_of(kb * T, T), T), :]
            ge = jnp.where(blk >= cand, 1, 0).astype(I32)
            return cnt + jnp.sum(ge.reshape(T // 8, 8, T), axis=0)
        cnt = lax.fori_loop(0, nkb, body, jnp.zeros((8, T), I32))
        return jnp.sum(cnt, axis=0, keepdims=True)

    def bit_step(i, u):
        cand_u = u | lax.shift_left(jnp.int32(1), 31 - i)
        cnt = count_ge(cand_u ^ INT_MIN)
        return jnp.where(cnt >= n_keep, cand_u, u)

    u = lax.fori_loop(0, 32, bit_step, jnp.zeros((1, T), I32))
    thr = jnp.maximum(u ^ INT_MIN, INT_MIN + 1)

    for h in range(A_HEADS):
        m_refs[h][...] = jnp.full(m_refs[h].shape, NEG, F32)
        l_refs[h][...] = jnp.zeros(l_refs[h].shape, F32)
        acc_refs[h][...] = jnp.zeros(acc_refs[h].shape, F32)

    states = list(zip(m_refs, l_refs, acc_refs))
    far_bias = [bias_ref[2, h, 0:1, :] for h in range(A_HEADS)]

    def masked_scores(kb, selb):
        selb[...] = jnp.where(key_ref[pl.ds(pl.multiple_of(kb * T, T), T), :] >= thr, 0.0, NEG)
        ckv = ckv_ref[0, kb]
        return [jnp.dot(ckv, qlat_ref[0, h * A_KV_RANK:(h + 1) * A_KV_RANK, :], preferred_element_type=F32) + selb[...]
                for h in range(A_HEADS)]

    def far_body(i, carry):
        tiles = []
        for u, selb in enumerate(selb_refs):
            kb_raw = len(selb_refs) * i + u
            live = kb_raw < n_far
            kb = jnp.minimum(kb_raw, n_far - 1)
            segs = [[(T, jnp.where(live, far_bias[h], NEG))] for h in range(A_HEADS)]
            tiles.append((list(zip(masked_scores(kb, selb), segs)), [ckvT_ref[0, kb]] * A_HEADS))
        for s_all, v_all in tiles:
            _flash_step(s_all, v_all, states)
        return carry

    def near_body(kb, carry):
        s_all = [s + bias_ref[qi - kb, h] for h, s in enumerate(masked_scores(kb, selb_refs[0]))]
        _flash_step(s_all, [ckvT_ref[0, kb]] * A_HEADS, states)
        return carry

    n_far = jnp.maximum(qi - 1, 0)
    lax.fori_loop(0, lax.div(n_far + (len(selb_refs) - 1), jnp.int32(len(selb_refs))), far_body, 0)
    lax.fori_loop(n_far, nkb, near_body, 0)

    for h in range(A_HEADS):
        o_lat = _flash_finish(m_refs[h][...], l_refs[h][...], acc_refs[h][...]).astype(BF16)
        o_ref[0, h * HEAD_DIM:(h + 1) * HEAD_DIM, :] = jnp.dot(
            wuvt_ref[h], o_lat, preferred_element_type=F32).astype(o_ref.dtype)


def dsa_attention(qidxT, sT, qlatT, kidx, ckv, ckvT, w_uv, bias3):
    T = DSA_T
    assert T == EVEN_T
    bsz, nq = kidx.shape[0], kidx.shape[1]
    seq = nq * T
    n_keep = min(DSA_TOPK, seq // 4)
    wuvt = jnp.transpose(w_uv, (1, 2, 0)).astype(BF16)
    return pl.pallas_call(
        functools.partial(_dsa_kernel, n_keep=n_keep),
        grid=(bsz, nq),
        in_specs=[pl.BlockSpec((1, IDX_HEADS * IDX_DIM, T), lambda b, i: (b, 0, i)),
                  pl.BlockSpec((1, SMALL_ROWS, T), lambda b, i: (b, 0, i)),
                  pl.BlockSpec((1, A_HEADS * A_KV_RANK, T), lambda b, i: (b, 0, i)),
                  pl.BlockSpec((1, nq, T, IDX_DIM), lambda b, i: (b, 0, 0, 0)),
                  pl.BlockSpec((1, nq, T, A_KV_RANK), lambda b, i: (b, 0, 0, 0)),
                  pl.BlockSpec((1, nq, A_KV_RANK, T), lambda b, i: (b, 0, 0, 0)),
                  pl.BlockSpec((3, A_HEADS, T, T), lambda b, i: (0, 0, 0, 0)),
                  pl.BlockSpec((A_HEADS, HEAD_DIM, A_KV_RANK), lambda b, i: (0, 0, 0))],
        out_specs=pl.BlockSpec((1, A_HEADS * HEAD_DIM, T), lambda b, i: (b, 0, i)),
        out_shape=jax.ShapeDtypeStruct((bsz, A_HEADS * HEAD_DIM, seq), BF16),
        scratch_shapes=([pltpu.VMEM((seq, T), I32), pltpu.VMEM((T, T), F32), pltpu.VMEM((T, T), F32)]
                        + [pltpu.VMEM((1, T), F32)] * (2 * A_HEADS)
                        + [pltpu.VMEM((A_KV_RANK, T), F32)] * A_HEADS),
        compiler_params=_cparams(("arbitrary", "arbitrary")),
        name="dsa_attention",
    )(qidxT, sT, qlatT, kidx, ckv, ckvT, bias3, wuvt)


def dsa_bias_tiles(rel_bias):
    assert DSA_T + 1 >= T5_FAR
    return bias_tiles(rel_bias, [0, DSA_T, 4 * DSA_T], A_HEADS, 0, DSA_T, DSA_T, -1, 1, False)


N_CMP_PAD = 256


def _compress_kernel(blk_ref, pos_ref, w1_ref, w2_ref, o_ref):
    x = (blk_ref[0].astype(F32) + pos_ref[...]).astype(BF16)
    hid = jax.nn.gelu(jnp.dot(x, w1_ref[...], preferred_element_type=F32))
    o_ref[0] = jnp.dot(hid.astype(BF16), w2_ref[...], preferred_element_type=F32).astype(o_ref.dtype)


def nsa_compress(a, pos, w1, w2, bsz, seq):
    n_chunk = seq // CMP_STRIDE
    assert CMP_LEN == 2 * CMP_STRIDE and n_chunk <= N_CMP_PAD
    width = CMP_STRIDE * HEAD_DIM
    chunks = a.reshape(bsz, n_chunk, CMP_STRIDE, B_GROUPS, HEAD_DIM).transpose(0, 3, 1, 2, 4)
    chunks = chunks.reshape(bsz * B_GROUPS, n_chunk, width)
    blocks = jnp.concatenate([chunks[:, :-1], chunks[:, 1:]], axis=-1)
    blocks = jnp.pad(blocks, ((0, 0), (0, N_CMP_PAD - (n_chunk - 1)), (0, 0)))
    out = pl.pallas_call(
        _compress_kernel,
        grid=(bsz * B_GROUPS,),
        in_specs=[pl.BlockSpec((1, N_CMP_PAD, 2 * width), lambda i: (i, 0, 0)),
                  pl.BlockSpec((1, 2 * width), lambda i: (0, 0)),
                  pl.BlockSpec((2 * width, HEAD_DIM), lambda i: (0, 0)),
                  pl.BlockSpec((HEAD_DIM, HEAD_DIM), lambda i: (0, 0))],
        out_specs=pl.BlockSpec((1, N_CMP_PAD, HEAD_DIM), lambda i: (i, 0, 0)),
        out_shape=jax.ShapeDtypeStruct((bsz * B_GROUPS, N_CMP_PAD, HEAD_DIM), BF16),
        compiler_params=_cparams(("arbitrary",)),
        name="nsa_compress",
    )(blocks, pos.reshape(1, 2 * width), w1.reshape(2 * width, HEAD_DIM).astype(BF16), w2.astype(BF16))
    return out.reshape(bsz, B_GROUPS, N_CMP_PAD, HEAD_DIM)


NSA_TQ = 128
NSA_L = B_HPG * NSA_TQ
NSA_KT = 128
NSA_SLC_REL = 3
NSA_WIN_REL = 5
NSA_FAR_SPLIT = 4


def _flash_step(s_all, vT_all, states):
    probs = []
    for item, (m_ref, _, _) in zip(s_all, states):
        s, segs = item if isinstance(item, tuple) else (item, None)
        m_new, alpha, p = _flash_probs(s, m_ref[...], segs)
        m_ref[...] = m_new
        probs.append((alpha, p))
    for (alpha, p), vT, (_, l_ref, acc_ref) in zip(probs, vT_all, states):
        d = vT.shape[0]
        ones = jnp.ones((BF16_ROWS, vT.shape[1]), BF16)
        pv = jnp.dot(jnp.concatenate([vT, ones], axis=0), p, preferred_element_type=F32)
        acc_ref[...] = alpha * acc_ref[...] + pv[:d]
        l_ref[...] = alpha * l_ref[...] + pv[d:d + 1]


def _flash_loop(lo, hi, scores, values, states, segs, unroll=2):
    def body(i, carry):
        tiles = []
        for u in range(unroll):
            j_raw = lo + unroll * i + u
            live = j_raw < hi
            j = jnp.minimum(j_raw, hi - 1)
            sg = [[(n, jnp.where(live, c, NEG)) for n, c in chain] for chain in segs(j)]
            tiles.append((list(zip(scores(j), sg)), values(j)))
        for s_all, v_all in tiles:
            _flash_step(s_all, v_all, states)
        return carry

    lax.fori_loop(0, lax.div(hi - lo + (unroll - 1), jnp.int32(unroll)), body, 0)


def _nsa_kernel(qT_ref, kc_ref, vcT_ref, biasc_ref, ovl_ref, ks_ref, kw_ref, vT_ref,
                toes_ref, toew_ref, sT_ref, o_ref, selb_ref, *st, n_cmp, n_sel, n_slc):
    qi = pl.program_id(1)
    TQ, L = NSA_TQ, NSA_L
    q0 = qi * TQ
    qTs, qTs_pad = [], []
    for g in range(B_GROUPS):
        q = jnp.concatenate([qT_ref[0, (g * B_HPG + n) * HEAD_DIM:(g * B_HPG + n + 1) * HEAD_DIM, :]
                             for n in range(B_HPG)], axis=1)
        parts = [jnp.zeros_like(q)] * B_GROUPS
        parts[g] = q
        qTs.append(q)
        qTs_pad.append(jnp.concatenate(parts, axis=0))
    t_lane = q0 + (lax.broadcasted_iota(I32, (1, L), 1) & (TQ - 1))

    o_cs = []
    for g in range(B_GROUPS):
        s = jnp.dot(kc_ref[0, g], qTs[g], preferred_element_type=F32) + biasc_ref[g, 0]
        i_idx = lax.broadcasted_iota(I32, (N_CMP_PAD, L), 0)
        valid = jnp.where(i_idx < n_cmp, i_idx * CMP_STRIDE + (CMP_LEN - 1), 2 ** 30) <= t_lane
        s = jnp.where(valid, s, NEG)
        m = jnp.max(s, axis=0, keepdims=True)
        p = jnp.where(valid, jnp.exp2(s - m), 0.0)
        l = jnp.sum(p, axis=0, keepdims=True)
        p_c = p / jnp.where(l > 0, l, 1.0)
        o_c = jnp.dot(vcT_ref[0, g], p_c.astype(BF16), preferred_element_type=F32)

        psum = p_c[:, 0:TQ]
        for n in range(1, B_HPG):
            psum = psum + p_c[:, n * TQ:(n + 1) * TQ]
        sc = jnp.dot(ovl_ref[...], psum, preferred_element_type=F32, precision=lax.Precision.HIGHEST)
        j_idx = lax.broadcasted_iota(I32, (n_slc, TQ), 0)
        cur = (q0 + lax.broadcasted_iota(I32, (1, TQ), 1)) // SLC_BLOCK
        adm = j_idx <= cur
        forced = (j_idx == 0) | (j_idx == cur) | (j_idx == cur - 1)
        scv = jnp.where(adm, jnp.where(forced, jnp.inf, sc), -jnp.inf)
        rank = jnp.zeros((n_slc, TQ), I32)
        for jp in range(n_slc):
            row = scv[jp:jp + 1, :]
            beats = jnp.where(row > scv, 1, jnp.where((row == scv) & (jp < j_idx), 1, 0))
            rank = rank + beats
        selb = jnp.where(rank < n_sel, 0.0, NEG).astype(F32)
        selb4 = jnp.concatenate([selb] * B_HPG, axis=1)
        for j in range(n_slc):
            selb_ref[g, j] = selb4[j:j + 1, :]
        o_cs.append(o_c)

    for ref in st[0::3]:
        ref[...] = jnp.full(ref.shape, NEG, F32)
    for ref in st[1::3] + st[2::3]:
        ref[...] = jnp.zeros(ref.shape, F32)
    slc_st = [st[6 * g:6 * g + 3] for g in range(B_GROUPS)]
    win_st = [st[6 * g + 3:6 * g + 6] for g in range(B_GROUPS)]
    n_main = 6 * B_GROUPS
    xtr_st = [[st[n_main + 3 * (g * (NSA_FAR_SPLIT - 1) + r):n_main + 3 * (g * (NSA_FAR_SPLIT - 1) + r) + 3]
               for r in range(NSA_FAR_SPLIT - 1)] for g in range(B_GROUPS)]
    per_kt = NSA_KT // SLC_BLOCK

    groups = range(B_GROUPS)
    far_bias = [toes_ref[g, NSA_SLC_REL - 1, 0:1, :] for g in groups]

    def slc_scores(g, jt, near):
        s = jnp.dot(ks_ref[0, jt], qTs_pad[g], preferred_element_type=F32)
        return s + toes_ref[g, jnp.minimum(qi - jt, NSA_SLC_REL - 1)] if near else s

    def slc_segs(g, jt, near):
        return [(SLC_BLOCK, selb_ref[g, per_kt * jt + r] + (0.0 if near else far_bias[g])) for r in range(per_kt)]

    def win_scores(g, jt):
        rel = jnp.minimum(qi - jt, NSA_WIN_REL - 1)
        return jnp.dot(kw_ref[0, jt], qTs_pad[g], preferred_element_type=F32) + toew_ref[g, rel]

    gd = B_GROUPS * HEAD_DIM
    v_slc = lambda g, jt: vT_ref[0, jt, g * HEAD_DIM:(g + 1) * HEAD_DIM, :]
    v_win = lambda g, jt: vT_ref[0, jt, gd + g * HEAD_DIM:gd + (g + 1) * HEAD_DIM, :]

    assert NSA_WIN_REL >= NSA_SLC_REL
    j_lo = jnp.maximum(qi - (NSA_WIN_REL - 1), 0)

    def far_body(i, carry):
        s_all, v_all, chains = [], [], []
        for r in range(NSA_FAR_SPLIT):
            jt_raw = NSA_FAR_SPLIT * i + r
            live = jt_raw < j_lo
            jt = jnp.minimum(jt_raw, j_lo - 1)
            for g in groups:
                segs = [(n, jnp.where(live, c, NEG)) for n, c in slc_segs(g, jt, False)]
                s_all.append((slc_scores(g, jt, False), segs))
                v_all.append(v_slc(g, jt))
                chains.append(slc_st[g] if r == 0 else xtr_st[g][r - 1])
        _flash_step(s_all, v_all, chains)
        return carry

    zero_row = jnp.zeros((1, L), F32)
    lax.fori_loop(0, lax.div(j_lo + (NSA_FAR_SPLIT - 1), jnp.int32(NSA_FAR_SPLIT)), far_body, 0)
    _flash_loop(j_lo, qi + 1,
                lambda jt: [slc_scores(g, jt, True) for g in groups] + [win_scores(g, jt) for g in groups],
                lambda jt: [v_slc(g, jt) for g in groups] + [v_win(g, jt) for g in groups],
                slc_st + win_st,
                lambda jt: [slc_segs(g, jt, True) for g in groups] + [[(NSA_KT, zero_row)] for g in groups])

    for g in range(B_GROUPS):
        o_s = _flash_finish(*_flash_merge([slc_st[g]] + xtr_st[g]))
        o_w = _flash_finish(*[r[...] for r in win_st[g]])
        row0 = GATE_ROW0 + g * 3 * B_HPG
        gate = [jax.nn.sigmoid(jnp.concatenate([sT_ref[0, row0 + j * B_HPG + n:row0 + j * B_HPG + n + 1, :]
                                                for n in range(B_HPG)], axis=1)) for j in range(3)]
        o = (gate[0] * o_cs[g] + gate[1] * o_s + gate[2] * o_w).astype(o_ref.dtype)
        for n in range(B_HPG):
            o_ref[0, (g * B_HPG + n) * HEAD_DIM:(g * B_HPG + n + 1) * HEAD_DIM, :] = o[:, n * TQ:(n + 1) * TQ]


def nsa_bias_inputs(rel_bias, seq):
    TQ, L, KT = NSA_TQ, NSA_L, NSA_KT
    nq = seq // TQ
    bc = bias_tiles(rel_bias, [-(CMP_LEN - 1)], B_HEADS, A_HEADS, N_CMP_PAD, seq, -CMP_STRIDE, 1, False, 0)
    bc = bc.reshape(B_GROUPS, B_HPG, N_CMP_PAD, nq, TQ).transpose(0, 3, 2, 1, 4).reshape(B_GROUPS, nq, N_CMP_PAD, L)

    def lanes(t):
        v = t.shape[0]
        return t.reshape(v, B_GROUPS, B_HPG, KT, TQ).transpose(1, 0, 3, 2, 4).reshape(B_GROUPS, v, KT, L)

    assert KT == TQ and (NSA_SLC_REL - 1) * KT - (KT - 1) >= T5_FAR
    toe_s = bias_tiles(rel_bias, [v * KT for v in range(NSA_SLC_REL - 1)] + [64 * KT],
                       B_HEADS, A_HEADS, KT, TQ, -1, 1, True, 0)
    assert (NSA_WIN_REL - 1) * KT - (KT - 1) < WINDOW <= NSA_WIN_REL * KT - (KT - 1)
    toe_w = bias_tiles(rel_bias, [v * KT for v in range(NSA_WIN_REL)], B_HEADS, A_HEADS, KT, TQ, -1, 1, True, WINDOW)
    return bc, lanes(toe_s), lanes(toe_w)


def nsa_overlap(seq):
    n_cmp = (seq - CMP_LEN) // CMP_STRIDE + 1
    n_slc = seq // SLC_BLOCK
    cs = np.arange(N_CMP_PAD) * CMP_STRIDE
    ss = np.arange(n_slc) * SLC_BLOCK
    ov = ((cs[None, :] + CMP_LEN - 1 >= ss[:, None]) & (cs[None, :] <= ss[:, None] + SLC_BLOCK - 1)
          & (np.arange(N_CMP_PAD)[None, :] < n_cmp))
    return jnp.asarray(ov.astype(np.float32))


def nsa_attention(qbT, kc, vc, kslc, kwin, vT, sT, biasc, toe_s, toe_w):
    TQ, L, KT, G = NSA_TQ, NSA_L, NSA_KT, B_GROUPS
    bsz, n_kt = vT.shape[0], vT.shape[1]
    seq = n_kt * KT
    nq = seq // TQ
    n_slc = seq // SLC_BLOCK
    n_cmp = (seq - CMP_LEN) // CMP_STRIDE + 1
    n_sel = min(SLC_TOPN, n_slc)
    gd = G * HEAD_DIM
    vcT = vc.transpose(0, 1, 3, 2)
    once = pl.Buffered(1)
    k_spec = pl.BlockSpec((1, n_kt, KT, gd), lambda b, i: (b, 0, 0, 0))
    n_chain = 2 * G + G * (NSA_FAR_SPLIT - 1)
    return pl.pallas_call(
        functools.partial(_nsa_kernel, n_cmp=n_cmp, n_sel=n_sel, n_slc=n_slc),
        grid=(bsz, nq),
        in_specs=[pl.BlockSpec((1, B_HEADS * HEAD_DIM, TQ), lambda b, i: (b, 0, i)),
                  pl.BlockSpec((1, G, N_CMP_PAD, HEAD_DIM), lambda b, i: (b, 0, 0, 0)),
                  pl.BlockSpec((1, G, HEAD_DIM, N_CMP_PAD), lambda b, i: (b, 0, 0, 0)),
                  pl.BlockSpec((G, 1, N_CMP_PAD, L), lambda b, i: (0, i, 0, 0)),
                  pl.BlockSpec((n_slc, N_CMP_PAD), lambda b, i: (0, 0), pipeline_mode=once),
                  k_spec, k_spec,
                  pl.BlockSpec((1, n_kt, 2 * gd, KT), lambda b, i: (b, 0, 0, 0)),
                  pl.BlockSpec((G, NSA_SLC_REL, KT, L), lambda b, i: (0, 0, 0, 0), pipeline_mode=once),
                  pl.BlockSpec((G, NSA_WIN_REL, KT, L), lambda b, i: (0, 0, 0, 0), pipeline_mode=once),
                  pl.BlockSpec((1, SMALL_ROWS, TQ), lambda b, i: (b, 0, i))],
        out_specs=pl.BlockSpec((1, B_HEADS * HEAD_DIM, TQ), lambda b, i: (b, 0, i)),
        out_shape=jax.ShapeDtypeStruct((bsz, B_HEADS * HEAD_DIM, seq), BF16),
        scratch_shapes=([pltpu.VMEM((G, n_slc, 1, L), F32)]
                        + [pltpu.VMEM((1, L), F32), pltpu.VMEM((1, L), F32), pltpu.VMEM((HEAD_DIM, L), F32)] * n_chain),
        compiler_params=_cparams(("arbitrary", "arbitrary")),
        name="nsa_attention",
    )(qbT, kc, vcT, biasc, nsa_overlap(seq), kslc.reshape(bsz, n_kt, KT, gd), kwin.reshape(bsz, n_kt, KT, gd),
      vT, toe_s, toe_w, sT)


MOBA_T = MOBA_BLOCK


MOBA_HB = 8


PAIR = 2 * HEAD_DIM


def _moba_inproj_kernel(x_ref, wqT_ref, wk_ref, wvT_ref, qT_ref, k_ref, vT_ref):
    xb = x_ref[0].astype(BF16)
    nt = (((1,), (1,)), ((), ()))
    qT_ref[0] = lax.dot_general(wqT_ref[...], xb, nt, preferred_element_type=F32).astype(BF16)
    k_ref[0, 0] = jnp.dot(xb, wk_ref[...], preferred_element_type=F32).astype(BF16)
    vT_ref[0, 0] = lax.dot_general(wvT_ref[...], xb, nt, preferred_element_type=F32).astype(BF16)


def _pair_padded_qT(wq):
    n_heads = wq.shape[1] // HEAD_DIM
    wT = wq.T.reshape(n_heads, HEAD_DIM, wq.shape[0])
    z = jnp.zeros_like(wT)
    even = jnp.concatenate([wT, z], axis=1)
    odd = jnp.concatenate([z, wT], axis=1)
    is_even = (jnp.arange(n_heads) % 2 == 0)[:, None, None]
    return jnp.where(is_even, even, odd).reshape(n_heads * PAIR, wq.shape[0])


def moba_inproj(x3, w_in):
    bsz, seq, d = x3.shape
    T = MOBA_T
    n_blk = seq // T
    hd = C_HEADS * HEAD_DIM
    wqT = _pair_padded_qT(w_in[:, :hd] * QK_SCALE).astype(BF16)
    wk = w_in[:, hd:2 * hd].astype(BF16)
    wvT = w_in[:, 2 * hd:].T.astype(BF16)
    once = pl.Buffered(1)
    return pl.pallas_call(
        _moba_inproj_kernel,
        grid=(bsz, n_blk),
        in_specs=[pl.BlockSpec((1, T, d), lambda b, i: (b, i, 0)),
                  pl.BlockSpec(wqT.shape, lambda b, i: (0, 0), pipeline_mode=once),
                  pl.BlockSpec(wk.shape, lambda b, i: (0, 0), pipeline_mode=once),
                  pl.BlockSpec(wvT.shape, lambda b, i: (0, 0), pipeline_mode=once)],
        out_specs=[pl.BlockSpec((1, C_HEADS * PAIR, T), lambda b, i: (b, 0, i)),
                   pl.BlockSpec((1, 1, T, hd), lambda b, i: (b, i, 0, 0)),
                   pl.BlockSpec((1, 1, hd, T), lambda b, i: (b, i, 0, 0))],
        out_shape=[jax.ShapeDtypeStruct((bsz, C_HEADS * PAIR, seq), BF16),
                   jax.ShapeDtypeStruct((bsz, n_blk, T, hd), BF16),
                   jax.ShapeDtypeStruct((bsz, n_blk, hd, T), BF16)],
        compiler_params=_cparams(("arbitrary", "arbitrary")),
        name="moba_inproj",
    )(x3, wqT, wk, wvT)


def _moba_kernel(qT_ref, k_ref, vT_ref, bias_ref, o_ref, kmean_ref, selb_ref, *st, n_sel):
    qi = pl.program_id(2)
    T = MOBA_T
    n_blk = k_ref.shape[1]
    states = [st[3 * hh:3 * hh + 3] for hh in range(MOBA_HB)]

    @pl.when(qi == 0)
    def _():
        for j in range(n_blk):
            kmean_ref[j:j + 1, :] = jnp.mean(k_ref[0, j].astype(F32), axis=0, keepdims=True)

    qTs = [qT_ref[0, hh * PAIR:(hh + 1) * PAIR, :] for hh in range(MOBA_HB)]
    pair = lambda hh: slice((hh // 2) * PAIR, (hh // 2 + 1) * PAIR)
    j_idx = lax.broadcasted_iota(I32, (n_blk, T), 0)
    for hh in range(MOBA_HB):
        gate = jnp.dot(kmean_ref[:, pair(hh)], qTs[hh].astype(F32), preferred_element_type=F32,
                       precision=lax.Precision.HIGHEST)
        gv = jnp.where(j_idx < qi, gate, -jnp.inf)
        rank = jnp.zeros((n_blk, T), I32)
        for jp in range(n_blk):
            row = gv[jp:jp + 1, :]
            rank = rank + jnp.where(row > gv, 1, jnp.where((row == gv) & (jp < j_idx), 1, 0))
        selb = jnp.where(j_idx < qi, jnp.where(rank < n_sel, 0.0, NEG),
                         jnp.where(j_idx == qi, 0.0, NEG)).astype(F32)
        for j in range(n_blk):
            selb_ref[hh, j] = selb[j:j + 1, :]
        m_ref, l_ref, acc_ref = states[hh]
        m_ref[...] = jnp.full(m_ref.shape, NEG, F32)
        l_ref[...] = jnp.zeros(l_ref.shape, F32)
        acc_ref[...] = jnp.zeros(acc_ref.shape, F32)

    heads = range(MOBA_HB)
    far_bias = [bias_ref[2, hh, 0:1, :] for hh in heads]

    def far_scores(kb):
        return [jnp.dot(k_ref[0, kb, :, pair(hh)], qTs[hh], preferred_element_type=F32) for hh in heads]

    def far_segs(kb):
        return [[(T, selb_ref[hh, kb] + far_bias[hh])] for hh in heads]

    def near_scores(kb):
        return [jnp.dot(k_ref[0, kb, :, pair(hh)], qTs[hh], preferred_element_type=F32) + bias_ref[qi - kb, hh]
                for hh in heads]

    def near_segs(kb):
        return [[(T, selb_ref[hh, kb])] for hh in heads]

    def values(kb):
        return [vT_ref[0, kb, hh * HEAD_DIM:(hh + 1) * HEAD_DIM, :] for hh in heads]

    n_far = jnp.maximum(qi - 1, 0)
    _flash_loop(0, n_far, far_scores, values, states, far_segs)
    _flash_loop(n_far, qi + 1, near_scores, values, states, near_segs)
    for hh in range(MOBA_HB):
        o_ref[0, hh * HEAD_DIM:(hh + 1) * HEAD_DIM, :] = _flash_finish(*[r[...] for r in states[hh]]).astype(o_ref.dtype)


def moba_bias_tiles(rel_bias):
    assert MOBA_T + 1 >= T5_FAR
    t0 = bias_tiles(rel_bias, [0], C_HEADS, 0, MOBA_T, MOBA_T, -1, 1, True)
    t12 = bias_tiles(rel_bias, [MOBA_T, 4 * MOBA_T], C_HEADS, 0, MOBA_T, MOBA_T, -1, 1, False)
    return jnp.concatenate([t0, t12], axis=0)


def moba_attention(qT, k, vT, bias3):
    T = MOBA_T
    bsz, n_blk = k.shape[0], k.shape[1]
    seq = n_blk * T
    n_sel = min(MOBA_TOPK, n_blk - 1)
    HB = MOBA_HB
    assert HB % 2 == 0
    out = pl.pallas_call(
        functools.partial(_moba_kernel, n_sel=n_sel),
        grid=(bsz, C_HEADS // HB, n_blk),
        in_specs=[pl.BlockSpec((1, HB * PAIR, T), lambda b, h, i: (b, h, i)),
                  pl.BlockSpec((1, n_blk, T, HB * HEAD_DIM), lambda b, h, i: (b, 0, 0, h)),
                  pl.BlockSpec((1, n_blk, HB * HEAD_DIM, T), lambda b, h, i: (b, 0, h, 0)),
                  pl.BlockSpec((3, HB, T, T), lambda b, h, i: (0, h, 0, 0))],
        out_specs=pl.BlockSpec((1, HB * HEAD_DIM, T), lambda b, h, i: (b, h, i)),
        out_shape=jax.ShapeDtypeStruct((bsz, C_HEADS * HEAD_DIM, seq), BF16),
        scratch_shapes=([pltpu.VMEM((n_blk, HB * HEAD_DIM), F32), pltpu.VMEM((HB, n_blk, 1, T), F32)]
                        + [pltpu.VMEM((1, T), F32), pltpu.VMEM((1, T), F32), pltpu.VMEM((HEAD_DIM, T), F32)] * HB),
        compiler_params=_cparams(("arbitrary", "arbitrary", "arbitrary")),
        name="moba_attention",
    )(qT, k, vT, bias3)
    return out


def _projT_ln_kernel(*refs, n_in):
    aT_refs, w_refs = refs[:n_in], refs[n_in:2 * n_in]
    x_ref, g_ref, b_ref, o_ref = refs[2 * n_in:]
    tn = (((0,), (0,)), ((), ()))
    mix = lax.dot_general(aT_refs[0][0], w_refs[0][...], tn, preferred_element_type=F32)
    for aT_ref, w_ref in zip(aT_refs[1:], w_refs[1:]):
        mix = mix + lax.dot_general(aT_ref[0], w_ref[...], tn, preferred_element_type=F32)
    o_ref[0] = _layer_norm_rows(ALPHA * x_ref[0] + mix, g_ref[...], b_ref[...])


def projT_residual_ln(aTs, w, x3, g, b, tm=256):
    bsz, seq, d = x3.shape
    ws, k0 = [], 0
    for aT in aTs:
        ws.append(w[k0:k0 + aT.shape[1]])
        k0 += aT.shape[1]
    n_in = len(aTs)
    once = pl.Buffered(1)
    vec = pl.BlockSpec((1, d), lambda bb, i: (0, 0))
    return pl.pallas_call(
        functools.partial(_projT_ln_kernel, n_in=n_in),
        grid=(bsz, seq // tm),
        in_specs=([pl.BlockSpec((1, aT.shape[1], tm), lambda bb, i: (bb, 0, i)) for aT in aTs]
                  + [pl.BlockSpec(wi.shape, lambda bb, i: (0, 0), pipeline_mode=once) for wi in ws]
                  + [pl.BlockSpec((1, tm, d), lambda bb, i: (bb, i, 0)), vec, vec]),
        out_specs=pl.BlockSpec((1, tm, d), lambda bb, i: (bb, i, 0)),
        out_shape=jax.ShapeDtypeStruct((bsz, seq, d), F32),
        compiler_params=_cparams(("arbitrary", "arbitrary")),
        name="projT_residual_ln",
    )(*aTs, *ws, x3, g.reshape(1, d), b.reshape(1, d))


def _router_kernel(h_ref, w_ref, o_ref):
    o_ref[...] = jnp.dot(h_ref[...], w_ref[...], preferred_element_type=F32, precision=lax.Precision.HIGHEST)


def router_logits(h, router, tm=1024):
    m, d = h.shape
    lanes = 128
    w = jnp.pad(router, ((0, 0), (0, lanes - N_EXPERTS)))
    out = pl.pallas_call(
        _router_kernel,
        grid=(m // tm,),
        in_specs=[pl.BlockSpec((tm, d), lambda i: (i, 0)), pl.BlockSpec((d, lanes), lambda i: (0, 0))],
        out_specs=pl.BlockSpec((tm, lanes), lambda i: (i, 0)),
        out_shape=jax.ShapeDtypeStruct((m, lanes), F32),
        compiler_params=_cparams(("arbitrary",)),
        name="router_logits",
    )(h, w)
    return out[:, :N_EXPERTS]


IDX_LANES = 128


def _issue_row_gather(idx_vmem_ref, idx_smem, sem_i, src_hbm, dst_slot_ref, sem_slot, n_rows):
    cp = pltpu.make_async_copy(idx_vmem_ref.at[0], idx_smem, sem_i)
    cp.start()
    cp.wait()

    for r in range(n_rows):
        row = idx_smem[r // IDX_LANES, r % IDX_LANES]
        pltpu.make_async_copy(src_hbm.at[pl.ds(row, 1)], dst_slot_ref.at[pl.ds(r, 1)], sem_slot).start()


def _pipelined_gather(idx0_ref, idxn_ref, idx_smem, sem_i, src_hbm, buf, sem_buf, n_rows):
    g = pl.program_id(0)
    slot = lax.rem(g, 2)

    @pl.when(g == 0)
    def _():
        _issue_row_gather(idx0_ref, idx_smem, sem_i, src_hbm, buf.at[0], sem_buf.at[0], n_rows)

    @pl.when(g + 1 < pl.num_programs(0))
    def _():
        _issue_row_gather(idxn_ref, idx_smem, sem_i, src_hbm, buf.at[1 - slot], sem_buf.at[1 - slot], n_rows)

    pltpu.make_async_copy(buf.at[slot], buf.at[slot], sem_buf.at[slot]).wait()
    return slot


def _gather_specs(n_steps, k):
    first = lambda g, *_: (0, 0, 0)
    nxt = lambda g, *_: (jnp.minimum(g + 1, n_steps - 1), 0, 0)
    return pl.BlockSpec((1, k, IDX_LANES), first), pl.BlockSpec((1, k, IDX_LANES), nxt)


def _moe_ffn_kernel(ge_ref, idx0_ref, idxn_ref, h_hbm, w1_ref, w3_ref, w2_ref, o_ref,
                    xbuf, idx_smem, sem_i, sem_x, *, ff_chunk):
    del ge_ref
    g = pl.program_id(0)
    slot = lax.rem(g, 2)
    wait_slot = lambda s: pltpu.make_async_copy(xbuf.at[s], xbuf.at[s], sem_x.at[s]).wait()

    @pl.when(g == 0)
    def _():
        _issue_row_gather(idx0_ref, idx_smem, sem_i, h_hbm, xbuf.at[0], sem_x.at[0], EXPERT_ROWS)

    wait_slot(slot)
    cp = pltpu.make_async_copy(idxn_ref.at[0], idx_smem, sem_i)
    cp.start()
    xb = xbuf[slot].astype(BF16)
    d_ff = w1_ref.shape[2]
    n_chunks = d_ff // ff_chunk
    n_issue = max(n_chunks - 2, 1)
    per_chunk = -(-EXPERT_ROWS // n_issue)
    acc = jnp.zeros((EXPERT_ROWS, w2_ref.shape[2]), F32)
    for ci in range(n_chunks):
        c = ci * ff_chunk
        a = jnp.dot(xb, w1_ref[0, :, c:c + ff_chunk], preferred_element_type=F32)
        u = jnp.dot(xb, w3_ref[0, :, c:c + ff_chunk], preferred_element_type=F32)
        hid = (a * jax.nn.sigmoid(a) * u).astype(BF16)
        acc = acc + jnp.dot(hid, w2_ref[0, c:c + ff_chunk, :], preferred_element_type=F32)
        if ci == 0:
            cp.wait()
        for r in range(ci * per_chunk, min((ci + 1) * per_chunk, EXPERT_ROWS)):
            row = idx_smem[r // IDX_LANES, r % IDX_LANES]
            pltpu.make_async_copy(h_hbm.at[pl.ds(row, 1)], xbuf.at[1 - slot, pl.ds(r, 1)], sem_x.at[1 - slot]).start()
    o_ref[...] = acc

    @pl.when(g == pl.num_programs(0) - 1)
    def _():
        wait_slot(1 - slot)


def moe_expert_ffn(h, row_tok, grp_e, w1, w3, w2, ff_chunk=512):
    d = h.shape[1]
    d_ff = w1.shape[2]
    n_groups = grp_e.shape[0]
    k = EXPERT_ROWS // IDX_LANES
    idx = row_tok.reshape(n_groups, k, IDX_LANES)
    once = pl.Buffered(1)
    idx0_spec, idxn_spec = _gather_specs(n_groups, k)
    grid_spec = pltpu.PrefetchScalarGridSpec(
        num_scalar_prefetch=1,
        grid=(n_groups,),
        in_specs=[idx0_spec, idxn_spec, pl.BlockSpec(memory_space=pl.ANY),
                  pl.BlockSpec((1, d, d_ff), lambda g, ge: (ge[g], 0, 0), pipeline_mode=once),
                  pl.BlockSpec((1, d, d_ff), lambda g, ge: (ge[g], 0, 0), pipeline_mode=once),
                  pl.BlockSpec((1, d_ff, d), lambda g, ge: (ge[g], 0, 0), pipeline_mode=once)],
        out_specs=pl.BlockSpec((EXPERT_ROWS, d), lambda g, ge: (g, 0)),
        scratch_shapes=[pltpu.VMEM((2, EXPERT_ROWS, d), F32), pltpu.SMEM((k, IDX_LANES), I32),
                        pltpu.SemaphoreType.DMA(()), pltpu.SemaphoreType.DMA((2,))],
    )
    return pl.pallas_call(
        functools.partial(_moe_ffn_kernel, ff_chunk=ff_chunk),
        grid_spec=grid_spec,
        out_shape=jax.ShapeDtypeStruct((n_groups * EXPERT_ROWS, d), F32),
        compiler_params=_cparams(("arbitrary",)),
        name="moe_expert_ffn",
    )(grp_e, idx, idx, h, w1, w3, w2)


COMBINE_TM = 256


def _moe_combine_ln_kernel(idx0_ref, idxn_ref, y_hbm, h_ref, gate_ref, g_ref, b_ref, o_ref,
                           ybuf, idx_smem, sem_i, sem_y):
    tm = COMBINE_TM
    slot = _pipelined_gather(idx0_ref, idxn_ref, idx_smem, sem_i, y_hbm, ybuf, sem_y, TOP_K * tm)
    y = gate_ref[:, 0:1] * ybuf[slot, 0:tm, :]
    for j in range(1, TOP_K):
        y = y + gate_ref[:, j:j + 1] * ybuf[slot, j * tm:(j + 1) * tm, :]
    o_ref[...] = _layer_norm_rows(ALPHA * h_ref[...] + y, g_ref[...], b_ref[...])


def moe_combine_ln(h, y_rows, dest, gate, g, b):
    m, d = h.shape
    tm = COMBINE_TM
    n_tiles = m // tm
    k = TOP_K * tm // IDX_LANES
    idx = dest.reshape(n_tiles, tm, TOP_K).transpose(0, 2, 1).reshape(n_tiles, k, IDX_LANES)
    idx0_spec, idxn_spec = _gather_specs(n_tiles, k)
    row = pl.BlockSpec((tm, d), lambda i: (i, 0))
    vec = pl.BlockSpec((1, d), lambda i: (0, 0))
    return pl.pallas_call(
        _moe_combine_ln_kernel,
        grid=(n_tiles,),
        in_specs=[idx0_spec, idxn_spec, pl.BlockSpec(memory_space=pl.ANY), row,
                  pl.BlockSpec((tm, TOP_K), lambda i: (i, 0)), vec, vec],
        out_specs=row,
        out_shape=jax.ShapeDtypeStruct((m, d), F32),
        scratch_shapes=[pltpu.VMEM((2, TOP_K * tm, d), F32), pltpu.SMEM((k, IDX_LANES), I32),
                        pltpu.SemaphoreType.DMA(()), pltpu.SemaphoreType.DMA((2,))],
        compiler_params=_cparams(("arbitrary",)),
        name="moe_combine_ln",
    )(idx, idx, y_rows, h, gate, g.reshape(1, d), b.reshape(1, d))


def moe_dispatch_plan(logits):
    n_tok = logits.shape[0]
    top_val, top_e = lax.top_k(logits, TOP_K)
    gate = jax.nn.softmax(top_val, axis=-1)
    e_flat = top_e.reshape(-1)
    onehot = (e_flat[:, None] == jnp.arange(N_EXPERTS, dtype=e_flat.dtype)[None, :]).astype(I32)
    rank = jnp.take_along_axis(jnp.cumsum(onehot, axis=0) - onehot, e_flat[:, None], axis=1)[:, 0]
    counts = jnp.sum(onehot, axis=0)
    padded = (counts + EXPERT_ROWS - 1) // EXPERT_ROWS * EXPERT_ROWS
    pend = jnp.cumsum(padded)
    pstart = pend - padded
    dest = pstart[e_flat] + rank
    n_assign = n_tok * TOP_K
    n_rows = -(-n_assign // EXPERT_ROWS) * EXPERT_ROWS + N_EXPERTS * EXPERT_ROWS
    n_groups = n_rows // EXPERT_ROWS
    tok_flat = jnp.repeat(jnp.arange(n_tok, dtype=I32), TOP_K)
    row_tok = jnp.zeros((n_rows,), I32).at[dest].set(tok_flat)
    grp_e = jnp.minimum(jnp.searchsorted(pend, jnp.arange(n_groups, dtype=I32) * EXPERT_ROWS, side='right'),
                        N_EXPERTS - 1).astype(I32)
    return gate, dest.astype(I32), row_tok, grp_e


def kernel(x, rel_bias, e_w_in, e_q_norm, e_kv_norm, e_w_uq, e_w_uk, e_w_uv, e_w_qidx, e_pos_k, e_pos_v, e_ck1, e_ck2, e_cv1, e_cv2, e_w_out, e_ln1_g, e_ln1_b, e_ffn_w1, e_ffn_w3, e_ffn_w2, e_ln2_g, e_ln2_b, o_w_in, o_w_out, o_ln1_g, o_ln1_b, o_router, o_moe_w1, o_moe_w3, o_moe_w2, o_ln2_g, o_ln2_b):
    bsz, seq, d = x.shape
    m = bsz * seq
    xf = x.reshape(m, d)
    dsa_bias = dsa_bias_tiles(rel_bias)
    nsa_bc, nsa_toe_s, nsa_toe_w = nsa_bias_inputs(rel_bias, seq)
    moba_bias = moba_bias_tiles(rel_bias)
    gd = B_GROUPS * HEAD_DIM
    for layer in range(DEPTH):
        i = layer // 2
        if layer % 2 == 0:
            x3 = xf.reshape(bsz, seq, d)
            (qidxT, qlatT, sT, kidx, ckv, ckvT, qbT, kcmp, vcmp, kslc, kwin, vT) = even_inproj(
                x3, e_w_in[i], e_q_norm[i], e_kv_norm[i], e_w_uq[i], e_w_uk[i], e_w_qidx[i])
            o_aT = dsa_attention(qidxT, sT, qlatT, kidx, ckv, ckvT, e_w_uv[i], dsa_bias)
            kc = nsa_compress(kcmp.reshape(m, gd), e_pos_k[i], e_ck1[i], e_ck2[i], bsz, seq)
            vc = nsa_compress(vcmp.reshape(m, gd), e_pos_v[i], e_cv1[i], e_cv2[i], bsz, seq)
            o_bT = nsa_attention(qbT, kc, vc, kslc, kwin, vT, sT, nsa_bc, nsa_toe_s, nsa_toe_w)
            h = projT_residual_ln([o_aT, o_bT], e_w_out[i].astype(BF16), x3, e_ln1_g[i], e_ln1_b[i]).reshape(m, d)
            tm = 1024
            xf = swiglu_ffn(h, jnp.zeros((m // tm,), I32), e_ffn_w1[i][None].astype(BF16),
                            e_ffn_w3[i][None].astype(BF16), e_ffn_w2[i][None].astype(BF16),
                            e_ln2_g[i], e_ln2_b[i], with_ln=True, out_dtype=F32, tm=tm, ff_chunk=1408)
        else:
            x3 = xf.reshape(bsz, seq, d)
            o_cT = moba_attention(*moba_inproj(x3, o_w_in[i]), moba_bias)
            h = projT_residual_ln([o_cT], o_w_out[i].astype(BF16), x3, o_ln1_g[i], o_ln1_b[i]).reshape(m, d)
            gate, dest, row_tok, grp_e = moe_dispatch_plan(router_logits(h, o_router[i]))
            y_rows = moe_expert_ffn(h, row_tok, grp_e, o_moe_w1[i].astype(BF16), o_moe_w3[i].astype(BF16),
                                    o_moe_w2[i].astype(BF16))
            xf = moe_combine_ln(h, y_rows, dest, gate, o_ln2_g[i], o_ln2_b[i])
    return xf.reshape(bsz, seq, d)
```

```python
import functools
import math

import numpy as np
import jax
import jax.numpy as jnp
from jax import lax
from jax.experimental import pallas as pl
from jax.experimental.pallas import tpu as pltpu

F32 = jnp.float32
BF16 = jnp.bfloat16
I32 = jnp.int32
BF16_ROWS = 16

HEAD_DIM = 64
NUM_BUCKETS = 32
MAX_DISTANCE = 128
N_BIAS_HEADS = 16
A_HEADS = 8
A_Q_RANK = 256
A_KV_RANK = 128
IDX_HEADS = 16
IDX_DIM = 64
DSA_TOPK = 256
B_HEADS = 8
B_GROUPS = 2
B_HPG = B_HEADS // B_GROUPS
CMP_LEN = 32
CMP_STRIDE = 16
SLC_BLOCK = 64
SLC_TOPN = 16
WINDOW = 512
C_HEADS = 16
MOBA_BLOCK = 256
MOBA_TOPK = 3
N_EXPERTS = 8
TOP_K = 2
EXPERT_ROWS = 256
DEPTH = 2
ALPHA = (2 * DEPTH) ** 0.25

LOG2E = 1.4426950408889634
QK_SCALE = HEAD_DIM ** -0.5 * LOG2E
NEG = -1e30
NEG_HALF = -5e29
INT_MIN = -2 ** 31
VMEM_LIMIT = 56 * 1024 * 1024


def _t5_thresholds():
    def bucket(n):
        if n < NUM_BUCKETS // 2:
            return n
        v = np.log(np.float32(n) / np.float32(NUM_BUCKETS // 2)) / np.float32(math.log(MAX_DISTANCE / (NUM_BUCKETS // 2)))
        return min(NUM_BUCKETS // 2 + int(np.float32(v) * (NUM_BUCKETS - NUM_BUCKETS // 2)), NUM_BUCKETS - 1)
    b = [bucket(i) for i in range(4 * MAX_DISTANCE)]
    return [0] + [min(i for i in range(len(b)) if b[i] >= k) for k in range(1, NUM_BUCKETS)]


T5_THR = _t5_thresholds()
T5_FAR = T5_THR[-1]


def _cparams(sem):
    return pltpu.CompilerParams(dimension_semantics=sem, vmem_limit_bytes=VMEM_LIMIT)


def _bias_kernel(tab_ref, off_ref, o_ref, *, c_row, c_col, h0, causal_neg, window):
    v = pl.program_id(0)
    h = pl.program_id(1) + h0
    shape = o_ref.shape[2:]
    dist = (c_col * lax.broadcasted_iota(I32, shape, 1) + c_row * lax.broadcasted_iota(I32, shape, 0) + off_ref[v])
    n = jnp.maximum(dist, 0)
    acc = jnp.full(shape, tab_ref[h] * LOG2E, F32)
    for k in range(1, NUM_BUCKETS):
        acc = jnp.where(n >= T5_THR[k], tab_ref[k * N_BIAS_HEADS + h] * LOG2E, acc)
    if causal_neg:
        acc = jnp.where(dist >= 0, acc, NEG)
    if window:
        acc = jnp.where(dist < window, acc, NEG)
    o_ref[0, 0] = acc


def bias_tiles(rel_bias, offs, n_heads, h0, rows, cols, c_row, c_col, causal_neg, window=0):
    offs = jnp.asarray(offs, I32)
    nv = offs.shape[0]
    return pl.pallas_call(
        functools.partial(_bias_kernel, c_row=c_row, c_col=c_col, h0=h0, causal_neg=causal_neg, window=window),
        grid=(nv, n_heads),
        in_specs=[pl.BlockSpec(memory_space=pltpu.SMEM), pl.BlockSpec(memory_space=pltpu.SMEM)],
        out_specs=pl.BlockSpec((1, 1, rows, cols), lambda v, h: (v, h, 0, 0)),
        out_shape=jax.ShapeDtypeStruct((nv, n_heads, rows, cols), F32),
        compiler_params=_cparams(("arbitrary", "arbitrary")),
        name="t5_bias_tiles",
    )(rel_bias.reshape(-1), offs)


def _layer_norm_rows(z, g, b):
    mu = jnp.mean(z, axis=-1, keepdims=True)
    zc = z - mu
    var = jnp.mean(zc * zc, axis=-1, keepdims=True)
    return zc * lax.rsqrt(var + 1e-5) * g + b


def _ffn_kernel(ge_ref, x_ref, w1_ref, w3_ref, w2_ref, g_ref, b_ref, o_ref, *, ff_chunk, with_ln):
    del ge_ref
    x = x_ref[...]
    xb = x.astype(BF16)
    d_ff = w1_ref.shape[2]
    acc = jnp.zeros((x.shape[0], w2_ref.shape[2]), F32)
    for c in range(0, d_ff, ff_chunk):
        a = jnp.dot(xb, w1_ref[0, :, c:c + ff_chunk], preferred_element_type=F32)
        u = jnp.dot(xb, w3_ref[0, :, c:c + ff_chunk], preferred_element_type=F32)
        hid = (a * jax.nn.sigmoid(a) * u).astype(BF16)
        acc = acc + jnp.dot(hid, w2_ref[0, c:c + ff_chunk, :], preferred_element_type=F32)
    if with_ln:
        o_ref[...] = _layer_norm_rows(ALPHA * x.astype(F32) + acc, g_ref[...], b_ref[...]).astype(o_ref.dtype)
    else:
        o_ref[...] = acc.astype(o_ref.dtype)


def swiglu_ffn(x_rows, grp_e, w1, w3, w2, ln_g, ln_b, *, with_ln, out_dtype, tm, ff_chunk):
    m, d = x_rows.shape
    d_ff = w1.shape[2]
    once = pl.Buffered(1)
    grid_spec = pltpu.PrefetchScalarGridSpec(
        num_scalar_prefetch=1,
        grid=(m // tm,),
        in_specs=[pl.BlockSpec((tm, d), lambda i, ge: (i, 0)),
                  pl.BlockSpec((1, d, d_ff), lambda i, ge: (ge[i], 0, 0), pipeline_mode=once),
                  pl.BlockSpec((1, d, d_ff), lambda i, ge: (ge[i], 0, 0), pipeline_mode=once),
                  pl.BlockSpec((1, d_ff, d), lambda i, ge: (ge[i], 0, 0), pipeline_mode=once),
                  pl.BlockSpec((1, d), lambda i, ge: (0, 0)), pl.BlockSpec((1, d), lambda i, ge: (0, 0))],
        out_specs=pl.BlockSpec((tm, d), lambda i, ge: (i, 0)),
    )
    return pl.pallas_call(
        functools.partial(_ffn_kernel, ff_chunk=ff_chunk, with_ln=with_ln),
        grid_spec=grid_spec,
        out_shape=jax.ShapeDtypeStruct((m, d), out_dtype),
        compiler_params=_cparams(("arbitrary",)),
        name="swiglu_ffn",
    )(grp_e, x_rows, w1, w3, w2, ln_g.reshape(1, d), ln_b.reshape(1, d))


def _flash_probs(s, m, segs=None):
    if segs is None:
        m_new = jnp.maximum(m, jnp.max(s, axis=0, keepdims=True))
        p = jnp.exp2(s - m_new)
    else:
        m_new, r0 = m, 0
        for n, c in segs:
            seg_max = jnp.max(s[r0:r0 + n], axis=0, keepdims=True)
            m_new = jnp.maximum(m_new, jnp.where(c > NEG_HALF, seg_max + c, NEG))
            r0 += n
        parts, r0 = [], 0
        for n, c in segs:
            shift = jnp.where(c > NEG_HALF, m_new - c, -NEG)
            parts.append(jnp.exp2(s[r0:r0 + n] - shift))
            r0 += n
        p = parts[0] if len(parts) == 1 else jnp.concatenate(parts, axis=0)
    return m_new, jnp.exp2(m - m_new), p.astype(BF16)


def _flash_merge(states):
    ms = [st[0][...] for st in states]
    m = functools.reduce(jnp.maximum, ms)
    ws = [jnp.exp2(mi - m) for mi in ms]
    l = sum(w * st[1][...] for w, st in zip(ws, states))
    acc = sum(w * st[2][...] for w, st in zip(ws, states))
    return m, l, acc


def _flash_finish(m, l, acc):
    return jnp.where(m > NEG_HALF, acc / l, 0.0)


def _rms_rows(x, g):
    return x * lax.rsqrt(jnp.mean(x * x, axis=-1, keepdims=True) + 1e-6) * g


EVEN_T = 256
SMALL_ROWS = 48
GATE_ROW0 = IDX_HEADS
NT_DIMS = (((1,), (1,)), ((), ()))


def _even_inproj_kernel(x_ref, wa_ref, wsT_ref, qn_ref, kvn_ref, wuq_ref, wuk_ref, wqiT_ref, wqbT_ref, wk4_ref, wvT_ref,
                        qidxT_ref, qlatT_ref, sT_ref, kidx_ref, ckv_ref, ckvT_ref,
                        qbT_ref, kcmp_ref, vcmp_ref, kslc_ref, kwin_ref, vT_ref):
    xb = x_ref[0].astype(BF16)
    ya = jnp.dot(xb, wa_ref[...], preferred_element_type=F32)
    cqn = _rms_rows(ya[:, :A_Q_RANK], qn_ref[...]).astype(BF16)
    ckvn = _rms_rows(ya[:, A_Q_RANK:A_Q_RANK + A_KV_RANK], kvn_ref[...])
    kidx_ref[0, 0] = ya[:, A_Q_RANK + A_KV_RANK:A_Q_RANK + A_KV_RANK + IDX_DIM].astype(BF16)
    ckv_ref[0, 0] = ckvn.astype(BF16)
    ckvT_ref[0, 0] = ckvn.T.astype(BF16)
    sT_ref[0] = lax.dot_general(wsT_ref[...], xb, NT_DIMS, preferred_element_type=F32)
    q = jnp.dot(cqn, wuq_ref[...], preferred_element_type=F32).astype(BF16)
    for h in range(A_HEADS):
        qlT = lax.dot_general(wuk_ref[h], q[:, h * HEAD_DIM:(h + 1) * HEAD_DIM], NT_DIMS, preferred_element_type=F32)
        qlatT_ref[0, h * A_KV_RANK:(h + 1) * A_KV_RANK, :] = (qlT * QK_SCALE).astype(BF16)
    qidxT_ref[0] = lax.dot_general(wqiT_ref[...], cqn, NT_DIMS, preferred_element_type=F32).astype(BF16)
    qbT_ref[0] = lax.dot_general(wqbT_ref[...], xb, NT_DIMS, preferred_element_type=F32).astype(BF16)
    yk = jnp.dot(xb, wk4_ref[...], preferred_element_type=F32).astype(BF16)
    gd = B_GROUPS * HEAD_DIM
    for j, ref in enumerate((kcmp_ref, vcmp_ref, kslc_ref, kwin_ref)):
        ref[0] = yk[:, j * gd:(j + 1) * gd]
    vT = lax.dot_general(wvT_ref[...], xb, NT_DIMS, preferred_element_type=F32).astype(BF16)
    for j in range(EVEN_T // NSA_KT):
        vT_ref[0, j] = vT[:, j * NSA_KT:(j + 1) * NSA_KT]


def even_inproj(x3, w_in, q_norm, kv_norm, w_uq, w_uk, w_qidx):
    bsz, seq, d = x3.shape
    T = EVEN_T
    nq = seq // T
    gd = B_GROUPS * HEAD_DIM
    n_kt = seq // NSA_KT
    o_kidx = A_Q_RANK + A_KV_RANK
    o_widx = o_kidx + IDX_DIM
    o_qb = o_widx + IDX_HEADS
    o_kv = o_qb + B_HEADS * HEAD_DIM
    o_gate = o_kv + 6 * gd
    kv = lambda j: w_in[:, o_kv + j * gd:o_kv + (j + 1) * gd]
    wa = jnp.pad(w_in[:, :o_widx], ((0, 0), (0, 512 - o_widx))).astype(BF16)
    w_gate = w_in[:, o_gate:].reshape(d, B_GROUPS, B_HPG, 3).transpose(0, 1, 3, 2).reshape(d, 3 * B_HEADS)
    wsT = jnp.concatenate([w_in[:, o_widx:o_qb] * IDX_HEADS ** -0.5, w_gate,
                           jnp.zeros((d, SMALL_ROWS - IDX_HEADS - 3 * B_HEADS), w_in.dtype)], axis=1).T.astype(BF16)
    wuq = w_uq.reshape(A_Q_RANK, A_HEADS * HEAD_DIM).astype(BF16)
    wuk = jnp.transpose(w_uk, (1, 0, 2)).astype(BF16)
    wqiT = w_qidx.reshape(A_Q_RANK, IDX_HEADS * IDX_DIM).T.astype(BF16)
    wqbT = (w_in[:, o_qb:o_kv] * QK_SCALE).T.astype(BF16)
    wk4 = jnp.concatenate([kv(0), kv(1), kv(2), kv(4)], axis=1).astype(BF16)
    wvT = jnp.concatenate([kv(3), kv(5)], axis=1).T.astype(BF16)
    weights = (wa, wsT, q_norm.reshape(1, -1), kv_norm.reshape(1, -1), wuq, wuk, wqiT, wqbT, wk4, wvT)
    once = pl.Buffered(1)
    w_specs = [pl.BlockSpec(w.shape, (lambda b, i, n=w.ndim: (0,) * n), pipeline_mode=once) for w in weights]
    fm = lambda rows: pl.BlockSpec((1, rows, T), lambda b, i: (b, 0, i))
    tok = lambda cols: pl.BlockSpec((1, T, cols), lambda b, i: (b, i, 0))
    blk = lambda r, c: pl.BlockSpec((1, 1, r, c), lambda b, i: (b, i, 0, 0))
    sds = jax.ShapeDtypeStruct
    return pl.pallas_call(
        _even_inproj_kernel,
        grid=(bsz, nq),
        in_specs=[pl.BlockSpec((1, T, d), lambda b, i: (b, i, 0))] + w_specs,
        out_specs=[fm(IDX_HEADS * IDX_DIM), fm(A_HEADS * A_KV_RANK), fm(SMALL_ROWS),
                   blk(T, IDX_DIM), blk(T, A_KV_RANK), blk(A_KV_RANK, T),
                   fm(B_HEADS * HEAD_DIM), tok(gd), tok(gd), tok(gd), tok(gd),
                   pl.BlockSpec((1, T // NSA_KT, 2 * gd, NSA_KT), lambda b, i: (b, i, 0, 0))],
        out_shape=[sds((bsz, IDX_HEADS * IDX_DIM, seq), BF16), sds((bsz, A_HEADS * A_KV_RANK, seq), BF16),
                   sds((bsz, SMALL_ROWS, seq), F32),
                   sds((bsz, nq, T, IDX_DIM), BF16), sds((bsz, nq, T, A_KV_RANK), BF16), sds((bsz, nq, A_KV_RANK, T), BF16),
                   sds((bsz, B_HEADS * HEAD_DIM, seq), BF16),
                   sds((bsz, seq, gd), BF16), sds((bsz, seq, gd), BF16), sds((bsz, seq, gd), BF16), sds((bsz, seq, gd), BF16),
                   sds((bsz, n_kt, 2 * gd, NSA_KT), BF16)],
        compiler_params=_cparams(("arbitrary", "arbitrary")),
        name="even_inproj",
    )(x3, *weights)


DSA_T = 256
SUB = 128


def _dsa_kernel(qidx_ref, wT_ref, qlat_ref, kidx_ref, ckv_ref, ckvT_ref, bias_ref, wuvt_ref, o_ref,
                key_ref, selb0_ref, selb1_ref, *state_refs, n_keep):
    selb_refs = (selb0_ref, selb1_ref)
    m_refs, l_refs, acc_refs = (state_refs[0:A_HEADS], state_refs[A_HEADS:2 * A_HEADS], state_refs[2 * A_HEADS:])
    qi = pl.program_id(1)
    nkb = qi + 1
    T = DSA_T

    def score_block(kb, carry):
        for sub in range(T // SUB):
            k = kidx_ref[0, kb, sub * SUB:(sub + 1) * SUB, :]
            acc = jnp.zeros((SUB, T), F32)
            for h in range(IDX_HEADS):
                d = jnp.dot(k, qidx_ref[0, h * IDX_DIM:(h + 1) * IDX_DIM, :], preferred_element_type=F32)
                acc = acc + jnp.maximum(d, 0.0) * wT_ref[0, h:h + 1, :]
            bits = lax.bitcast_convert_type(acc, I32)
            key = bits ^ (lax.shift_right_arithmetic(bits, 31) & 0x7FFFFFFF)
            s_pos = kb * T + sub * SUB + lax.broadcasted_iota(I32, (SUB, T), 0)
            t_pos = qi * T + lax.broadcasted_iota(I32, (SUB, T), 1)
            key = jnp.where(s_pos <= t_pos, key, INT_MIN)
            key_ref[pl.ds(pl.multiple_of(kb * T + sub * SUB, SUB), SUB), :] = key
        return carry

    lax.fori_loop(0, nkb, score_block, 0)

    def count_ge(cand):
        def body(kb, cnt):
            blk = key_ref[pl.ds(pl.multiple_of(kb * T, T), T), :]
            ge = jnp.where(blk >= cand, 1, 0).astype(I32)
            return cnt + jnp.sum(ge.reshape(T // 8, 8, T), axis=0)
        cnt = lax.fori_loop(0, nkb, body, jnp.zeros((8, T), I32))
        return jnp.sum(cnt, axis=0, keepdims=True)

    def bit_step(i, u):
        cand_u = u | lax.shift_left(jnp.int32(1), 31 - i)
        cnt = count_ge(cand_u ^ INT_MIN)
        return jnp.where(cnt >= n_keep, cand_u, u)

    u = lax.fori_loop(0, 32, bit_step, jnp.zeros((1, T), I32))
    thr = jnp.maximum(u ^ INT_MIN, INT_MIN + 1)

    for h in range(A_HEADS):
        m_refs[h][...] = jnp.full(m_refs[h].shape, NEG, F32)
        l_refs[h][...] = jnp.zeros(l_refs[h].shape, F32)
        acc_refs[h][...] = jnp.zeros(acc_refs[h].shape, F32)

    states = list(zip(m_refs, l_refs, acc_refs))
    far_bias = [bias_ref[2, h, 0:1, :] for h in range(A_HEADS)]

    def masked_scores(kb, selb):
        selb[...] = jnp.where(key_ref[pl.ds(pl.multiple_of(kb * T, T), T), :] >= thr, 0.0, NEG)
        ckv = ckv_ref[0, kb]
        return [jnp.dot(ckv, qlat_ref[0, h * A_KV_RANK:(h + 1) * A_KV_RANK, :], preferred_element_type=F32) + selb[...]
                for h in range(A_HEADS)]

    def far_body(i, carry):
        tiles = []
        for u, selb in enumerate(selb_refs):
            kb_raw = len(selb_refs) * i + u
            live = kb_raw < n_far
            kb = jnp.minimum(kb_raw, n_far - 1)
            segs = [[(T, jnp.where(live, far_bias[h], NEG))] for h in range(A_HEADS)]
            tiles.append((list(zip(masked_scores(kb, selb), segs)), [ckvT_ref[0, kb]] * A_HEADS))
        for s_all, v_all in tiles:
            _flash_step(s_all, v_all, states)
        return carry

    def near_body(kb, carry):
        s_all = [s + bias_ref[qi - kb, h] for h, s in enumerate(masked_scores(kb, selb_refs[0]))]
        _flash_step(s_all, [ckvT_ref[0, kb]] * A_HEADS, states)
        return carry

    n_far = jnp.maximum(qi - 1, 0)
    lax.fori_loop(0, lax.div(n_far + (len(selb_refs) - 1), jnp.int32(len(selb_refs))), far_body, 0)
    lax.fori_loop(n_far, nkb, near_body, 0)

    for h in range(A_HEADS):
        o_lat = _flash_finish(m_refs[h][...], l_refs[h][...], acc_refs[h][...]).astype(BF16)
        o_ref[0, h * HEAD_DIM:(h + 1) * HEAD_DIM, :] = jnp.dot(
            wuvt_ref[h], o_lat, preferred_element_type=F32).astype(o_ref.dtype)


def dsa_attention(qidxT, sT, qlatT, kidx, ckv, ckvT, w_uv, bias3):
    T = DSA_T
    assert T == EVEN_T
    bsz, nq = kidx.shape[0], kidx.shape[1]
    seq = nq * T
    n_keep = min(DSA_TOPK, seq // 4)
    wuvt = jnp.transpose(w_uv, (1, 2, 0)).astype(BF16)
    return pl.pallas_call(
        functools.partial(_dsa_kernel, n_keep=n_keep),
        grid=(bsz, nq),
        in_specs=[pl.BlockSpec((1, IDX_HEADS * IDX_DIM, T), lambda b, i: (b, 0, i)),
                  pl.BlockSpec((1, SMALL_ROWS, T), lambda b, i: (b, 0, i)),
                  pl.BlockSpec((1, A_HEADS * A_KV_RANK, T), lambda b, i: (b, 0, i)),
                  pl.BlockSpec((1, nq, T, IDX_DIM), lambda b, i: (b, 0, 0, 0)),
                  pl.BlockSpec((1, nq, T, A_KV_RANK), lambda b, i: (b, 0, 0, 0)),
                  pl.BlockSpec((1, nq, A_KV_RANK, T), lambda b, i: (b, 0, 0, 0)),
                  pl.BlockSpec((3, A_HEADS, T, T), lambda b, i: (0, 0, 0, 0)),
                  pl.BlockSpec((A_HEADS, HEAD_DIM, A_KV_RANK), lambda b, i: (0, 0, 0))],
        out_specs=pl.BlockSpec((1, A_HEADS * HEAD_DIM, T), lambda b, i: (b, 0, i)),
        out_shape=jax.ShapeDtypeStruct((bsz, A_HEADS * HEAD_DIM, seq), BF16),
        scratch_shapes=([pltpu.VMEM((seq, T), I32), pltpu.VMEM((T, T), F32), pltpu.VMEM((T, T), F32)]
                        + [pltpu.VMEM((1, T), F32)] * (2 * A_HEADS)
                        + [pltpu.VMEM((A_KV_RANK, T), F32)] * A_HEADS),
        compiler_params=_cparams(("arbitrary", "arbitrary")),
        name="dsa_attention",
    )(qidxT, sT, qlatT, kidx, ckv, ckvT, bias3, wuvt)


def dsa_bias_tiles(rel_bias):
    assert DSA_T + 1 >= T5_FAR
    return bias_tiles(rel_bias, [0, DSA_T, 4 * DSA_T], A_HEADS, 0, DSA_T, DSA_T, -1, 1, False)


N_CMP_PAD = 256


def _compress_kernel(blk_ref, pos_ref, w1_ref, w2_ref, o_ref):
    x = (blk_ref[0].astype(F32) + pos_ref[...]).astype(BF16)
    hid = jax.nn.gelu(jnp.dot(x, w1_ref[...], preferred_element_type=F32))
    o_ref[0] = jnp.dot(hid.astype(BF16), w2_ref[...], preferred_element_type=F32).astype(o_ref.dtype)


def nsa_compress(a, pos, w1, w2, bsz, seq):
    n_chunk = seq // CMP_STRIDE
    assert CMP_LEN == 2 * CMP_STRIDE and n_chunk <= N_CMP_PAD
    width = CMP_STRIDE * HEAD_DIM
    chunks = a.reshape(bsz, n_chunk, CMP_STRIDE, B_GROUPS, HEAD_DIM).transpose(0, 3, 1, 2, 4)
    chunks = chunks.reshape(bsz * B_GROUPS, n_chunk, width)
    blocks = jnp.concatenate([chunks[:, :-1], chunks[:, 1:]], axis=-1)
    blocks = jnp.pad(blocks, ((0, 0), (0, N_CMP_PAD - (n_chunk - 1)), (0, 0)))
    out = pl.pallas_call(
        _compress_kernel,
        grid=(bsz * B_GROUPS,),
        in_specs=[pl.BlockSpec((1, N_CMP_PAD, 2 * width), lambda i: (i, 0, 0)),
                  pl.BlockSpec((1, 2 * width), lambda i: (0, 0)),
                  pl.BlockSpec((2 * width, HEAD_DIM), lambda i: (0, 0)),
                  pl.BlockSpec((HEAD_DIM, HEAD_DIM), lambda i: (0, 0))],
        out_specs=pl.BlockSpec((1, N_CMP_PAD, HEAD_DIM), lambda i: (i, 0, 0)),
        out_shape=jax.ShapeDtypeStruct((bsz * B_GROUPS, N_CMP_PAD, HEAD_DIM), BF16),
        compiler_params=_cparams(("arbitrary",)),
        name="nsa_compress",
    )(blocks, pos.reshape(1, 2 * width), w1.reshape(2 * width, HEAD_DIM).astype(BF16), w2.astype(BF16))
    return out.reshape(bsz, B_GROUPS, N_CMP_PAD, HEAD_DIM)


NSA_TQ = 128
NSA_L = B_HPG * NSA_TQ
NSA_KT = 128
NSA_SLC_REL = 3
NSA_WIN_REL = 5
NSA_FAR_SPLIT = 4


def _flash_step(s_all, vT_all, states):
    probs = []
    for item, (m_ref, _, _) in zip(s_all, states):
        s, segs = item if isinstance(item, tuple) else (item, None)
        m_new, alpha, p = _flash_probs(s, m_ref[...], segs)
        m_ref[...] = m_new
        probs.append((alpha, p))
    for (alpha, p), vT, (_, l_ref, acc_ref) in zip(probs, vT_all, states):
        d = vT.shape[0]
        ones = jnp.ones((BF16_ROWS, vT.shape[1]), BF16)
        pv = jnp.dot(jnp.concatenate([vT, ones], axis=0), p, preferred_element_type=F32)
        acc_ref[...] = alpha * acc_ref[...] + pv[:d]
        l_ref[...] = alpha * l_ref[...] + pv[d:d + 1]


def _flash_loop(lo, hi, scores, values, states, segs, unroll=2):
    def body(i, carry):
        tiles = []
        for u in range(unroll):
            j_raw = lo + unroll * i + u
            live = j_raw < hi
            j = jnp.minimum(j_raw, hi - 1)
            sg = [[(n, jnp.where(live, c, NEG)) for n, c in chain] for chain in segs(j)]
            tiles.append((list(zip(scores(j), sg)), values(j)))
        for s_all, v_all in tiles:
            _flash_step(s_all, v_all, states)
        return carry

    lax.fori_loop(0, lax.div(hi - lo + (unroll - 1), jnp.int32(unroll)), body, 0)


def _nsa_kernel(qT_ref, kc_ref, vcT_ref, biasc_ref, ovl_ref, ks_ref, kw_ref, vT_ref,
                toes_ref, toew_ref, sT_ref, o_ref, selb_ref, *st, n_cmp, n_sel, n_slc):
    qi = pl.program_id(1)
    TQ, L = NSA_TQ, NSA_L
    q0 = qi * TQ
    qTs, qTs_pad = [], []
    for g in range(B_GROUPS):
        q = jnp.concatenate([qT_ref[0, (g * B_HPG + n) * HEAD_DIM:(g * B_HPG + n + 1) * HEAD_DIM, :]
                             for n in range(B_HPG)], axis=1)
        parts = [jnp.zeros_like(q)] * B_GROUPS
        parts[g] = q
        qTs.append(q)
        qTs_pad.append(jnp.concatenate(parts, axis=0))
    t_lane = q0 + (lax.broadcasted_iota(I32, (1, L), 1) & (TQ - 1))

    o_cs = []
    for g in range(B_GROUPS):
        s = jnp.dot(kc_ref[0, g], qTs[g], preferred_element_type=F32) + biasc_ref[g, 0]
        i_idx = lax.broadcasted_iota(I32, (N_CMP_PAD, L), 0)
        valid = jnp.where(i_idx < n_cmp, i_idx * CMP_STRIDE + (CMP_LEN - 1), 2 ** 30) <= t_lane
        s = jnp.where(valid, s, NEG)
        m = jnp.max(s, axis=0, keepdims=True)
        p = jnp.where(valid, jnp.exp2(s - m), 0.0)
        l = jnp.sum(p, axis=0, keepdims=True)
        p_c = p / jnp.where(l > 0, l, 1.0)
        o_c = jnp.dot(vcT_ref[0, g], p_c.astype(BF16), preferred_element_type=F32)

        psum = p_c[:, 0:TQ]
        for n in range(1, B_HPG):
            psum = psum + p_c[:, n * TQ:(n + 1) * TQ]
        sc = jnp.dot(ovl_ref[...], psum, preferred_element_type=F32, precision=lax.Precision.HIGHEST)
        j_idx = lax.broadcasted_iota(I32, (n_slc, TQ), 0)
        cur = (q0 + lax.broadcasted_iota(I32, (1, TQ), 1)) // SLC_BLOCK
        adm = j_idx <= cur
        forced = (j_idx == 0) | (j_idx == cur) | (j_idx == cur - 1)
        scv = jnp.where(adm, jnp.where(forced, jnp.inf, sc), -jnp.inf)
        rank = jnp.zeros((n_slc, TQ), I32)
        for jp in range(n_slc):
            row = scv[jp:jp + 1, :]
            beats = jnp.where(row > scv, 1, jnp.where((row == scv) & (jp < j_idx), 1, 0))
            rank = rank + beats
        selb = jnp.where(rank < n_sel, 0.0, NEG).astype(F32)
        selb4 = jnp.concatenate([selb] * B_HPG, axis=1)
        for j in range(n_slc):
            selb_ref[g, j] = selb4[j:j + 1, :]
        o_cs.append(o_c)

    for ref in st[0::3]:
        ref[...] = jnp.full(ref.shape, NEG, F32)
    for ref in st[1::3] + st[2::3]:
        ref[...] = jnp.zeros(ref.shape, F32)
    slc_st = [st[6 * g:6 * g + 3] for g in range(B_GROUPS)]
    win_st = [st[6 * g + 3:6 * g + 6] for g in range(B_GROUPS)]
    n_main = 6 * B_GROUPS
    xtr_st = [[st[n_main + 3 * (g * (NSA_FAR_SPLIT - 1) + r):n_main + 3 * (g * (NSA_FAR_SPLIT - 1) + r) + 3]
               for r in range(NSA_FAR_SPLIT - 1)] for g in range(B_GROUPS)]
    per_kt = NSA_KT // SLC_BLOCK

    groups = range(B_GROUPS)
    far_bias = [toes_ref[g, NSA_SLC_REL - 1, 0:1, :] for g in groups]

    def slc_scores(g, jt, near):
        s = jnp.dot(ks_ref[0, jt], qTs_pad[g], preferred_element_type=F32)
        return s + toes_ref[g, jnp.minimum(qi - jt, NSA_SLC_REL - 1)] if near else s

    def slc_segs(g, jt, near):
        return [(SLC_BLOCK, selb_ref[g, per_kt * jt + r] + (0.0 if near else far_bias[g])) for r in range(per_kt)]

    def win_scores(g, jt):
        rel = jnp.minimum(qi - jt, NSA_WIN_REL - 1)
        return jnp.dot(kw_ref[0, jt], qTs_pad[g], preferred_element_type=F32) + toew_ref[g, rel]

    gd = B_GROUPS * HEAD_DIM
    v_slc = lambda g, jt: vT_ref[0, jt, g * HEAD_DIM:(g + 1) * HEAD_DIM, :]
    v_win = lambda g, jt: vT_ref[0, jt, gd + g * HEAD_DIM:gd + (g + 1) * HEAD_DIM, :]

    assert NSA_WIN_REL >= NSA_SLC_REL
    j_lo = jnp.maximum(qi - (NSA_WIN_REL - 1), 0)

    def far_body(i, carry):
        s_all, v_all, chains = [], [], []
        for r in range(NSA_FAR_SPLIT):
            jt_raw = NSA_FAR_SPLIT * i + r
            live = jt_raw < j_lo
            jt = jnp.minimum(jt_raw, j_lo - 1)
            for g in groups:
                segs = [(n, jnp.where(live, c, NEG)) for n, c in slc_segs(g, jt, False)]
                s_all.append((slc_scores(g, jt, False), segs))
                v_all.append(v_slc(g, jt))
                chains.append(slc_st[g] if r == 0 else xtr_st[g][r - 1])
        _flash_step(s_all, v_all, chains)
        return carry

    zero_row = jnp.zeros((1, L), F32)
    lax.fori_loop(0, lax.div(j_lo + (NSA_FAR_SPLIT - 1), jnp.int32(NSA_FAR_SPLIT)), far_body, 0)
    _flash_loop(j_lo, qi + 1,
                lambda jt: [slc_scores(g, jt, True) for g in groups] + [win_scores(g, jt) for g in groups],
                lambda jt: [v_slc(g, jt) for g in groups] + [v_win(g, jt) for g in groups],
                slc_st + win_st,
                lambda jt: [slc_segs(g, jt, True) for g in groups] + [[(NSA_KT, zero_row)] for g in groups])

    for g in range(B_GROUPS):
        o_s = _flash_finish(*_flash_merge([slc_st[g]] + xtr_st[g]))
        o_w = _flash_finish(*[r[...] for r in win_st[g]])
        row0 = GATE_ROW0 + g * 3 * B_HPG
        gate = [jax.nn.sigmoid(jnp.concatenate([sT_ref[0, row0 + j * B_HPG + n:row0 + j * B_HPG + n + 1, :]
                                                for n in range(B_HPG)], axis=1)) for j in range(3)]
        o = (gate[0] * o_cs[g] + gate[1] * o_s + gate[2] * o_w).astype(o_ref.dtype)
        for n in range(B_HPG):
            o_ref[0, (g * B_HPG + n) * HEAD_DIM:(g * B_HPG + n + 1) * HEAD_DIM, :] = o[:, n * TQ:(n + 1) * TQ]


def nsa_bias_inputs(rel_bias, seq):
    TQ, L, KT = NSA_TQ, NSA_L, NSA_KT
    nq = seq // TQ
    bc = bias_tiles(rel_bias, [-(CMP_LEN - 1)], B_HEADS, A_HEADS, N_CMP_PAD, seq, -CMP_STRIDE, 1, False, 0)
    bc = bc.reshape(B_GROUPS, B_HPG, N_CMP_PAD, nq, TQ).transpose(0, 3, 2, 1, 4).reshape(B_GROUPS, nq, N_CMP_PAD, L)

    def lanes(t):
        v = t.shape[0]
        return t.reshape(v, B_GROUPS, B_HPG, KT, TQ).transpose(1, 0, 3, 2, 4).reshape(B_GROUPS, v, KT, L)

    assert KT == TQ and (NSA_SLC_REL - 1) * KT - (KT - 1) >= T5_FAR
    toe_s = bias_tiles(rel_bias, [v * KT for v in range(NSA_SLC_REL - 1)] + [64 * KT],
                       B_HEADS, A_HEADS, KT, TQ, -1, 1, True, 0)
    assert (NSA_WIN_REL - 1) * KT - (KT - 1) < WINDOW <= NSA_WIN_REL * KT - (KT - 1)
    toe_w = bias_tiles(rel_bias, [v * KT for v in range(NSA_WIN_REL)], B_HEADS, A_HEADS, KT, TQ, -1, 1, True, WINDOW)
    return bc, lanes(toe_s), lanes(toe_w)


def nsa_overlap(seq):
    n_cmp = (seq - CMP_LEN) // CMP_STRIDE + 1
    n_slc = seq // SLC_BLOCK
    cs = np.arange(N_CMP_PAD) * CMP_STRIDE
    ss = np.arange(n_slc) * SLC_BLOCK
    ov = ((cs[None, :] + CMP_LEN - 1 >= ss[:, None]) & (cs[None, :] <= ss[:, None] + SLC_BLOCK - 1)
          & (np.arange(N_CMP_PAD)[None, :] < n_cmp))
    return jnp.asarray(ov.astype(np.float32))


def nsa_attention(qbT, kc, vc, kslc, kwin, vT, sT, biasc, toe_s, toe_w):
    TQ, L, KT, G = NSA_TQ, NSA_L, NSA_KT, B_GROUPS
    bsz, n_kt = vT.shape[0], vT.shape[1]
    seq = n_kt * KT
    nq = seq // TQ
    n_slc = seq // SLC_BLOCK
    n_cmp = (seq - CMP_LEN) // CMP_STRIDE + 1
    n_sel = min(SLC_TOPN, n_slc)
    gd = G * HEAD_DIM
    vcT = vc.transpose(0, 1, 3, 2)
    once = pl.Buffered(1)
    k_spec = pl.BlockSpec((1, n_kt, KT, gd), lambda b, i: (b, 0, 0, 0))
    n_chain = 2 * G + G * (NSA_FAR_SPLIT - 1)
    return pl.pallas_call(
        functools.partial(_nsa_kernel, n_cmp=n_cmp, n_sel=n_sel, n_slc=n_slc),
        grid=(bsz, nq),
        in_specs=[pl.BlockSpec((1, B_HEADS * HEAD_DIM, TQ), lambda b, i: (b, 0, i)),
                  pl.BlockSpec((1, G, N_CMP_PAD, HEAD_DIM), lambda b, i: (b, 0, 0, 0)),
                  pl.BlockSpec((1, G, HEAD_DIM, N_CMP_PAD), lambda b, i: (b, 0, 0, 0)),
                  pl.BlockSpec((G, 1, N_CMP_PAD, L), lambda b, i: (0, i, 0, 0)),
                  pl.BlockSpec((n_slc, N_CMP_PAD), lambda b, i: (0, 0), pipeline_mode=once),
                  k_spec, k_spec,
                  pl.BlockSpec((1, n_kt, 2 * gd, KT), lambda b, i: (b, 0, 0, 0)),
                  pl.BlockSpec((G, NSA_SLC_REL, KT, L), lambda b, i: (0, 0, 0, 0), pipeline_mode=once),
                  pl.BlockSpec((G, NSA_WIN_REL, KT, L), lambda b, i: (0, 0, 0, 0), pipeline_mode=once),
                  pl.BlockSpec((1, SMALL_ROWS, TQ), lambda b, i: (b, 0, i))],
        out_specs=pl.BlockSpec((1, B_HEADS * HEAD_DIM, TQ), lambda b, i: (b, 0, i)),
        out_shape=jax.ShapeDtypeStruct((bsz, B_HEADS * HEAD_DIM, seq), BF16),
        scratch_shapes=([pltpu.VMEM((G, n_slc, 1, L), F32)]
                        + [pltpu.VMEM((1, L), F32), pltpu.VMEM((1, L), F32), pltpu.VMEM((HEAD_DIM, L), F32)] * n_chain),
        compiler_params=_cparams(("arbitrary", "arbitrary")),
        name="nsa_attention",
    )(qbT, kc, vcT, biasc, nsa_overlap(seq), kslc.reshape(bsz, n_kt, KT, gd), kwin.reshape(bsz, n_kt, KT, gd),
      vT, toe_s, toe_w, sT)


MOBA_T = MOBA_BLOCK


MOBA_HB = 8


PAIR = 2 * HEAD_DIM


def _moba_inproj_kernel(x_ref, wqT_ref, wk_ref, wvT_ref, qT_ref, k_ref, vT_ref):
    xb = x_ref[0].astype(BF16)
    nt = (((1,), (1,)), ((), ()))
    qT_ref[0] = lax.dot_general(wqT_ref[...], xb, nt, preferred_element_type=F32).astype(BF16)
    k_ref[0, 0] = jnp.dot(xb, wk_ref[...], preferred_element_type=F32).astype(BF16)
    vT_ref[0, 0] = lax.dot_general(wvT_ref[...], xb, nt, preferred_element_type=F32).astype(BF16)


def _pair_padded_qT(wq):
    n_heads = wq.shape[1] // HEAD_DIM
    wT = wq.T.reshape(n_heads, HEAD_DIM, wq.shape[0])
    z = jnp.zeros_like(wT)
    even = jnp.concatenate([wT, z], axis=1)
    odd = jnp.concatenate([z, wT], axis=1)
    is_even = (jnp.arange(n_heads) % 2 == 0)[:, None, None]
    return jnp.where(is_even, even, odd).reshape(n_heads * PAIR, wq.shape[0])


def moba_inproj(x3, w_in):
    bsz, seq, d = x3.shape
    T = MOBA_T
    n_blk = seq // T
    hd = C_HEADS * HEAD_DIM
    wqT = _pair_padded_qT(w_in[:, :hd] * QK_SCALE).astype(BF16)
    wk = w_in[:, hd:2 * hd].astype(BF16)
    wvT = w_in[:, 2 * hd:].T.astype(BF16)
    once = pl.Buffered(1)
    return pl.pallas_call(
        _moba_inproj_kernel,
        grid=(bsz, n_blk),
        in_specs=[pl.BlockSpec((1, T, d), lambda b, i: (b, i, 0)),
                  pl.BlockSpec(wqT.shape, lambda b, i: (0, 0), pipeline_mode=once),
                  pl.BlockSpec(wk.shape, lambda b, i: (0, 0), pipeline_mode=once),
                  pl.BlockSpec(wvT.shape, lambda b, i: (0, 0), pipeline_mode=once)],
        out_specs=[pl.BlockSpec((1, C_HEADS * PAIR, T), lambda b, i: (b, 0, i)),
                   pl.BlockSpec((1, 1, T, hd), lambda b, i: (b, i, 0, 0)),
                   pl.BlockSpec((1, 1, hd, T), lambda b, i: (b, i, 0, 0))],
        out_shape=[jax.ShapeDtypeStruct((bsz, C_HEADS * PAIR, seq), BF16),
                   jax.ShapeDtypeStruct((bsz, n_blk, T, hd), BF16),
                   jax.ShapeDtypeStruct((bsz, n_blk, hd, T), BF16)],
        compiler_params=_cparams(("arbitrary", "arbitrary")),
        name="moba_inproj",
    )(x3, wqT, wk, wvT)


def _moba_kernel(qT_ref, k_ref, vT_ref, bias_ref, o_ref, kmean_ref, selb_ref, *st, n_sel):
    qi = pl.program_id(2)
    T = MOBA_T
    n_blk = k_ref.shape[1]
    states = [st[3 * hh:3 * hh + 3] for hh in range(MOBA_HB)]

    @pl.when(qi == 0)
    def _():
        for j in range(n_blk):
            kmean_ref[j:j + 1, :] = jnp.mean(k_ref[0, j].astype(F32), axis=0, keepdims=True)

    qTs = [qT_ref[0, hh * PAIR:(hh + 1) * PAIR, :] for hh in range(MOBA_HB)]
    pair = lambda hh: slice((hh // 2) * PAIR, (hh // 2 + 1) * PAIR)
    j_idx = lax.broadcasted_iota(I32, (n_blk, T), 0)
    for hh in range(MOBA_HB):
        gate = jnp.dot(kmean_ref[:, pair(hh)], qTs[hh].astype(F32), preferred_element_type=F32,
                       precision=lax.Precision.HIGHEST)
        gv = jnp.where(j_idx < qi, gate, -jnp.inf)
        rank = jnp.zeros((n_blk, T), I32)
        for jp in range(n_blk):
            row = gv[jp:jp + 1, :]
            rank = rank + jnp.where(row > gv, 1, jnp.where((row == gv) & (jp < j_idx), 1, 0))
        selb = jnp.where(j_idx < qi, jnp.where(rank < n_sel, 0.0, NEG),
                         jnp.where(j_idx == qi, 0.0, NEG)).astype(F32)
        for j in range(n_blk):
            selb_ref[hh, j] = selb[j:j + 1, :]
        m_ref, l_ref, acc_ref = states[hh]
        m_ref[...] = jnp.full(m_ref.shape, NEG, F32)
        l_ref[...] = jnp.zeros(l_ref.shape, F32)
        acc_ref[...] = jnp.zeros(acc_ref.shape, F32)

    heads = range(MOBA_HB)
    far_bias = [bias_ref[2, hh, 0:1, :] for hh in heads]

    def far_scores(kb):
        return [jnp.dot(k_ref[0, kb, :, pair(hh)], qTs[hh], preferred_element_type=F32) for hh in heads]

    def far_segs(kb):
        return [[(T, selb_ref[hh, kb] + far_bias[hh])] for hh in heads]

    def near_scores(kb):
        return [jnp.dot(k_ref[0, kb, :, pair(hh)], qTs[hh], preferred_element_type=F32) + bias_ref[qi - kb, hh]
                for hh in heads]

    def near_segs(kb):
        return [[(T, selb_ref[hh, kb])] for hh in heads]

    def values(kb):
        return [vT_ref[0, kb, hh * HEAD_DIM:(hh + 1) * HEAD_DIM, :] for hh in heads]

    n_far = jnp.maximum(qi - 1, 0)
    _flash_loop(0, n_far, far_scores, values, states, far_segs)
    _flash_loop(n_far, qi + 1, near_scores, values, states, near_segs)
    for hh in range(MOBA_HB):
        o_ref[0, hh * HEAD_DIM:(hh + 1) * HEAD_DIM, :] = _flash_finish(*[r[...] for r in states[hh]]).astype(o_ref.dtype)


def moba_bias_tiles(rel_bias):
    assert MOBA_T + 1 >= T5_FAR
    t0 = bias_tiles(rel_bias, [0], C_HEADS, 0, MOBA_T, MOBA_T, -1, 1, True)
    t12 = bias_tiles(rel_bias, [MOBA_T, 4 * MOBA_T], C_HEADS, 0, MOBA_T, MOBA_T, -1, 1, False)
    return jnp.concatenate([t0, t12], axis=0)


def moba_attention(qT, k, vT, bias3):
    T = MOBA_T
    bsz, n_blk = k.shape[0], k.shape[1]
    seq = n_blk * T
    n_sel = min(MOBA_TOPK, n_blk - 1)
    HB = MOBA_HB
    assert HB % 2 == 0
    out = pl.pallas_call(
        functools.partial(_moba_kernel, n_sel=n_sel),
        grid=(bsz, C_HEADS // HB, n_blk),
        in_specs=[pl.BlockSpec((1, HB * PAIR, T), lambda b, h, i: (b, h, i)),
                  pl.BlockSpec((1, n_blk, T, HB * HEAD_DIM), lambda b, h, i: (b, 0, 0, h)),
                  pl.BlockSpec((1, n_blk, HB * HEAD_DIM, T), lambda b, h, i: (b, 0, h, 0)),
                  pl.BlockSpec((3, HB, T, T), lambda b, h, i: (0, h, 0, 0))],
        out_specs=pl.BlockSpec((1, HB * HEAD_DIM, T), lambda b, h, i: (b, h, i)),
        out_shape=jax.ShapeDtypeStruct((bsz, C_HEADS * HEAD_DIM, seq), BF16),
        scratch_shapes=([pltpu.VMEM((n_blk, HB * HEAD_DIM), F32), pltpu.VMEM((HB, n_blk, 1, T), F32)]
                        + [pltpu.VMEM((1, T), F32), pltpu.VMEM((1, T), F32), pltpu.VMEM((HEAD_DIM, T), F32)] * HB),
        compiler_params=_cparams(("arbitrary", "arbitrary", "arbitrary")),
        name="moba_attention",
    )(qT, k, vT, bias3)
    return out


ROUTER_LANES = 128


def _projT_ln_kernel(*refs, n_in, with_router):
    aT_refs, w_refs = refs[:n_in], refs[n_in:2 * n_in]
    x_ref, g_ref, b_ref = refs[2 * n_in:2 * n_in + 3]
    tn = (((0,), (0,)), ((), ()))
    mix = lax.dot_general(aT_refs[0][0], w_refs[0][...], tn, preferred_element_type=F32)
    for aT_ref, w_ref in zip(aT_refs[1:], w_refs[1:]):
        mix = mix + lax.dot_general(aT_ref[0], w_ref[...], tn, preferred_element_type=F32)
    h = _layer_norm_rows(ALPHA * x_ref[0] + mix, g_ref[...], b_ref[...])
    if with_router:
        wr_ref, o_ref, r_ref = refs[2 * n_in + 3:]
        r_ref[0] = jnp.dot(h, wr_ref[...], preferred_element_type=F32, precision=lax.Precision.HIGHEST)
    else:
        o_ref, = refs[2 * n_in + 3:]
    o_ref[0] = h


def projT_residual_ln(aTs, w, x3, g, b, router=None, tm=256):
    bsz, seq, d = x3.shape
    ws, k0 = [], 0
    for aT in aTs:
        ws.append(w[k0:k0 + aT.shape[1]])
        k0 += aT.shape[1]
    n_in = len(aTs)
    once = pl.Buffered(1)
    vec = pl.BlockSpec((1, d), lambda bb, i: (0, 0))
    rows = lambda n: pl.BlockSpec((1, tm, n), lambda bb, i: (bb, i, 0))
    in_specs = ([pl.BlockSpec((1, aT.shape[1], tm), lambda bb, i: (bb, 0, i)) for aT in aTs]
                + [pl.BlockSpec(wi.shape, lambda bb, i: (0, 0), pipeline_mode=once) for wi in ws]
                + [rows(d), vec, vec])
    args = [*aTs, *ws, x3, g.reshape(1, d), b.reshape(1, d)]
    out_specs, out_shape = rows(d), jax.ShapeDtypeStruct((bsz, seq, d), F32)
    if router is not None:
        in_specs.append(pl.BlockSpec((d, ROUTER_LANES), lambda bb, i: (0, 0), pipeline_mode=once))
        args.append(jnp.pad(router, ((0, 0), (0, ROUTER_LANES - router.shape[1]))))
        out_specs = [out_specs, rows(ROUTER_LANES)]
        out_shape = [out_shape, jax.ShapeDtypeStruct((bsz, seq, ROUTER_LANES), F32)]
    return pl.pallas_call(
        functools.partial(_projT_ln_kernel, n_in=n_in, with_router=router is not None),
        grid=(bsz, seq // tm),
        in_specs=in_specs,
        out_specs=out_specs,
        out_shape=out_shape,
        compiler_params=_cparams(("arbitrary", "arbitrary")),
        name="projT_residual_ln",
    )(*args)


IDX_LANES = 128


def _issue_row_gather(idx_vmem_ref, idx_smem, sem_i, src_hbm, dst_slot_ref, sem_slot, n_rows):
    cp = pltpu.make_async_copy(idx_vmem_ref.at[0], idx_smem, sem_i)
    cp.start()
    cp.wait()

    for r in range(n_rows):
        row = idx_smem[r // IDX_LANES, r % IDX_LANES]
        pltpu.make_async_copy(src_hbm.at[pl.ds(row, 1)], dst_slot_ref.at[pl.ds(r, 1)], sem_slot).start()


def _pipelined_gather(idx0_ref, idxn_ref, idx_smem, sem_i, src_hbm, buf, sem_buf, n_rows):
    g = pl.program_id(0)
    slot = lax.rem(g, 2)

    @pl.when(g == 0)
    def _():
        _issue_row_gather(idx0_ref, idx_smem, sem_i, src_hbm, buf.at[0], sem_buf.at[0], n_rows)

    @pl.when(g + 1 < pl.num_programs(0))
    def _():
        _issue_row_gather(idxn_ref, idx_smem, sem_i, src_hbm, buf.at[1 - slot], sem_buf.at[1 - slot], n_rows)

    pltpu.make_async_copy(buf.at[slot], buf.at[slot], sem_buf.at[slot]).wait()
    return slot


def _gather_specs(n_steps, k):
    first = lambda g, *_: (0, 0, 0)
    nxt = lambda g, *_: (jnp.minimum(g + 1, n_steps - 1), 0, 0)
    return pl.BlockSpec((1, k, IDX_LANES), first), pl.BlockSpec((1, k, IDX_LANES), nxt)


def _moe_ffn_kernel(ge_ref, idx0_ref, idxn_ref, h_hbm, w1_ref, w3_ref, w2_ref, o_ref,
                    xbuf, idx_smem, sem_i, sem_x, *, ff_chunk):
    del ge_ref
    g = pl.program_id(0)
    slot = lax.rem(g, 2)
    wait_slot = lambda s: pltpu.make_async_copy(xbuf.at[s], xbuf.at[s], sem_x.at[s]).wait()

    @pl.when(g == 0)
    def _():
        _issue_row_gather(idx0_ref, idx_smem, sem_i, h_hbm, xbuf.at[0], sem_x.at[0], EXPERT_ROWS)

    wait_slot(slot)
    cp = pltpu.make_async_copy(idxn_ref.at[0], idx_smem, sem_i)
    cp.start()
    xb = xbuf[slot].astype(BF16)
    d_ff = w1_ref.shape[2]
    n_chunks = d_ff // ff_chunk
    n_issue = max(n_chunks - 2, 1)
    per_chunk = -(-EXPERT_ROWS // n_issue)
    acc = jnp.zeros((EXPERT_ROWS, w2_ref.shape[2]), F32)
    for ci in range(n_chunks):
        c = ci * ff_chunk
        a = jnp.dot(xb, w1_ref[0, :, c:c + ff_chunk], preferred_element_type=F32)
        u = jnp.dot(xb, w3_ref[0, :, c:c + ff_chunk], preferred_element_type=F32)
        hid = (a * jax.nn.sigmoid(a) * u).astype(BF16)
        acc = acc + jnp.dot(hid, w2_ref[0, c:c + ff_chunk, :], preferred_element_type=F32)
        if ci == 0:
            cp.wait()
        for r in range(ci * per_chunk, min((ci + 1) * per_chunk, EXPERT_ROWS)):
            row = idx_smem[r // IDX_LANES, r % IDX_LANES]
            pltpu.make_async_copy(h_hbm.at[pl.ds(row, 1)], xbuf.at[1 - slot, pl.ds(r, 1)], sem_x.at[1 - slot]).start()
    o_ref[...] = acc

    @pl.when(g == pl.num_programs(0) - 1)
    def _():
        wait_slot(1 - slot)


def moe_expert_ffn(h, row_tok, grp_e, w1, w3, w2, ff_chunk=512):
    d = h.shape[1]
    d_ff = w1.shape[2]
    n_groups = grp_e.shape[0]
    k = EXPERT_ROWS // IDX_LANES
    idx = row_tok.reshape(n_groups, k, IDX_LANES)
    once = pl.Buffered(1)
    idx0_spec, idxn_spec = _gather_specs(n_groups, k)
    grid_spec = pltpu.PrefetchScalarGridSpec(
        num_scalar_prefetch=1,
        grid=(n_groups,),
        in_specs=[idx0_spec, idxn_spec, pl.BlockSpec(memory_space=pl.ANY),
                  pl.BlockSpec((1, d, d_ff), lambda g, ge: (ge[g], 0, 0), pipeline_mode=once),
                  pl.BlockSpec((1, d, d_ff), lambda g, ge: (ge[g], 0, 0), pipeline_mode=once),
                  pl.BlockSpec((1, d_ff, d), lambda g, ge: (ge[g], 0, 0), pipeline_mode=once)],
        out_specs=pl.BlockSpec((EXPERT_ROWS, d), lambda g, ge: (g, 0)),
        scratch_shapes=[pltpu.VMEM((2, EXPERT_ROWS, d), F32), pltpu.SMEM((k, IDX_LANES), I32),
                        pltpu.SemaphoreType.DMA(()), pltpu.SemaphoreType.DMA((2,))],
    )
    return pl.pallas_call(
        functools.partial(_moe_ffn_kernel, ff_chunk=ff_chunk),
        grid_spec=grid_spec,
        out_shape=jax.ShapeDtypeStruct((n_groups * EXPERT_ROWS, d), F32),
        compiler_params=_cparams(("arbitrary",)),
        name="moe_expert_ffn",
    )(grp_e, idx, idx, h, w1, w3, w2)


COMBINE_TM = 256


def _moe_combine_ln_kernel(idx0_ref, idxn_ref, y_hbm, h_ref, gate_ref, g_ref, b_ref, o_ref,
                           ybuf, idx_smem, sem_i, sem_y):
    tm = COMBINE_TM
    slot = _pipelined_gather(idx0_ref, idxn_ref, idx_smem, sem_i, y_hbm, ybuf, sem_y, TOP_K * tm)
    y = gate_ref[:, 0:1] * ybuf[slot, 0:tm, :]
    for j in range(1, TOP_K):
        y = y + gate_ref[:, j:j + 1] * ybuf[slot, j * tm:(j + 1) * tm, :]
    o_ref[...] = _layer_norm_rows(ALPHA * h_ref[...] + y, g_ref[...], b_ref[...])


def moe_combine_ln(h, y_rows, dest, gate, g, b):
    m, d = h.shape
    tm = COMBINE_TM
    n_tiles = m // tm
    k = TOP_K * tm // IDX_LANES
    idx = dest.reshape(n_tiles, tm, TOP_K).transpose(0, 2, 1).reshape(n_tiles, k, IDX_LANES)
    idx0_spec, idxn_spec = _gather_specs(n_tiles, k)
    row = pl.BlockSpec((tm, d), lambda i: (i, 0))
    vec = pl.BlockSpec((1, d), lambda i: (0, 0))
    return pl.pallas_call(
        _moe_combine_ln_kernel,
        grid=(n_tiles,),
        in_specs=[idx0_spec, idxn_spec, pl.BlockSpec(memory_space=pl.ANY), row,
                  pl.BlockSpec((tm, TOP_K), lambda i: (i, 0)), vec, vec],
        out_specs=row,
        out_shape=jax.ShapeDtypeStruct((m, d), F32),
        scratch_shapes=[pltpu.VMEM((2, TOP_K * tm, d), F32), pltpu.SMEM((k, IDX_LANES), I32),
                        pltpu.SemaphoreType.DMA(()), pltpu.SemaphoreType.DMA((2,))],
        compiler_params=_cparams(("arbitrary",)),
        name="moe_combine_ln",
    )(idx, idx, y_rows, h, gate, g.reshape(1, d), b.reshape(1, d))


def moe_dispatch_plan(logits):
    n_tok = logits.shape[0]
    top_val, top_e = lax.top_k(logits, TOP_K)
    gate = jax.nn.softmax(top_val, axis=-1)
    e_flat = top_e.reshape(-1)
    onehot = (e_flat[:, None] == jnp.arange(N_EXPERTS, dtype=e_flat.dtype)[None, :]).astype(I32)
    rank = jnp.take_along_axis(jnp.cumsum(onehot, axis=0) - onehot, e_flat[:, None], axis=1)[:, 0]
    counts = jnp.sum(onehot, axis=0)
    padded = (counts + EXPERT_ROWS - 1) // EXPERT_ROWS * EXPERT_ROWS
    pend = jnp.cumsum(padded)
    pstart = pend - padded
    dest = pstart[e_flat] + rank
    n_assign = n_tok * TOP_K
    n_rows = -(-n_assign // EXPERT_ROWS) * EXPERT_ROWS + N_EXPERTS * EXPERT_ROWS
    n_groups = n_rows // EXPERT_ROWS
    tok_flat = jnp.repeat(jnp.arange(n_tok, dtype=I32), TOP_K)
    row_tok = jnp.zeros((n_rows,), I32).at[dest].set(tok_flat)
    grp_e = jnp.minimum(jnp.searchsorted(pend, jnp.arange(n_groups, dtype=I32) * EXPERT_ROWS, side='right'),
                        N_EXPERTS - 1).astype(I32)
    return gate, dest.astype(I32), row_tok, grp_e


def kernel(x, rel_bias, e_w_in, e_q_norm, e_kv_norm, e_w_uq, e_w_uk, e_w_uv, e_w_qidx, e_pos_k, e_pos_v, e_ck1, e_ck2, e_cv1, e_cv2, e_w_out, e_ln1_g, e_ln1_b, e_ffn_w1, e_ffn_w3, e_ffn_w2, e_ln2_g, e_ln2_b, o_w_in, o_w_out, o_ln1_g, o_ln1_b, o_router, o_moe_w1, o_moe_w3, o_moe_w2, o_ln2_g, o_ln2_b):
    bsz, seq, d = x.shape
    m = bsz * seq
    xf = x.reshape(m, d)
    dsa_bias = dsa_bias_tiles(rel_bias)
    nsa_bc, nsa_toe_s, nsa_toe_w = nsa_bias_inputs(rel_bias, seq)
    moba_bias = moba_bias_tiles(rel_bias)
    gd = B_GROUPS * HEAD_DIM
    for layer in range(DEPTH):
        i = layer // 2
        if layer % 2 == 0:
            x3 = xf.reshape(bsz, seq, d)
            (qidxT, qlatT, sT, kidx, ckv, ckvT, qbT, kcmp, vcmp, kslc, kwin, vT) = even_inproj(
                x3, e_w_in[i], e_q_norm[i], e_kv_norm[i], e_w_uq[i], e_w_uk[i], e_w_qidx[i])
            o_aT = dsa_attention(qidxT, sT, qlatT, kidx, ckv, ckvT, e_w_uv[i], dsa_bias)
            kc = nsa_compress(kcmp.reshape(m, gd), e_pos_k[i], e_ck1[i], e_ck2[i], bsz, seq)
            vc = nsa_compress(vcmp.reshape(m, gd), e_pos_v[i], e_cv1[i], e_cv2[i], bsz, seq)
            o_bT = nsa_attention(qbT, kc, vc, kslc, kwin, vT, sT, nsa_bc, nsa_toe_s, nsa_toe_w)
            h = projT_residual_ln([o_aT, o_bT], e_w_out[i].astype(BF16), x3, e_ln1_g[i], e_ln1_b[i]).reshape(m, d)
            tm = 1024
            xf = swiglu_ffn(h, jnp.zeros((m // tm,), I32), e_ffn_w1[i][None].astype(BF16),
                            e_ffn_w3[i][None].astype(BF16), e_ffn_w2[i][None].astype(BF16),
                            e_ln2_g[i], e_ln2_b[i], with_ln=True, out_dtype=F32, tm=tm, ff_chunk=1408)
        else:
            x3 = xf.reshape(bsz, seq, d)
            o_cT = moba_attention(*moba_inproj(x3, o_w_in[i]), moba_bias)
            h, logits = projT_residual_ln([o_cT], o_w_out[i].astype(BF16), x3, o_ln1_g[i], o_ln1_b[i],
                                          router=o_router[i])
            h = h.reshape(m, d)
            gate, dest, row_tok, grp_e = moe_dispatch_plan(logits.reshape(m, ROUTER_LANES)[:, :N_EXPERTS])
            y_rows = moe_expert_ffn(h, row_tok, grp_e, o_moe_w1[i].astype(BF16), o_moe_w3[i].astype(BF16),
                                    o_moe_w2[i].astype(BF16))
            xf = moe_combine_ln(h, y_rows, dest, gate, o_ln2_g[i], o_ln2_b[i])
    return xf.reshape(bsz, seq, d)
```

```python
import functools
import math

import numpy as np
import jax
import jax.numpy as jnp
from jax import lax
from jax.experimental import pallas as pl
from jax.experimental.pallas import tpu as pltpu

F32 = jnp.float32
BF16 = jnp.bfloat16
I32 = jnp.int32
BF16_ROWS = 16

HEAD_DIM = 64
NUM_BUCKETS = 32
MAX_DISTANCE = 128
N_BIAS_HEADS = 16
A_HEADS = 8
A_Q_RANK = 256
A_KV_RANK = 128
IDX_HEADS = 16
IDX_DIM = 64
DSA_TOPK = 256
B_HEADS = 8
B_GROUPS = 2
B_HPG = B_HEADS // B_GROUPS
CMP_LEN = 32
CMP_STRIDE = 16
SLC_BLOCK = 64
SLC_TOPN = 16
WINDOW = 512
C_HEADS = 16
MOBA_BLOCK = 256
MOBA_TOPK = 3
N_EXPERTS = 8
TOP_K = 2
EXPERT_ROWS = 256
DEPTH = 2
ALPHA = (2 * DEPTH) ** 0.25

LOG2E = 1.4426950408889634
QK_SCALE = HEAD_DIM ** -0.5 * LOG2E
NEG = -1e30
NEG_HALF = -5e29
INT_MIN = -2 ** 31
VMEM_LIMIT = 56 * 1024 * 1024


def _t5_thresholds():
    def bucket(n):
        if n < NUM_BUCKETS // 2:
            return n
        v = np.log(np.float32(n) / np.float32(NUM_BUCKETS // 2)) / np.float32(math.log(MAX_DISTANCE / (NUM_BUCKETS // 2)))
        return min(NUM_BUCKETS // 2 + int(np.float32(v) * (NUM_BUCKETS - NUM_BUCKETS // 2)), NUM_BUCKETS - 1)
    b = [bucket(i) for i in range(4 * MAX_DISTANCE)]
    return [0] + [min(i for i in range(len(b)) if b[i] >= k) for k in range(1, NUM_BUCKETS)]


T5_THR = _t5_thresholds()
T5_FAR = T5_THR[-1]


def _cparams(sem):
    return pltpu.CompilerParams(dimension_semantics=sem, vmem_limit_bytes=VMEM_LIMIT)


def _bias_kernel(tab_ref, off_ref, o_ref, *, c_row, c_col, h0, causal_neg, window):
    v = pl.program_id(0)
    h = pl.program_id(1) + h0
    shape = o_ref.shape[2:]
    dist = (c_col * lax.broadcasted_iota(I32, shape, 1) + c_row * lax.broadcasted_iota(I32, shape, 0) + off_ref[v])
    n = jnp.maximum(dist, 0)
    acc = jnp.full(shape, tab_ref[h] * LOG2E, F32)
    for k in range(1, NUM_BUCKETS):
        acc = jnp.where(n >= T5_THR[k], tab_ref[k * N_BIAS_HEADS + h] * LOG2E, acc)
    if causal_neg:
        acc = jnp.where(dist >= 0, acc, NEG)
    if window:
        acc = jnp.where(dist < window, acc, NEG)
    o_ref[0, 0] = acc


def bias_tiles(rel_bias, offs, n_heads, h0, rows, cols, c_row, c_col, causal_neg, window=0):
    offs = jnp.asarray(offs, I32)
    nv = offs.shape[0]
    return pl.pallas_call(
        functools.partial(_bias_kernel, c_row=c_row, c_col=c_col, h0=h0, causal_neg=causal_neg, window=window),
        grid=(nv, n_heads),
        in_specs=[pl.BlockSpec(memory_space=pltpu.SMEM), pl.BlockSpec(memory_space=pltpu.SMEM)],
        out_specs=pl.BlockSpec((1, 1, rows, cols), lambda v, h: (v, h, 0, 0)),
        out_shape=jax.ShapeDtypeStruct((nv, n_heads, rows, cols), F32),
        compiler_params=_cparams(("arbitrary", "arbitrary")),
        name="t5_bias_tiles",
    )(rel_bias.reshape(-1), offs)


def _layer_norm_rows(z, g, b):
    mu = jnp.mean(z, axis=-1, keepdims=True)
    zc = z - mu
    var = jnp.mean(zc * zc, axis=-1, keepdims=True)
    return zc * lax.rsqrt(var + 1e-5) * g + b


def _ffn_kernel(ge_ref, x_ref, w1_ref, w3_ref, w2_ref, g_ref, b_ref, o_ref, *, ff_chunk, with_ln):
    del ge_ref
    x = x_ref[...]
    xb = x.astype(BF16)
    d_ff = w1_ref.shape[2]
    acc = jnp.zeros((x.shape[0], w2_ref.shape[2]), F32)
    for c in range(0, d_ff, ff_chunk):
        a = jnp.dot(xb, w1_ref[0, :, c:c + ff_chunk], preferred_element_type=F32)
        u = jnp.dot(xb, w3_ref[0, :, c:c + ff_chunk], preferred_element_type=F32)
        hid = (a * jax.nn.sigmoid(a) * u).astype(BF16)
        acc = acc + jnp.dot(hid, w2_ref[0, c:c + ff_chunk, :], preferred_element_type=F32)
    if with_ln:
        o_ref[...] = _layer_norm_rows(ALPHA * x.astype(F32) + acc, g_ref[...], b_ref[...]).astype(o_ref.dtype)
    else:
        o_ref[...] = acc.astype(o_ref.dtype)


def swiglu_ffn(x_rows, grp_e, w1, w3, w2, ln_g, ln_b, *, with_ln, out_dtype, tm, ff_chunk):
    m, d = x_rows.shape
    d_ff = w1.shape[2]
    once = pl.Buffered(1)
    grid_spec = pltpu.PrefetchScalarGridSpec(
        num_scalar_prefetch=1,
        grid=(m // tm,),
        in_specs=[pl.BlockSpec((tm, d), lambda i, ge: (i, 0)),
                  pl.BlockSpec((1, d, d_ff), lambda i, ge: (ge[i], 0, 0), pipeline_mode=once),
                  pl.BlockSpec((1, d, d_ff), lambda i, ge: (ge[i], 0, 0), pipeline_mode=once),
                  pl.BlockSpec((1, d_ff, d), lambda i, ge: (ge[i], 0, 0), pipeline_mode=once),
                  pl.BlockSpec((1, d), lambda i, ge: (0, 0)), pl.BlockSpec((1, d), lambda i, ge: (0, 0))],
        out_specs=pl.BlockSpec((tm, d), lambda i, ge: (i, 0)),
    )
    return pl.pallas_call(
        functools.partial(_ffn_kernel, ff_chunk=ff_chunk, with_ln=with_ln),
        grid_spec=grid_spec,
        out_shape=jax.ShapeDtypeStruct((m, d), out_dtype),
        compiler_params=_cparams(("arbitrary",)),
        name="swiglu_ffn",
    )(grp_e, x_rows, w1, w3, w2, ln_g.reshape(1, d), ln_b.reshape(1, d))


def _flash_probs(s, m, segs=None):
    if segs is None:
        m_new = jnp.maximum(m, jnp.max(s, axis=0, keepdims=True))
        p = jnp.exp2(s - m_new)
    else:
        m_new, r0 = m, 0
        for n, c in segs:
            seg_max = jnp.max(s[r0:r0 + n], axis=0, keepdims=True)
            m_new = jnp.maximum(m_new, jnp.where(c > NEG_HALF, seg_max + c, NEG))
            r0 += n
        parts, r0 = [], 0
        for n, c in segs:
            shift = jnp.where(c > NEG_HALF, m_new - c, -NEG)
            parts.append(jnp.exp2(s[r0:r0 + n] - shift))
            r0 += n
        p = parts[0] if len(parts) == 1 else jnp.concatenate(parts, axis=0)
    return m_new, jnp.exp2(m - m_new), p.astype(BF16)


def _flash_merge(states):
    ms = [st[0][...] for st in states]
    m = functools.reduce(jnp.maximum, ms)
    ws = [jnp.exp2(mi - m) for mi in ms]
    l = sum(w * st[1][...] for w, st in zip(ws, states))
    acc = sum(w * st[2][...] for w, st in zip(ws, states))
    return m, l, acc


def _flash_finish(m, l, acc):
    return jnp.where(m > NEG_HALF, acc / l, 0.0)


def _rms_rows(x, g):
    return x * lax.rsqrt(jnp.mean(x * x, axis=-1, keepdims=True) + 1e-6) * g


EVEN_T = 256
SMALL_ROWS = 48
GATE_ROW0 = IDX_HEADS
NT_DIMS = (((1,), (1,)), ((), ()))


def _even_inproj_kernel(x_ref, wa_ref, wsT_ref, qn_ref, kvn_ref, wuq_ref, wuk_ref, wqiT_ref, wqbT_ref, wk4_ref, wvT_ref,
                        qidxT_ref, qlatT_ref, sT_ref, kidx_ref, ckv_ref, ckvT_ref,
                        qbT_ref, kcmp_ref, vcmp_ref, kslc_ref, kwin_ref, vT_ref):
    xb = x_ref[0].astype(BF16)
    ya = jnp.dot(xb, wa_ref[...], preferred_element_type=F32)
    cqn = _rms_rows(ya[:, :A_Q_RANK], qn_ref[...]).astype(BF16)
    ckvn = _rms_rows(ya[:, A_Q_RANK:A_Q_RANK + A_KV_RANK], kvn_ref[...])
    kidx_ref[0, 0] = ya[:, A_Q_RANK + A_KV_RANK:A_Q_RANK + A_KV_RANK + IDX_DIM].astype(BF16)
    ckv_ref[0, 0] = ckvn.astype(BF16)
    ckvT_ref[0, 0] = ckvn.T.astype(BF16)
    sT_ref[0] = lax.dot_general(wsT_ref[...], xb, NT_DIMS, preferred_element_type=F32)
    q = jnp.dot(cqn, wuq_ref[...], preferred_element_type=F32).astype(BF16)
    for h in range(A_HEADS):
        qlT = lax.dot_general(wuk_ref[h], q[:, h * HEAD_DIM:(h + 1) * HEAD_DIM], NT_DIMS, preferred_element_type=F32)
        qlatT_ref[0, h * A_KV_RANK:(h + 1) * A_KV_RANK, :] = (qlT * QK_SCALE).astype(BF16)
    qidxT_ref[0] = lax.dot_general(wqiT_ref[...], cqn, NT_DIMS, preferred_element_type=F32).astype(BF16)
    qbT_ref[0] = lax.dot_general(wqbT_ref[...], xb, NT_DIMS, preferred_element_type=F32).astype(BF16)
    yk = jnp.dot(xb, wk4_ref[...], preferred_element_type=F32).astype(BF16)
    gd = B_GROUPS * HEAD_DIM
    for j, ref in enumerate((kcmp_ref, vcmp_ref, kslc_ref, kwin_ref)):
        ref[0] = yk[:, j * gd:(j + 1) * gd]
    vT = lax.dot_general(wvT_ref[...], xb, NT_DIMS, preferred_element_type=F32).astype(BF16)
    for j in range(EVEN_T // NSA_KT):
        vT_ref[0, j] = vT[:, j * NSA_KT:(j + 1) * NSA_KT]


def even_inproj(x3, w_in, q_norm, kv_norm, w_uq, w_uk, w_qidx):
    bsz, seq, d = x3.shape
    T = EVEN_T
    nq = seq // T
    gd = B_GROUPS * HEAD_DIM
    n_kt = seq // NSA_KT
    o_kidx = A_Q_RANK + A_KV_RANK
    o_widx = o_kidx + IDX_DIM
    o_qb = o_widx + IDX_HEADS
    o_kv = o_qb + B_HEADS * HEAD_DIM
    o_gate = o_kv + 6 * gd
    kv = lambda j: w_in[:, o_kv + j * gd:o_kv + (j + 1) * gd]
    wa = jnp.pad(w_in[:, :o_widx], ((0, 0), (0, 512 - o_widx))).astype(BF16)
    w_gate = w_in[:, o_gate:].reshape(d, B_GROUPS, B_HPG, 3).transpose(0, 1, 3, 2).reshape(d, 3 * B_HEADS)
    wsT = jnp.concatenate([w_in[:, o_widx:o_qb] * IDX_HEADS ** -0.5, w_gate,
                           jnp.zeros((d, SMALL_ROWS - IDX_HEADS - 3 * B_HEADS), w_in.dtype)], axis=1).T.astype(BF16)
    wuq = w_uq.reshape(A_Q_RANK, A_HEADS * HEAD_DIM).astype(BF16)
    wuk = jnp.transpose(w_uk, (1, 0, 2)).astype(BF16)
    wqiT = w_qidx.reshape(A_Q_RANK, IDX_HEADS * IDX_DIM).T.astype(BF16)
    wqbT = (w_in[:, o_qb:o_kv] * QK_SCALE).T.astype(BF16)
    wk4 = jnp.concatenate([kv(0), kv(1), kv(2), kv(4)], axis=1).astype(BF16)
    wvT = jnp.concatenate([kv(3), kv(5)], axis=1).T.astype(BF16)
    weights = (wa, wsT, q_norm.reshape(1, -1), kv_norm.reshape(1, -1), wuq, wuk, wqiT, wqbT, wk4, wvT)
    once = pl.Buffered(1)
    w_specs = [pl.BlockSpec(w.shape, (lambda b, i, n=w.ndim: (0,) * n), pipeline_mode=once) for w in weights]
    fm = lambda rows: pl.BlockSpec((1, rows, T), lambda b, i: (b, 0, i))
    tok = lambda cols: pl.BlockSpec((1, T, cols), lambda b, i: (b, i, 0))
    blk = lambda r, c: pl.BlockSpec((1, 1, r, c), lambda b, i: (b, i, 0, 0))
    sds = jax.ShapeDtypeStruct
    return pl.pallas_call(
        _even_inproj_kernel,
        grid=(bsz, nq),
        in_specs=[pl.BlockSpec((1, T, d), lambda b, i: (b, i, 0))] + w_specs,
        out_specs=[fm(IDX_HEADS * IDX_DIM), fm(A_HEADS * A_KV_RANK), fm(SMALL_ROWS),
                   blk(T, IDX_DIM), blk(T, A_KV_RANK), blk(A_KV_RANK, T),
                   fm(B_HEADS * HEAD_DIM), tok(gd), tok(gd), tok(gd), tok(gd),
                   pl.BlockSpec((1, T // NSA_KT, 2 * gd, NSA_KT), lambda b, i: (b, i, 0, 0))],
        out_shape=[sds((bsz, IDX_HEADS * IDX_DIM, seq), BF16), sds((bsz, A_HEADS * A_KV_RANK, seq), BF16),
                   sds((bsz, SMALL_ROWS, seq), F32),
                   sds((bsz, nq, T, IDX_DIM), BF16), sds((bsz, nq, T, A_KV_RANK), BF16), sds((bsz, nq, A_KV_RANK, T), BF16),
                   sds((bsz, B_HEADS * HEAD_DIM, seq), BF16),
                   sds((bsz, seq, gd), BF16), sds((bsz, seq, gd), BF16), sds((bsz, seq, gd), BF16), sds((bsz, seq, gd), BF16),
                   sds((bsz, n_kt, 2 * gd, NSA_KT), BF16)],
        compiler_params=_cparams(("arbitrary", "arbitrary")),
        name="even_inproj",
    )(x3, *weights)


DSA_T = 256
SUB = 128


def _dsa_kernel(qidx_ref, wT_ref, qlat_ref, kidx_ref, ckv_ref, ckvT_ref, bias_ref, wuvt_ref, o_ref,
                key_ref, selb0_ref, selb1_ref, *state_refs, n_keep):
    selb_refs = (selb0_ref, selb1_ref)
    m_refs, l_refs, acc_refs = (state_refs[0:A_HEADS], state_refs[A_HEADS:2 * A_HEADS], state_refs[2 * A_HEADS:])
    qi = pl.program_id(1)
    nkb = qi + 1
    T = DSA_T

    def score_block(kb, carry):
        for sub in range(T // SUB):
            k = kidx_ref[0, kb, sub * SUB:(sub + 1) * SUB, :]
            acc = jnp.zeros((SUB, T), F32)
            for h in range(IDX_HEADS):
                d = jnp.dot(k, qidx_ref[0, h * IDX_DIM:(h + 1) * IDX_DIM, :], preferred_element_type=F32)
                acc = acc + jnp.maximum(d, 0.0) * wT_ref[0, h:h + 1, :]
            bits = lax.bitcast_convert_type(acc, I32)
            key = bits ^ (lax.shift_right_arithmetic(bits, 31) & 0x7FFFFFFF)
            s_pos = kb * T + sub * SUB + lax.broadcasted_iota(I32, (SUB, T), 0)
            t_pos = qi * T + lax.broadcasted_iota(I32, (SUB, T), 1)
            key = jnp.where(s_pos <= t_pos, key, INT_MIN)
            key_ref[pl.ds(pl.multiple_of(kb * T + sub * SUB, SUB), SUB), :] = key
        return carry

    lax.fori_loop(0, nkb, score_block, 0)

    def count_ge(cand):
        def body(kb, cnt):
            blk = key_ref[pl.ds(pl.multiple_of(kb * T, T), T), :]
            ge = jnp.where(blk >= cand, 1, 0).astype(I32)
            return cnt + jnp.sum(ge.reshape(T // 8, 8, T), axis=0)
        cnt = lax.fori_loop(0, nkb, body, jnp.zeros((8, T), I32))
        return jnp.sum(cnt, axis=0, keepdims=True)

    def bit_step(i, u):
        cand_u = u | lax.shift_left(jnp.int32(1), 31 - i)
        cnt = count_ge(cand_u ^ INT_MIN)
        return jnp.where(cnt >= n_keep, cand_u, u)

    u = lax.fori_loop(0, 32, bit_step, jnp.zeros((1, T), I32))
    thr = jnp.maximum(u ^ INT_MIN, INT_MIN + 1)

    for h in range(A_HEADS):
        m_refs[h][...] = jnp.full(m_refs[h].shape, NEG, F32)
        l_refs[h][...] = jnp.zeros(l_refs[h].shape, F32)
        acc_refs[h][...] = jnp.zeros(acc_refs[h].shape, F32)

    states = list(zip(m_refs, l_refs, acc_refs))
    far_bias = [bias_ref[2, h, 0:1, :] for h in range(A_HEADS)]

    def masked_scores(kb, selb):
        selb[...] = jnp.where(key_ref[pl.ds(pl.multiple_of(kb * T, T), T), :] >= thr, 0.0, NEG)
        ckv = ckv_ref[0, kb]
        return [jnp.dot(ckv, qlat_ref[0, h * A_KV_RANK:(h + 1) * A_KV_RANK, :], preferred_element_type=F32) + selb[...]
                for h in range(A_HEADS)]

    def far_body(i, carry):
        tiles = []
        for u, selb in enumerate(selb_refs):
            kb_raw = len(selb_refs) * i + u
            live = kb_raw < n_far
            kb = jnp.minimum(kb_raw, n_far - 1)
            segs = [[(T, jnp.where(live, far_bias[h], NEG))] for h in range(A_HEADS)]
            tiles.append((list(zip(masked_scores(kb, selb), segs)), [ckvT_ref[0, kb]] * A_HEADS))
        for s_all, v_all in tiles:
            _flash_step(s_all, v_all, states)
        return carry

    def near_body(kb, carry):
        s_all = [s + bias_ref[qi - kb, h] for h, s in enumerate(masked_scores(kb, selb_refs[0]))]
        _flash_step(s_all, [ckvT_ref[0, kb]] * A_HEADS, states)
        return carry

    n_far = jnp.maximum(qi - 1, 0)
    lax.fori_loop(0, lax.div(n_far + (len(selb_refs) - 1), jnp.int32(len(selb_refs))), far_body, 0)
    lax.fori_loop(n_far, nkb, near_body, 0)

    for h in range(A_HEADS):
        o_lat = _flash_finish(m_refs[h][...], l_refs[h][...], acc_refs[h][...]).astype(BF16)
        o_ref[0, h * HEAD_DIM:(h + 1) * HEAD_DIM, :] = jnp.dot(
            wuvt_ref[h], o_lat, preferred_element_type=F32).astype(o_ref.dtype)


def dsa_attention(qidxT, sT, qlatT, kidx, ckv, ckvT, w_uv, bias3):
    T = DSA_T
    assert T == EVEN_T
    bsz, nq = kidx.shape[0], kidx.shape[1]
    seq = nq * T
    n_keep = min(DSA_TOPK, seq // 4)
    wuvt = jnp.transpose(w_uv, (1, 2, 0)).astype(BF16)
    return pl.pallas_call(
        functools.partial(_dsa_kernel, n_keep=n_keep),
        grid=(bsz, nq),
        in_specs=[pl.BlockSpec((1, IDX_HEADS * IDX_DIM, T), lambda b, i: (b, 0, i)),
                  pl.BlockSpec((1, SMALL_ROWS, T), lambda b, i: (b, 0, i)),
                  pl.BlockSpec((1, A_HEADS * A_KV_RANK, T), lambda b, i: (b, 0, i)),
                  pl.BlockSpec((1, nq, T, IDX_DIM), lambda b, i: (b, 0, 0, 0)),
                  pl.BlockSpec((1, nq, T, A_KV_RANK), lambda b, i: (b, 0, 0, 0)),
                  pl.BlockSpec((1, nq, A_KV_RANK, T), lambda b, i: (b, 0, 0, 0)),
                  pl.BlockSpec((3, A_HEADS, T, T), lambda b, i: (0, 0, 0, 0)),
                  pl.BlockSpec((A_HEADS, HEAD_DIM, A_KV_RANK), lambda b, i: (0, 0, 0))],
        out_specs=pl.BlockSpec((1, A_HEADS * HEAD_DIM, T), lambda b, i: (b, 0, i)),
        out_shape=jax.ShapeDtypeStruct((bsz, A_HEADS * HEAD_DIM, seq), BF16),
        scratch_shapes=([pltpu.VMEM((seq, T), I32), pltpu.VMEM((T, T), F32), pltpu.VMEM((T, T), F32)]
                        + [pltpu.VMEM((1, T), F32)] * (2 * A_HEADS)
                        + [pltpu.VMEM((A_KV_RANK, T), F32)] * A_HEADS),
        compiler_params=_cparams(("arbitrary", "arbitrary")),
        name="dsa_attention",
    )(qidxT, sT, qlatT, kidx, ckv, ckvT, bias3, wuvt)


def dsa_bias_tiles(rel_bias):
    assert DSA_T + 1 >= T5_FAR
    return bias_tiles(rel_bias, [0, DSA_T, 4 * DSA_T], A_HEADS, 0, DSA_T, DSA_T, -1, 1, False)


N_CMP_PAD = 256


def _compress_kernel(blk_ref, pos_ref, w1_ref, w2_ref, o_ref):
    x = (blk_ref[0].astype(F32) + pos_ref[...]).astype(BF16)
    hid = jax.nn.gelu(jnp.dot(x, w1_ref[...], preferred_element_type=F32))
    o_ref[0] = jnp.dot(hid.astype(BF16), w2_ref[...], preferred_element_type=F32).astype(o_ref.dtype)


def nsa_compress(a, pos, w1, w2, bsz, seq):
    n_chunk = seq // CMP_STRIDE
    assert CMP_LEN == 2 * CMP_STRIDE and n_chunk <= N_CMP_PAD
    width = CMP_STRIDE * HEAD_DIM
    chunks = a.reshape(bsz, n_chunk, CMP_STRIDE, B_GROUPS, HEAD_DIM).transpose(0, 3, 1, 2, 4)
    chunks = chunks.reshape(bsz * B_GROUPS, n_chunk, width)
    blocks = jnp.concatenate([chunks[:, :-1], chunks[:, 1:]], axis=-1)
    blocks = jnp.pad(blocks, ((0, 0), (0, N_CMP_PAD - (n_chunk - 1)), (0, 0)))
    out = pl.pallas_call(
        _compress_kernel,
        grid=(bsz * B_GROUPS,),
        in_specs=[pl.BlockSpec((1, N_CMP_PAD, 2 * width), lambda i: (i, 0, 0)),
                  pl.BlockSpec((1, 2 * width), lambda i: (0, 0)),
                  pl.BlockSpec((2 * width, HEAD_DIM), lambda i: (0, 0)),
                  pl.BlockSpec((HEAD_DIM, HEAD_DIM), lambda i: (0, 0))],
        out_specs=pl.BlockSpec((1, N_CMP_PAD, HEAD_DIM), lambda i: (i, 0, 0)),
        out_shape=jax.ShapeDtypeStruct((bsz * B_GROUPS, N_CMP_PAD, HEAD_DIM), BF16),
        compiler_params=_cparams(("arbitrary",)),
        name="nsa_compress",
    )(blocks, pos.reshape(1, 2 * width), w1.reshape(2 * width, HEAD_DIM).astype(BF16), w2.astype(BF16))
    return out.reshape(bsz, B_GROUPS, N_CMP_PAD, HEAD_DIM)


NSA_TQ = 128
NSA_L = B_HPG * NSA_TQ
NSA_KT = 128
NSA_SLC_REL = 3
NSA_WIN_REL = 5
NSA_FAR_SPLIT = 4


def _flash_step(s_all, vT_all, states):
    probs = []
    for item, (m_ref, _, _) in zip(s_all, states):
        s, segs = item if isinstance(item, tuple) else (item, None)
        m_new, alpha, p = _flash_probs(s, m_ref[...], segs)
        m_ref[...] = m_new
        probs.append((alpha, p))
    for (alpha, p), vT, (_, l_ref, acc_ref) in zip(probs, vT_all, states):
        d = vT.shape[0]
        ones = jnp.ones((BF16_ROWS, vT.shape[1]), BF16)
        pv = jnp.dot(jnp.concatenate([vT, ones], axis=0), p, preferred_element_type=F32)
        acc_ref[...] = alpha * acc_ref[...] + pv[:d]
        l_ref[...] = alpha * l_ref[...] + pv[d:d + 1]


def _flash_loop(lo, hi, scores, values, states, segs, unroll=2):
    def body(i, carry):
        tiles = []
        for u in range(unroll):
            j_raw = lo + unroll * i + u
            live = j_raw < hi
            j = jnp.minimum(j_raw, hi - 1)
            sg = [[(n, jnp.where(live, c, NEG)) for n, c in chain] for chain in segs(j)]
            tiles.append((list(zip(scores(j), sg)), values(j)))
        for s_all, v_all in tiles:
            _flash_step(s_all, v_all, states)
        return carry

    lax.fori_loop(0, lax.div(hi - lo + (unroll - 1), jnp.int32(unroll)), body, 0)


def _nsa_kernel(qT_ref, kc_ref, vcT_ref, biasc_ref, ovl_ref, ks_ref, kw_ref, vT_ref,
                toes_ref, toew_ref, sT_ref, o_ref, selb_ref, *st, n_cmp, n_sel, n_slc):
    qi = pl.program_id(1)
    TQ, L = NSA_TQ, NSA_L
    q0 = qi * TQ
    qTs, qTs_pad = [], []
    for g in range(B_GROUPS):
        q = jnp.concatenate([qT_ref[0, (g * B_HPG + n) * HEAD_DIM:(g * B_HPG + n + 1) * HEAD_DIM, :]
                             for n in range(B_HPG)], axis=1)
        parts = [jnp.zeros_like(q)] * B_GROUPS
        parts[g] = q
        qTs.append(q)
        qTs_pad.append(jnp.concatenate(parts, axis=0))
    t_lane = q0 + (lax.broadcasted_iota(I32, (1, L), 1) & (TQ - 1))

    o_cs = []
    for g in range(B_GROUPS):
        s = jnp.dot(kc_ref[0, g], qTs[g], preferred_element_type=F32) + biasc_ref[g, 0]
        i_idx = lax.broadcasted_iota(I32, (N_CMP_PAD, L), 0)
        valid = jnp.where(i_idx < n_cmp, i_idx * CMP_STRIDE + (CMP_LEN - 1), 2 ** 30) <= t_lane
        s = jnp.where(valid, s, NEG)
        m = jnp.max(s, axis=0, keepdims=True)
        p = jnp.where(valid, jnp.exp2(s - m), 0.0)
        l = jnp.sum(p, axis=0, keepdims=True)
        p_c = p / jnp.where(l > 0, l, 1.0)
        o_c = jnp.dot(vcT_ref[0, g], p_c.astype(BF16), preferred_element_type=F32)

        psum = p_c[:, 0:TQ]
        for n in range(1, B_HPG):
            psum = psum + p_c[:, n * TQ:(n + 1) * TQ]
        sc = jnp.dot(ovl_ref[...], psum, preferred_element_type=F32, precision=lax.Precision.HIGHEST)
        j_idx = lax.broadcasted_iota(I32, (n_slc, TQ), 0)
        cur = (q0 + lax.broadcasted_iota(I32, (1, TQ), 1)) // SLC_BLOCK
        adm = j_idx <= cur
        forced = (j_idx == 0) | (j_idx == cur) | (j_idx == cur - 1)
        scv = jnp.where(adm, jnp.where(forced, jnp.inf, sc), -jnp.inf)
        rank = jnp.zeros((n_slc, TQ), I32)
        for jp in range(n_slc):
            row = scv[jp:jp + 1, :]
            beats = jnp.where(row > scv, 1, jnp.where((row == scv) & (jp < j_idx), 1, 0))
            rank = rank + beats
        selb = jnp.where(rank < n_sel, 0.0, NEG).astype(F32)
        selb4 = jnp.concatenate([selb] * B_HPG, axis=1)
        for j in range(n_slc):
            selb_ref[g, j] = selb4[j:j + 1, :]
        o_cs.append(o_c)

    for ref in st[0::3]:
        ref[...] = jnp.full(ref.shape, NEG, F32)
    for ref in st[1::3] + st[2::3]:
        ref[...] = jnp.zeros(ref.shape, F32)
    slc_st = [st[6 * g:6 * g + 3] for g in range(B_GROUPS)]
    win_st = [st[6 * g + 3:6 * g + 6] for g in range(B_GROUPS)]
    n_main = 6 * B_GROUPS
    xtr_st = [[st[n_main + 3 * (g * (NSA_FAR_SPLIT - 1) + r):n_main + 3 * (g * (NSA_FAR_SPLIT - 1) + r) + 3]
               for r in range(NSA_FAR_SPLIT - 1)] for g in range(B_GROUPS)]
    per_kt = NSA_KT // SLC_BLOCK

    groups = range(B_GROUPS)
    far_bias = [toes_ref[g, NSA_SLC_REL - 1, 0:1, :] for g in groups]

    def slc_scores(g, jt, near):
        s = jnp.dot(ks_ref[0, jt], qTs_pad[g], preferred_element_type=F32)
        return s + toes_ref[g, jnp.minimum(qi - jt, NSA_SLC_REL - 1)] if near else s

    def slc_segs(g, jt, near):
        return [(SLC_BLOCK, selb_ref[g, per_kt * jt + r] + (0.0 if near else far_bias[g])) for r in range(per_kt)]

    def win_scores(g, jt):
        rel = jnp.minimum(qi - jt, NSA_WIN_REL - 1)
        return jnp.dot(kw_ref[0, jt], qTs_pad[g], preferred_element_type=F32) + toew_ref[g, rel]

    gd = B_GROUPS * HEAD_DIM
    v_slc = lambda g, jt: vT_ref[0, jt, g * HEAD_DIM:(g + 1) * HEAD_DIM, :]
    v_win = lambda g, jt: vT_ref[0, jt, gd + g * HEAD_DIM:gd + (g + 1) * HEAD_DIM, :]

    assert NSA_WIN_REL >= NSA_SLC_REL
    j_lo = jnp.maximum(qi - (NSA_WIN_REL - 1), 0)

    def far_body(i, carry):
        s_all, v_all, chains = [], [], []
        for r in range(NSA_FAR_SPLIT):
            jt_raw = NSA_FAR_SPLIT * i + r
            live = jt_raw < j_lo
            jt = jnp.minimum(jt_raw, j_lo - 1)
            for g in groups:
                segs = [(n, jnp.where(live, c, NEG)) for n, c in slc_segs(g, jt, False)]
                s_all.append((slc_scores(g, jt, False), segs))
                v_all.append(v_slc(g, jt))
                chains.append(slc_st[g] if r == 0 else xtr_st[g][r - 1])
        _flash_step(s_all, v_all, chains)
        return carry

    zero_row = jnp.zeros((1, L), F32)
    lax.fori_loop(0, lax.div(j_lo + (NSA_FAR_SPLIT - 1), jnp.int32(NSA_FAR_SPLIT)), far_body, 0)
    _flash_loop(j_lo, qi + 1,
                lambda jt: [slc_scores(g, jt, True) for g in groups] + [win_scores(g, jt) for g in groups],
                lambda jt: [v_slc(g, jt) for g in groups] + [v_win(g, jt) for g in groups],
                slc_st + win_st,
                lambda jt: [slc_segs(g, jt, True) for g in groups] + [[(NSA_KT, zero_row)] for g in groups])

    for g in range(B_GROUPS):
        o_s = _flash_finish(*_flash_merge([slc_st[g]] + xtr_st[g]))
        o_w = _flash_finish(*[r[...] for r in win_st[g]])
        row0 = GATE_ROW0 + g * 3 * B_HPG
        gate = [jax.nn.sigmoid(jnp.concatenate([sT_ref[0, row0 + j * B_HPG + n:row0 + j * B_HPG + n + 1, :]
                                                for n in range(B_HPG)], axis=1)) for j in range(3)]
        o = (gate[0] * o_cs[g] + gate[1] * o_s + gate[2] * o_w).astype(o_ref.dtype)
        for n in range(B_HPG):
            o_ref[0, (g * B_HPG + n) * HEAD_DIM:(g * B_HPG + n + 1) * HEAD_DIM, :] = o[:, n * TQ:(n + 1) * TQ]


def nsa_bias_inputs(rel_bias, seq):
    TQ, L, KT = NSA_TQ, NSA_L, NSA_KT
    nq = seq // TQ
    bc = bias_tiles(rel_bias, [-(CMP_LEN - 1)], B_HEADS, A_HEADS, N_CMP_PAD, seq, -CMP_STRIDE, 1, False, 0)
    bc = bc.reshape(B_GROUPS, B_HPG, N_CMP_PAD, nq, TQ).transpose(0, 3, 2, 1, 4).reshape(B_GROUPS, nq, N_CMP_PAD, L)

    def lanes(t):
        v = t.shape[0]
        return t.reshape(v, B_GROUPS, B_HPG, KT, TQ).transpose(1, 0, 3, 2, 4).reshape(B_GROUPS, v, KT, L)

    assert KT == TQ and (NSA_SLC_REL - 1) * KT - (KT - 1) >= T5_FAR
    toe_s = bias_tiles(rel_bias, [v * KT for v in range(NSA_SLC_REL - 1)] + [64 * KT],
                       B_HEADS, A_HEADS, KT, TQ, -1, 1, True, 0)
    assert (NSA_WIN_REL - 1) * KT - (KT - 1) < WINDOW <= NSA_WIN_REL * KT - (KT - 1)
    toe_w = bias_tiles(rel_bias, [v * KT for v in range(NSA_WIN_REL)], B_HEADS, A_HEADS, KT, TQ, -1, 1, True, WINDOW)
    return bc, lanes(toe_s), lanes(toe_w)


def nsa_overlap(seq):
    n_cmp = (seq - CMP_LEN) // CMP_STRIDE + 1
    n_slc = seq // SLC_BLOCK
    cs = np.arange(N_CMP_PAD) * CMP_STRIDE
    ss = np.arange(n_slc) * SLC_BLOCK
    ov = ((cs[None, :] + CMP_LEN - 1 >= ss[:, None]) & (cs[None, :] <= ss[:, None] + SLC_BLOCK - 1)
          & (np.arange(N_CMP_PAD)[None, :] < n_cmp))
    return jnp.asarray(ov.astype(np.float32))


def nsa_attention(qbT, kc, vc, kslc, kwin, vT, sT, biasc, toe_s, toe_w):
    TQ, L, KT, G = NSA_TQ, NSA_L, NSA_KT, B_GROUPS
    bsz, n_kt = vT.shape[0], vT.shape[1]
    seq = n_kt * KT
    nq = seq // TQ
    n_slc = seq // SLC_BLOCK
    n_cmp = (seq - CMP_LEN) // CMP_STRIDE + 1
    n_sel = min(SLC_TOPN, n_slc)
    gd = G * HEAD_DIM
    vcT = vc.transpose(0, 1, 3, 2)
    once = pl.Buffered(1)
    k_spec = pl.BlockSpec((1, n_kt, KT, gd), lambda b, i: (b, 0, 0, 0))
    n_chain = 2 * G + G * (NSA_FAR_SPLIT - 1)
    return pl.pallas_call(
        functools.partial(_nsa_kernel, n_cmp=n_cmp, n_sel=n_sel, n_slc=n_slc),
        grid=(bsz, nq),
        in_specs=[pl.BlockSpec((1, B_HEADS * HEAD_DIM, TQ), lambda b, i: (b, 0, i)),
                  pl.BlockSpec((1, G, N_CMP_PAD, HEAD_DIM), lambda b, i: (b, 0, 0, 0)),
                  pl.BlockSpec((1, G, HEAD_DIM, N_CMP_PAD), lambda b, i: (b, 0, 0, 0)),
                  pl.BlockSpec((G, 1, N_CMP_PAD, L), lambda b, i: (0, i, 0, 0)),
                  pl.BlockSpec((n_slc, N_CMP_PAD), lambda b, i: (0, 0), pipeline_mode=once),
                  k_spec, k_spec,
                  pl.BlockSpec((1, n_kt, 2 * gd, KT), lambda b, i: (b, 0, 0, 0)),
                  pl.BlockSpec((G, NSA_SLC_REL, KT, L), lambda b, i: (0, 0, 0, 0), pipeline_mode=once),
                  pl.BlockSpec((G, NSA_WIN_REL, KT, L), lambda b, i: (0, 0, 0, 0), pipeline_mode=once),
                  pl.BlockSpec((1, SMALL_ROWS, TQ), lambda b, i: (b, 0, i))],
        out_specs=pl.BlockSpec((1, B_HEADS * HEAD_DIM, TQ), lambda b, i: (b, 0, i)),
        out_shape=jax.ShapeDtypeStruct((bsz, B_HEADS * HEAD_DIM, seq), BF16),
        scratch_shapes=([pltpu.VMEM((G, n_slc, 1, L), F32)]
                        + [pltpu.VMEM((1, L), F32), pltpu.VMEM((1, L), F32), pltpu.VMEM((HEAD_DIM, L), F32)] * n_chain),
        compiler_params=_cparams(("arbitrary", "arbitrary")),
        name="nsa_attention",
    )(qbT, kc, vcT, biasc, nsa_overlap(seq), kslc.reshape(bsz, n_kt, KT, gd), kwin.reshape(bsz, n_kt, KT, gd),
      vT, toe_s, toe_w, sT)


MOBA_T = MOBA_BLOCK


MOBA_HB = 8


PAIR = 2 * HEAD_DIM


def _moba_inproj_kernel(x_ref, wqT_ref, wk_ref, wvT_ref, qT_ref, k_ref, vT_ref):
    xb = x_ref[0].astype(BF16)
    nt = (((1,), (1,)), ((), ()))
    qT_ref[0] = lax.dot_general(wqT_ref[...], xb, nt, preferred_element_type=F32).astype(BF16)
    k_ref[0, 0] = jnp.dot(xb, wk_ref[...], preferred_element_type=F32).astype(BF16)
    vT_ref[0, 0] = lax.dot_general(wvT_ref[...], xb, nt, preferred_element_type=F32).astype(BF16)


def _pair_padded_qT(wq):
    n_heads = wq.shape[1] // HEAD_DIM
    wT = wq.T.reshape(n_heads, HEAD_DIM, wq.shape[0])
    z = jnp.zeros_like(wT)
    even = jnp.concatenate([wT, z], axis=1)
    odd = jnp.concatenate([z, wT], axis=1)
    is_even = (jnp.arange(n_heads) % 2 == 0)[:, None, None]
    return jnp.where(is_even, even, odd).reshape(n_heads * PAIR, wq.shape[0])


def moba_inproj(x3, w_in):
    bsz, seq, d = x3.shape
    T = MOBA_T
    n_blk = seq // T
    hd = C_HEADS * HEAD_DIM
    wqT = _pair_padded_qT(w_in[:, :hd] * QK_SCALE).astype(BF16)
    wk = w_in[:, hd:2 * hd].astype(BF16)
    wvT = w_in[:, 2 * hd:].T.astype(BF16)
    once = pl.Buffered(1)
    return pl.pallas_call(
        _moba_inproj_kernel,
        grid=(bsz, n_blk),
        in_specs=[pl.BlockSpec((1, T, d), lambda b, i: (b, i, 0)),
                  pl.BlockSpec(wqT.shape, lambda b, i: (0, 0), pipeline_mode=once),
                  pl.BlockSpec(wk.shape, lambda b, i: (0, 0), pipeline_mode=once),
                  pl.BlockSpec(wvT.shape, lambda b, i: (0, 0), pipeline_mode=once)],
        out_specs=[pl.BlockSpec((1, C_HEADS * PAIR, T), lambda b, i: (b, 0, i)),
                   pl.BlockSpec((1, 1, T, hd), lambda b, i: (b, i, 0, 0)),
                   pl.BlockSpec((1, 1, hd, T), lambda b, i: (b, i, 0, 0))],
        out_shape=[jax.ShapeDtypeStruct((bsz, C_HEADS * PAIR, seq), BF16),
                   jax.ShapeDtypeStruct((bsz, n_blk, T, hd), BF16),
                   jax.ShapeDtypeStruct((bsz, n_blk, hd, T), BF16)],
        compiler_params=_cparams(("arbitrary", "arbitrary")),
        name="moba_inproj",
    )(x3, wqT, wk, wvT)


def _moba_kernel(qT_ref, k_ref, vT_ref, bias_ref, o_ref, kmean_ref, selb_ref, *st, n_sel):
    qi = pl.program_id(2)
    T = MOBA_T
    n_blk = k_ref.shape[1]
    states = [st[3 * hh:3 * hh + 3] for hh in range(MOBA_HB)]

    @pl.when(qi == 0)
    def _():
        for j in range(n_blk):
            kmean_ref[j:j + 1, :] = jnp.mean(k_ref[0, j].astype(F32), axis=0, keepdims=True)

    qTs = [qT_ref[0, hh * PAIR:(hh + 1) * PAIR, :] for hh in range(MOBA_HB)]
    pair = lambda hh: slice((hh // 2) * PAIR, (hh // 2 + 1) * PAIR)
    j_idx = lax.broadcasted_iota(I32, (n_blk, T), 0)
    for hh in range(MOBA_HB):
        gate = jnp.dot(kmean_ref[:, pair(hh)], qTs[hh].astype(F32), preferred_element_type=F32,
                       precision=lax.Precision.HIGHEST)
        gv = jnp.where(j_idx < qi, gate, -jnp.inf)
        rank = jnp.zeros((n_blk, T), I32)
        for jp in range(n_blk):
            row = gv[jp:jp + 1, :]
            rank = rank + jnp.where(row > gv, 1, jnp.where((row == gv) & (jp < j_idx), 1, 0))
        selb = jnp.where(j_idx < qi, jnp.where(rank < n_sel, 0.0, NEG),
                         jnp.where(j_idx == qi, 0.0, NEG)).astype(F32)
        for j in range(n_blk):
            selb_ref[hh, j] = selb[j:j + 1, :]
        m_ref, l_ref, acc_ref = states[hh]
        m_ref[...] = jnp.full(m_ref.shape, NEG, F32)
        l_ref[...] = jnp.zeros(l_ref.shape, F32)
        acc_ref[...] = jnp.zeros(acc_ref.shape, F32)

    heads = range(MOBA_HB)
    far_bias = [bias_ref[2, hh, 0:1, :] for hh in heads]

    def far_scores(kb):
        return [jnp.dot(k_ref[0, kb, :, pair(hh)], qTs[hh], preferred_element_type=F32) for hh in heads]

    def far_segs(kb):
        return [[(T, selb_ref[hh, kb] + far_bias[hh])] for hh in heads]

    def near_scores(kb):
        return [jnp.dot(k_ref[0, kb, :, pair(hh)], qTs[hh], preferred_element_type=F32) + bias_ref[qi - kb, hh]
                for hh in heads]

    def near_segs(kb):
        return [[(T, selb_ref[hh, kb])] for hh in heads]

    def values(kb):
        return [vT_ref[0, kb, hh * HEAD_DIM:(hh + 1) * HEAD_DIM, :] for hh in heads]

    n_far = jnp.maximum(qi - 1, 0)
    _flash_loop(0, n_far, far_scores, values, states, far_segs)
    _flash_loop(n_far, qi + 1, near_scores, values, states, near_segs)
    for hh in range(MOBA_HB):
        o_ref[0, hh * HEAD_DIM:(hh + 1) * HEAD_DIM, :] = _flash_finish(*[r[...] for r in states[hh]]).astype(o_ref.dtype)


def moba_bias_tiles(rel_bias):
    assert MOBA_T + 1 >= T5_FAR
    t0 = bias_tiles(rel_bias, [0], C_HEADS, 0, MOBA_T, MOBA_T, -1, 1, True)
    t12 = bias_tiles(rel_bias, [MOBA_T, 4 * MOBA_T], C_HEADS, 0, MOBA_T, MOBA_T, -1, 1, False)
    return jnp.concatenate([t0, t12], axis=0)


def moba_attention(qT, k, vT, bias3):
    T = MOBA_T
    bsz, n_blk = k.shape[0], k.shape[1]
    seq = n_blk * T
    n_sel = min(MOBA_TOPK, n_blk - 1)
    HB = MOBA_HB
    assert HB % 2 == 0
    out = pl.pallas_call(
        functools.partial(_moba_kernel, n_sel=n_sel),
        grid=(bsz, C_HEADS // HB, n_blk),
        in_specs=[pl.BlockSpec((1, HB * PAIR, T), lambda b, h, i: (b, h, i)),
                  pl.BlockSpec((1, n_blk, T, HB * HEAD_DIM), lambda b, h, i: (b, 0, 0, h)),
                  pl.BlockSpec((1, n_blk, HB * HEAD_DIM, T), lambda b, h, i: (b, 0, h, 0)),
                  pl.BlockSpec((3, HB, T, T), lambda b, h, i: (0, h, 0, 0))],
        out_specs=pl.BlockSpec((1, HB * HEAD_DIM, T), lambda b, h, i: (b, h, i)),
        out_shape=jax.ShapeDtypeStruct((bsz, C_HEADS * HEAD_DIM, seq), BF16),
        scratch_shapes=([pltpu.VMEM((n_blk, HB * HEAD_DIM), F32), pltpu.VMEM((HB, n_blk, 1, T), F32)]
                        + [pltpu.VMEM((1, T), F32), pltpu.VMEM((1, T), F32), pltpu.VMEM((HEAD_DIM, T), F32)] * HB),
        compiler_params=_cparams(("arbitrary", "arbitrary", "arbitrary")),
        name="moba_attention",
    )(qT, k, vT, bias3)
    return out


ROUTER_LANES = 128


def _projT_ln_kernel(*refs, n_in):
    aT_refs, w_refs = refs[:n_in], refs[n_in:2 * n_in]
    x_ref, g_ref, b_ref, o_ref = refs[2 * n_in:]
    tn = (((0,), (0,)), ((), ()))
    mix = lax.dot_general(aT_refs[0][0], w_refs[0][...], tn, preferred_element_type=F32)
    for aT_ref, w_ref in zip(aT_refs[1:], w_refs[1:]):
        mix = mix + lax.dot_general(aT_ref[0], w_ref[...], tn, preferred_element_type=F32)
    o_ref[0] = _layer_norm_rows(ALPHA * x_ref[0] + mix, g_ref[...], b_ref[...])


def projT_residual_ln(aTs, w, x3, g, b, tm=512):
    bsz, seq, d = x3.shape
    ws, k0 = [], 0
    for aT in aTs:
        ws.append(w[k0:k0 + aT.shape[1]])
        k0 += aT.shape[1]
    n_in = len(aTs)
    once = pl.Buffered(1)
    vec = pl.BlockSpec((1, d), lambda bb, i: (0, 0))
    return pl.pallas_call(
        functools.partial(_projT_ln_kernel, n_in=n_in),
        grid=(bsz, seq // tm),
        in_specs=([pl.BlockSpec((1, aT.shape[1], tm), lambda bb, i: (bb, 0, i)) for aT in aTs]
                  + [pl.BlockSpec(wi.shape, lambda bb, i: (0, 0), pipeline_mode=once) for wi in ws]
                  + [pl.BlockSpec((1, tm, d), lambda bb, i: (bb, i, 0)), vec, vec]),
        out_specs=pl.BlockSpec((1, tm, d), lambda bb, i: (bb, i, 0)),
        out_shape=jax.ShapeDtypeStruct((bsz, seq, d), F32),
        compiler_params=_cparams(("arbitrary", "arbitrary")),
        name="projT_residual_ln",
    )(*aTs, *ws, x3, g.reshape(1, d), b.reshape(1, d))


def _router_kernel(h_ref, w_ref, o_ref):
    o_ref[...] = jnp.dot(h_ref[...], w_ref[...], preferred_element_type=F32, precision=lax.Precision.HIGHEST)


def router_logits(h, router, tm=1024):
    m, d = h.shape
    w = jnp.pad(router, ((0, 0), (0, ROUTER_LANES - N_EXPERTS)))
    out = pl.pallas_call(
        _router_kernel,
        grid=(m // tm,),
        in_specs=[pl.BlockSpec((tm, d), lambda i: (i, 0)), pl.BlockSpec((d, ROUTER_LANES), lambda i: (0, 0))],
        out_specs=pl.BlockSpec((tm, ROUTER_LANES), lambda i: (i, 0)),
        out_shape=jax.ShapeDtypeStruct((m, ROUTER_LANES), F32),
        compiler_params=_cparams(("arbitrary",)),
        name="router_logits",
    )(h, w)
    return out[:, :N_EXPERTS]


IDX_LANES = 128


def _issue_row_gather(idx_vmem_ref, idx_smem, sem_i, src_hbm, dst_slot_ref, sem_slot, n_rows):
    cp = pltpu.make_async_copy(idx_vmem_ref.at[0], idx_smem, sem_i)
    cp.start()
    cp.wait()

    for r in range(n_rows):
        row = idx_smem[r // IDX_LANES, r % IDX_LANES]
        pltpu.make_async_copy(src_hbm.at[pl.ds(row, 1)], dst_slot_ref.at[pl.ds(r, 1)], sem_slot).start()


def _pipelined_gather(idx0_ref, idxn_ref, idx_smem, sem_i, src_hbm, buf, sem_buf, n_rows):
    g = pl.program_id(0)
    slot = lax.rem(g, 2)

    @pl.when(g == 0)
    def _():
        _issue_row_gather(idx0_ref, idx_smem, sem_i, src_hbm, buf.at[0], sem_buf.at[0], n_rows)

    @pl.when(g + 1 < pl.num_programs(0))
    def _():
        _issue_row_gather(idxn_ref, idx_smem, sem_i, src_hbm, buf.at[1 - slot], sem_buf.at[1 - slot], n_rows)

    pltpu.make_async_copy(buf.at[slot], buf.at[slot], sem_buf.at[slot]).wait()
    return slot


def _gather_specs(n_steps, k):
    first = lambda g, *_: (0, 0, 0)
    nxt = lambda g, *_: (jnp.minimum(g + 1, n_steps - 1), 0, 0)
    return pl.BlockSpec((1, k, IDX_LANES), first), pl.BlockSpec((1, k, IDX_LANES), nxt)


def _moe_ffn_kernel(ge_ref, idx0_ref, idxn_ref, h_hbm, w1_ref, w3_ref, w2_ref, o_ref,
                    xbuf, idx_smem, sem_i, sem_x, *, ff_chunk):
    del ge_ref
    g = pl.program_id(0)
    slot = lax.rem(g, 2)
    wait_slot = lambda s: pltpu.make_async_copy(xbuf.at[s], xbuf.at[s], sem_x.at[s]).wait()

    @pl.when(g == 0)
    def _():
        _issue_row_gather(idx0_ref, idx_smem, sem_i, h_hbm, xbuf.at[0], sem_x.at[0], EXPERT_ROWS)

    wait_slot(slot)
    cp = pltpu.make_async_copy(idxn_ref.at[0], idx_smem, sem_i)
    cp.start()
    xb = xbuf[slot].astype(BF16)
    d_ff = w1_ref.shape[2]
    n_chunks = d_ff // ff_chunk
    n_issue = max(n_chunks - 2, 1)
    per_chunk = -(-EXPERT_ROWS // n_issue)
    acc = jnp.zeros((EXPERT_ROWS, w2_ref.shape[2]), F32)
    for ci in range(n_chunks):
        c = ci * ff_chunk
        a = jnp.dot(xb, w1_ref[0, :, c:c + ff_chunk], preferred_element_type=F32)
        u = jnp.dot(xb, w3_ref[0, :, c:c + ff_chunk], preferred_element_type=F32)
        hid = (a * jax.nn.sigmoid(a) * u).astype(BF16)
        acc = acc + jnp.dot(hid, w2_ref[0, c:c + ff_chunk, :], preferred_element_type=F32)
        if ci == 0:
            cp.wait()
        for r in range(ci * per_chunk, min((ci + 1) * per_chunk, EXPERT_ROWS)):
            row = idx_smem[r // IDX_LANES, r % IDX_LANES]
            pltpu.make_async_copy(h_hbm.at[pl.ds(row, 1)], xbuf.at[1 - slot, pl.ds(r, 1)], sem_x.at[1 - slot]).start()
    o_ref[...] = acc

    @pl.when(g == pl.num_programs(0) - 1)
    def _():
        wait_slot(1 - slot)


def moe_expert_ffn(h, row_tok, grp_e, w1, w3, w2, ff_chunk=512):
    d = h.shape[1]
    d_ff = w1.shape[2]
    n_groups = grp_e.shape[0]
    k = EXPERT_ROWS // IDX_LANES
    idx = row_tok.reshape(n_groups, k, IDX_LANES)
    once = pl.Buffered(1)
    idx0_spec, idxn_spec = _gather_specs(n_groups, k)
    grid_spec = pltpu.PrefetchScalarGridSpec(
        num_scalar_prefetch=1,
        grid=(n_groups,),
        in_specs=[idx0_spec, idxn_spec, pl.BlockSpec(memory_space=pl.ANY),
                  pl.BlockSpec((1, d, d_ff), lambda g, ge: (ge[g], 0, 0), pipeline_mode=once),
                  pl.BlockSpec((1, d, d_ff), lambda g, ge: (ge[g], 0, 0), pipeline_mode=once),
                  pl.BlockSpec((1, d_ff, d), lambda g, ge: (ge[g], 0, 0), pipeline_mode=once)],
        out_specs=pl.BlockSpec((EXPERT_ROWS, d), lambda g, ge: (g, 0)),
        scratch_shapes=[pltpu.VMEM((2, EXPERT_ROWS, d), F32), pltpu.SMEM((k, IDX_LANES), I32),
                        pltpu.SemaphoreType.DMA(()), pltpu.SemaphoreType.DMA((2,))],
    )
    return pl.pallas_call(
        functools.partial(_moe_ffn_kernel, ff_chunk=ff_chunk),
        grid_spec=grid_spec,
        out_shape=jax.ShapeDtypeStruct((n_groups * EXPERT_ROWS, d), F32),
        compiler_params=_cparams(("arbitrary",)),
        name="moe_expert_ffn",
    )(grp_e, idx, idx, h, w1, w3, w2)


COMBINE_TM = 256


def _moe_combine_ln_kernel(idx0_ref, idxn_ref, y_hbm, h_ref, gate_ref, g_ref, b_ref, o_ref,
                           ybuf, idx_smem, sem_i, sem_y):
    tm = COMBINE_TM
    slot = _pipelined_gather(idx0_ref, idxn_ref, idx_smem, sem_i, y_hbm, ybuf, sem_y, TOP_K * tm)
    y = gate_ref[:, 0:1] * ybuf[slot, 0:tm, :]
    for j in range(1, TOP_K):
        y = y + gate_ref[:, j:j + 1] * ybuf[slot, j * tm:(j + 1) * tm, :]
    o_ref[...] = _layer_norm_rows(ALPHA * h_ref[...] + y, g_ref[...], b_ref[...])


def moe_combine_ln(h, y_rows, dest, gate, g, b):
    m, d = h.shape
    tm = COMBINE_TM
    n_tiles = m // tm
    k = TOP_K * tm // IDX_LANES
    idx = dest.reshape(n_tiles, tm, TOP_K).transpose(0, 2, 1).reshape(n_tiles, k, IDX_LANES)
    idx0_spec, idxn_spec = _gather_specs(n_tiles, k)
    row = pl.BlockSpec((tm, d), lambda i: (i, 0))
    vec = pl.BlockSpec((1, d), lambda i: (0, 0))
    return pl.pallas_call(
        _moe_combine_ln_kernel,
        grid=(n_tiles,),
        in_specs=[idx0_spec, idxn_spec, pl.BlockSpec(memory_space=pl.ANY), row,
                  pl.BlockSpec((tm, TOP_K), lambda i: (i, 0)), vec, vec],
        out_specs=row,
        out_shape=jax.ShapeDtypeStruct((m, d), F32),
        scratch_shapes=[pltpu.VMEM((2, TOP_K * tm, d), F32), pltpu.SMEM((k, IDX_LANES), I32),
                        pltpu.SemaphoreType.DMA(()), pltpu.SemaphoreType.DMA((2,))],
        compiler_params=_cparams(("arbitrary",)),
        name="moe_combine_ln",
    )(idx, idx, y_rows, h, gate, g.reshape(1, d), b.reshape(1, d))


def moe_dispatch_plan(logits):
    n_tok = logits.shape[0]
    top_val, top_e = lax.top_k(logits, TOP_K)
    gate = jax.nn.softmax(top_val, axis=-1)
    e_flat = top_e.reshape(-1)
    onehot = (e_flat[:, None] == jnp.arange(N_EXPERTS, dtype=e_flat.dtype)[None, :]).astype(I32)
    rank = jnp.take_along_axis(jnp.cumsum(onehot, axis=0) - onehot, e_flat[:, None], axis=1)[:, 0]
    counts = jnp.sum(onehot, axis=0)
    padded = (counts + EXPERT_ROWS - 1) // EXPERT_ROWS * EXPERT_ROWS
    pend = jnp.cumsum(padded)
    pstart = pend - padded
    dest = pstart[e_flat] + rank
    n_assign = n_tok * TOP_K
    n_rows = -(-n_assign // EXPERT_ROWS) * EXPERT_ROWS + N_EXPERTS * EXPERT_ROWS
    n_groups = n_rows // EXPERT_ROWS
    tok_flat = jnp.repeat(jnp.arange(n_tok, dtype=I32), TOP_K)
    row_tok = jnp.zeros((n_rows,), I32).at[dest].set(tok_flat)
    grp_e = jnp.minimum(jnp.searchsorted(pend, jnp.arange(n_groups, dtype=I32) * EXPERT_ROWS, side='right'),
                        N_EXPERTS - 1).astype(I32)
    return gate, dest.astype(I32), row_tok, grp_e


def kernel(x, rel_bias, e_w_in, e_q_norm, e_kv_norm, e_w_uq, e_w_uk, e_w_uv, e_w_qidx, e_pos_k, e_pos_v, e_ck1, e_ck2, e_cv1, e_cv2, e_w_out, e_ln1_g, e_ln1_b, e_ffn_w1, e_ffn_w3, e_ffn_w2, e_ln2_g, e_ln2_b, o_w_in, o_w_out, o_ln1_g, o_ln1_b, o_router, o_moe_w1, o_moe_w3, o_moe_w2, o_ln2_g, o_ln2_b):
    bsz, seq, d = x.shape
    m = bsz * seq
    xf = x.reshape(m, d)
    dsa_bias = dsa_bias_tiles(rel_bias)
    nsa_bc, nsa_toe_s, nsa_toe_w = nsa_bias_inputs(rel_bias, seq)
    moba_bias = moba_bias_tiles(rel_bias)
    gd = B_GROUPS * HEAD_DIM
    for layer in range(DEPTH):
        i = layer // 2
        if layer % 2 == 0:
            x3 = xf.reshape(bsz, seq, d)
            (qidxT, qlatT, sT, kidx, ckv, ckvT, qbT, kcmp, vcmp, kslc, kwin, vT) = even_inproj(
                x3, e_w_in[i], e_q_norm[i], e_kv_norm[i], e_w_uq[i], e_w_uk[i], e_w_qidx[i])
            o_aT = dsa_attention(qidxT, sT, qlatT, kidx, ckv, ckvT, e_w_uv[i], dsa_bias)
            kc = nsa_compress(kcmp.reshape(m, gd), e_pos_k[i], e_ck1[i], e_ck2[i], bsz, seq)
            vc = nsa_compress(vcmp.reshape(m, gd), e_pos_v[i], e_cv1[i], e_cv2[i], bsz, seq)
            o_bT = nsa_attention(qbT, kc, vc, kslc, kwin, vT, sT, nsa_bc, nsa_toe_s, nsa_toe_w)
            h = projT_residual_ln([o_aT, o_bT], e_w_out[i].astype(BF16), x3, e_ln1_g[i], e_ln1_b[i]).reshape(m, d)
            tm = 1024
            xf = swiglu_ffn(h, jnp.zeros((m // tm,), I32), e_ffn_w1[i][None].astype(BF16),
                            e_ffn_w3[i][None].astype(BF16), e_ffn_w2[i][None].astype(BF16),
                            e_ln2_g[i], e_ln2_b[i], with_ln=True, out_dtype=F32, tm=tm, ff_chunk=1408)
        else:
            x3 = xf.reshape(bsz, seq, d)
            o_cT = moba_attention(*moba_inproj(x3, o_w_in[i]), moba_bias)
            h = projT_residual_ln([o_cT], o_w_out[i].astype(BF16), x3, o_ln1_g[i], o_ln1_b[i]).reshape(m, d)
            gate, dest, row_tok, grp_e = moe_dispatch_plan(router_logits(h, o_router[i]))
            y_rows = moe_expert_ffn(h, row_tok, grp_e, o_moe_w1[i].astype(BF16), o_moe_w3[i].astype(BF16),
                                    o_moe_w2[i].astype(BF16))
            xf = moe_combine_ln(h, y_rows, dest, gate, o_ln2_g[i], o_ln2_b[i])
    return xf.reshape(bsz, seq, d)
```

```python
import functools
import math

import numpy as np
import jax
import jax.numpy as jnp
from jax import lax
from jax.experimental import pallas as pl
from jax.experimental.pallas import tpu as pltpu

F32 = jnp.float32
BF16 = jnp.bfloat16
I32 = jnp.int32
BF16_ROWS = 16

HEAD_DIM = 64
NUM_BUCKETS = 32
MAX_DISTANCE = 128
N_BIAS_HEADS = 16
A_HEADS = 8
A_Q_RANK = 256
A_KV_RANK = 128
IDX_HEADS = 16
IDX_DIM = 64
DSA_TOPK = 256
B_HEADS = 8
B_GROUPS = 2
B_HPG = B_HEADS // B_GROUPS
CMP_LEN = 32
CMP_STRIDE = 16
SLC_BLOCK = 64
SLC_TOPN = 16
WINDOW = 512
C_HEADS = 16
MOBA_BLOCK = 256
MOBA_TOPK = 3
N_EXPERTS = 8
TOP_K = 2
EXPERT_ROWS = 256
DEPTH = 2
ALPHA = (2 * DEPTH) ** 0.25

LOG2E = 1.4426950408889634
QK_SCALE = HEAD_DIM ** -0.5 * LOG2E
NEG = -1e30
NEG_HALF = -5e29
INT_MIN = -2 ** 31
VMEM_LIMIT = 56 * 1024 * 1024


def _t5_thresholds():
    def bucket(n):
        if n < NUM_BUCKETS // 2:
            return n
        v = np.log(np.float32(n) / np.float32(NUM_BUCKETS // 2)) / np.float32(math.log(MAX_DISTANCE / (NUM_BUCKETS // 2)))
        return min(NUM_BUCKETS // 2 + int(np.float32(v) * (NUM_BUCKETS - NUM_BUCKETS // 2)), NUM_BUCKETS - 1)
    b = [bucket(i) for i in range(4 * MAX_DISTANCE)]
    return [0] + [min(i for i in range(len(b)) if b[i] >= k) for k in range(1, NUM_BUCKETS)]


T5_THR = _t5_thresholds()
T5_FAR = T5_THR[-1]


def _cparams(sem):
    return pltpu.CompilerParams(dimension_semantics=sem, vmem_limit_bytes=VMEM_LIMIT)


def _bias_kernel(tab_ref, off_ref, o_ref, *, c_row, c_col, h0, causal_neg, window):
    v = pl.program_id(0)
    h = pl.program_id(1) + h0
    shape = o_ref.shape[2:]
    dist = (c_col * lax.broadcasted_iota(I32, shape, 1) + c_row * lax.broadcasted_iota(I32, shape, 0) + off_ref[v])
    n = jnp.maximum(dist, 0)
    acc = jnp.full(shape, tab_ref[h] * LOG2E, F32)
    for k in range(1, NUM_BUCKETS):
        acc = jnp.where(n >= T5_THR[k], tab_ref[k * N_BIAS_HEADS + h] * LOG2E, acc)
    if causal_neg:
        acc = jnp.where(dist >= 0, acc, NEG)
    if window:
        acc = jnp.where(dist < window, acc, NEG)
    o_ref[0, 0] = acc


def bias_tiles(rel_bias, offs, n_heads, h0, rows, cols, c_row, c_col, causal_neg, window=0):
    offs = jnp.asarray(offs, I32)
    nv = offs.shape[0]
    return pl.pallas_call(
        functools.partial(_bias_kernel, c_row=c_row, c_col=c_col, h0=h0, causal_neg=causal_neg, window=window),
        grid=(nv, n_heads),
        in_specs=[pl.BlockSpec(memory_space=pltpu.SMEM), pl.BlockSpec(memory_space=pltpu.SMEM)],
        out_specs=pl.BlockSpec((1, 1, rows, cols), lambda v, h: (v, h, 0, 0)),
        out_shape=jax.ShapeDtypeStruct((nv, n_heads, rows, cols), F32),
        compiler_params=_cparams(("arbitrary", "arbitrary")),
        name="t5_bias_tiles",
    )(rel_bias.reshape(-1), offs)


def _layer_norm_rows(z, g, b):
    mu = jnp.mean(z, axis=-1, keepdims=True)
    zc = z - mu
    var = jnp.mean(zc * zc, axis=-1, keepdims=True)
    return zc * lax.rsqrt(var + 1e-5) * g + b


def _ffn_kernel(ge_ref, x_ref, w1_ref, w3_ref, w2_ref, g_ref, b_ref, o_ref, *, ff_chunk, with_ln):
    del ge_ref
    x = x_ref[...]
    xb = x.astype(BF16)
    d_ff = w1_ref.shape[2]
    acc = jnp.zeros((x.shape[0], w2_ref.shape[2]), F32)
    for c in range(0, d_ff, ff_chunk):
        a = jnp.dot(xb, w1_ref[0, :, c:c + ff_chunk], preferred_element_type=F32)
        u = jnp.dot(xb, w3_ref[0, :, c:c + ff_chunk], preferred_element_type=F32)
        hid = (a * jax.nn.sigmoid(a) * u).astype(BF16)
        acc = acc + jnp.dot(hid, w2_ref[0, c:c + ff_chunk, :], preferred_element_type=F32)
    if with_ln:
        o_ref[...] = _layer_norm_rows(ALPHA * x.astype(F32) + acc, g_ref[...], b_ref[...]).astype(o_ref.dtype)
    else:
        o_ref[...] = acc.astype(o_ref.dtype)


def swiglu_ffn(x_rows, grp_e, w1, w3, w2, ln_g, ln_b, *, with_ln, out_dtype, tm, ff_chunk):
    m, d = x_rows.shape
    d_ff = w1.shape[2]
    once = pl.Buffered(1)
    grid_spec = pltpu.PrefetchScalarGridSpec(
        num_scalar_prefetch=1,
        grid=(m // tm,),
        in_specs=[pl.BlockSpec((tm, d), lambda i, ge: (i, 0)),
                  pl.BlockSpec((1, d, d_ff), lambda i, ge: (ge[i], 0, 0), pipeline_mode=once),
                  pl.BlockSpec((1, d, d_ff), lambda i, ge: (ge[i], 0, 0), pipeline_mode=once),
                  pl.BlockSpec((1, d_ff, d), lambda i, ge: (ge[i], 0, 0), pipeline_mode=once),
                  pl.BlockSpec((1, d), lambda i, ge: (0, 0)), pl.BlockSpec((1, d), lambda i, ge: (0, 0))],
        out_specs=pl.BlockSpec((tm, d), lambda i, ge: (i, 0)),
    )
    return pl.pallas_call(
        functools.partial(_ffn_kernel, ff_chunk=ff_chunk, with_ln=with_ln),
        grid_spec=grid_spec,
        out_shape=jax.ShapeDtypeStruct((m, d), out_dtype),
        compiler_params=_cparams(("arbitrary",)),
        name="swiglu_ffn",
    )(grp_e, x_rows, w1, w3, w2, ln_g.reshape(1, d), ln_b.reshape(1, d))


def _flash_probs(s, m, segs=None):
    if segs is None:
        m_new = jnp.maximum(m, jnp.max(s, axis=0, keepdims=True))
        p = jnp.exp2(s - m_new)
    else:
        m_new, r0 = m, 0
        for n, c in segs:
            seg_max = jnp.max(s[r0:r0 + n], axis=0, keepdims=True)
            m_new = jnp.maximum(m_new, jnp.where(c > NEG_HALF, seg_max + c, NEG))
            r0 += n
        parts, r0 = [], 0
        for n, c in segs:
            shift = jnp.where(c > NEG_HALF, m_new - c, -NEG)
            parts.append(jnp.exp2(s[r0:r0 + n] - shift))
            r0 += n
        p = parts[0] if len(parts) == 1 else jnp.concatenate(parts, axis=0)
    return m_new, jnp.exp2(m - m_new), p.astype(BF16)


def _flash_merge(states):
    ms = [st[0][...] for st in states]
    m = functools.reduce(jnp.maximum, ms)
    ws = [jnp.exp2(mi - m) for mi in ms]
    l = sum(w * st[1][...] for w, st in zip(ws, states))
    acc = sum(w * st[2][...] for w, st in zip(ws, states))
    return m, l, acc


def _flash_finish(m, l, acc):
    return jnp.where(m > NEG_HALF, acc / l, 0.0)


def _rms_rows(x, g):
    return x * lax.rsqrt(jnp.mean(x * x, axis=-1, keepdims=True) + 1e-6) * g


EVEN_T = 256
SMALL_ROWS = 48
GATE_ROW0 = IDX_HEADS
NT_DIMS = (((1,), (1,)), ((), ()))


def _even_inproj_kernel(x_ref, wa_ref, wsT_ref, qn_ref, kvn_ref, wuq_ref, wuk_ref, wqiT_ref, wqbT_ref, wk4_ref, wvT_ref,
                        qidxT_ref, qlatT_ref, sT_ref, kidx_ref, ckv_ref, ckvT_ref,
                        qbT_ref, kcmp_ref, vcmp_ref, kslc_ref, kwin_ref, vT_ref):
    xb = x_ref[0].astype(BF16)
    ya = jnp.dot(xb, wa_ref[...], preferred_element_type=F32)
    cqn = _rms_rows(ya[:, :A_Q_RANK], qn_ref[...]).astype(BF16)
    ckvn = _rms_rows(ya[:, A_Q_RANK:A_Q_RANK + A_KV_RANK], kvn_ref[...])
    kidx_ref[0, 0] = ya[:, A_Q_RANK + A_KV_RANK:A_Q_RANK + A_KV_RANK + IDX_DIM].astype(BF16)
    ckv_ref[0, 0] = ckvn.astype(BF16)
    ckvT_ref[0, 0] = ckvn.T.astype(BF16)
    sT_ref[0] = lax.dot_general(wsT_ref[...], xb, NT_DIMS, preferred_element_type=F32)
    q = jnp.dot(cqn, wuq_ref[...], preferred_element_type=F32).astype(BF16)
    for h in range(A_HEADS):
        qlT = lax.dot_general(wuk_ref[h], q[:, h * HEAD_DIM:(h + 1) * HEAD_DIM], NT_DIMS, preferred_element_type=F32)
        qlatT_ref[0, h * A_KV_RANK:(h + 1) * A_KV_RANK, :] = (qlT * QK_SCALE).astype(BF16)
    qidxT_ref[0] = lax.dot_general(wqiT_ref[...], cqn, NT_DIMS, preferred_element_type=F32).astype(BF16)
    qbT_ref[0] = lax.dot_general(wqbT_ref[...], xb, NT_DIMS, preferred_element_type=F32).astype(BF16)
    yk = jnp.dot(xb, wk4_ref[...], preferred_element_type=F32).astype(BF16)
    gd = B_GROUPS * HEAD_DIM
    for j, ref in enumerate((kcmp_ref, vcmp_ref, kslc_ref, kwin_ref)):
        ref[0] = yk[:, j * gd:(j + 1) * gd]
    vT = lax.dot_general(wvT_ref[...], xb, NT_DIMS, preferred_element_type=F32).astype(BF16)
    for j in range(EVEN_T // NSA_KT):
        vT_ref[0, j] = vT[:, j * NSA_KT:(j + 1) * NSA_KT]


def even_inproj(x3, w_in, q_norm, kv_norm, w_uq, w_uk, w_qidx):
    bsz, seq, d = x3.shape
    T = EVEN_T
    nq = seq // T
    gd = B_GROUPS * HEAD_DIM
    n_kt = seq // NSA_KT
    o_kidx = A_Q_RANK + A_KV_RANK
    o_widx = o_kidx + IDX_DIM
    o_qb = o_widx + IDX_HEADS
    o_kv = o_qb + B_HEADS * HEAD_DIM
    o_gate = o_kv + 6 * gd
    kv = lambda j: w_in[:, o_kv + j * gd:o_kv + (j + 1) * gd]
    wa = jnp.pad(w_in[:, :o_widx], ((0, 0), (0, 512 - o_widx))).astype(BF16)
    w_gate = w_in[:, o_gate:].reshape(d, B_GROUPS, B_HPG, 3).transpose(0, 1, 3, 2).reshape(d, 3 * B_HEADS)
    wsT = jnp.concatenate([w_in[:, o_widx:o_qb] * IDX_HEADS ** -0.5, w_gate,
                           jnp.zeros((d, SMALL_ROWS - IDX_HEADS - 3 * B_HEADS), w_in.dtype)], axis=1).T.astype(BF16)
    wuq = w_uq.reshape(A_Q_RANK, A_HEADS * HEAD_DIM).astype(BF16)
    wuk = jnp.transpose(w_uk, (1, 0, 2)).astype(BF16)
    wqiT = w_qidx.reshape(A_Q_RANK, IDX_HEADS * IDX_DIM).T.astype(BF16)
    wqbT = (w_in[:, o_qb:o_kv] * QK_SCALE).T.astype(BF16)
    wk4 = jnp.concatenate([kv(0), kv(1), kv(2), kv(4)], axis=1).astype(BF16)
    wvT = jnp.concatenate([kv(3), kv(5)], axis=1).T.astype(BF16)
    weights = (wa, wsT, q_norm.reshape(1, -1), kv_norm.reshape(1, -1), wuq, wuk, wqiT, wqbT, wk4, wvT)
    once = pl.Buffered(1)
    w_specs = [pl.BlockSpec(w.shape, (lambda b, i, n=w.ndim: (0,) * n), pipeline_mode=once) for w in weights]
    fm = lambda rows: pl.BlockSpec((1, rows, T), lambda b, i: (b, 0, i))
    tok = lambda cols: pl.BlockSpec((1, T, cols), lambda b, i: (b, i, 0))
    blk = lambda r, c: pl.BlockSpec((1, 1, r, c), lambda b, i: (b, i, 0, 0))
    sds = jax.ShapeDtypeStruct
    return pl.pallas_call(
        _even_inproj_kernel,
        grid=(bsz, nq),
        in_specs=[pl.BlockSpec((1, T, d), lambda b, i: (b, i, 0))] + w_specs,
        out_specs=[fm(IDX_HEADS * IDX_DIM), fm(A_HEADS * A_KV_RANK), fm(SMALL_ROWS),
                   blk(T, IDX_DIM), blk(T, A_KV_RANK), blk(A_KV_RANK, T),
                   fm(B_HEADS * HEAD_DIM), tok(gd), tok(gd), tok(gd), tok(gd),
                   pl.BlockSpec((1, T // NSA_KT, 2 * gd, NSA_KT), lambda b, i: (b, i, 0, 0))],
        out_shape=[sds((bsz, IDX_HEADS * IDX_DIM, seq), BF16), sds((bsz, A_HEADS * A_KV_RANK, seq), BF16),
                   sds((bsz, SMALL_ROWS, seq), F32),
                   sds((bsz, nq, T, IDX_DIM), BF16), sds((bsz, nq, T, A_KV_RANK), BF16), sds((bsz, nq, A_KV_RANK, T), BF16),
                   sds((bsz, B_HEADS * HEAD_DIM, seq), BF16),
                   sds((bsz, seq, gd), BF16), sds((bsz, seq, gd), BF16), sds((bsz, seq, gd), BF16), sds((bsz, seq, gd), BF16),
                   sds((bsz, n_kt, 2 * gd, NSA_KT), BF16)],
        compiler_params=_cparams(("arbitrary", "arbitrary")),
        name="even_inproj",
    )(x3, *weights)


DSA_T = 256
SUB = 128


def _dsa_kernel(qidx_ref, wT_ref, qlat_ref, kidx_ref, ckv_ref, ckvT_ref, bias_ref, wuvt_ref, o_ref,
                key_ref, selb0_ref, selb1_ref, *state_refs, n_keep):
    selb_refs = (selb0_ref, selb1_ref)
    m_refs, l_refs, acc_refs = (state_refs[0:A_HEADS], state_refs[A_HEADS:2 * A_HEADS], state_refs[2 * A_HEADS:])
    qi = pl.program_id(1)
    nkb = qi + 1
    T = DSA_T

    def score_block(kb, carry):
        for sub in range(T // SUB):
            k = kidx_ref[0, kb, sub * SUB:(sub + 1) * SUB, :]
            acc = jnp.zeros((SUB, T), F32)
            for h in range(IDX_HEADS):
                d = jnp.dot(k, qidx_ref[0, h * IDX_DIM:(h + 1) * IDX_DIM, :], preferred_element_type=F32)
                acc = acc + jnp.maximum(d, 0.0) * wT_ref[0, h:h + 1, :]
            bits = lax.bitcast_convert_type(acc, I32)
            key = bits ^ (lax.shift_right_arithmetic(bits, 31) & 0x7FFFFFFF)
            s_pos = kb * T + sub * SUB + lax.broadcasted_iota(I32, (SUB, T), 0)
            t_pos = qi * T + lax.broadcasted_iota(I32, (SUB, T), 1)
            key = jnp.where(s_pos <= t_pos, key, INT_MIN)
            key_ref[pl.ds(pl.multiple_of(kb * T + sub * SUB, SUB), SUB), :] = key
        return carry

    lax.fori_loop(0, nkb, score_block, 0)

    def count_ge(cand):
        def body(kb, cnt):
            blk = key_ref[pl.ds(pl.multiple_of(kb * T, T), T), :]
            ge = jnp.where(blk >= cand, 1, 0).astype(I32)
            return cnt + jnp.sum(ge.reshape(T // 8, 8, T), axis=0)
        cnt = lax.fori_loop(0, nkb, body, jnp.zeros((8, T), I32))
        return jnp.sum(cnt, axis=0, keepdims=True)

    def bit_step(i, u):
        cand_u = u | lax.shift_left(jnp.int32(1), 31 - i)
        cnt = count_ge(cand_u ^ INT_MIN)
        return jnp.where(cnt >= n_keep, cand_u, u)

    u = lax.fori_loop(0, 32, bit_step, jnp.zeros((1, T), I32))
    thr = jnp.maximum(u ^ INT_MIN, INT_MIN + 1)

    for h in range(A_HEADS):
        m_refs[h][...] = jnp.full(m_refs[h].shape, NEG, F32)
        l_refs[h][...] = jnp.zeros(l_refs[h].shape, F32)
        acc_refs[h][...] = jnp.zeros(acc_refs[h].shape, F32)

    states = list(zip(m_refs, l_refs, acc_refs))
    far_bias = [bias_ref[2, h, 0:1, :] for h in range(A_HEADS)]

    def masked_scores(kb, selb):
        selb[...] = jnp.where(key_ref[pl.ds(pl.multiple_of(kb * T, T), T), :] >= thr, 0.0, NEG)
        ckv = ckv_ref[0, kb]
        return [jnp.dot(ckv, qlat_ref[0, h * A_KV_RANK:(h + 1) * A_KV_RANK, :], preferred_element_type=F32) + selb[...]
                for h in range(A_HEADS)]

    def far_body(i, carry):
        tiles = []
        for u, selb in enumerate(selb_refs):
            kb_raw = len(selb_refs) * i + u
            live = kb_raw < n_far
            kb = jnp.minimum(kb_raw, n_far - 1)
            segs = [[(T, jnp.where(live, far_bias[h], NEG))] for h in range(A_HEADS)]
            tiles.append((list(zip(masked_scores(kb, selb), segs)), [ckvT_ref[0, kb]] * A_HEADS))
        for s_all, v_all in tiles:
            _flash_step(s_all, v_all, states)
        return carry

    def near_body(kb, carry):
        s_all = [s + bias_ref[qi - kb, h] for h, s in enumerate(masked_scores(kb, selb_refs[0]))]
        _flash_step(s_all, [ckvT_ref[0, kb]] * A_HEADS, states)
        return carry

    n_far = jnp.maximum(qi - 1, 0)
    lax.fori_loop(0, lax.div(n_far + (len(selb_refs) - 1), jnp.int32(len(selb_refs))), far_body, 0)
    lax.fori_loop(n_far, nkb, near_body, 0)

    for h in range(A_HEADS):
        o_lat = _flash_finish(m_refs[h][...], l_refs[h][...], acc_refs[h][...]).astype(BF16)
        o_ref[0, h * HEAD_DIM:(h + 1) * HEAD_DIM, :] = jnp.dot(
            wuvt_ref[h], o_lat, preferred_element_type=F32).astype(o_ref.dtype)


def dsa_attention(qidxT, sT, qlatT, kidx, ckv, ckvT, w_uv, bias3):
    T = DSA_T
    assert T == EVEN_T
    bsz, nq = kidx.shape[0], kidx.shape[1]
    seq = nq * T
    n_keep = min(DSA_TOPK, seq // 4)
    wuvt = jnp.transpose(w_uv, (1, 2, 0)).astype(BF16)
    return pl.pallas_call(
        functools.partial(_dsa_kernel, n_keep=n_keep),
        grid=(bsz, nq),
        in_specs=[pl.BlockSpec((1, IDX_HEADS * IDX_DIM, T), lambda b, i: (b, 0, i)),
                  pl.BlockSpec((1, SMALL_ROWS, T), lambda b, i: (b, 0, i)),
                  pl.BlockSpec((1, A_HEADS * A_KV_RANK, T), lambda b, i: (b, 0, i)),
                  pl.BlockSpec((1, nq, T, IDX_DIM), lambda b, i: (b, 0, 0, 0)),
                  pl.BlockSpec((1, nq, T, A_KV_RANK), lambda b, i: (b, 0, 0, 0)),
                  pl.BlockSpec((1, nq, A_KV_RANK, T), lambda b, i: (b, 0, 0, 0)),
                  pl.BlockSpec((3, A_HEADS, T, T), lambda b, i: (0, 0, 0, 0)),
                  pl.BlockSpec((A_HEADS, HEAD_DIM, A_KV_RANK), lambda b, i: (0, 0, 0))],
        out_specs=pl.BlockSpec((1, A_HEADS * HEAD_DIM, T), lambda b, i: (b, 0, i)),
        out_shape=jax.ShapeDtypeStruct((bsz, A_HEADS * HEAD_DIM, seq), BF16),
        scratch_shapes=([pltpu.VMEM((seq, T), I32), pltpu.VMEM((T, T), F32), pltpu.VMEM((T, T), F32)]
                        + [pltpu.VMEM((1, T), F32)] * (2 * A_HEADS)
                        + [pltpu.VMEM((A_KV_RANK, T), F32)] * A_HEADS),
        compiler_params=_cparams(("arbitrary", "arbitrary")),
        name="dsa_attention",
    )(qidxT, sT, qlatT, kidx, ckv, ckvT, bias3, wuvt)


def dsa_bias_tiles(rel_bias):
    assert DSA_T + 1 >= T5_FAR
    return bias_tiles(rel_bias, [0, DSA_T, 4 * DSA_T], A_HEADS, 0, DSA_T, DSA_T, -1, 1, False)


N_CMP_PAD = 256


def _compress_kernel(blk_ref, pos_ref, w1_ref, w2_ref, o_ref):
    x = (blk_ref[0].astype(F32) + pos_ref[...]).astype(BF16)
    hid = jax.nn.gelu(jnp.dot(x, w1_ref[...], preferred_element_type=F32))
    o_ref[0] = jnp.dot(hid.astype(BF16), w2_ref[...], preferred_element_type=F32).astype(o_ref.dtype)


def nsa_compress(a, pos, w1, w2, bsz, seq):
    n_chunk = seq // CMP_STRIDE
    assert CMP_LEN == 2 * CMP_STRIDE and n_chunk <= N_CMP_PAD
    width = CMP_STRIDE * HEAD_DIM
    chunks = a.reshape(bsz, n_chunk, CMP_STRIDE, B_GROUPS, HEAD_DIM).transpose(0, 3, 1, 2, 4)
    chunks = chunks.reshape(bsz * B_GROUPS, n_chunk, width)
    blocks = jnp.concatenate([chunks[:, :-1], chunks[:, 1:]], axis=-1)
    blocks = jnp.pad(blocks, ((0, 0), (0, N_CMP_PAD - (n_chunk - 1)), (0, 0)))
    out = pl.pallas_call(
        _compress_kernel,
        grid=(bsz * B_GROUPS,),
        in_specs=[pl.BlockSpec((1, N_CMP_PAD, 2 * width), lambda i: (i, 0, 0)),
                  pl.BlockSpec((1, 2 * width), lambda i: (0, 0)),
                  pl.BlockSpec((2 * width, HEAD_DIM), lambda i: (0, 0)),
                  pl.BlockSpec((HEAD_DIM, HEAD_DIM), lambda i: (0, 0))],
        out_specs=pl.BlockSpec((1, N_CMP_PAD, HEAD_DIM), lambda i: (i, 0, 0)),
        out_shape=jax.ShapeDtypeStruct((bsz * B_GROUPS, N_CMP_PAD, HEAD_DIM), BF16),
        compiler_params=_cparams(("arbitrary",)),
        name="nsa_compress",
    )(blocks, pos.reshape(1, 2 * width), w1.reshape(2 * width, HEAD_DIM).astype(BF16), w2.astype(BF16))
    return out.reshape(bsz, B_GROUPS, N_CMP_PAD, HEAD_DIM)


NSA_TQ = 128
NSA_L = B_HPG * NSA_TQ
NSA_KT = 128
NSA_SLC_REL = 3
NSA_WIN_REL = 5
NSA_FAR_SPLIT = 4


def _flash_step(s_all, vT_all, states):
    probs = []
    for item, (m_ref, _, _) in zip(s_all, states):
        s, segs = item if isinstance(item, tuple) else (item, None)
        m_new, alpha, p = _flash_probs(s, m_ref[...], segs)
        m_ref[...] = m_new
        probs.append((alpha, p))
    for (alpha, p), vT, (_, l_ref, acc_ref) in zip(probs, vT_all, states):
        d = vT.shape[0]
        ones = jnp.ones((BF16_ROWS, vT.shape[1]), BF16)
        pv = jnp.dot(jnp.concatenate([vT, ones], axis=0), p, preferred_element_type=F32)
        acc_ref[...] = alpha * acc_ref[...] + pv[:d]
        l_ref[...] = alpha * l_ref[...] + pv[d:d + 1]


def _flash_loop(lo, hi, scores, values, states, segs, unroll=2):
    def body(i, carry):
        tiles = []
        for u in range(unroll):
            j_raw = lo + unroll * i + u
            live = j_raw < hi
            j = jnp.minimum(j_raw, hi - 1)
            sg = [[(n, jnp.where(live, c, NEG)) for n, c in chain] for chain in segs(j)]
            tiles.append((list(zip(scores(j), sg)), values(j)))
        for s_all, v_all in tiles:
            _flash_step(s_all, v_all, states)
        return carry

    lax.fori_loop(0, lax.div(hi - lo + (unroll - 1), jnp.int32(unroll)), body, 0)


def _nsa_kernel(qT_ref, kc_ref, vcT_ref, biasc_ref, ovl_ref, ks_ref, kw_ref, vT_ref,
                toes_ref, toew_ref, sT_ref, o_ref, selb_ref, *st, n_cmp, n_sel, n_slc):
    qi = pl.program_id(1)
    TQ, L = NSA_TQ, NSA_L
    q0 = qi * TQ
    qTs, qTs_pad = [], []
    for g in range(B_GROUPS):
        q = jnp.concatenate([qT_ref[0, (g * B_HPG + n) * HEAD_DIM:(g * B_HPG + n + 1) * HEAD_DIM, :]
                             for n in range(B_HPG)], axis=1)
        parts = [jnp.zeros_like(q)] * B_GROUPS
        parts[g] = q
        qTs.append(q)
        qTs_pad.append(jnp.concatenate(parts, axis=0))
    t_lane = q0 + (lax.broadcasted_iota(I32, (1, L), 1) & (TQ - 1))

    o_cs = []
    for g in range(B_GROUPS):
        s = jnp.dot(kc_ref[0, g], qTs[g], preferred_element_type=F32) + biasc_ref[g, 0]
        i_idx = lax.broadcasted_iota(I32, (N_CMP_PAD, L), 0)
        valid = jnp.where(i_idx < n_cmp, i_idx * CMP_STRIDE + (CMP_LEN - 1), 2 ** 30) <= t_lane
        s = jnp.where(valid, s, NEG)
        m = jnp.max(s, axis=0, keepdims=True)
        p = jnp.where(valid, jnp.exp2(s - m), 0.0)
        l = jnp.sum(p, axis=0, keepdims=True)
        p_c = p / jnp.where(l > 0, l, 1.0)
        o_c = jnp.dot(vcT_ref[0, g], p_c.astype(BF16), preferred_element_type=F32)

        psum = p_c[:, 0:TQ]
        for n in range(1, B_HPG):
            psum = psum + p_c[:, n * TQ:(n + 1) * TQ]
        sc = jnp.dot(ovl_ref[...], psum, preferred_element_type=F32, precision=lax.Precision.HIGHEST)
        j_idx = lax.broadcasted_iota(I32, (n_slc, TQ), 0)
        cur = (q0 + lax.broadcasted_iota(I32, (1, TQ), 1)) // SLC_BLOCK
        adm = j_idx <= cur
        forced = (j_idx == 0) | (j_idx == cur) | (j_idx == cur - 1)
        scv = jnp.where(adm, jnp.where(forced, jnp.inf, sc), -jnp.inf)
        rank = jnp.zeros((n_slc, TQ), I32)
        for jp in range(n_slc):
            row = scv[jp:jp + 1, :]
            beats = jnp.where(row > scv, 1, jnp.where((row == scv) & (jp < j_idx), 1, 0))
            rank = rank + beats
        selb = jnp.where(rank < n_sel, 0.0, NEG).astype(F32)
        selb4 = jnp.concatenate([selb] * B_HPG, axis=1)
        for j in range(n_slc):
            selb_ref[g, j] = selb4[j:j + 1, :]
        o_cs.append(o_c)

    for ref in st[0::3]:
        ref[...] = jnp.full(ref.shape, NEG, F32)
    for ref in st[1::3] + st[2::3]:
        ref[...] = jnp.zeros(ref.shape, F32)
    slc_st = [st[6 * g:6 * g + 3] for g in range(B_GROUPS)]
    win_st = [st[6 * g + 3:6 * g + 6] for g in range(B_GROUPS)]
    n_main = 6 * B_GROUPS
    xtr_st = [[st[n_main + 3 * (g * (NSA_FAR_SPLIT - 1) + r):n_main + 3 * (g * (NSA_FAR_SPLIT - 1) + r) + 3]
               for r in range(NSA_FAR_SPLIT - 1)] for g in range(B_GROUPS)]
    per_kt = NSA_KT // SLC_BLOCK

    groups = range(B_GROUPS)
    far_bias = [toes_ref[g, NSA_SLC_REL - 1, 0:1, :] for g in groups]

    def slc_scores(g, jt, near):
        s = jnp.dot(ks_ref[0, jt], qTs_pad[g], preferred_element_type=F32)
        return s + toes_ref[g, jnp.minimum(qi - jt, NSA_SLC_REL - 1)] if near else s

    def slc_segs(g, jt, near):
        return [(SLC_BLOCK, selb_ref[g, per_kt * jt + r] + (0.0 if near else far_bias[g])) for r in range(per_kt)]

    def win_scores(g, jt):
        rel = jnp.minimum(qi - jt, NSA_WIN_REL - 1)
        return jnp.dot(kw_ref[0, jt], qTs_pad[g], preferred_element_type=F32) + toew_ref[g, rel]

    gd = B_GROUPS * HEAD_DIM
    v_slc = lambda g, jt: vT_ref[0, jt, g * HEAD_DIM:(g + 1) * HEAD_DIM, :]
    v_win = lambda g, jt: vT_ref[0, jt, gd + g * HEAD_DIM:gd + (g + 1) * HEAD_DIM, :]

    assert NSA_WIN_REL >= NSA_SLC_REL
    j_lo = jnp.maximum(qi - (NSA_WIN_REL - 1), 0)

    def far_body(i, carry):
        s_all, v_all, chains = [], [], []
        for r in range(NSA_FAR_SPLIT):
            jt_raw = NSA_FAR_SPLIT * i + r
            live = jt_raw < j_lo
            jt = jnp.minimum(jt_raw, j_lo - 1)
            for g in groups:
                segs = [(n, jnp.where(live, c, NEG)) for n, c in slc_segs(g, jt, False)]
                s_all.append((slc_scores(g, jt, False), segs))
                v_all.append(v_slc(g, jt))
                chains.append(slc_st[g] if r == 0 else xtr_st[g][r - 1])
        _flash_step(s_all, v_all, chains)
        return carry

    zero_row = jnp.zeros((1, L), F32)
    lax.fori_loop(0, lax.div(j_lo + (NSA_FAR_SPLIT - 1), jnp.int32(NSA_FAR_SPLIT)), far_body, 0)
    _flash_loop(j_lo, qi + 1,
                lambda jt: [slc_scores(g, jt, True) for g in groups] + [win_scores(g, jt) for g in groups],
                lambda jt: [v_slc(g, jt) for g in groups] + [v_win(g, jt) for g in groups],
                slc_st + win_st,
                lambda jt: [slc_segs(g, jt, True) for g in groups] + [[(NSA_KT, zero_row)] for g in groups])

    for g in range(B_GROUPS):
        o_s = _flash_finish(*_flash_merge([slc_st[g]] + xtr_st[g]))
        o_w = _flash_finish(*[r[...] for r in win_st[g]])
        row0 = GATE_ROW0 + g * 3 * B_HPG
        gate = [jax.nn.sigmoid(jnp.concatenate([sT_ref[0, row0 + j * B_HPG + n:row0 + j * B_HPG + n + 1, :]
                                                for n in range(B_HPG)], axis=1)) for j in range(3)]
        o = (gate[0] * o_cs[g] + gate[1] * o_s + gate[2] * o_w).astype(o_ref.dtype)
        for n in range(B_HPG):
            o_ref[0, (g * B_HPG + n) * HEAD_DIM:(g * B_HPG + n + 1) * HEAD_DIM, :] = o[:, n * TQ:(n + 1) * TQ]


def nsa_bias_inputs(rel_bias, seq):
    TQ, L, KT = NSA_TQ, NSA_L, NSA_KT
    nq = seq // TQ
    bc = bias_tiles(rel_bias, [-(CMP_LEN - 1)], B_HEADS, A_HEADS, N_CMP_PAD, seq, -CMP_STRIDE, 1, False, 0)
    bc = bc.reshape(B_GROUPS, B_HPG, N_CMP_PAD, nq, TQ).transpose(0, 3, 2, 1, 4).reshape(B_GROUPS, nq, N_CMP_PAD, L)

    def lanes(t):
        v = t.shape[0]
        return t.reshape(v, B_GROUPS, B_HPG, KT, TQ).transpose(1, 0, 3, 2, 4).reshape(B_GROUPS, v, KT, L)

    assert KT == TQ and (NSA_SLC_REL - 1) * KT - (KT - 1) >= T5_FAR
    toe_s = bias_tiles(rel_bias, [v * KT for v in range(NSA_SLC_REL - 1)] + [64 * KT],
                       B_HEADS, A_HEADS, KT, TQ, -1, 1, True, 0)
    assert (NSA_WIN_REL - 1) * KT - (KT - 1) < WINDOW <= NSA_WIN_REL * KT - (KT - 1)
    toe_w = bias_tiles(rel_bias, [v * KT for v in range(NSA_WIN_REL)], B_HEADS, A_HEADS, KT, TQ, -1, 1, True, WINDOW)
    return bc, lanes(toe_s), lanes(toe_w)


def nsa_overlap(seq):
    n_cmp = (seq - CMP_LEN) // CMP_STRIDE + 1
    n_slc = seq // SLC_BLOCK
    cs = np.arange(N_CMP_PAD) * CMP_STRIDE
    ss = np.arange(n_slc) * SLC_BLOCK
    ov = ((cs[None, :] + CMP_LEN - 1 >= ss[:, None]) & (cs[None, :] <= ss[:, None] + SLC_BLOCK - 1)
          & (np.arange(N_CMP_PAD)[None, :] < n_cmp))
    return jnp.asarray(ov.astype(np.float32))


def nsa_attention(qbT, kc, vc, kslc, kwin, vT, sT, biasc, toe_s, toe_w):
    TQ, L, KT, G = NSA_TQ, NSA_L, NSA_KT, B_GROUPS
    bsz, n_kt = vT.shape[0], vT.shape[1]
    seq = n_kt * KT
    nq = seq // TQ
    n_slc = seq // SLC_BLOCK
    n_cmp = (seq - CMP_LEN) // CMP_STRIDE + 1
    n_sel = min(SLC_TOPN, n_slc)
    gd = G * HEAD_DIM
    vcT = vc.transpose(0, 1, 3, 2)
    once = pl.Buffered(1)
    k_spec = pl.BlockSpec((1, n_kt, KT, gd), lambda b, i: (b, 0, 0, 0))
    n_chain = 2 * G + G * (NSA_FAR_SPLIT - 1)
    return pl.pallas_call(
        functools.partial(_nsa_kernel, n_cmp=n_cmp, n_sel=n_sel, n_slc=n_slc),
        grid=(bsz, nq),
        in_specs=[pl.BlockSpec((1, B_HEADS * HEAD_DIM, TQ), lambda b, i: (b, 0, i)),
                  pl.BlockSpec((1, G, N_CMP_PAD, HEAD_DIM), lambda b, i: (b, 0, 0, 0)),
                  pl.BlockSpec((1, G, HEAD_DIM, N_CMP_PAD), lambda b, i: (b, 0, 0, 0)),
                  pl.BlockSpec((G, 1, N_CMP_PAD, L), lambda b, i: (0, i, 0, 0)),
                  pl.BlockSpec((n_slc, N_CMP_PAD), lambda b, i: (0, 0), pipeline_mode=once),
                  k_spec, k_spec,
                  pl.BlockSpec((1, n_kt, 2 * gd, KT), lambda b, i: (b, 0, 0, 0)),
                  pl.BlockSpec((G, NSA_SLC_REL, KT, L), lambda b, i: (0, 0, 0, 0), pipeline_mode=once),
                  pl.BlockSpec((G, NSA_WIN_REL, KT, L), lambda b, i: (0, 0, 0, 0), pipeline_mode=once),
                  pl.BlockSpec((1, SMALL_ROWS, TQ), lambda b, i: (b, 0, i))],
        out_specs=pl.BlockSpec((1, B_HEADS * HEAD_DIM, TQ), lambda b, i: (b, 0, i)),
        out_shape=jax.ShapeDtypeStruct((bsz, B_HEADS * HEAD_DIM, seq), BF16),
        scratch_shapes=([pltpu.VMEM((G, n_slc, 1, L), F32)]
                        + [pltpu.VMEM((1, L), F32), pltpu.VMEM((1, L), F32), pltpu.VMEM((HEAD_DIM, L), F32)] * n_chain),
        compiler_params=_cparams(("arbitrary", "arbitrary")),
        name="nsa_attention",
    )(qbT, kc, vcT, biasc, nsa_overlap(seq), kslc.reshape(bsz, n_kt, KT, gd), kwin.reshape(bsz, n_kt, KT, gd),
      vT, toe_s, toe_w, sT)


MOBA_T = MOBA_BLOCK


MOBA_HB = 16


PAIR = 2 * HEAD_DIM


def _moba_inproj_kernel(x_ref, wqT_ref, wk_ref, wvT_ref, qT_ref, k_ref, vT_ref):
    xb = x_ref[0].astype(BF16)
    nt = (((1,), (1,)), ((), ()))
    qT_ref[0] = lax.dot_general(wqT_ref[...], xb, nt, preferred_element_type=F32).astype(BF16)
    k_ref[0, 0] = jnp.dot(xb, wk_ref[...], preferred_element_type=F32).astype(BF16)
    vT_ref[0, 0] = lax.dot_general(wvT_ref[...], xb, nt, preferred_element_type=F32).astype(BF16)


def _pair_padded_qT(wq):
    n_heads = wq.shape[1] // HEAD_DIM
    wT = wq.T.reshape(n_heads, HEAD_DIM, wq.shape[0])
    z = jnp.zeros_like(wT)
    even = jnp.concatenate([wT, z], axis=1)
    odd = jnp.concatenate([z, wT], axis=1)
    is_even = (jnp.arange(n_heads) % 2 == 0)[:, None, None]
    return jnp.where(is_even, even, odd).reshape(n_heads * PAIR, wq.shape[0])


def moba_inproj(x3, w_in):
    bsz, seq, d = x3.shape
    T = MOBA_T
    n_blk = seq // T
    hd = C_HEADS * HEAD_DIM
    wqT = _pair_padded_qT(w_in[:, :hd] * QK_SCALE).astype(BF16)
    wk = w_in[:, hd:2 * hd].astype(BF16)
    wvT = w_in[:, 2 * hd:].T.astype(BF16)
    once = pl.Buffered(1)
    return pl.pallas_call(
        _moba_inproj_kernel,
        grid=(bsz, n_blk),
        in_specs=[pl.BlockSpec((1, T, d), lambda b, i: (b, i, 0)),
                  pl.BlockSpec(wqT.shape, lambda b, i: (0, 0), pipeline_mode=once),
                  pl.BlockSpec(wk.shape, lambda b, i: (0, 0), pipeline_mode=once),
                  pl.BlockSpec(wvT.shape, lambda b, i: (0, 0), pipeline_mode=once)],
        out_specs=[pl.BlockSpec((1, C_HEADS * PAIR, T), lambda b, i: (b, 0, i)),
                   pl.BlockSpec((1, 1, T, hd), lambda b, i: (b, i, 0, 0)),
                   pl.BlockSpec((1, 1, hd, T), lambda b, i: (b, i, 0, 0))],
        out_shape=[jax.ShapeDtypeStruct((bsz, C_HEADS * PAIR, seq), BF16),
                   jax.ShapeDtypeStruct((bsz, n_blk, T, hd), BF16),
                   jax.ShapeDtypeStruct((bsz, n_blk, hd, T), BF16)],
        compiler_params=_cparams(("arbitrary", "arbitrary")),
        name="moba_inproj",
    )(x3, wqT, wk, wvT)


def _moba_kernel(qT_ref, k_ref, vT_ref, bias_ref, o_ref, kmean_ref, selb_ref, *st, n_sel):
    qi = pl.program_id(2)
    T = MOBA_T
    n_blk = k_ref.shape[1]
    states = [st[3 * hh:3 * hh + 3] for hh in range(MOBA_HB)]

    @pl.when(qi == 0)
    def _():
        for j in range(n_blk):
            kmean_ref[j:j + 1, :] = jnp.mean(k_ref[0, j].astype(F32), axis=0, keepdims=True)

    qTs = [qT_ref[0, hh * PAIR:(hh + 1) * PAIR, :] for hh in range(MOBA_HB)]
    pair = lambda hh: slice((hh // 2) * PAIR, (hh // 2 + 1) * PAIR)
    j_idx = lax.broadcasted_iota(I32, (n_blk, T), 0)
    for hh in range(MOBA_HB):
        gate = jnp.dot(kmean_ref[:, pair(hh)], qTs[hh].astype(F32), preferred_element_type=F32,
                       precision=lax.Precision.HIGHEST)
        gv = jnp.where(j_idx < qi, gate, -jnp.inf)
        rank = jnp.zeros((n_blk, T), I32)
        for jp in range(n_blk):
            row = gv[jp:jp + 1, :]
            rank = rank + jnp.where(row > gv, 1, jnp.where((row == gv) & (jp < j_idx), 1, 0))
        selb = jnp.where(j_idx < qi, jnp.where(rank < n_sel, 0.0, NEG),
                         jnp.where(j_idx == qi, 0.0, NEG)).astype(F32)
        for j in range(n_blk):
            selb_ref[hh, j] = selb[j:j + 1, :]
        m_ref, l_ref, acc_ref = states[hh]
        m_ref[...] = jnp.full(m_ref.shape, NEG, F32)
        l_ref[...] = jnp.zeros(l_ref.shape, F32)
        acc_ref[...] = jnp.zeros(acc_ref.shape, F32)

    heads = range(MOBA_HB)
    far_bias = [bias_ref[2, hh, 0:1, :] for hh in heads]

    def far_scores(kb):
        return [jnp.dot(k_ref[0, kb, :, pair(hh)], qTs[hh], preferred_element_type=F32) for hh in heads]

    def far_segs(kb):
        return [[(T, selb_ref[hh, kb] + far_bias[hh])] for hh in heads]

    def near_scores(kb):
        return [jnp.dot(k_ref[0, kb, :, pair(hh)], qTs[hh], preferred_element_type=F32) + bias_ref[qi - kb, hh]
                for hh in heads]

    def near_segs(kb):
        return [[(T, selb_ref[hh, kb])] for hh in heads]

    def values(kb):
        return [vT_ref[0, kb, hh * HEAD_DIM:(hh + 1) * HEAD_DIM, :] for hh in heads]

    n_far = jnp.maximum(qi - 1, 0)
    _flash_loop(0, n_far, far_scores, values, states, far_segs)
    _flash_loop(n_far, qi + 1, near_scores, values, states, near_segs)
    for hh in range(MOBA_HB):
        o_ref[0, hh * HEAD_DIM:(hh + 1) * HEAD_DIM, :] = _flash_finish(*[r[...] for r in states[hh]]).astype(o_ref.dtype)


def moba_bias_tiles(rel_bias):
    assert MOBA_T + 1 >= T5_FAR
    t0 = bias_tiles(rel_bias, [0], C_HEADS, 0, MOBA_T, MOBA_T, -1, 1, True)
    t12 = bias_tiles(rel_bias, [MOBA_T, 4 * MOBA_T], C_HEADS, 0, MOBA_T, MOBA_T, -1, 1, False)
    return jnp.concatenate([t0, t12], axis=0)


def moba_attention(qT, k, vT, bias3):
    T = MOBA_T
    bsz, n_blk = k.shape[0], k.shape[1]
    seq = n_blk * T
    n_sel = min(MOBA_TOPK, n_blk - 1)
    HB = MOBA_HB
    assert HB % 2 == 0
    once = pl.Buffered(1)
    out = pl.pallas_call(
        functools.partial(_moba_kernel, n_sel=n_sel),
        grid=(bsz, C_HEADS // HB, n_blk),
        in_specs=[pl.BlockSpec((1, HB * PAIR, T), lambda b, h, i: (b, h, i)),
                  pl.BlockSpec((1, n_blk, T, HB * HEAD_DIM), lambda b, h, i: (b, 0, 0, h), pipeline_mode=once),
                  pl.BlockSpec((1, n_blk, HB * HEAD_DIM, T), lambda b, h, i: (b, 0, h, 0), pipeline_mode=once),
                  pl.BlockSpec((3, HB, T, T), lambda b, h, i: (0, h, 0, 0), pipeline_mode=once)],
        out_specs=pl.BlockSpec((1, HB * HEAD_DIM, T), lambda b, h, i: (b, h, i)),
        out_shape=jax.ShapeDtypeStruct((bsz, C_HEADS * HEAD_DIM, seq), BF16),
        scratch_shapes=([pltpu.VMEM((n_blk, HB * HEAD_DIM), F32), pltpu.VMEM((HB, n_blk, 1, T), F32)]
                        + [pltpu.VMEM((1, T), F32), pltpu.VMEM((1, T), F32), pltpu.VMEM((HEAD_DIM, T), F32)] * HB),
        compiler_params=_cparams(("arbitrary", "arbitrary", "arbitrary")),
        name="moba_attention",
    )(qT, k, vT, bias3)
    return out


ROUTER_LANES = 128


def _projT_ln_kernel(*refs, n_in):
    aT_refs, w_refs = refs[:n_in], refs[n_in:2 * n_in]
    x_ref, g_ref, b_ref, o_ref = refs[2 * n_in:]
    tn = (((0,), (0,)), ((), ()))
    mix = lax.dot_general(aT_refs[0][0], w_refs[0][...], tn, preferred_element_type=F32)
    for aT_ref, w_ref in zip(aT_refs[1:], w_refs[1:]):
        mix = mix + lax.dot_general(aT_ref[0], w_ref[...], tn, preferred_element_type=F32)
    o_ref[0] = _layer_norm_rows(ALPHA * x_ref[0] + mix, g_ref[...], b_ref[...])


def projT_residual_ln(aTs, w, x3, g, b, tm=512):
    bsz, seq, d = x3.shape
    ws, k0 = [], 0
    for aT in aTs:
        ws.append(w[k0:k0 + aT.shape[1]])
        k0 += aT.shape[1]
    n_in = len(aTs)
    once = pl.Buffered(1)
    vec = pl.BlockSpec((1, d), lambda bb, i: (0, 0))
    return pl.pallas_call(
        functools.partial(_projT_ln_kernel, n_in=n_in),
        grid=(bsz, seq // tm),
        in_specs=([pl.BlockSpec((1, aT.shape[1], tm), lambda bb, i: (bb, 0, i)) for aT in aTs]
                  + [pl.BlockSpec(wi.shape, lambda bb, i: (0, 0), pipeline_mode=once) for wi in ws]
                  + [pl.BlockSpec((1, tm, d), lambda bb, i: (bb, i, 0)), vec, vec]),
        out_specs=pl.BlockSpec((1, tm, d), lambda bb, i: (bb, i, 0)),
        out_shape=jax.ShapeDtypeStruct((bsz, seq, d), F32),
        compiler_params=_cparams(("arbitrary", "arbitrary")),
        name="projT_residual_ln",
    )(*aTs, *ws, x3, g.reshape(1, d), b.reshape(1, d))


def _router_kernel(h_ref, w_ref, o_ref):
    o_ref[...] = jnp.dot(h_ref[...], w_ref[...], preferred_element_type=F32, precision=lax.Precision.HIGHEST)


def router_logits(h, router, tm=1024):
    m, d = h.shape
    w = jnp.pad(router, ((0, 0), (0, ROUTER_LANES - N_EXPERTS)))
    out = pl.pallas_call(
        _router_kernel,
        grid=(m // tm,),
        in_specs=[pl.BlockSpec((tm, d), lambda i: (i, 0)), pl.BlockSpec((d, ROUTER_LANES), lambda i: (0, 0))],
        out_specs=pl.BlockSpec((tm, ROUTER_LANES), lambda i: (i, 0)),
        out_shape=jax.ShapeDtypeStruct((m, ROUTER_LANES), F32),
        compiler_params=_cparams(("arbitrary",)),
        name="router_logits",
    )(h, w)
    return out[:, :N_EXPERTS]


IDX_LANES = 128


def _issue_row_gather(idx_vmem_ref, idx_smem, sem_i, src_hbm, dst_slot_ref, sem_slot, n_rows):
    cp = pltpu.make_async_copy(idx_vmem_ref.at[0], idx_smem, sem_i)
    cp.start()
    cp.wait()

    for r in range(n_rows):
        row = idx_smem[r // IDX_LANES, r % IDX_LANES]
        pltpu.make_async_copy(src_hbm.at[pl.ds(row, 1)], dst_slot_ref.at[pl.ds(r, 1)], sem_slot).start()


def _pipelined_gather(idx0_ref, idxn_ref, idx_smem, sem_i, src_hbm, buf, sem_buf, n_rows):
    g = pl.program_id(0)
    slot = lax.rem(g, 2)

    @pl.when(g == 0)
    def _():
        _issue_row_gather(idx0_ref, idx_smem, sem_i, src_hbm, buf.at[0], sem_buf.at[0], n_rows)

    @pl.when(g + 1 < pl.num_programs(0))
    def _():
        _issue_row_gather(idxn_ref, idx_smem, sem_i, src_hbm, buf.at[1 - slot], sem_buf.at[1 - slot], n_rows)

    pltpu.make_async_copy(buf.at[slot], buf.at[slot], sem_buf.at[slot]).wait()
    return slot


def _gather_specs(n_steps, k):
    first = lambda g, *_: (0, 0, 0)
    nxt = lambda g, *_: (jnp.minimum(g + 1, n_steps - 1), 0, 0)
    return pl.BlockSpec((1, k, IDX_LANES), first), pl.BlockSpec((1, k, IDX_LANES), nxt)


def _moe_ffn_kernel(ge_ref, idx0_ref, idxn_ref, h_hbm, w1_ref, w3_ref, w2_ref, o_ref,
                    xbuf, idx_smem, sem_i, sem_x, *, ff_chunk):
    del ge_ref
    g = pl.program_id(0)
    slot = lax.rem(g, 2)
    wait_slot = lambda s: pltpu.make_async_copy(xbuf.at[s], xbuf.at[s], sem_x.at[s]).wait()

    @pl.when(g == 0)
    def _():
        _issue_row_gather(idx0_ref, idx_smem, sem_i, h_hbm, xbuf.at[0], sem_x.at[0], EXPERT_ROWS)

    wait_slot(slot)
    cp = pltpu.make_async_copy(idxn_ref.at[0], idx_smem, sem_i)
    cp.start()
    xb = xbuf[slot].astype(BF16)
    d_ff = w1_ref.shape[2]
    n_chunks = d_ff // ff_chunk
    n_issue = max(n_chunks - 2, 1)
    per_chunk = -(-EXPERT_ROWS // n_issue)
    acc = jnp.zeros((EXPERT_ROWS, w2_ref.shape[2]), F32)
    for ci in range(n_chunks):
        c = ci * ff_chunk
        a = jnp.dot(xb, w1_ref[0, :, c:c + ff_chunk], preferred_element_type=F32)
        u = jnp.dot(xb, w3_ref[0, :, c:c + ff_chunk], preferred_element_type=F32)
        hid = (a * jax.nn.sigmoid(a) * u).astype(BF16)
        acc = acc + jnp.dot(hid, w2_ref[0, c:c + ff_chunk, :], preferred_element_type=F32)
        if ci == 0:
            cp.wait()
        for r in range(ci * per_chunk, min((ci + 1) * per_chunk, EXPERT_ROWS)):
            row = idx_smem[r // IDX_LANES, r % IDX_LANES]
            pltpu.make_async_copy(h_hbm.at[pl.ds(row, 1)], xbuf.at[1 - slot, pl.ds(r, 1)], sem_x.at[1 - slot]).start()
    o_ref[...] = acc

    @pl.when(g == pl.num_programs(0) - 1)
    def _():
        wait_slot(1 - slot)


def moe_expert_ffn(h, row_tok, grp_e, w1, w3, w2, ff_chunk=512):
    d = h.shape[1]
    d_ff = w1.shape[2]
    n_groups = grp_e.shape[0]
    k = EXPERT_ROWS // IDX_LANES
    idx = row_tok.reshape(n_groups, k, IDX_LANES)
    once = pl.Buffered(1)
    idx0_spec, idxn_spec = _gather_specs(n_groups, k)
    grid_spec = pltpu.PrefetchScalarGridSpec(
        num_scalar_prefetch=1,
        grid=(n_groups,),
        in_specs=[idx0_spec, idxn_spec, pl.BlockSpec(memory_space=pl.ANY),
                  pl.BlockSpec((1, d, d_ff), lambda g, ge: (ge[g], 0, 0), pipeline_mode=once),
                  pl.BlockSpec((1, d, d_ff), lambda g, ge: (ge[g], 0, 0), pipeline_mode=once),
                  pl.BlockSpec((1, d_ff, d), lambda g, ge: (ge[g], 0, 0), pipeline_mode=once)],
        out_specs=pl.BlockSpec((EXPERT_ROWS, d), lambda g, ge: (g, 0)),
        scratch_shapes=[pltpu.VMEM((2, EXPERT_ROWS, d), F32), pltpu.SMEM((k, IDX_LANES), I32),
                        pltpu.SemaphoreType.DMA(()), pltpu.SemaphoreType.DMA((2,))],
    )
    return pl.pallas_call(
        functools.partial(_moe_ffn_kernel, ff_chunk=ff_chunk),
        grid_spec=grid_spec,
        out_shape=jax.ShapeDtypeStruct((n_groups * EXPERT_ROWS, d), F32),
        compiler_params=_cparams(("arbitrary",)),
        name="moe_expert_ffn",
    )(grp_e, idx, idx, h, w1, w3, w2)


COMBINE_TM = 256


def _moe_combine_ln_kernel(idx0_ref, idxn_ref, y_hbm, h_ref, gate_ref, g_ref, b_ref, o_ref,
                           ybuf, idx_smem, sem_i, sem_y):
    tm = COMBINE_TM
    slot = _pipelined_gather(idx0_ref, idxn_ref, idx_smem, sem_i, y_hbm, ybuf, sem_y, TOP_K * tm)
    y = gate_ref[:, 0:1] * ybuf[slot, 0:tm, :]
    for j in range(1, TOP_K):
        y = y + gate_ref[:, j:j + 1] * ybuf[slot, j * tm:(j + 1) * tm, :]
    o_ref[...] = _layer_norm_rows(ALPHA * h_ref[...] + y, g_ref[...], b_ref[...])


def moe_combine_ln(h, y_rows, dest, gate, g, b):
    m, d = h.shape
    tm = COMBINE_TM
    n_tiles = m // tm
    k = TOP_K * tm // IDX_LANES
    idx = dest.reshape(n_tiles, tm, TOP_K).transpose(0, 2, 1).reshape(n_tiles, k, IDX_LANES)
    idx0_spec, idxn_spec = _gather_specs(n_tiles, k)
    row = pl.BlockSpec((tm, d), lambda i: (i, 0))
    vec = pl.BlockSpec((1, d), lambda i: (0, 0))
    return pl.pallas_call(
        _moe_combine_ln_kernel,
        grid=(n_tiles,),
        in_specs=[idx0_spec, idxn_spec, pl.BlockSpec(memory_space=pl.ANY), row,
                  pl.BlockSpec((tm, TOP_K), lambda i: (i, 0)), vec, vec],
        out_specs=row,
        out_shape=jax.ShapeDtypeStruct((m, d), F32),
        scratch_shapes=[pltpu.VMEM((2, TOP_K * tm, d), F32), pltpu.SMEM((k, IDX_LANES), I32),
                        pltpu.SemaphoreType.DMA(()), pltpu.SemaphoreType.DMA((2,))],
        compiler_params=_cparams(("arbitrary",)),
        name="moe_combine_ln",
    )(idx, idx, y_rows, h, gate, g.reshape(1, d), b.reshape(1, d))


def moe_dispatch_plan(logits):
    n_tok = logits.shape[0]
    top_val, top_e = lax.top_k(logits, TOP_K)
    gate = jax.nn.softmax(top_val, axis=-1)
    e_flat = top_e.reshape(-1)
    onehot = (e_flat[:, None] == jnp.arange(N_EXPERTS, dtype=e_flat.dtype)[None, :]).astype(I32)
    rank = jnp.take_along_axis(jnp.cumsum(onehot, axis=0) - onehot, e_flat[:, None], axis=1)[:, 0]
    counts = jnp.sum(onehot, axis=0)
    padded = (counts + EXPERT_ROWS - 1) // EXPERT_ROWS * EXPERT_ROWS
    pend = jnp.cumsum(padded)
    pstart = pend - padded
    dest = pstart[e_flat] + rank
    n_assign = n_tok * TOP_K
    n_rows = -(-n_assign // EXPERT_ROWS) * EXPERT_ROWS + N_EXPERTS * EXPERT_ROWS
    n_groups = n_rows // EXPERT_ROWS
    tok_flat = jnp.repeat(jnp.arange(n_tok, dtype=I32), TOP_K)
    row_tok = jnp.zeros((n_rows,), I32).at[dest].set(tok_flat)
    grp_e = jnp.minimum(jnp.searchsorted(pend, jnp.arange(n_groups, dtype=I32) * EXPERT_ROWS, side='right'),
                        N_EXPERTS - 1).astype(I32)
    return gate, dest.astype(I32), row_tok, grp_e


def kernel(x, rel_bias, e_w_in, e_q_norm, e_kv_norm, e_w_uq, e_w_uk, e_w_uv, e_w_qidx, e_pos_k, e_pos_v, e_ck1, e_ck2, e_cv1, e_cv2, e_w_out, e_ln1_g, e_ln1_b, e_ffn_w1, e_ffn_w3, e_ffn_w2, e_ln2_g, e_ln2_b, o_w_in, o_w_out, o_ln1_g, o_ln1_b, o_router, o_moe_w1, o_moe_w3, o_moe_w2, o_ln2_g, o_ln2_b):
    bsz, seq, d = x.shape
    m = bsz * seq
    xf = x.reshape(m, d)
    dsa_bias = dsa_bias_tiles(rel_bias)
    nsa_bc, nsa_toe_s, nsa_toe_w = nsa_bias_inputs(rel_bias, seq)
    moba_bias = moba_bias_tiles(rel_bias)
    gd = B_GROUPS * HEAD_DIM
    for layer in range(DEPTH):
        i = layer // 2
        if layer % 2 == 0:
            x3 = xf.reshape(bsz, seq, d)
            (qidxT, qlatT, sT, kidx, ckv, ckvT, qbT, kcmp, vcmp, kslc, kwin, vT) = even_inproj(
                x3, e_w_in[i], e_q_norm[i], e_kv_norm[i], e_w_uq[i], e_w_uk[i], e_w_qidx[i])
            o_aT = dsa_attention(qidxT, sT, qlatT, kidx, ckv, ckvT, e_w_uv[i], dsa_bias)
            kc = nsa_compress(kcmp.reshape(m, gd), e_pos_k[i], e_ck1[i], e_ck2[i], bsz, seq)
            vc = nsa_compress(vcmp.reshape(m, gd), e_pos_v[i], e_cv1[i], e_cv2[i], bsz, seq)
            o_bT = nsa_attention(qbT, kc, vc, kslc, kwin, vT, sT, nsa_bc, nsa_toe_s, nsa_toe_w)
            h = projT_residual_ln([o_aT, o_bT], e_w_out[i].astype(BF16), x3, e_ln1_g[i], e_ln1_b[i]).reshape(m, d)
            tm = 1024
            xf = swiglu_ffn(h, jnp.zeros((m // tm,), I32), e_ffn_w1[i][None].astype(BF16),
                            e_ffn_w3[i][None].astype(BF16), e_ffn_w2[i][None].astype(BF16),
                            e_ln2_g[i], e_ln2_b[i], with_ln=True, out_dtype=F32, tm=tm, ff_chunk=1408)
        else:
            x3 = xf.reshape(bsz, seq, d)
            o_cT = moba_attention(*moba_inproj(x3, o_w_in[i]), moba_bias)
            h = projT_residual_ln([o_cT], o_w_out[i].astype(BF16), x3, o_ln1_g[i], o_ln1_b[i]).reshape(m, d)
            gate, dest, row_tok, grp_e = moe_dispatch_plan(router_logits(h, o_router[i]))
            y_rows = moe_expert_ffn(h, row_tok, grp_e, o_moe_w1[i].astype(BF16), o_moe_w3[i].astype(BF16),
                                    o_moe_w2[i].astype(BF16))
            xf = moe_combine_ln(h, y_rows, dest, gate, o_ln2_g[i], o_ln2_b[i])
    return xf.reshape(bsz, seq, d)
```

```python
import functools
import math

import numpy as np
import jax
import jax.numpy as jnp
from jax import lax
from jax.experimental import pallas as pl
from jax.experimental.pallas import tpu as pltpu

F32 = jnp.float32
BF16 = jnp.bfloat16
I32 = jnp.int32
BF16_ROWS = 16

HEAD_DIM = 64
NUM_BUCKETS = 32
MAX_DISTANCE = 128
N_BIAS_HEADS = 16
A_HEADS = 8
A_Q_RANK = 256
A_KV_RANK = 128
IDX_HEADS = 16
IDX_DIM = 64
DSA_TOPK = 256
B_HEADS = 8
B_GROUPS = 2
B_HPG = B_HEADS // B_GROUPS
CMP_LEN = 32
CMP_STRIDE = 16
SLC_BLOCK = 64
SLC_TOPN = 16
WINDOW = 512
C_HEADS = 16
MOBA_BLOCK = 256
MOBA_TOPK = 3
N_EXPERTS = 8
TOP_K = 2
EXPERT_ROWS = 256
DEPTH = 2
ALPHA = (2 * DEPTH) ** 0.25

LOG2E = 1.4426950408889634
QK_SCALE = HEAD_DIM ** -0.5 * LOG2E
NEG = -1e30
NEG_HALF = -5e29
INT_MIN = -2 ** 31
VMEM_LIMIT = 56 * 1024 * 1024


def _t5_thresholds():
    def bucket(n):
        if n < NUM_BUCKETS // 2:
            return n
        v = np.log(np.float32(n) / np.float32(NUM_BUCKETS // 2)) / np.float32(math.log(MAX_DISTANCE / (NUM_BUCKETS // 2)))
        return min(NUM_BUCKETS // 2 + int(np.float32(v) * (NUM_BUCKETS - NUM_BUCKETS // 2)), NUM_BUCKETS - 1)
    b = [bucket(i) for i in range(4 * MAX_DISTANCE)]
    return [0] + [min(i for i in range(len(b)) if b[i] >= k) for k in range(1, NUM_BUCKETS)]


T5_THR = _t5_thresholds()
T5_FAR = T5_THR[-1]


def _cparams(sem):
    return pltpu.CompilerParams(dimension_semantics=sem, vmem_limit_bytes=VMEM_LIMIT)


def _bias_kernel(tab_ref, off_ref, o_ref, *, c_row, c_col, h0, causal_neg, window):
    v = pl.program_id(0)
    h = pl.program_id(1) + h0
    shape = o_ref.shape[2:]
    dist = (c_col * lax.broadcasted_iota(I32, shape, 1) + c_row * lax.broadcasted_iota(I32, shape, 0) + off_ref[v])
    n = jnp.maximum(dist, 0)
    acc = jnp.full(shape, tab_ref[h] * LOG2E, F32)
    for k in range(1, NUM_BUCKETS):
        acc = jnp.where(n >= T5_THR[k], tab_ref[k * N_BIAS_HEADS + h] * LOG2E, acc)
    if causal_neg:
        acc = jnp.where(dist >= 0, acc, NEG)
    if window:
        acc = jnp.where(dist < window, acc, NEG)
    o_ref[0, 0] = acc


def bias_tiles(rel_bias, offs, n_heads, h0, rows, cols, c_row, c_col, causal_neg, window=0):
    offs = jnp.asarray(offs, I32)
    nv = offs.shape[0]
    return pl.pallas_call(
        functools.partial(_bias_kernel, c_row=c_row, c_col=c_col, h0=h0, causal_neg=causal_neg, window=window),
        grid=(nv, n_heads),
        in_specs=[pl.BlockSpec(memory_space=pltpu.SMEM), pl.BlockSpec(memory_space=pltpu.SMEM)],
        out_specs=pl.BlockSpec((1, 1, rows, cols), lambda v, h: (v, h, 0, 0)),
        out_shape=jax.ShapeDtypeStruct((nv, n_heads, rows, cols), F32),
        compiler_params=_cparams(("arbitrary", "arbitrary")),
        name="t5_bias_tiles",
    )(rel_bias.reshape(-1), offs)


def _layer_norm_rows(z, g, b):
    mu = jnp.mean(z, axis=-1, keepdims=True)
    zc = z - mu
    var = jnp.mean(zc * zc, axis=-1, keepdims=True)
    return zc * lax.rsqrt(var + 1e-5) * g + b


def _ffn_kernel(ge_ref, x_ref, w1_ref, w3_ref, w2_ref, g_ref, b_ref, o_ref, *, ff_chunk, with_ln):
    del ge_ref
    x = x_ref[...]
    xb = x.astype(BF16)
    d_ff = w1_ref.shape[2]
    acc = jnp.zeros((x.shape[0], w2_ref.shape[2]), F32)
    for c in range(0, d_ff, ff_chunk):
        a = jnp.dot(xb, w1_ref[0, :, c:c + ff_chunk], preferred_element_type=F32)
        u = jnp.dot(xb, w3_ref[0, :, c:c + ff_chunk], preferred_element_type=F32)
        hid = (a * jax.nn.sigmoid(a) * u).astype(BF16)
        acc = acc + jnp.dot(hid, w2_ref[0, c:c + ff_chunk, :], preferred_element_type=F32)
    if with_ln:
        o_ref[...] = _layer_norm_rows(ALPHA * x.astype(F32) + acc, g_ref[...], b_ref[...]).astype(o_ref.dtype)
    else:
        o_ref[...] = acc.astype(o_ref.dtype)


def swiglu_ffn(x_rows, grp_e, w1, w3, w2, ln_g, ln_b, *, with_ln, out_dtype, tm, ff_chunk):
    m, d = x_rows.shape
    d_ff = w1.shape[2]
    once = pl.Buffered(1)
    grid_spec = pltpu.PrefetchScalarGridSpec(
        num_scalar_prefetch=1,
        grid=(m // tm,),
        in_specs=[pl.BlockSpec((tm, d), lambda i, ge: (i, 0)),
                  pl.BlockSpec((1, d, d_ff), lambda i, ge: (ge[i], 0, 0), pipeline_mode=once),
                  pl.BlockSpec((1, d, d_ff), lambda i, ge: (ge[i], 0, 0), pipeline_mode=once),
                  pl.BlockSpec((1, d_ff, d), lambda i, ge: (ge[i], 0, 0), pipeline_mode=once),
                  pl.BlockSpec((1, d), lambda i, ge: (0, 0)), pl.BlockSpec((1, d), lambda i, ge: (0, 0))],
        out_specs=pl.BlockSpec((tm, d), lambda i, ge: (i, 0)),
    )
    return pl.pallas_call(
        functools.partial(_ffn_kernel, ff_chunk=ff_chunk, with_ln=with_ln),
        grid_spec=grid_spec,
        out_shape=jax.ShapeDtypeStruct((m, d), out_dtype),
        compiler_params=_cparams(("arbitrary",)),
        name="swiglu_ffn",
    )(grp_e, x_rows, w1, w3, w2, ln_g.reshape(1, d), ln_b.reshape(1, d))


def _flash_probs(s, m, segs=None):
    if segs is None:
        m_new = jnp.maximum(m, jnp.max(s, axis=0, keepdims=True))
        p = jnp.exp2(s - m_new)
    else:
        m_new, r0 = m, 0
        for n, c in segs:
            seg_max = jnp.max(s[r0:r0 + n], axis=0, keepdims=True)
            m_new = jnp.maximum(m_new, jnp.where(c > NEG_HALF, seg_max + c, NEG))
            r0 += n
        parts, r0 = [], 0
        for n, c in segs:
            shift = jnp.where(c > NEG_HALF, m_new - c, -NEG)
            parts.append(jnp.exp2(s[r0:r0 + n] - shift))
            r0 += n
        p = parts[0] if len(parts) == 1 else jnp.concatenate(parts, axis=0)
    return m_new, jnp.exp2(m - m_new), p.astype(BF16)


def _flash_merge(states):
    ms = [st[0][...] for st in states]
    m = functools.reduce(jnp.maximum, ms)
    ws = [jnp.exp2(mi - m) for mi in ms]
    l = sum(w * st[1][...] for w, st in zip(ws, states))
    acc = sum(w * st[2][...] for w, st in zip(ws, states))
    return m, l, acc


def _flash_finish(m, l, acc):
    return jnp.where(m > NEG_HALF, acc / l, 0.0)


def _rms_rows(x, g):
    return x * lax.rsqrt(jnp.mean(x * x, axis=-1, keepdims=True) + 1e-6) * g


EVEN_T = 256
SMALL_ROWS = 48
GATE_ROW0 = IDX_HEADS
NT_DIMS = (((1,), (1,)), ((), ()))


def _even_inproj_kernel(x_ref, wa_ref, wsT_ref, qn_ref, kvn_ref, wuq_ref, wuk_ref, wqiT_ref, wqbT_ref, wk4_ref, wvT_ref,
                        qidxT_ref, qlatT_ref, sT_ref, kidx_ref, ckv_ref, ckvT_ref,
                        qbT_ref, kcmp_ref, vcmp_ref, kslc_ref, kwin_ref, vT_ref):
    xb = x_ref[0].astype(BF16)
    ya = jnp.dot(xb, wa_ref[...], preferred_element_type=F32)
    cqn = _rms_rows(ya[:, :A_Q_RANK], qn_ref[...]).astype(BF16)
    ckvn = _rms_rows(ya[:, A_Q_RANK:A_Q_RANK + A_KV_RANK], kvn_ref[...])
    kidx_ref[0, 0] = ya[:, A_Q_RANK + A_KV_RANK:A_Q_RANK + A_KV_RANK + IDX_DIM].astype(BF16)
    ckv_ref[0, 0] = ckvn.astype(BF16)
    ckvT_ref[0, 0] = ckvn.T.astype(BF16)
    sT_ref[0] = lax.dot_general(wsT_ref[...], xb, NT_DIMS, preferred_element_type=F32)
    q = jnp.dot(cqn, wuq_ref[...], preferred_element_type=F32).astype(BF16)
    for h in range(A_HEADS):
        qlT = lax.dot_general(wuk_ref[h], q[:, h * HEAD_DIM:(h + 1) * HEAD_DIM], NT_DIMS, preferred_element_type=F32)
        qlatT_ref[0, h * A_KV_RANK:(h + 1) * A_KV_RANK, :] = (qlT * QK_SCALE).astype(BF16)
    qidxT_ref[0] = lax.dot_general(wqiT_ref[...], cqn, NT_DIMS, preferred_element_type=F32).astype(BF16)
    qbT_ref[0] = lax.dot_general(wqbT_ref[...], xb, NT_DIMS, preferred_element_type=F32).astype(BF16)
    yk = jnp.dot(xb, wk4_ref[...], preferred_element_type=F32).astype(BF16)
    gd = B_GROUPS * HEAD_DIM
    for j, ref in enumerate((kcmp_ref, vcmp_ref, kslc_ref, kwin_ref)):
        ref[0] = yk[:, j * gd:(j + 1) * gd]
    vT = lax.dot_general(wvT_ref[...], xb, NT_DIMS, preferred_element_type=F32).astype(BF16)
    for j in range(EVEN_T // NSA_KT):
        vT_ref[0, j] = vT[:, j * NSA_KT:(j + 1) * NSA_KT]


def even_inproj(x3, w_in, q_norm, kv_norm, w_uq, w_uk, w_qidx):
    bsz, seq, d = x3.shape
    T = EVEN_T
    nq = seq // T
    gd = B_GROUPS * HEAD_DIM
    n_kt = seq // NSA_KT
    o_kidx = A_Q_RANK + A_KV_RANK
    o_widx = o_kidx + IDX_DIM
    o_qb = o_widx + IDX_HEADS
    o_kv = o_qb + B_HEADS * HEAD_DIM
    o_gate = o_kv + 6 * gd
    kv = lambda j: w_in[:, o_kv + j * gd:o_kv + (j + 1) * gd]
    wa = jnp.pad(w_in[:, :o_widx], ((0, 0), (0, 512 - o_widx))).astype(BF16)
    w_gate = w_in[:, o_gate:].reshape(d, B_GROUPS, B_HPG, 3).transpose(0, 1, 3, 2).reshape(d, 3 * B_HEADS)
    wsT = jnp.concatenate([w_in[:, o_widx:o_qb] * IDX_HEADS ** -0.5, w_gate,
                           jnp.zeros((d, SMALL_ROWS - IDX_HEADS - 3 * B_HEADS), w_in.dtype)], axis=1).T.astype(BF16)
    wuq = w_uq.reshape(A_Q_RANK, A_HEADS * HEAD_DIM).astype(BF16)
    wuk = jnp.transpose(w_uk, (1, 0, 2)).astype(BF16)
    wqiT = w_qidx.reshape(A_Q_RANK, IDX_HEADS * IDX_DIM).T.astype(BF16)
    wqbT = (w_in[:, o_qb:o_kv] * QK_SCALE).T.astype(BF16)
    wk4 = jnp.concatenate([kv(0), kv(1), kv(2), kv(4)], axis=1).astype(BF16)
    wvT = jnp.concatenate([kv(3), kv(5)], axis=1).T.astype(BF16)
    weights = (wa, wsT, q_norm.reshape(1, -1), kv_norm.reshape(1, -1), wuq, wuk, wqiT, wqbT, wk4, wvT)
    once = pl.Buffered(1)
    w_specs = [pl.BlockSpec(w.shape, (lambda b, i, n=w.ndim: (0,) * n), pipeline_mode=once) for w in weights]
    fm = lambda rows: pl.BlockSpec((1, rows, T), lambda b, i: (b, 0, i))
    tok = lambda cols: pl.BlockSpec((1, T, cols), lambda b, i: (b, i, 0))
    blk = lambda r, c: pl.BlockSpec((1, 1, r, c), lambda b, i: (b, i, 0, 0))
    sds = jax.ShapeDtypeStruct
    return pl.pallas_call(
        _even_inproj_kernel,
        grid=(bsz, nq),
        in_specs=[pl.BlockSpec((1, T, d), lambda b, i: (b, i, 0))] + w_specs,
        out_specs=[fm(IDX_HEADS * IDX_DIM), fm(A_HEADS * A_KV_RANK), fm(SMALL_ROWS),
                   blk(T, IDX_DIM), blk(T, A_KV_RANK), blk(A_KV_RANK, T),
                   fm(B_HEADS * HEAD_DIM), tok(gd), tok(gd), tok(gd), tok(gd),
                   pl.BlockSpec((1, T // NSA_KT, 2 * gd, NSA_KT), lambda b, i: (b, i, 0, 0))],
        out_shape=[sds((bsz, IDX_HEADS * IDX_DIM, seq), BF16), sds((bsz, A_HEADS * A_KV_RANK, seq), BF16),
                   sds((bsz, SMALL_ROWS, seq), F32),
                   sds((bsz, nq, T, IDX_DIM), BF16), sds((bsz, nq, T, A_KV_RANK), BF16), sds((bsz, nq, A_KV_RANK, T), BF16),
                   sds((bsz, B_HEADS * HEAD_DIM, seq), BF16),
                   sds((bsz, seq, gd), BF16), sds((bsz, seq, gd), BF16), sds((bsz, seq, gd), BF16), sds((bsz, seq, gd), BF16),
                   sds((bsz, n_kt, 2 * gd, NSA_KT), BF16)],
        compiler_params=_cparams(("arbitrary", "arbitrary")),
        name="even_inproj",
    )(x3, *weights)


DSA_T = 256
SUB = 128


def _dsa_kernel(qidx_ref, wT_ref, qlat_ref, kidx_ref, ckv_ref, ckvT_ref, bias_ref, wuvt_ref, o_ref,
                key_ref, selb0_ref, selb1_ref, *state_refs, n_keep):
    selb_refs = (selb0_ref, selb1_ref)
    m_refs, l_refs, acc_refs = (state_refs[0:A_HEADS], state_refs[A_HEADS:2 * A_HEADS], state_refs[2 * A_HEADS:])
    qi = pl.program_id(1)
    nkb = qi + 1
    T = DSA_T

    def score_block(kb, carry):
        for sub in range(T // SUB):
            k = kidx_ref[0, kb, sub * SUB:(sub + 1) * SUB, :]
            acc = jnp.zeros((SUB, T), F32)
            for h in range(IDX_HEADS):
                d = jnp.dot(k, qidx_ref[0, h * IDX_DIM:(h + 1) * IDX_DIM, :], preferred_element_type=F32)
                acc = acc + jnp.maximum(d, 0.0) * wT_ref[0, h:h + 1, :]
            bits = lax.bitcast_convert_type(acc, I32)
            key = bits ^ (lax.shift_right_arithmetic(bits, 31) & 0x7FFFFFFF)
            s_pos = kb * T + sub * SUB + lax.broadcasted_iota(I32, (SUB, T), 0)
            t_pos = qi * T + lax.broadcasted_iota(I32, (SUB, T), 1)
            key = jnp.where(s_pos <= t_pos, key, INT_MIN)
            key_ref[pl.ds(pl.multiple_of(kb * T + sub * SUB, SUB), SUB), :] = key
        return carry

    lax.fori_loop(0, nkb, score_block, 0)

    def count_ge(cand):
        def body(kb, cnt):
            blk = key_ref[pl.ds(pl.multiple_of(kb * T, T), T), :]
            ge = jnp.where(blk >= cand, 1, 0).astype(I32)
            return cnt + jnp.sum(ge.reshape(T // 8, 8, T), axis=0)
        cnt = lax.fori_loop(0, nkb, body, jnp.zeros((8, T), I32))
        return jnp.sum(cnt, axis=0, keepdims=True)

    def bit_step(i, u):
        cand_u = u | lax.shift_left(jnp.int32(1), 31 - i)
        cnt = count_ge(cand_u ^ INT_MIN)
        return jnp.where(cnt >= n_keep, cand_u, u)

    u = lax.fori_loop(0, 32, bit_step, jnp.zeros((1, T), I32))
    thr = jnp.maximum(u ^ INT_MIN, INT_MIN + 1)

    for h in range(A_HEADS):
        m_refs[h][...] = jnp.full(m_refs[h].shape, NEG, F32)
        l_refs[h][...] = jnp.zeros(l_refs[h].shape, F32)
        acc_refs[h][...] = jnp.zeros(acc_refs[h].shape, F32)

    states = list(zip(m_refs, l_refs, acc_refs))
    far_bias = [bias_ref[2, h, 0:1, :] for h in range(A_HEADS)]

    def masked_scores(kb, selb):
        selb[...] = jnp.where(key_ref[pl.ds(pl.multiple_of(kb * T, T), T), :] >= thr, 0.0, NEG)
        ckv = ckv_ref[0, kb]
        return [jnp.dot(ckv, qlat_ref[0, h * A_KV_RANK:(h + 1) * A_KV_RANK, :], preferred_element_type=F32) + selb[...]
                for h in range(A_HEADS)]

    def far_body(i, carry):
        tiles = []
        for u, selb in enumerate(selb_refs):
            kb_raw = len(selb_refs) * i + u
            live = kb_raw < n_far
            kb = jnp.minimum(kb_raw, n_far - 1)
            segs = [[(T, jnp.where(live, far_bias[h], NEG))] for h in range(A_HEADS)]
            tiles.append((list(zip(masked_scores(kb, selb), segs)), [ckvT_ref[0, kb]] * A_HEADS))
        for s_all, v_all in tiles:
            _flash_step(s_all, v_all, states)
        return carry

    def near_body(kb, carry):
        s_all = [s + bias_ref[qi - kb, h] for h, s in enumerate(masked_scores(kb, selb_refs[0]))]
        _flash_step(s_all, [ckvT_ref[0, kb]] * A_HEADS, states)
        return carry

    n_far = jnp.maximum(qi - 1, 0)
    lax.fori_loop(0, lax.div(n_far + (len(selb_refs) - 1), jnp.int32(len(selb_refs))), far_body, 0)
    lax.fori_loop(n_far, nkb, near_body, 0)

    for h in range(A_HEADS):
        o_lat = _flash_finish(m_refs[h][...], l_refs[h][...], acc_refs[h][...]).astype(BF16)
        o_ref[0, h * HEAD_DIM:(h + 1) * HEAD_DIM, :] = jnp.dot(
            wuvt_ref[h], o_lat, preferred_element_type=F32).astype(o_ref.dtype)


def dsa_attention(qidxT, sT, qlatT, kidx, ckv, ckvT, w_uv, bias3):
    T = DSA_T
    assert T == EVEN_T
    bsz, nq = kidx.shape[0], kidx.shape[1]
    seq = nq * T
    n_keep = min(DSA_TOPK, seq // 4)
    wuvt = jnp.transpose(w_uv, (1, 2, 0)).astype(BF16)
    return pl.pallas_call(
        functools.partial(_dsa_kernel, n_keep=n_keep),
        grid=(bsz, nq),
        in_specs=[pl.BlockSpec((1, IDX_HEADS * IDX_DIM, T), lambda b, i: (b, 0, i)),
                  pl.BlockSpec((1, SMALL_ROWS, T), lambda b, i: (b, 0, i)),
                  pl.BlockSpec((1, A_HEADS * A_KV_RANK, T), lambda b, i: (b, 0, i)),
                  pl.BlockSpec((1, nq, T, IDX_DIM), lambda b, i: (b, 0, 0, 0)),
                  pl.BlockSpec((1, nq, T, A_KV_RANK), lambda b, i: (b, 0, 0, 0)),
                  pl.BlockSpec((1, nq, A_KV_RANK, T), lambda b, i: (b, 0, 0, 0)),
                  pl.BlockSpec((3, A_HEADS, T, T), lambda b, i: (0, 0, 0, 0)),
                  pl.BlockSpec((A_HEADS, HEAD_DIM, A_KV_RANK), lambda b, i: (0, 0, 0))],
        out_specs=pl.BlockSpec((1, A_HEADS * HEAD_DIM, T), lambda b, i: (b, 0, i)),
        out_shape=jax.ShapeDtypeStruct((bsz, A_HEADS * HEAD_DIM, seq), BF16),
        scratch_shapes=([pltpu.VMEM((seq, T), I32), pltpu.VMEM((T, T), F32), pltpu.VMEM((T, T), F32)]
                        + [pltpu.VMEM((1, T), F32)] * (2 * A_HEADS)
                        + [pltpu.VMEM((A_KV_RANK, T), F32)] * A_HEADS),
        compiler_params=_cparams(("arbitrary", "arbitrary")),
        name="dsa_attention",
    )(qidxT, sT, qlatT, kidx, ckv, ckvT, bias3, wuvt)


def dsa_bias_tiles(rel_bias):
    assert DSA_T + 1 >= T5_FAR
    return bias_tiles(rel_bias, [0, DSA_T, 4 * DSA_T], A_HEADS, 0, DSA_T, DSA_T, -1, 1, False)


N_CMP_PAD = 256


def _compress_kernel(blk_ref, pos_ref, w1_ref, w2_ref, o_ref):
    x = (blk_ref[0].astype(F32) + pos_ref[...]).astype(BF16)
    hid = jax.nn.gelu(jnp.dot(x, w1_ref[...], preferred_element_type=F32))
    o_ref[0] = jnp.dot(hid.astype(BF16), w2_ref[...], preferred_element_type=F32).astype(o_ref.dtype)


def nsa_compress(a, pos, w1, w2, bsz, seq):
    n_chunk = seq // CMP_STRIDE
    assert CMP_LEN == 2 * CMP_STRIDE and n_chunk <= N_CMP_PAD
    width = CMP_STRIDE * HEAD_DIM
    chunks = a.reshape(bsz, n_chunk, CMP_STRIDE, B_GROUPS, HEAD_DIM).transpose(0, 3, 1, 2, 4)
    chunks = chunks.reshape(bsz * B_GROUPS, n_chunk, width)
    blocks = jnp.concatenate([chunks[:, :-1], chunks[:, 1:]], axis=-1)
    blocks = jnp.pad(blocks, ((0, 0), (0, N_CMP_PAD - (n_chunk - 1)), (0, 0)))
    out = pl.pallas_call(
        _compress_kernel,
        grid=(bsz * B_GROUPS,),
        in_specs=[pl.BlockSpec((1, N_CMP_PAD, 2 * width), lambda i: (i, 0, 0)),
                  pl.BlockSpec((1, 2 * width), lambda i: (0, 0)),
                  pl.BlockSpec((2 * width, HEAD_DIM), lambda i: (0, 0)),
                  pl.BlockSpec((HEAD_DIM, HEAD_DIM), lambda i: (0, 0))],
        out_specs=pl.BlockSpec((1, N_CMP_PAD, HEAD_DIM), lambda i: (i, 0, 0)),
        out_shape=jax.ShapeDtypeStruct((bsz * B_GROUPS, N_CMP_PAD, HEAD_DIM), BF16),
        compiler_params=_cparams(("arbitrary",)),
        name="nsa_compress",
    )(blocks, pos.reshape(1, 2 * width), w1.reshape(2 * width, HEAD_DIM).astype(BF16), w2.astype(BF16))
    return out.reshape(bsz, B_GROUPS, N_CMP_PAD, HEAD_DIM)


NSA_TQ = 128
NSA_L = B_HPG * NSA_TQ
NSA_KT = 128
NSA_SLC_REL = 3
NSA_WIN_REL = 5
NSA_FAR_SPLIT = 4


def _flash_step(s_all, vT_all, states):
    probs = []
    for item, (m_ref, _, _) in zip(s_all, states):
        s, segs = item if isinstance(item, tuple) else (item, None)
        m_new, alpha, p = _flash_probs(s, m_ref[...], segs)
        m_ref[...] = m_new
        probs.append((alpha, p))
    for (alpha, p), vT, (_, l_ref, acc_ref) in zip(probs, vT_all, states):
        d = vT.shape[0]
        ones = jnp.ones((BF16_ROWS, vT.shape[1]), BF16)
        pv = jnp.dot(jnp.concatenate([vT, ones], axis=0), p, preferred_element_type=F32)
        acc_ref[...] = alpha * acc_ref[...] + pv[:d]
        l_ref[...] = alpha * l_ref[...] + pv[d:d + 1]


def _flash_loop(lo, hi, scores, values, states, segs, unroll=2):
    def body(i, carry):
        tiles = []
        for u in range(unroll):
            j_raw = lo + unroll * i + u
            live = j_raw < hi
            j = jnp.minimum(j_raw, hi - 1)
            sg = [[(n, jnp.where(live, c, NEG)) for n, c in chain] for chain in segs(j)]
            tiles.append((list(zip(scores(j), sg)), values(j)))
        for s_all, v_all in tiles:
            _flash_step(s_all, v_all, states)
        return carry

    lax.fori_loop(0, lax.div(hi - lo + (unroll - 1), jnp.int32(unroll)), body, 0)


def _nsa_kernel(qT_ref, kc_ref, vcT_ref, biasc_ref, ovl_ref, ks_ref, kw_ref, vT_ref,
                toes_ref, toew_ref, sT_ref, o_ref, selb_ref, *st, n_cmp, n_sel, n_slc):
    qi = pl.program_id(1)
    TQ, L = NSA_TQ, NSA_L
    q0 = qi * TQ
    qTs, qTs_pad = [], []
    for g in range(B_GROUPS):
        q = jnp.concatenate([qT_ref[0, (g * B_HPG + n) * HEAD_DIM:(g * B_HPG + n + 1) * HEAD_DIM, :]
                             for n in range(B_HPG)], axis=1)
        parts = [jnp.zeros_like(q)] * B_GROUPS
        parts[g] = q
        qTs.append(q)
        qTs_pad.append(jnp.concatenate(parts, axis=0))
    t_lane = q0 + (lax.broadcasted_iota(I32, (1, L), 1) & (TQ - 1))

    o_cs = []
    for g in range(B_GROUPS):
        s = jnp.dot(kc_ref[0, g], qTs[g], preferred_element_type=F32) + biasc_ref[g, 0]
        i_idx = lax.broadcasted_iota(I32, (N_CMP_PAD, L), 0)
        valid = jnp.where(i_idx < n_cmp, i_idx * CMP_STRIDE + (CMP_LEN - 1), 2 ** 30) <= t_lane
        s = jnp.where(valid, s, NEG)
        m = jnp.max(s, axis=0, keepdims=True)
        p = jnp.where(valid, jnp.exp2(s - m), 0.0)
        l = jnp.sum(p, axis=0, keepdims=True)
        p_c = p / jnp.where(l > 0, l, 1.0)
        o_c = jnp.dot(vcT_ref[0, g], p_c.astype(BF16), preferred_element_type=F32)

        psum = p_c[:, 0:TQ]
        for n in range(1, B_HPG):
            psum = psum + p_c[:, n * TQ:(n + 1) * TQ]
        sc = jnp.dot(ovl_ref[...], psum, preferred_element_type=F32, precision=lax.Precision.HIGHEST)
        j_idx = lax.broadcasted_iota(I32, (n_slc, TQ), 0)
        cur = (q0 + lax.broadcasted_iota(I32, (1, TQ), 1)) // SLC_BLOCK
        adm = j_idx <= cur
        forced = (j_idx == 0) | (j_idx == cur) | (j_idx == cur - 1)
        scv = jnp.where(adm, jnp.where(forced, jnp.inf, sc), -jnp.inf)
        rank = jnp.zeros((n_slc, TQ), I32)
        for jp in range(n_slc):
            row = scv[jp:jp + 1, :]
            beats = jnp.where(row > scv, 1, jnp.where((row == scv) & (jp < j_idx), 1, 0))
            rank = rank + beats
        selb = jnp.where(rank < n_sel, 0.0, NEG).astype(F32)
        selb4 = jnp.concatenate([selb] * B_HPG, axis=1)
        for j in range(n_slc):
            selb_ref[g, j] = selb4[j:j + 1, :]
        o_cs.append(o_c)

    for ref in st[0::3]:
        ref[...] = jnp.full(ref.shape, NEG, F32)
    for ref in st[1::3] + st[2::3]:
        ref[...] = jnp.zeros(ref.shape, F32)
    slc_st = [st[6 * g:6 * g + 3] for g in range(B_GROUPS)]
    win_st = [st[6 * g + 3:6 * g + 6] for g in range(B_GROUPS)]
    n_main = 6 * B_GROUPS
    xtr_st = [[st[n_main + 3 * (g * (NSA_FAR_SPLIT - 1) + r):n_main + 3 * (g * (NSA_FAR_SPLIT - 1) + r) + 3]
               for r in range(NSA_FAR_SPLIT - 1)] for g in range(B_GROUPS)]
    per_kt = NSA_KT // SLC_BLOCK

    groups = range(B_GROUPS)
    far_bias = [toes_ref[g, NSA_SLC_REL - 1, 0:1, :] for g in groups]

    def slc_scores(g, jt, near):
        s = jnp.dot(ks_ref[0, jt], qTs_pad[g], preferred_element_type=F32)
        return s + toes_ref[g, jnp.minimum(qi - jt, NSA_SLC_REL - 1)] if near else s

    def slc_segs(g, jt, near):
        return [(SLC_BLOCK, selb_ref[g, per_kt * jt + r] + (0.0 if near else far_bias[g])) for r in range(per_kt)]

    def win_scores(g, jt):
        rel = jnp.minimum(qi - jt, NSA_WIN_REL - 1)
        return jnp.dot(kw_ref[0, jt], qTs_pad[g], preferred_element_type=F32) + toew_ref[g, rel]

    gd = B_GROUPS * HEAD_DIM
    v_slc = lambda g, jt: vT_ref[0, jt, g * HEAD_DIM:(g + 1) * HEAD_DIM, :]
    v_win = lambda g, jt: vT_ref[0, jt, gd + g * HEAD_DIM:gd + (g + 1) * HEAD_DIM, :]

    assert NSA_WIN_REL >= NSA_SLC_REL
    j_lo = jnp.maximum(qi - (NSA_WIN_REL - 1), 0)

    def far_body(i, carry):
        s_all, v_all, chains = [], [], []
        for r in range(NSA_FAR_SPLIT):
            jt_raw = NSA_FAR_SPLIT * i + r
            live = jt_raw < j_lo
            jt = jnp.minimum(jt_raw, j_lo - 1)
            for g in groups:
                segs = [(n, jnp.where(live, c, NEG)) for n, c in slc_segs(g, jt, False)]
                s_all.append((slc_scores(g, jt, False), segs))
                v_all.append(v_slc(g, jt))
                chains.append(slc_st[g] if r == 0 else xtr_st[g][r - 1])
        _flash_step(s_all, v_all, chains)
        return carry

    zero_row = jnp.zeros((1, L), F32)
    lax.fori_loop(0, lax.div(j_lo + (NSA_FAR_SPLIT - 1), jnp.int32(NSA_FAR_SPLIT)), far_body, 0)
    _flash_loop(j_lo, qi + 1,
                lambda jt: [slc_scores(g, jt, True) for g in groups] + [win_scores(g, jt) for g in groups],
                lambda jt: [v_slc(g, jt) for g in groups] + [v_win(g, jt) for g in groups],
                slc_st + win_st,
                lambda jt: [slc_segs(g, jt, True) for g in groups] + [[(NSA_KT, zero_row)] for g in groups])

    for g in range(B_GROUPS):
        o_s = _flash_finish(*_flash_merge([slc_st[g]] + xtr_st[g]))
        o_w = _flash_finish(*[r[...] for r in win_st[g]])
        row0 = GATE_ROW0 + g * 3 * B_HPG
        gate = [jax.nn.sigmoid(jnp.concatenate([sT_ref[0, row0 + j * B_HPG + n:row0 + j * B_HPG + n + 1, :]
                                                for n in range(B_HPG)], axis=1)) for j in range(3)]
        o = (gate[0] * o_cs[g] + gate[1] * o_s + gate[2] * o_w).astype(o_ref.dtype)
        for n in range(B_HPG):
            o_ref[0, (g * B_HPG + n) * HEAD_DIM:(g * B_HPG + n + 1) * HEAD_DIM, :] = o[:, n * TQ:(n + 1) * TQ]


def nsa_bias_inputs(rel_bias, seq):
    TQ, L, KT = NSA_TQ, NSA_L, NSA_KT
    nq = seq // TQ
    bc = bias_tiles(rel_bias, [-(CMP_LEN - 1)], B_HEADS, A_HEADS, N_CMP_PAD, seq, -CMP_STRIDE, 1, False, 0)
    bc = bc.reshape(B_GROUPS, B_HPG, N_CMP_PAD, nq, TQ).transpose(0, 3, 2, 1, 4).reshape(B_GROUPS, nq, N_CMP_PAD, L)

    def lanes(t):
        v = t.shape[0]
        return t.reshape(v, B_GROUPS, B_HPG, KT, TQ).transpose(1, 0, 3, 2, 4).reshape(B_GROUPS, v, KT, L)

    assert KT == TQ and (NSA_SLC_REL - 1) * KT - (KT - 1) >= T5_FAR
    toe_s = bias_tiles(rel_bias, [v * KT for v in range(NSA_SLC_REL - 1)] + [64 * KT],
                       B_HEADS, A_HEADS, KT, TQ, -1, 1, True, 0)
    assert (NSA_WIN_REL - 1) * KT - (KT - 1) < WINDOW <= NSA_WIN_REL * KT - (KT - 1)
    toe_w = bias_tiles(rel_bias, [v * KT for v in range(NSA_WIN_REL)], B_HEADS, A_HEADS, KT, TQ, -1, 1, True, WINDOW)
    return bc, lanes(toe_s), lanes(toe_w)


def nsa_overlap(seq):
    n_cmp = (seq - CMP_LEN) // CMP_STRIDE + 1
    n_slc = seq // SLC_BLOCK
    cs = np.arange(N_CMP_PAD) * CMP_STRIDE
    ss = np.arange(n_slc) * SLC_BLOCK
    ov = ((cs[None, :] + CMP_LEN - 1 >= ss[:, None]) & (cs[None, :] <= ss[:, None] + SLC_BLOCK - 1)
          & (np.arange(N_CMP_PAD)[None, :] < n_cmp))
    return jnp.asarray(ov.astype(np.float32))


def nsa_attention(qbT, kc, vc, kslc, kwin, vT, sT, biasc, toe_s, toe_w):
    TQ, L, KT, G = NSA_TQ, NSA_L, NSA_KT, B_GROUPS
    bsz, n_kt = vT.shape[0], vT.shape[1]
    seq = n_kt * KT
    nq = seq // TQ
    n_slc = seq // SLC_BLOCK
    n_cmp = (seq - CMP_LEN) // CMP_STRIDE + 1
    n_sel = min(SLC_TOPN, n_slc)
    gd = G * HEAD_DIM
    vcT = vc.transpose(0, 1, 3, 2)
    once = pl.Buffered(1)
    k_spec = pl.BlockSpec((1, n_kt, KT, gd), lambda b, i: (b, 0, 0, 0))
    n_chain = 2 * G + G * (NSA_FAR_SPLIT - 1)
    return pl.pallas_call(
        functools.partial(_nsa_kernel, n_cmp=n_cmp, n_sel=n_sel, n_slc=n_slc),
        grid=(bsz, nq),
        in_specs=[pl.BlockSpec((1, B_HEADS * HEAD_DIM, TQ), lambda b, i: (b, 0, i)),
                  pl.BlockSpec((1, G, N_CMP_PAD, HEAD_DIM), lambda b, i: (b, 0, 0, 0)),
                  pl.BlockSpec((1, G, HEAD_DIM, N_CMP_PAD), lambda b, i: (b, 0, 0, 0)),
                  pl.BlockSpec((G, 1, N_CMP_PAD, L), lambda b, i: (0, i, 0, 0)),
                  pl.BlockSpec((n_slc, N_CMP_PAD), lambda b, i: (0, 0), pipeline_mode=once),
                  k_spec, k_spec,
                  pl.BlockSpec((1, n_kt, 2 * gd, KT), lambda b, i: (b, 0, 0, 0)),
                  pl.BlockSpec((G, NSA_SLC_REL, KT, L), lambda b, i: (0, 0, 0, 0), pipeline_mode=once),
                  pl.BlockSpec((G, NSA_WIN_REL, KT, L), lambda b, i: (0, 0, 0, 0), pipeline_mode=once),
                  pl.BlockSpec((1, SMALL_ROWS, TQ), lambda b, i: (b, 0, i))],
        out_specs=pl.BlockSpec((1, B_HEADS * HEAD_DIM, TQ), lambda b, i: (b, 0, i)),
        out_shape=jax.ShapeDtypeStruct((bsz, B_HEADS * HEAD_DIM, seq), BF16),
        scratch_shapes=([pltpu.VMEM((G, n_slc, 1, L), F32)]
                        + [pltpu.VMEM((1, L), F32), pltpu.VMEM((1, L), F32), pltpu.VMEM((HEAD_DIM, L), F32)] * n_chain),
        compiler_params=_cparams(("arbitrary", "arbitrary")),
        name="nsa_attention",
    )(qbT, kc, vcT, biasc, nsa_overlap(seq), kslc.reshape(bsz, n_kt, KT, gd), kwin.reshape(bsz, n_kt, KT, gd),
      vT, toe_s, toe_w, sT)


MOBA_T = MOBA_BLOCK


MOBA_HB = 16


PAIR = 2 * HEAD_DIM


def _moba_inproj_kernel(x_ref, wqT_ref, wk_ref, wvT_ref, qT_ref, k_ref, vT_ref):
    xb = x_ref[0].astype(BF16)
    nt = (((1,), (1,)), ((), ()))
    qT_ref[0] = lax.dot_general(wqT_ref[...], xb, nt, preferred_element_type=F32).astype(BF16)
    k_ref[0, 0] = jnp.dot(xb, wk_ref[...], preferred_element_type=F32).astype(BF16)
    vT_ref[0, 0] = lax.dot_general(wvT_ref[...], xb, nt, preferred_element_type=F32).astype(BF16)


def _pair_padded_qT(wq):
    n_heads = wq.shape[1] // HEAD_DIM
    wT = wq.T.reshape(n_heads, HEAD_DIM, wq.shape[0])
    z = jnp.zeros_like(wT)
    even = jnp.concatenate([wT, z], axis=1)
    odd = jnp.concatenate([z, wT], axis=1)
    is_even = (jnp.arange(n_heads) % 2 == 0)[:, None, None]
    return jnp.where(is_even, even, odd).reshape(n_heads * PAIR, wq.shape[0])


def moba_inproj(x3, w_in):
    bsz, seq, d = x3.shape
    T = MOBA_T
    n_blk = seq // T
    hd = C_HEADS * HEAD_DIM
    wqT = _pair_padded_qT(w_in[:, :hd] * QK_SCALE).astype(BF16)
    wk = w_in[:, hd:2 * hd].astype(BF16)
    wvT = w_in[:, 2 * hd:].T.astype(BF16)
    once = pl.Buffered(1)
    return pl.pallas_call(
        _moba_inproj_kernel,
        grid=(bsz, n_blk),
        in_specs=[pl.BlockSpec((1, T, d), lambda b, i: (b, i, 0)),
                  pl.BlockSpec(wqT.shape, lambda b, i: (0, 0), pipeline_mode=once),
                  pl.BlockSpec(wk.shape, lambda b, i: (0, 0), pipeline_mode=once),
                  pl.BlockSpec(wvT.shape, lambda b, i: (0, 0), pipeline_mode=once)],
        out_specs=[pl.BlockSpec((1, C_HEADS * PAIR, T), lambda b, i: (b, 0, i)),
                   pl.BlockSpec((1, 1, T, hd), lambda b, i: (b, i, 0, 0)),
                   pl.BlockSpec((1, 1, hd, T), lambda b, i: (b, i, 0, 0))],
        out_shape=[jax.ShapeDtypeStruct((bsz, C_HEADS * PAIR, seq), BF16),
                   jax.ShapeDtypeStruct((bsz, n_blk, T, hd), BF16),
                   jax.ShapeDtypeStruct((bsz, n_blk, hd, T), BF16)],
        compiler_params=_cparams(("arbitrary", "arbitrary")),
        name="moba_inproj",
    )(x3, wqT, wk, wvT)


def _moba_kernel(qT_ref, k_ref, vT_ref, bias_ref, o_ref, kmean_ref, selb_ref, *st, n_sel):
    qi = pl.program_id(2)
    T = MOBA_T
    n_blk = k_ref.shape[1]
    states = [st[3 * hh:3 * hh + 3] for hh in range(MOBA_HB)]

    @pl.when(qi == 0)
    def _():
        for j in range(n_blk):
            kmean_ref[j:j + 1, :] = jnp.mean(k_ref[0, j].astype(F32), axis=0, keepdims=True)

    qTs = [qT_ref[0, hh * PAIR:(hh + 1) * PAIR, :] for hh in range(MOBA_HB)]
    pair = lambda hh: slice((hh // 2) * PAIR, (hh // 2 + 1) * PAIR)
    j_idx = lax.broadcasted_iota(I32, (n_blk, T), 0)
    for hh in range(MOBA_HB):
        gate = jnp.dot(kmean_ref[:, pair(hh)], qTs[hh].astype(F32), preferred_element_type=F32,
                       precision=lax.Precision.HIGHEST)
        gv = jnp.where(j_idx < qi, gate, -jnp.inf)
        rank = jnp.zeros((n_blk, T), I32)
        for jp in range(n_blk):
            row = gv[jp:jp + 1, :]
            rank = rank + jnp.where(row > gv, 1, jnp.where((row == gv) & (jp < j_idx), 1, 0))
        selb = jnp.where(j_idx < qi, jnp.where(rank < n_sel, 0.0, NEG),
                         jnp.where(j_idx == qi, 0.0, NEG)).astype(F32)
        for j in range(n_blk):
            selb_ref[hh, j] = selb[j:j + 1, :]
        m_ref, l_ref, acc_ref = states[hh]
        m_ref[...] = jnp.full(m_ref.shape, NEG, F32)
        l_ref[...] = jnp.zeros(l_ref.shape, F32)
        acc_ref[...] = jnp.zeros(acc_ref.shape, F32)

    heads = range(MOBA_HB)
    far_bias = [bias_ref[2, hh, 0:1, :] for hh in heads]

    def far_scores(kb):
        return [jnp.dot(k_ref[0, kb, :, pair(hh)], qTs[hh], preferred_element_type=F32) for hh in heads]

    def far_segs(kb):
        return [[(T, selb_ref[hh, kb] + far_bias[hh])] for hh in heads]

    def near_scores(kb):
        return [jnp.dot(k_ref[0, kb, :, pair(hh)], qTs[hh], preferred_element_type=F32) + bias_ref[qi - kb, hh]
                for hh in heads]

    def near_segs(kb):
        return [[(T, selb_ref[hh, kb])] for hh in heads]

    def values(kb):
        return [vT_ref[0, kb, hh * HEAD_DIM:(hh + 1) * HEAD_DIM, :] for hh in heads]

    n_far = jnp.maximum(qi - 1, 0)
    _flash_loop(0, n_far, far_scores, values, states, far_segs)
    _flash_loop(n_far, qi + 1, near_scores, values, states, near_segs)
    for hh in range(MOBA_HB):
        o_ref[0, hh * HEAD_DIM:(hh + 1) * HEAD_DIM, :] = _flash_finish(*[r[...] for r in states[hh]]).astype(o_ref.dtype)


def moba_bias_tiles(rel_bias):
    assert MOBA_T + 1 >= T5_FAR
    t0 = bias_tiles(rel_bias, [0], C_HEADS, 0, MOBA_T, MOBA_T, -1, 1, True)
    t12 = bias_tiles(rel_bias, [MOBA_T, 4 * MOBA_T], C_HEADS, 0, MOBA_T, MOBA_T, -1, 1, False)
    return jnp.concatenate([t0, t12], axis=0)


def moba_attention(qT, k, vT, bias3):
    T = MOBA_T
    bsz, n_blk = k.shape[0], k.shape[1]
    seq = n_blk * T
    n_sel = min(MOBA_TOPK, n_blk - 1)
    HB = MOBA_HB
    assert HB % 2 == 0
    once = pl.Buffered(1)
    out = pl.pallas_call(
        functools.partial(_moba_kernel, n_sel=n_sel),
        grid=(bsz, C_HEADS // HB, n_blk),
        in_specs=[pl.BlockSpec((1, HB * PAIR, T), lambda b, h, i: (b, h, i)),
                  pl.BlockSpec((1, n_blk, T, HB * HEAD_DIM), lambda b, h, i: (b, 0, 0, h), pipeline_mode=once),
                  pl.BlockSpec((1, n_blk, HB * HEAD_DIM, T), lambda b, h, i: (b, 0, h, 0), pipeline_mode=once),
                  pl.BlockSpec((3, HB, T, T), lambda b, h, i: (0, h, 0, 0), pipeline_mode=once)],
        out_specs=pl.BlockSpec((1, HB * HEAD_DIM, T), lambda b, h, i: (b, h, i)),
        out_shape=jax.ShapeDtypeStruct((bsz, C_HEADS * HEAD_DIM, seq), BF16),
        scratch_shapes=([pltpu.VMEM((n_blk, HB * HEAD_DIM), F32), pltpu.VMEM((HB, n_blk, 1, T), F32)]
                        + [pltpu.VMEM((1, T), F32), pltpu.VMEM((1, T), F32), pltpu.VMEM((HEAD_DIM, T), F32)] * HB),
        compiler_params=_cparams(("arbitrary", "arbitrary", "arbitrary")),
        name="moba_attention",
    )(qT, k, vT, bias3)
    return out


ROUTER_LANES = 128


def _projT_ln_kernel(*refs, n_in):
    aT_refs, w_refs = refs[:n_in], refs[n_in:2 * n_in]
    x_ref, g_ref, b_ref, o_ref = refs[2 * n_in:]
    tn = (((0,), (0,)), ((), ()))
    mix = lax.dot_general(aT_refs[0][0], w_refs[0][...], tn, preferred_element_type=F32)
    for aT_ref, w_ref in zip(aT_refs[1:], w_refs[1:]):
        mix = mix + lax.dot_general(aT_ref[0], w_ref[...], tn, preferred_element_type=F32)
    o_ref[0] = _layer_norm_rows(ALPHA * x_ref[0] + mix, g_ref[...], b_ref[...])


def projT_residual_ln(aTs, w, x3, g, b, tm=512):
    bsz, seq, d = x3.shape
    ws, k0 = [], 0
    for aT in aTs:
        ws.append(w[k0:k0 + aT.shape[1]])
        k0 += aT.shape[1]
    n_in = len(aTs)
    once = pl.Buffered(1)
    vec = pl.BlockSpec((1, d), lambda bb, i: (0, 0))
    return pl.pallas_call(
        functools.partial(_projT_ln_kernel, n_in=n_in),
        grid=(bsz, seq // tm),
        in_specs=([pl.BlockSpec((1, aT.shape[1], tm), lambda bb, i: (bb, 0, i)) for aT in aTs]
                  + [pl.BlockSpec(wi.shape, lambda bb, i: (0, 0), pipeline_mode=once) for wi in ws]
                  + [pl.BlockSpec((1, tm, d), lambda bb, i: (bb, i, 0)), vec, vec]),
        out_specs=pl.BlockSpec((1, tm, d), lambda bb, i: (bb, i, 0)),
        out_shape=jax.ShapeDtypeStruct((bsz, seq, d), F32),
        compiler_params=_cparams(("arbitrary", "arbitrary")),
        name="projT_residual_ln",
    )(*aTs, *ws, x3, g.reshape(1, d), b.reshape(1, d))


def _router_kernel(h_ref, w_ref, o_ref):
    o_ref[...] = jnp.dot(h_ref[...], w_ref[...], preferred_element_type=F32, precision=lax.Precision.HIGHEST)


def router_logits(h, router, tm=1024):
    m, d = h.shape
    w = jnp.pad(router, ((0, 0), (0, ROUTER_LANES - N_EXPERTS)))
    out = pl.pallas_call(
        _router_kernel,
        grid=(m // tm,),
        in_specs=[pl.BlockSpec((tm, d), lambda i: (i, 0)), pl.BlockSpec((d, ROUTER_LANES), lambda i: (0, 0))],
        out_specs=pl.BlockSpec((tm, ROUTER_LANES), lambda i: (i, 0)),
        out_shape=jax.ShapeDtypeStruct((m, ROUTER_LANES), F32),
        compiler_params=_cparams(("arbitrary",)),
        name="router_logits",
    )(h, w)
    return out[:, :N_EXPERTS]


IDX_LANES = 128


def _issue_row_gather(idx_vmem_ref, idx_smem, sem_i, src_hbm, dst_slot_ref, sem_slot, n_rows):
    cp = pltpu.make_async_copy(idx_vmem_ref.at[0], idx_smem, sem_i)
    cp.start()
    cp.wait()

    for r in range(n_rows):
        row = idx_smem[r // IDX_LANES, r % IDX_LANES]
        pltpu.make_async_copy(src_hbm.at[pl.ds(row, 1)], dst_slot_ref.at[pl.ds(r, 1)], sem_slot).start()


def _pipelined_gather(idx0_ref, idxn_ref, idx_smem, sem_i, src_hbm, buf, sem_buf, n_rows):
    g = pl.program_id(0)
    slot = lax.rem(g, 2)

    @pl.when(g == 0)
    def _():
        _issue_row_gather(idx0_ref, idx_smem, sem_i, src_hbm, buf.at[0], sem_buf.at[0], n_rows)

    @pl.when(g + 1 < pl.num_programs(0))
    def _():
        _issue_row_gather(idxn_ref, idx_smem, sem_i, src_hbm, buf.at[1 - slot], sem_buf.at[1 - slot], n_rows)

    pltpu.make_async_copy(buf.at[slot], buf.at[slot], sem_buf.at[slot]).wait()
    return slot


def _gather_specs(n_steps, k):
    first = lambda g, *_: (0, 0, 0)
    nxt = lambda g, *_: (jnp.minimum(g + 1, n_steps - 1), 0, 0)
    return pl.BlockSpec((1, k, IDX_LANES), first), pl.BlockSpec((1, k, IDX_LANES), nxt)


def _moe_ffn_kernel(ge_ref, idx0_ref, idxn_ref, h_hbm, w1_ref, w3_ref, w2_ref, o_ref,
                    xbuf, idx_smem, sem_i, sem_x, *, ff_chunk):
    del ge_ref
    g = pl.program_id(0)
    slot = lax.rem(g, 2)
    wait_slot = lambda s: pltpu.make_async_copy(xbuf.at[s], xbuf.at[s], sem_x.at[s]).wait()

    @pl.when(g == 0)
    def _():
        _issue_row_gather(idx0_ref, idx_smem, sem_i, h_hbm, xbuf.at[0], sem_x.at[0], EXPERT_ROWS)

    wait_slot(slot)
    cp = pltpu.make_async_copy(idxn_ref.at[0], idx_smem, sem_i)
    cp.start()
    xb = xbuf[slot].astype(BF16)
    d_ff = w1_ref.shape[2]
    n_chunks = d_ff // ff_chunk
    n_issue = max(n_chunks - 2, 1)
    per_chunk = -(-EXPERT_ROWS // n_issue)
    acc = jnp.zeros((EXPERT_ROWS, w2_ref.shape[2]), F32)
    for ci in range(n_chunks):
        c = ci * ff_chunk
        a = jnp.dot(xb, w1_ref[0, :, c:c + ff_chunk], preferred_element_type=F32)
        u = jnp.dot(xb, w3_ref[0, :, c:c + ff_chunk], preferred_element_type=F32)
        hid = (a * jax.nn.sigmoid(a) * u).astype(BF16)
        acc = acc + jnp.dot(hid, w2_ref[0, c:c + ff_chunk, :], preferred_element_type=F32)
        if ci == 0:
            cp.wait()
        for r in range(ci * per_chunk, min((ci + 1) * per_chunk, EXPERT_ROWS)):
            row = idx_smem[r // IDX_LANES, r % IDX_LANES]
            pltpu.make_async_copy(h_hbm.at[pl.ds(row, 1)], xbuf.at[1 - slot, pl.ds(r, 1)], sem_x.at[1 - slot]).start()
    o_ref[...] = acc

    @pl.when(g == pl.num_programs(0) - 1)
    def _():
        wait_slot(1 - slot)


def moe_expert_ffn(h, row_tok, grp_e, w1, w3, w2, ff_chunk=512):
    d = h.shape[1]
    d_ff = w1.shape[2]
    n_groups = grp_e.shape[0]
    k = EXPERT_ROWS // IDX_LANES
    idx = row_tok.reshape(n_groups, k, IDX_LANES)
    once = pl.Buffered(1)
    idx0_spec, idxn_spec = _gather_specs(n_groups, k)
    grid_spec = pltpu.PrefetchScalarGridSpec(
        num_scalar_prefetch=1,
        grid=(n_groups,),
        in_specs=[idx0_spec, idxn_spec, pl.BlockSpec(memory_space=pl.ANY),
                  pl.BlockSpec((1, d, d_ff), lambda g, ge: (ge[g], 0, 0), pipeline_mode=once),
                  pl.BlockSpec((1, d, d_ff), lambda g, ge: (ge[g], 0, 0), pipeline_mode=once),
                  pl.BlockSpec((1, d_ff, d), lambda g, ge: (ge[g], 0, 0), pipeline_mode=once)],
        out_specs=pl.BlockSpec((EXPERT_ROWS, d), lambda g, ge: (g, 0)),
        scratch_shapes=[pltpu.VMEM((2, EXPERT_ROWS, d), F32), pltpu.SMEM((k, IDX_LANES), I32),
                        pltpu.SemaphoreType.DMA(()), pltpu.SemaphoreType.DMA((2,))],
    )
    return pl.pallas_call(
        functools.partial(_moe_ffn_kernel, ff_chunk=ff_chunk),
        grid_spec=grid_spec,
        out_shape=jax.ShapeDtypeStruct((n_groups * EXPERT_ROWS, d), F32),
        compiler_params=_cparams(("arbitrary",)),
        name="moe_expert_ffn",
    )(grp_e, idx, idx, h, w1, w3, w2)


COMBINE_TM = 256


def _moe_combine_ln_kernel(idx0_ref, idxn_ref, y_hbm, h_ref, gate_ref, g_ref, b_ref, o_ref,
                           ybuf, idx_smem, sem_i, sem_y):
    tm = COMBINE_TM
    slot = _pipelined_gather(idx0_ref, idxn_ref, idx_smem, sem_i, y_hbm, ybuf, sem_y, TOP_K * tm)
    y = gate_ref[:, 0:1] * ybuf[slot, 0:tm, :]
    for j in range(1, TOP_K):
        y = y + gate_ref[:, j:j + 1] * ybuf[slot, j * tm:(j + 1) * tm, :]
    o_ref[...] = _layer_norm_rows(ALPHA * h_ref[...] + y, g_ref[...], b_ref[...])


def moe_combine_ln(h, y_rows, dest, gate, g, b):
    m, d = h.shape
    tm = COMBINE_TM
    n_tiles = m // tm
    k = TOP_K * tm // IDX_LANES
    idx = dest.reshape(n_tiles, tm, TOP_K).transpose(0, 2, 1).reshape(n_tiles, k, IDX_LANES)
    idx0_spec, idxn_spec = _gather_specs(n_tiles, k)
    row = pl.BlockSpec((tm, d), lambda i: (i, 0))
    vec = pl.BlockSpec((1, d), lambda i: (0, 0))
    return pl.pallas_call(
        _moe_combine_ln_kernel,
        grid=(n_tiles,),
        in_specs=[idx0_spec, idxn_spec, pl.BlockSpec(memory_space=pl.ANY), row,
                  pl.BlockSpec((tm, TOP_K), lambda i: (i, 0)), vec, vec],
        out_specs=row,
        out_shape=jax.ShapeDtypeStruct((m, d), F32),
        scratch_shapes=[pltpu.VMEM((2, TOP_K * tm, d), F32), pltpu.SMEM((k, IDX_LANES), I32),
                        pltpu.SemaphoreType.DMA(()), pltpu.SemaphoreType.DMA((2,))],
        compiler_params=_cparams(("arbitrary",)),
        name="moe_combine_ln",
    )(idx, idx, y_rows, h, gate, g.reshape(1, d), b.reshape(1, d))


def moe_dispatch_plan(logits):
    n_tok = logits.shape[0]
    top_val, top_e = lax.top_k(logits, TOP_K)
    gate = jax.nn.softmax(top_val, axis=-1)
    e_flat = top_e.reshape(-1)
    onehot = (e_flat[:, None] == jnp.arange(N_EXPERTS, dtype=e_flat.dtype)[None, :]).astype(I32)
    rank = jnp.take_along_axis(jnp.cumsum(onehot, axis=0) - onehot, e_flat[:, None], axis=1)[:, 0]
    counts = jnp.sum(onehot, axis=0)
    padded = (counts + EXPERT_ROWS - 1) // EXPERT_ROWS * EXPERT_ROWS
    pend = jnp.cumsum(padded)
    pstart = pend - padded
    dest = pstart[e_flat] + rank
    n_assign = n_tok * TOP_K
    n_rows = -(-n_assign // EXPERT_ROWS) * EXPERT_ROWS + N_EXPERTS * EXPERT_ROWS
    n_groups = n_rows // EXPERT_ROWS
    tok_flat = jnp.repeat(jnp.arange(n_tok, dtype=I32), TOP_K)
    row_tok = jnp.zeros((n_rows,), I32).at[dest].set(tok_flat, unique_indices=True, mode="promise_in_bounds")
    grp_e = jnp.minimum(jnp.searchsorted(pend, jnp.arange(n_groups, dtype=I32) * EXPERT_ROWS, side='right'),
                        N_EXPERTS - 1).astype(I32)
    return gate, dest.astype(I32), row_tok, grp_e


def kernel(x, rel_bias, e_w_in, e_q_norm, e_kv_norm, e_w_uq, e_w_uk, e_w_uv, e_w_qidx, e_pos_k, e_pos_v, e_ck1, e_ck2, e_cv1, e_cv2, e_w_out, e_ln1_g, e_ln1_b, e_ffn_w1, e_ffn_w3, e_ffn_w2, e_ln2_g, e_ln2_b, o_w_in, o_w_out, o_ln1_g, o_ln1_b, o_router, o_moe_w1, o_moe_w3, o_moe_w2, o_ln2_g, o_ln2_b):
    bsz, seq, d = x.shape
    m = bsz * seq
    xf = x.reshape(m, d)
    dsa_bias = dsa_bias_tiles(rel_bias)
    nsa_bc, nsa_toe_s, nsa_toe_w = nsa_bias_inputs(rel_bias, seq)
    moba_bias = moba_bias_tiles(rel_bias)
    gd = B_GROUPS * HEAD_DIM
    for layer in range(DEPTH):
        i = layer // 2
        if layer % 2 == 0:
            x3 = xf.reshape(bsz, seq, d)
            (qidxT, qlatT, sT, kidx, ckv, ckvT, qbT, kcmp, vcmp, kslc, kwin, vT) = even_inproj(
                x3, e_w_in[i], e_q_norm[i], e_kv_norm[i], e_w_uq[i], e_w_uk[i], e_w_qidx[i])
            o_aT = dsa_attention(qidxT, sT, qlatT, kidx, ckv, ckvT, e_w_uv[i], dsa_bias)
            kc = nsa_compress(kcmp.reshape(m, gd), e_pos_k[i], e_ck1[i], e_ck2[i], bsz, seq)
            vc = nsa_compress(vcmp.reshape(m, gd), e_pos_v[i], e_cv1[i], e_cv2[i], bsz, seq)
            o_bT = nsa_attention(qbT, kc, vc, kslc, kwin, vT, sT, nsa_bc, nsa_toe_s, nsa_toe_w)
            h = projT_residual_ln([o_aT, o_bT], e_w_out[i].astype(BF16), x3, e_ln1_g[i], e_ln1_b[i]).reshape(m, d)
            tm = 1024
            xf = swiglu_ffn(h, jnp.zeros((m // tm,), I32), e_ffn_w1[i][None].astype(BF16),
                            e_ffn_w3[i][None].astype(BF16), e_ffn_w2[i][None].astype(BF16),
                            e_ln2_g[i], e_ln2_b[i], with_ln=True, out_dtype=F32, tm=tm, ff_chunk=1408)
        else:
            x3 = xf.reshape(bsz, seq, d)
            o_cT = moba_attention(*moba_inproj(x3, o_w_in[i]), moba_bias)
            h = projT_residual_ln([o_cT], o_w_out[i].astype(BF16), x3, o_ln1_g[i], o_ln1_b[i]).reshape(m, d)
            gate, dest, row_tok, grp_e = moe_dispatch_plan(router_logits(h, o_router[i]))
            y_rows = moe_expert_ffn(h, row_tok, grp_e, o_moe_w1[i].astype(BF16), o_moe_w3[i].astype(BF16),
                                    o_moe_w2[i].astype(BF16))
            xf = moe_combine_ln(h, y_rows, dest, gate, o_ln2_g[i], o_ln2_b[i])
    return xf.reshape(bsz, seq, d)
```

```python
import functools
import math

import numpy as np
import jax
import jax.numpy as jnp
from jax import lax
from jax.experimental import pallas as pl
from jax.experimental.pallas import tpu as pltpu

F32 = jnp.float32
BF16 = jnp.bfloat16
I32 = jnp.int32
BF16_ROWS = 16

HEAD_DIM = 64
NUM_BUCKETS = 32
MAX_DISTANCE = 128
N_BIAS_HEADS = 16
A_HEADS = 8
A_Q_RANK = 256
A_KV_RANK = 128
IDX_HEADS = 16
IDX_DIM = 64
DSA_TOPK = 256
B_HEADS = 8
B_GROUPS = 2
B_HPG = B_HEADS // B_GROUPS
CMP_LEN = 32
CMP_STRIDE = 16
SLC_BLOCK = 64
SLC_TOPN = 16
WINDOW = 512
C_HEADS = 16
MOBA_BLOCK = 256
MOBA_TOPK = 3
N_EXPERTS = 8
TOP_K = 2
EXPERT_ROWS = 256
DEPTH = 2
ALPHA = (2 * DEPTH) ** 0.25

LOG2E = 1.4426950408889634
QK_SCALE = HEAD_DIM ** -0.5 * LOG2E
NEG = -1e30
NEG_HALF = -5e29
INT_MIN = -2 ** 31
VMEM_LIMIT = 56 * 1024 * 1024


def _t5_thresholds():
    def bucket(n):
        if n < NUM_BUCKETS // 2:
            return n
        v = np.log(np.float32(n) / np.float32(NUM_BUCKETS // 2)) / np.float32(math.log(MAX_DISTANCE / (NUM_BUCKETS // 2)))
        return min(NUM_BUCKETS // 2 + int(np.float32(v) * (NUM_BUCKETS - NUM_BUCKETS // 2)), NUM_BUCKETS - 1)
    b = [bucket(i) for i in range(4 * MAX_DISTANCE)]
    return [0] + [min(i for i in range(len(b)) if b[i] >= k) for k in range(1, NUM_BUCKETS)]


T5_THR = _t5_thresholds()
T5_FAR = T5_THR[-1]


def _cparams(sem):
    return pltpu.CompilerParams(dimension_semantics=sem, vmem_limit_bytes=VMEM_LIMIT)


def _bias_kernel(tab_ref, off_ref, o_ref, *, c_row, c_col, h0, causal_neg, window):
    v = pl.program_id(0)
    h = pl.program_id(1) + h0
    shape = o_ref.shape[2:]
    dist = (c_col * lax.broadcasted_iota(I32, shape, 1) + c_row * lax.broadcasted_iota(I32, shape, 0) + off_ref[v])
    n = jnp.maximum(dist, 0)
    acc = jnp.full(shape, tab_ref[h] * LOG2E, F32)
    for k in range(1, NUM_BUCKETS):
        acc = jnp.where(n >= T5_THR[k], tab_ref[k * N_BIAS_HEADS + h] * LOG2E, acc)
    if causal_neg:
        acc = jnp.where(dist >= 0, acc, NEG)
    if window:
        acc = jnp.where(dist < window, acc, NEG)
    o_ref[0, 0] = acc


def bias_tiles(rel_bias, offs, n_heads, h0, rows, cols, c_row, c_col, causal_neg, window=0):
    offs = jnp.asarray(offs, I32)
    nv = offs.shape[0]
    return pl.pallas_call(
        functools.partial(_bias_kernel, c_row=c_row, c_col=c_col, h0=h0, causal_neg=causal_neg, window=window),
        grid=(nv, n_heads),
        in_specs=[pl.BlockSpec(memory_space=pltpu.SMEM), pl.BlockSpec(memory_space=pltpu.SMEM)],
        out_specs=pl.BlockSpec((1, 1, rows, cols), lambda v, h: (v, h, 0, 0)),
        out_shape=jax.ShapeDtypeStruct((nv, n_heads, rows, cols), F32),
        compiler_params=_cparams(("arbitrary", "arbitrary")),
        name="t5_bias_tiles",
    )(rel_bias.reshape(-1), offs)


def _layer_norm_rows(z, g, b):
    mu = jnp.mean(z, axis=-1, keepdims=True)
    zc = z - mu
    var = jnp.mean(zc * zc, axis=-1, keepdims=True)
    return zc * lax.rsqrt(var + 1e-5) * g + b


def _ffn_kernel(ge_ref, x_ref, w1_ref, w3_ref, w2_ref, g_ref, b_ref, o_ref, *, ff_chunk, with_ln):
    del ge_ref
    x = x_ref[...]
    xb = x.astype(BF16)
    d_ff = w1_ref.shape[2]
    acc = jnp.zeros((x.shape[0], w2_ref.shape[2]), F32)
    for c in range(0, d_ff, ff_chunk):
        a = jnp.dot(xb, w1_ref[0, :, c:c + ff_chunk], preferred_element_type=F32)
        u = jnp.dot(xb, w3_ref[0, :, c:c + ff_chunk], preferred_element_type=F32)
        hid = (a * jax.nn.sigmoid(a) * u).astype(BF16)
        acc = acc + jnp.dot(hid, w2_ref[0, c:c + ff_chunk, :], preferred_element_type=F32)
    if with_ln:
        o_ref[...] = _layer_norm_rows(ALPHA * x.astype(F32) + acc, g_ref[...], b_ref[...]).astype(o_ref.dtype)
    else:
        o_ref[...] = acc.astype(o_ref.dtype)


def swiglu_ffn(x_rows, grp_e, w1, w3, w2, ln_g, ln_b, *, with_ln, out_dtype, tm, ff_chunk):
    m, d = x_rows.shape
    d_ff = w1.shape[2]
    once = pl.Buffered(1)
    grid_spec = pltpu.PrefetchScalarGridSpec(
        num_scalar_prefetch=1,
        grid=(m // tm,),
        in_specs=[pl.BlockSpec((tm, d), lambda i, ge: (i, 0)),
                  pl.BlockSpec((1, d, d_ff), lambda i, ge: (ge[i], 0, 0), pipeline_mode=once),
                  pl.BlockSpec((1, d, d_ff), lambda i, ge: (ge[i], 0, 0), pipeline_mode=once),
                  pl.BlockSpec((1, d_ff, d), lambda i, ge: (ge[i], 0, 0), pipeline_mode=once),
                  pl.BlockSpec((1, d), lambda i, ge: (0, 0)), pl.BlockSpec((1, d), lambda i, ge: (0, 0))],
        out_specs=pl.BlockSpec((tm, d), lambda i, ge: (i, 0)),
    )
    return pl.pallas_call(
        functools.partial(_ffn_kernel, ff_chunk=ff_chunk, with_ln=with_ln),
        grid_spec=grid_spec,
        out_shape=jax.ShapeDtypeStruct((m, d), out_dtype),
        compiler_params=_cparams(("arbitrary",)),
        name="swiglu_ffn",
    )(grp_e, x_rows, w1, w3, w2, ln_g.reshape(1, d), ln_b.reshape(1, d))


def _flash_probs(s, m, segs=None):
    if segs is None:
        m_new = jnp.maximum(m, jnp.max(s, axis=0, keepdims=True))
        p = jnp.exp2(s - m_new)
    else:
        m_new, r0 = m, 0
        for n, c in segs:
            seg_max = jnp.max(s[r0:r0 + n], axis=0, keepdims=True)
            m_new = jnp.maximum(m_new, jnp.where(c > NEG_HALF, seg_max + c, NEG))
            r0 += n
        parts, r0 = [], 0
        for n, c in segs:
            shift = jnp.where(c > NEG_HALF, m_new - c, -NEG)
            parts.append(jnp.exp2(s[r0:r0 + n] - shift))
            r0 += n
        p = parts[0] if len(parts) == 1 else jnp.concatenate(parts, axis=0)
    return m_new, jnp.exp2(m - m_new), p.astype(BF16)


def _flash_merge(states):
    ms = [st[0][...] for st in states]
    m = functools.reduce(jnp.maximum, ms)
    ws = [jnp.exp2(mi - m) for mi in ms]
    l = sum(w * st[1][...] for w, st in zip(ws, states))
    acc = sum(w * st[2][...] for w, st in zip(ws, states))
    return m, l, acc


def _flash_finish(m, l, acc):
    return jnp.where(m > NEG_HALF, acc / l, 0.0)


def _rms_rows(x, g):
    return x * lax.rsqrt(jnp.mean(x * x, axis=-1, keepdims=True) + 1e-6) * g


EVEN_T = 256
SMALL_ROWS = 48
GATE_ROW0 = IDX_HEADS
NT_DIMS = (((1,), (1,)), ((), ()))


def _even_inproj_kernel(x_ref, wa_ref, wsT_ref, qn_ref, kvn_ref, wuq_ref, wuk_ref, wqiT_ref, wqbT_ref, wk4_ref, wvT_ref,
                        qidxT_ref, qlatT_ref, sT_ref, kidx_ref, ckv_ref, ckvT_ref,
                        qbT_ref, kcmp_ref, vcmp_ref, kslc_ref, kwin_ref, vT_ref):
    xb = x_ref[0].astype(BF16)
    ya = jnp.dot(xb, wa_ref[...], preferred_element_type=F32)
    cqn = _rms_rows(ya[:, :A_Q_RANK], qn_ref[...]).astype(BF16)
    ckvn = _rms_rows(ya[:, A_Q_RANK:A_Q_RANK + A_KV_RANK], kvn_ref[...])
    kidx_ref[0, 0] = ya[:, A_Q_RANK + A_KV_RANK:A_Q_RANK + A_KV_RANK + IDX_DIM].astype(BF16)
    ckv_ref[0, 0] = ckvn.astype(BF16)
    ckvT_ref[0, 0] = ckvn.T.astype(BF16)
    sT_ref[0] = lax.dot_general(wsT_ref[...], xb, NT_DIMS, preferred_element_type=F32)
    q = jnp.dot(cqn, wuq_ref[...], preferred_element_type=F32).astype(BF16)
    for h in range(A_HEADS):
        qlT = lax.dot_general(wuk_ref[h], q[:, h * HEAD_DIM:(h + 1) * HEAD_DIM], NT_DIMS, preferred_element_type=F32)
        qlatT_ref[0, h * A_KV_RANK:(h + 1) * A_KV_RANK, :] = (qlT * QK_SCALE).astype(BF16)
    qidxT_ref[0] = lax.dot_general(wqiT_ref[...], cqn, NT_DIMS, preferred_element_type=F32).astype(BF16)
    qbT_ref[0] = lax.dot_general(wqbT_ref[...], xb, NT_DIMS, preferred_element_type=F32).astype(BF16)
    yk = jnp.dot(xb, wk4_ref[...], preferred_element_type=F32).astype(BF16)
    gd = B_GROUPS * HEAD_DIM
    for j, ref in enumerate((kcmp_ref, vcmp_ref, kslc_ref, kwin_ref)):
        ref[0] = yk[:, j * gd:(j + 1) * gd]
    vT = lax.dot_general(wvT_ref[...], xb, NT_DIMS, preferred_element_type=F32).astype(BF16)
    for j in range(EVEN_T // NSA_KT):
        vT_ref[0, j] = vT[:, j * NSA_KT:(j + 1) * NSA_KT]


def even_inproj(x3, w_in, q_norm, kv_norm, w_uq, w_uk, w_qidx):
    bsz, seq, d = x3.shape
    T = EVEN_T
    nq = seq // T
    gd = B_GROUPS * HEAD_DIM
    n_kt = seq // NSA_KT
    o_kidx = A_Q_RANK + A_KV_RANK
    o_widx = o_kidx + IDX_DIM
    o_qb = o_widx + IDX_HEADS
    o_kv = o_qb + B_HEADS * HEAD_DIM
    o_gate = o_kv + 6 * gd
    kv = lambda j: w_in[:, o_kv + j * gd:o_kv + (j + 1) * gd]
    wa = jnp.pad(w_in[:, :o_widx], ((0, 0), (0, 512 - o_widx))).astype(BF16)
    w_gate = w_in[:, o_gate:].reshape(d, B_GROUPS, B_HPG, 3).transpose(0, 1, 3, 2).reshape(d, 3 * B_HEADS)
    wsT = jnp.concatenate([w_in[:, o_widx:o_qb] * IDX_HEADS ** -0.5, w_gate,
                           jnp.zeros((d, SMALL_ROWS - IDX_HEADS - 3 * B_HEADS), w_in.dtype)], axis=1).T.astype(BF16)
    wuq = w_uq.reshape(A_Q_RANK, A_HEADS * HEAD_DIM).astype(BF16)
    wuk = jnp.transpose(w_uk, (1, 0, 2)).astype(BF16)
    wqiT = w_qidx.reshape(A_Q_RANK, IDX_HEADS * IDX_DIM).T.astype(BF16)
    wqbT = (w_in[:, o_qb:o_kv] * QK_SCALE).T.astype(BF16)
    wk4 = jnp.concatenate([kv(0), kv(1), kv(2), kv(4)], axis=1).astype(BF16)
    wvT = jnp.concatenate([kv(3), kv(5)], axis=1).T.astype(BF16)
    weights = (wa, wsT, q_norm.reshape(1, -1), kv_norm.reshape(1, -1), wuq, wuk, wqiT, wqbT, wk4, wvT)
    once = pl.Buffered(1)
    w_specs = [pl.BlockSpec(w.shape, (lambda b, i, n=w.ndim: (0,) * n), pipeline_mode=once) for w in weights]
    fm = lambda rows: pl.BlockSpec((1, rows, T), lambda b, i: (b, 0, i))
    tok = lambda cols: pl.BlockSpec((1, T, cols), lambda b, i: (b, i, 0))
    blk = lambda r, c: pl.BlockSpec((1, 1, r, c), lambda b, i: (b, i, 0, 0))
    sds = jax.ShapeDtypeStruct
    return pl.pallas_call(
        _even_inproj_kernel,
        grid=(bsz, nq),
        in_specs=[pl.BlockSpec((1, T, d), lambda b, i: (b, i, 0))] + w_specs,
        out_specs=[fm(IDX_HEADS * IDX_DIM), fm(A_HEADS * A_KV_RANK), fm(SMALL_ROWS),
                   blk(T, IDX_DIM), blk(T, A_KV_RANK), blk(A_KV_RANK, T),
                   fm(B_HEADS * HEAD_DIM), tok(gd), tok(gd), tok(gd), tok(gd),
                   pl.BlockSpec((1, T // NSA_KT, 2 * gd, NSA_KT), lambda b, i: (b, i, 0, 0))],
        out_shape=[sds((bsz, IDX_HEADS * IDX_DIM, seq), BF16), sds((bsz, A_HEADS * A_KV_RANK, seq), BF16),
                   sds((bsz, SMALL_ROWS, seq), F32),
                   sds((bsz, nq, T, IDX_DIM), BF16), sds((bsz, nq, T, A_KV_RANK), BF16), sds((bsz, nq, A_KV_RANK, T), BF16),
                   sds((bsz, B_HEADS * HEAD_DIM, seq), BF16),
                   sds((bsz, seq, gd), BF16), sds((bsz, seq, gd), BF16), sds((bsz, seq, gd), BF16), sds((bsz, seq, gd), BF16),
                   sds((bsz, n_kt, 2 * gd, NSA_KT), BF16)],
        compiler_params=_cparams(("arbitrary", "arbitrary")),
        name="even_inproj",
    )(x3, *weights)


DSA_T = 256
SUB = 128


def _dsa_kernel(qidx_ref, wT_ref, qlat_ref, kidx_ref, ckv_ref, ckvT_ref, bias_ref, wuvt_ref, o_ref,
                key_ref, selb0_ref, selb1_ref, *state_refs, n_keep):
    selb_refs = (selb0_ref, selb1_ref)
    m_refs, l_refs, acc_refs = (state_refs[0:A_HEADS], state_refs[A_HEADS:2 * A_HEADS], state_refs[2 * A_HEADS:])
    qi = pl.program_id(1)
    nkb = qi + 1
    T = DSA_T

    def score_block(kb, carry):
        for sub in range(T // SUB):
            k = kidx_ref[0, kb, sub * SUB:(sub + 1) * SUB, :]
            acc = jnp.zeros((SUB, T), F32)
            for h in range(IDX_HEADS):
                d = jnp.dot(k, qidx_ref[0, h * IDX_DIM:(h + 1) * IDX_DIM, :], preferred_element_type=F32)
                acc = acc + jnp.maximum(d, 0.0) * wT_ref[0, h:h + 1, :]
            bits = lax.bitcast_convert_type(acc, I32)
            key = bits ^ (lax.shift_right_arithmetic(bits, 31) & 0x7FFFFFFF)
            s_pos = kb * T + sub * SUB + lax.broadcasted_iota(I32, (SUB, T), 0)
            t_pos = qi * T + lax.broadcasted_iota(I32, (SUB, T), 1)
            key = jnp.where(s_pos <= t_pos, key, INT_MIN)
            key_ref[pl.ds(pl.multiple_of(kb * T + sub * SUB, SUB), SUB), :] = key
        return carry

    lax.fori_loop(0, nkb, score_block, 0)

    def count_ge(cand):
        def body(kb, cnt):
            blk = key_ref[pl.ds(pl.multiple_of(kb * T, T), T), :]
            ge = jnp.where(blk >= cand, 1, 0).astype(I32)
            return cnt + jnp.sum(ge.reshape(T // 8, 8, T), axis=0)
        cnt = lax.fori_loop(0, nkb, body, jnp.zeros((8, T), I32))
        return jnp.sum(cnt, axis=0, keepdims=True)

    def bit_step(i, u):
        cand_u = u | lax.shift_left(jnp.int32(1), 31 - i)
        cnt = count_ge(cand_u ^ INT_MIN)
        return jnp.where(cnt >= n_keep, cand_u, u)

    u = lax.fori_loop(0, 32, bit_step, jnp.zeros((1, T), I32))
    thr = jnp.maximum(u ^ INT_MIN, INT_MIN + 1)

    for h in range(A_HEADS):
        m_refs[h][...] = jnp.full(m_refs[h].shape, NEG, F32)
        l_refs[h][...] = jnp.zeros(l_refs[h].shape, F32)
        acc_refs[h][...] = jnp.zeros(acc_refs[h].shape, F32)

    states = list(zip(m_refs, l_refs, acc_refs))
    far_bias = [bias_ref[2, h, 0:1, :] for h in range(A_HEADS)]

    def masked_scores(kb, selb):
        selb[...] = jnp.where(key_ref[pl.ds(pl.multiple_of(kb * T, T), T), :] >= thr, 0.0, NEG)
        ckv = ckv_ref[0, kb]
        return [jnp.dot(ckv, qlat_ref[0, h * A_KV_RANK:(h + 1) * A_KV_RANK, :], preferred_element_type=F32) + selb[...]
                for h in range(A_HEADS)]

    def far_body(i, carry):
        tiles = []
        for u, selb in enumerate(selb_refs):
            kb_raw = len(selb_refs) * i + u
            live = kb_raw < n_far
            kb = jnp.minimum(kb_raw, n_far - 1)
            segs = [[(T, jnp.where(live, far_bias[h], NEG))] for h in range(A_HEADS)]
            tiles.append((list(zip(masked_scores(kb, selb), segs)), [ckvT_ref[0, kb]] * A_HEADS))
        for s_all, v_all in tiles:
            _flash_step(s_all, v_all, states)
        return carry

    def near_body(kb, carry):
        s_all = [s + bias_ref[qi - kb, h] for h, s in enumerate(masked_scores(kb, selb_refs[0]))]
        _flash_step(s_all, [ckvT_ref[0, kb]] * A_HEADS, states)
        return carry

    n_far = jnp.maximum(qi - 1, 0)
    lax.fori_loop(0, lax.div(n_far + (len(selb_refs) - 1), jnp.int32(len(selb_refs))), far_body, 0)
    lax.fori_loop(n_far, nkb, near_body, 0)

    for h in range(A_HEADS):
        o_lat = _flash_finish(m_refs[h][...], l_refs[h][...], acc_refs[h][...]).astype(BF16)
        o_ref[0, h * HEAD_DIM:(h + 1) * HEAD_DIM, :] = jnp.dot(
            wuvt_ref[h], o_lat, preferred_element_type=F32).astype(o_ref.dtype)


def dsa_attention(qidxT, sT, qlatT, kidx, ckv, ckvT, w_uv, bias3):
    T = DSA_T
    assert T == EVEN_T
    bsz, nq = kidx.shape[0], kidx.shape[1]
    seq = nq * T
    n_keep = min(DSA_TOPK, seq // 4)
    wuvt = jnp.transpose(w_uv, (1, 2, 0)).astype(BF16)
    return pl.pallas_call(
        functools.partial(_dsa_kernel, n_keep=n_keep),
        grid=(bsz, nq),
        in_specs=[pl.BlockSpec((1, IDX_HEADS * IDX_DIM, T), lambda b, i: (b, 0, i)),
                  pl.BlockSpec((1, SMALL_ROWS, T), lambda b, i: (b, 0, i)),
                  pl.BlockSpec((1, A_HEADS * A_KV_RANK, T), lambda b, i: (b, 0, i)),
                  pl.BlockSpec((1, nq, T, IDX_DIM), lambda b, i: (b, 0, 0, 0)),
                  pl.BlockSpec((1, nq, T, A_KV_RANK), lambda b, i: (b, 0, 0, 0)),
                  pl.BlockSpec((1, nq, A_KV_RANK, T), lambda b, i: (b, 0, 0, 0)),
                  pl.BlockSpec((3, A_HEADS, T, T), lambda b, i: (0, 0, 0, 0)),
                  pl.BlockSpec((A_HEADS, HEAD_DIM, A_KV_RANK), lambda b, i: (0, 0, 0))],
        out_specs=pl.BlockSpec((1, A_HEADS * HEAD_DIM, T), lambda b, i: (b, 0, i)),
        out_shape=jax.ShapeDtypeStruct((bsz, A_HEADS * HEAD_DIM, seq), BF16),
        scratch_shapes=([pltpu.VMEM((seq, T), I32), pltpu.VMEM((T, T), F32), pltpu.VMEM((T, T), F32)]
                        + [pltpu.VMEM((1, T), F32)] * (2 * A_HEADS)
                        + [pltpu.VMEM((A_KV_RANK, T), F32)] * A_HEADS),
        compiler_params=_cparams(("arbitrary", "arbitrary")),
        name="dsa_attention",
    )(qidxT, sT, qlatT, kidx, ckv, ckvT, bias3, wuvt)


def dsa_bias_tiles(rel_bias):
    assert DSA_T + 1 >= T5_FAR
    return bias_tiles(rel_bias, [0, DSA_T, 4 * DSA_T], A_HEADS, 0, DSA_T, DSA_T, -1, 1, False)


N_CMP_PAD = 256


def _compress_kernel(blk_ref, pos_ref, w1_ref, w2_ref, o_ref):
    x = (blk_ref[0].astype(F32) + pos_ref[...]).astype(BF16)
    hid = jax.nn.gelu(jnp.dot(x, w1_ref[...], preferred_element_type=F32))
    o_ref[0] = jnp.dot(hid.astype(BF16), w2_ref[...], preferred_element_type=F32).astype(o_ref.dtype)


def nsa_compress(a, pos, w1, w2, bsz, seq):
    n_chunk = seq // CMP_STRIDE
    assert CMP_LEN == 2 * CMP_STRIDE and n_chunk <= N_CMP_PAD
    width = CMP_STRIDE * HEAD_DIM
    chunks = a.reshape(bsz, n_chunk, CMP_STRIDE, B_GROUPS, HEAD_DIM).transpose(0, 3, 1, 2, 4)
    chunks = chunks.reshape(bsz * B_GROUPS, n_chunk, width)
    blocks = jnp.concatenate([chunks[:, :-1], chunks[:, 1:]], axis=-1)
    blocks = jnp.pad(blocks, ((0, 0), (0, N_CMP_PAD - (n_chunk - 1)), (0, 0)))
    out = pl.pallas_call(
        _compress_kernel,
        grid=(bsz * B_GROUPS,),
        in_specs=[pl.BlockSpec((1, N_CMP_PAD, 2 * width), lambda i: (i, 0, 0)),
                  pl.BlockSpec((1, 2 * width), lambda i: (0, 0)),
                  pl.BlockSpec((2 * width, HEAD_DIM), lambda i: (0, 0)),
                  pl.BlockSpec((HEAD_DIM, HEAD_DIM), lambda i: (0, 0))],
        out_specs=pl.BlockSpec((1, N_CMP_PAD, HEAD_DIM), lambda i: (i, 0, 0)),
        out_shape=jax.ShapeDtypeStruct((bsz * B_GROUPS, N_CMP_PAD, HEAD_DIM), BF16),
        compiler_params=_cparams(("arbitrary",)),
        name="nsa_compress",
    )(blocks, pos.reshape(1, 2 * width), w1.reshape(2 * width, HEAD_DIM).astype(BF16), w2.astype(BF16))
    return out.reshape(bsz, B_GROUPS, N_CMP_PAD, HEAD_DIM)


NSA_TQ = 128
NSA_L = B_HPG * NSA_TQ
NSA_KT = 128
NSA_SLC_REL = 3
NSA_WIN_REL = 5
NSA_FAR_SPLIT = 4


def _flash_step(s_all, vT_all, states):
    probs = []
    for item, (m_ref, _, _) in zip(s_all, states):
        s, segs = item if isinstance(item, tuple) else (item, None)
        m_new, alpha, p = _flash_probs(s, m_ref[...], segs)
        m_ref[...] = m_new
        probs.append((alpha, p))
    for (alpha, p), vT, (_, l_ref, acc_ref) in zip(probs, vT_all, states):
        d = vT.shape[0]
        ones = jnp.ones((BF16_ROWS, vT.shape[1]), BF16)
        pv = jnp.dot(jnp.concatenate([vT, ones], axis=0), p, preferred_element_type=F32)
        acc_ref[...] = alpha * acc_ref[...] + pv[:d]
        l_ref[...] = alpha * l_ref[...] + pv[d:d + 1]


def _flash_loop(lo, hi, scores, values, states, segs, unroll=2):
    def body(i, carry):
        tiles = []
        for u in range(unroll):
            j_raw = lo + unroll * i + u
            live = j_raw < hi
            j = jnp.minimum(j_raw, hi - 1)
            sg = [[(n, jnp.where(live, c, NEG)) for n, c in chain] for chain in segs(j)]
            tiles.append((list(zip(scores(j), sg)), values(j)))
        for s_all, v_all in tiles:
            _flash_step(s_all, v_all, states)
        return carry

    lax.fori_loop(0, lax.div(hi - lo + (unroll - 1), jnp.int32(unroll)), body, 0)


def _nsa_kernel(qT_ref, kc_ref, vcT_ref, biasc_ref, ovl_ref, ks_ref, kw_ref, vT_ref,
                toes_ref, toew_ref, sT_ref, o_ref, selb_ref, *st, n_cmp, n_sel, n_slc):
    qi = pl.program_id(1)
    TQ, L = NSA_TQ, NSA_L
    q0 = qi * TQ
    qTs, qTs_pad = [], []
    for g in range(B_GROUPS):
        q = jnp.concatenate([qT_ref[0, (g * B_HPG + n) * HEAD_DIM:(g * B_HPG + n + 1) * HEAD_DIM, :]
                             for n in range(B_HPG)], axis=1)
        parts = [jnp.zeros_like(q)] * B_GROUPS
        parts[g] = q
        qTs.append(q)
        qTs_pad.append(jnp.concatenate(parts, axis=0))
    t_lane = q0 + (lax.broadcasted_iota(I32, (1, L), 1) & (TQ - 1))

    o_cs = []
    for g in range(B_GROUPS):
        s = jnp.dot(kc_ref[0, g], qTs[g], preferred_element_type=F32) + biasc_ref[g, 0]
        i_idx = lax.broadcasted_iota(I32, (N_CMP_PAD, L), 0)
        valid = jnp.where(i_idx < n_cmp, i_idx * CMP_STRIDE + (CMP_LEN - 1), 2 ** 30) <= t_lane
        s = jnp.where(valid, s, NEG)
        m = jnp.max(s, axis=0, keepdims=True)
        p = jnp.where(valid, jnp.exp2(s - m), 0.0)
        l = jnp.sum(p, axis=0, keepdims=True)
        p_c = p / jnp.where(l > 0, l, 1.0)
        o_c = jnp.dot(vcT_ref[0, g], p_c.astype(BF16), preferred_element_type=F32)

        psum = p_c[:, 0:TQ]
        for n in range(1, B_HPG):
            psum = psum + p_c[:, n * TQ:(n + 1) * TQ]
        sc = jnp.dot(ovl_ref[...], psum, preferred_element_type=F32, precision=lax.Precision.HIGHEST)
        j_idx = lax.broadcasted_iota(I32, (n_slc, TQ), 0)
        cur = (q0 + lax.broadcasted_iota(I32, (1, TQ), 1)) // SLC_BLOCK
        adm = j_idx <= cur
        forced = (j_idx == 0) | (j_idx == cur) | (j_idx == cur - 1)
        scv = jnp.where(adm, jnp.where(forced, jnp.inf, sc), -jnp.inf)
        rank = jnp.zeros((n_slc, TQ), I32)
        for jp in range(n_slc):
            row = scv[jp:jp + 1, :]
            beats = jnp.where(row > scv, 1, jnp.where((row == scv) & (jp < j_idx), 1, 0))
            rank = rank + beats
        selb = jnp.where(rank < n_sel, 0.0, NEG).astype(F32)
        selb4 = jnp.concatenate([selb] * B_HPG, axis=1)
        for j in range(n_slc):
            selb_ref[g, j] = selb4[j:j + 1, :]
        o_cs.append(o_c)

    for ref in st[0::3]:
        ref[...] = jnp.full(ref.shape, NEG, F32)
    for ref in st[1::3] + st[2::3]:
        ref[...] = jnp.zeros(ref.shape, F32)
    slc_st = [st[6 * g:6 * g + 3] for g in range(B_GROUPS)]
    win_st = [st[6 * g + 3:6 * g + 6] for g in range(B_GROUPS)]
    n_main = 6 * B_GROUPS
    xtr_st = [[st[n_main + 3 * (g * (NSA_FAR_SPLIT - 1) + r):n_main + 3 * (g * (NSA_FAR_SPLIT - 1) + r) + 3]
               for r in range(NSA_FAR_SPLIT - 1)] for g in range(B_GROUPS)]
    per_kt = NSA_KT // SLC_BLOCK

    groups = range(B_GROUPS)
    far_bias = [toes_ref[g, NSA_SLC_REL - 1, 0:1, :] for g in groups]

    def slc_scores(g, jt, near):
        s = jnp.dot(ks_ref[0, jt], qTs_pad[g], preferred_element_type=F32)
        return s + toes_ref[g, jnp.minimum(qi - jt, NSA_SLC_REL - 1)] if near else s

    def slc_segs(g, jt, near):
        return [(SLC_BLOCK, selb_ref[g, per_kt * jt + r] + (0.0 if near else far_bias[g])) for r in range(per_kt)]

    def win_scores(g, jt):
        rel = jnp.minimum(qi - jt, NSA_WIN_REL - 1)
        return jnp.dot(kw_ref[0, jt], qTs_pad[g], preferred_element_type=F32) + toew_ref[g, rel]

    gd = B_GROUPS * HEAD_DIM
    v_slc = lambda g, jt: vT_ref[0, jt, g * HEAD_DIM:(g + 1) * HEAD_DIM, :]
    v_win = lambda g, jt: vT_ref[0, jt, gd + g * HEAD_DIM:gd + (g + 1) * HEAD_DIM, :]

    assert NSA_WIN_REL >= NSA_SLC_REL
    j_lo = jnp.maximum(qi - (NSA_WIN_REL - 1), 0)

    def far_body(i, carry):
        s_all, v_all, chains = [], [], []
        for r in range(NSA_FAR_SPLIT):
            jt_raw = NSA_FAR_SPLIT * i + r
            live = jt_raw < j_lo
            jt = jnp.minimum(jt_raw, j_lo - 1)
            for g in groups:
                segs = [(n, jnp.where(live, c, NEG)) for n, c in slc_segs(g, jt, False)]
                s_all.append((slc_scores(g, jt, False), segs))
                v_all.append(v_slc(g, jt))
                chains.append(slc_st[g] if r == 0 else xtr_st[g][r - 1])
        _flash_step(s_all, v_all, chains)
        return carry

    zero_row = jnp.zeros((1, L), F32)
    lax.fori_loop(0, lax.div(j_lo + (NSA_FAR_SPLIT - 1), jnp.int32(NSA_FAR_SPLIT)), far_body, 0)
    _flash_loop(j_lo, qi + 1,
                lambda jt: [slc_scores(g, jt, True) for g in groups] + [win_scores(g, jt) for g in groups],
                lambda jt: [v_slc(g, jt) for g in groups] + [v_win(g, jt) for g in groups],
                slc_st + win_st,
                lambda jt: [slc_segs(g, jt, True) for g in groups] + [[(NSA_KT, zero_row)] for g in groups])

    for g in range(B_GROUPS):
        o_s = _flash_finish(*_flash_merge([slc_st[g]] + xtr_st[g]))
        o_w = _flash_finish(*[r[...] for r in win_st[g]])
        row0 = GATE_ROW0 + g * 3 * B_HPG
        gate = [jax.nn.sigmoid(jnp.concatenate([sT_ref[0, row0 + j * B_HPG + n:row0 + j * B_HPG + n + 1, :]
                                                for n in range(B_HPG)], axis=1)) for j in range(3)]
        o = (gate[0] * o_cs[g] + gate[1] * o_s + gate[2] * o_w).astype(o_ref.dtype)
        for n in range(B_HPG):
            o_ref[0, (g * B_HPG + n) * HEAD_DIM:(g * B_HPG + n + 1) * HEAD_DIM, :] = o[:, n * TQ:(n + 1) * TQ]


def nsa_bias_inputs(rel_bias, seq):
    TQ, L, KT = NSA_TQ, NSA_L, NSA_KT
    nq = seq // TQ
    bc = bias_tiles(rel_bias, [-(CMP_LEN - 1)], B_HEADS, A_HEADS, N_CMP_PAD, seq, -CMP_STRIDE, 1, False, 0)
    bc = bc.reshape(B_GROUPS, B_HPG, N_CMP_PAD, nq, TQ).transpose(0, 3, 2, 1, 4).reshape(B_GROUPS, nq, N_CMP_PAD, L)

    def lanes(t):
        v = t.shape[0]
        return t.reshape(v, B_GROUPS, B_HPG, KT, TQ).transpose(1, 0, 3, 2, 4).reshape(B_GROUPS, v, KT, L)

    assert KT == TQ and (NSA_SLC_REL - 1) * KT - (KT - 1) >= T5_FAR
    toe_s = bias_tiles(rel_bias, [v * KT for v in range(NSA_SLC_REL - 1)] + [64 * KT],
                       B_HEADS, A_HEADS, KT, TQ, -1, 1, True, 0)
    assert (NSA_WIN_REL - 1) * KT - (KT - 1) < WINDOW <= NSA_WIN_REL * KT - (KT - 1)
    toe_w = bias_tiles(rel_bias, [v * KT for v in range(NSA_WIN_REL)], B_HEADS, A_HEADS, KT, TQ, -1, 1, True, WINDOW)
    return bc, lanes(toe_s), lanes(toe_w)


def nsa_overlap(seq):
    n_cmp = (seq - CMP_LEN) // CMP_STRIDE + 1
    n_slc = seq // SLC_BLOCK
    cs = np.arange(N_CMP_PAD) * CMP_STRIDE
    ss = np.arange(n_slc) * SLC_BLOCK
    ov = ((cs[None, :] + CMP_LEN - 1 >= ss[:, None]) & (cs[None, :] <= ss[:, None] + SLC_BLOCK - 1)
          & (np.arange(N_CMP_PAD)[None, :] < n_cmp))
    return jnp.asarray(ov.astype(np.float32))


def nsa_attention(qbT, kc, vc, kslc, kwin, vT, sT, biasc, toe_s, toe_w):
    TQ, L, KT, G = NSA_TQ, NSA_L, NSA_KT, B_GROUPS
    bsz, n_kt = vT.shape[0], vT.shape[1]
    seq = n_kt * KT
    nq = seq // TQ
    n_slc = seq // SLC_BLOCK
    n_cmp = (seq - CMP_LEN) // CMP_STRIDE + 1
    n_sel = min(SLC_TOPN, n_slc)
    gd = G * HEAD_DIM
    vcT = vc.transpose(0, 1, 3, 2)
    once = pl.Buffered(1)
    k_spec = pl.BlockSpec((1, n_kt, KT, gd), lambda b, i: (b, 0, 0, 0))
    n_chain = 2 * G + G * (NSA_FAR_SPLIT - 1)
    return pl.pallas_call(
        functools.partial(_nsa_kernel, n_cmp=n_cmp, n_sel=n_sel, n_slc=n_slc),
        grid=(bsz, nq),
        in_specs=[pl.BlockSpec((1, B_HEADS * HEAD_DIM, TQ), lambda b, i: (b, 0, i)),
                  pl.BlockSpec((1, G, N_CMP_PAD, HEAD_DIM), lambda b, i: (b, 0, 0, 0)),
                  pl.BlockSpec((1, G, HEAD_DIM, N_CMP_PAD), lambda b, i: (b, 0, 0, 0)),
                  pl.BlockSpec((G, 1, N_CMP_PAD, L), lambda b, i: (0, i, 0, 0)),
                  pl.BlockSpec((n_slc, N_CMP_PAD), lambda b, i: (0, 0), pipeline_mode=once),
                  k_spec, k_spec,
                  pl.BlockSpec((1, n_kt, 2 * gd, KT), lambda b, i: (b, 0, 0, 0)),
                  pl.BlockSpec((G, NSA_SLC_REL, KT, L), lambda b, i: (0, 0, 0, 0), pipeline_mode=once),
                  pl.BlockSpec((G, NSA_WIN_REL, KT, L), lambda b, i: (0, 0, 0, 0), pipeline_mode=once),
                  pl.BlockSpec((1, SMALL_ROWS, TQ), lambda b, i: (b, 0, i))],
        out_specs=pl.BlockSpec((1, B_HEADS * HEAD_DIM, TQ), lambda b, i: (b, 0, i)),
        out_shape=jax.ShapeDtypeStruct((bsz, B_HEADS * HEAD_DIM, seq), BF16),
        scratch_shapes=([pltpu.VMEM((G, n_slc, 1, L), F32)]
                        + [pltpu.VMEM((1, L), F32), pltpu.VMEM((1, L), F32), pltpu.VMEM((HEAD_DIM, L), F32)] * n_chain),
        compiler_params=_cparams(("arbitrary", "arbitrary")),
        name="nsa_attention",
    )(qbT, kc, vcT, biasc, nsa_overlap(seq), kslc.reshape(bsz, n_kt, KT, gd), kwin.reshape(bsz, n_kt, KT, gd),
      vT, toe_s, toe_w, sT)


MOBA_T = MOBA_BLOCK


MOBA_HB = 16


PAIR = 2 * HEAD_DIM


def _moba_inproj_kernel(x_ref, wqT_ref, wk_ref, wvT_ref, qT_ref, k_ref, vT_ref):
    xb = x_ref[0].astype(BF16)
    nt = (((1,), (1,)), ((), ()))
    qT_ref[0] = lax.dot_general(wqT_ref[...], xb, nt, preferred_element_type=F32).astype(BF16)
    k_ref[0, 0] = jnp.dot(xb, wk_ref[...], preferred_element_type=F32).astype(BF16)
    vT_ref[0, 0] = lax.dot_general(wvT_ref[...], xb, nt, preferred_element_type=F32).astype(BF16)


def _pair_padded_qT(wq):
    n_heads = wq.shape[1] // HEAD_DIM
    wT = wq.T.reshape(n_heads, HEAD_DIM, wq.shape[0])
    z = jnp.zeros_like(wT)
    even = jnp.concatenate([wT, z], axis=1)
    odd = jnp.concatenate([z, wT], axis=1)
    is_even = (jnp.arange(n_heads) % 2 == 0)[:, None, None]
    return jnp.where(is_even, even, odd).reshape(n_heads * PAIR, wq.shape[0])


def moba_inproj(x3, w_in):
    bsz, seq, d = x3.shape
    T = MOBA_T
    n_blk = seq // T
    hd = C_HEADS * HEAD_DIM
    wqT = _pair_padded_qT(w_in[:, :hd] * QK_SCALE).astype(BF16)
    wk = w_in[:, hd:2 * hd].astype(BF16)
    wvT = w_in[:, 2 * hd:].T.astype(BF16)
    once = pl.Buffered(1)
    return pl.pallas_call(
        _moba_inproj_kernel,
        grid=(bsz, n_blk),
        in_specs=[pl.BlockSpec((1, T, d), lambda b, i: (b, i, 0)),
                  pl.BlockSpec(wqT.shape, lambda b, i: (0, 0), pipeline_mode=once),
                  pl.BlockSpec(wk.shape, lambda b, i: (0, 0), pipeline_mode=once),
                  pl.BlockSpec(wvT.shape, lambda b, i: (0, 0), pipeline_mode=once)],
        out_specs=[pl.BlockSpec((1, C_HEADS * PAIR, T), lambda b, i: (b, 0, i)),
                   pl.BlockSpec((1, 1, T, hd), lambda b, i: (b, i, 0, 0)),
                   pl.BlockSpec((1, 1, hd, T), lambda b, i: (b, i, 0, 0))],
        out_shape=[jax.ShapeDtypeStruct((bsz, C_HEADS * PAIR, seq), BF16),
                   jax.ShapeDtypeStruct((bsz, n_blk, T, hd), BF16),
                   jax.ShapeDtypeStruct((bsz, n_blk, hd, T), BF16)],
        compiler_params=_cparams(("arbitrary", "arbitrary")),
        name="moba_inproj",
    )(x3, wqT, wk, wvT)


def _moba_kernel(qT_ref, k_ref, vT_ref, bias_ref, o_ref, kmean_ref, selb_ref, *st, n_sel):
    qi = pl.program_id(2)
    T = MOBA_T
    n_blk = k_ref.shape[1]
    states = [st[3 * hh:3 * hh + 3] for hh in range(MOBA_HB)]

    @pl.when(qi == 0)
    def _():
        for j in range(n_blk):
            kmean_ref[j:j + 1, :] = jnp.mean(k_ref[0, j].astype(F32), axis=0, keepdims=True)

    qTs = [qT_ref[0, hh * PAIR:(hh + 1) * PAIR, :] for hh in range(MOBA_HB)]
    pair = lambda hh: slice((hh // 2) * PAIR, (hh // 2 + 1) * PAIR)
    j_idx = lax.broadcasted_iota(I32, (n_blk, T), 0)
    for hh in range(MOBA_HB):
        gate = jnp.dot(kmean_ref[:, pair(hh)], qTs[hh].astype(F32), preferred_element_type=F32,
                       precision=lax.Precision.HIGHEST)
        gv = jnp.where(j_idx < qi, gate, -jnp.inf)
        rank = jnp.zeros((n_blk, T), I32)
        for jp in range(n_blk):
            row = gv[jp:jp + 1, :]
            rank = rank + jnp.where(row > gv, 1, jnp.where((row == gv) & (jp < j_idx), 1, 0))
        selb = jnp.where(j_idx < qi, jnp.where(rank < n_sel, 0.0, NEG),
                         jnp.where(j_idx == qi, 0.0, NEG)).astype(F32)
        for j in range(n_blk):
            selb_ref[hh, j] = selb[j:j + 1, :]
        m_ref, l_ref, acc_ref = states[hh]
        m_ref[...] = jnp.full(m_ref.shape, NEG, F32)
        l_ref[...] = jnp.zeros(l_ref.shape, F32)
        acc_ref[...] = jnp.zeros(acc_ref.shape, F32)

    heads = range(MOBA_HB)
    far_bias = [bias_ref[2, hh, 0:1, :] for hh in heads]

    def far_scores(kb):
        return [jnp.dot(k_ref[0, kb, :, pair(hh)], qTs[hh], preferred_element_type=F32) for hh in heads]

    def far_segs(kb):
        return [[(T, selb_ref[hh, kb] + far_bias[hh])] for hh in heads]

    def near_scores(kb):
        return [jnp.dot(k_ref[0, kb, :, pair(hh)], qTs[hh], preferred_element_type=F32) + bias_ref[qi - kb, hh]
                for hh in heads]

    def near_segs(kb):
        return [[(T, selb_ref[hh, kb])] for hh in heads]

    def values(kb):
        return [vT_ref[0, kb, hh * HEAD_DIM:(hh + 1) * HEAD_DIM, :] for hh in heads]

    n_far = jnp.maximum(qi - 1, 0)
    _flash_loop(0, n_far, far_scores, values, states, far_segs)
    _flash_loop(n_far, qi + 1, near_scores, values, states, near_segs)
    for hh in range(MOBA_HB):
        o_ref[0, hh * HEAD_DIM:(hh + 1) * HEAD_DIM, :] = _flash_finish(*[r[...] for r in states[hh]]).astype(o_ref.dtype)


def moba_bias_tiles(rel_bias):
    assert MOBA_T + 1 >= T5_FAR
    t0 = bias_tiles(rel_bias, [0], C_HEADS, 0, MOBA_T, MOBA_T, -1, 1, True)
    t12 = bias_tiles(rel_bias, [MOBA_T, 4 * MOBA_T], C_HEADS, 0, MOBA_T, MOBA_T, -1, 1, False)
    return jnp.concatenate([t0, t12], axis=0)


def moba_attention(qT, k, vT, bias3):
    T = MOBA_T
    bsz, n_blk = k.shape[0], k.shape[1]
    seq = n_blk * T
    n_sel = min(MOBA_TOPK, n_blk - 1)
    HB = MOBA_HB
    assert HB % 2 == 0
    once = pl.Buffered(1)
    out = pl.pallas_call(
        functools.partial(_moba_kernel, n_sel=n_sel),
        grid=(bsz, C_HEADS // HB, n_blk),
        in_specs=[pl.BlockSpec((1, HB * PAIR, T), lambda b, h, i: (b, h, i)),
                  pl.BlockSpec((1, n_blk, T, HB * HEAD_DIM), lambda b, h, i: (b, 0, 0, h), pipeline_mode=once),
                  pl.BlockSpec((1, n_blk, HB * HEAD_DIM, T), lambda b, h, i: (b, 0, h, 0), pipeline_mode=once),
                  pl.BlockSpec((3, HB, T, T), lambda b, h, i: (0, h, 0, 0), pipeline_mode=once)],
        out_specs=pl.BlockSpec((1, HB * HEAD_DIM, T), lambda b, h, i: (b, h, i)),
        out_shape=jax.ShapeDtypeStruct((bsz, C_HEADS * HEAD_DIM, seq), BF16),
        scratch_shapes=([pltpu.VMEM((n_blk, HB * HEAD_DIM), F32), pltpu.VMEM((HB, n_blk, 1, T), F32)]
                        + [pltpu.VMEM((1, T), F32), pltpu.VMEM((1, T), F32), pltpu.VMEM((HEAD_DIM, T), F32)] * HB),
        compiler_params=_cparams(("arbitrary", "arbitrary", "arbitrary")),
        name="moba_attention",
    )(qT, k, vT, bias3)
    return out


ROUTER_LANES = 128


def _projT_ln_kernel(*refs, n_in):
    aT_refs, w_refs = refs[:n_in], refs[n_in:2 * n_in]
    x_ref, g_ref, b_ref, o_ref = refs[2 * n_in:]
    tn = (((0,), (0,)), ((), ()))
    mix = lax.dot_general(aT_refs[0][0], w_refs[0][...], tn, preferred_element_type=F32)
    for aT_ref, w_ref in zip(aT_refs[1:], w_refs[1:]):
        mix = mix + lax.dot_general(aT_ref[0], w_ref[...], tn, preferred_element_type=F32)
    o_ref[0] = _layer_norm_rows(ALPHA * x_ref[0] + mix, g_ref[...], b_ref[...])


def projT_residual_ln(aTs, w, x3, g, b, tm=512):
    bsz, seq, d = x3.shape
    ws, k0 = [], 0
    for aT in aTs:
        ws.append(w[k0:k0 + aT.shape[1]])
        k0 += aT.shape[1]
    n_in = len(aTs)
    once = pl.Buffered(1)
    vec = pl.BlockSpec((1, d), lambda bb, i: (0, 0))
    return pl.pallas_call(
        functools.partial(_projT_ln_kernel, n_in=n_in),
        grid=(bsz, seq // tm),
        in_specs=([pl.BlockSpec((1, aT.shape[1], tm), lambda bb, i: (bb, 0, i)) for aT in aTs]
                  + [pl.BlockSpec(wi.shape, lambda bb, i: (0, 0), pipeline_mode=once) for wi in ws]
                  + [pl.BlockSpec((1, tm, d), lambda bb, i: (bb, i, 0)), vec, vec]),
        out_specs=pl.BlockSpec((1, tm, d), lambda bb, i: (bb, i, 0)),
        out_shape=jax.ShapeDtypeStruct((bsz, seq, d), F32),
        compiler_params=_cparams(("arbitrary", "arbitrary")),
        name="projT_residual_ln",
    )(*aTs, *ws, x3, g.reshape(1, d), b.reshape(1, d))


def _router_kernel(h_ref, w_ref, o_ref):
    o_ref[...] = jnp.dot(h_ref[...], w_ref[...], preferred_element_type=F32, precision=lax.Precision.HIGHEST)


def router_logits(h, router, tm=1024):
    m, d = h.shape
    w = jnp.pad(router, ((0, 0), (0, ROUTER_LANES - N_EXPERTS)))
    out = pl.pallas_call(
        _router_kernel,
        grid=(m // tm,),
        in_specs=[pl.BlockSpec((tm, d), lambda i: (i, 0)), pl.BlockSpec((d, ROUTER_LANES), lambda i: (0, 0))],
        out_specs=pl.BlockSpec((tm, ROUTER_LANES), lambda i: (i, 0)),
        out_shape=jax.ShapeDtypeStruct((m, ROUTER_LANES), F32),
        compiler_params=_cparams(("arbitrary",)),
        name="router_logits",
    )(h, w)
    return out[:, :N_EXPERTS]


IDX_LANES = 128


def _issue_row_gather(idx_vmem_ref, idx_smem, sem_i, src_hbm, dst_slot_ref, sem_slot, n_rows):
    cp = pltpu.make_async_copy(idx_vmem_ref.at[0], idx_smem, sem_i)
    cp.start()
    cp.wait()

    for r in range(n_rows):
        row = idx_smem[r // IDX_LANES, r % IDX_LANES]
        pltpu.make_async_copy(src_hbm.at[pl.ds(row, 1)], dst_slot_ref.at[pl.ds(r, 1)], sem_slot).start()


def _pipelined_gather(idx0_ref, idxn_ref, idx_smem, sem_i, src_hbm, buf, sem_buf, n_rows):
    g = pl.program_id(0)
    slot = lax.rem(g, 2)

    @pl.when(g == 0)
    def _():
        _issue_row_gather(idx0_ref, idx_smem, sem_i, src_hbm, buf.at[0], sem_buf.at[0], n_rows)

    @pl.when(g + 1 < pl.num_programs(0))
    def _():
        _issue_row_gather(idxn_ref, idx_smem, sem_i, src_hbm, buf.at[1 - slot], sem_buf.at[1 - slot], n_rows)

    pltpu.make_async_copy(buf.at[slot], buf.at[slot], sem_buf.at[slot]).wait()
    return slot


def _gather_specs(n_steps, k):
    first = lambda g, *_: (0, 0, 0)
    nxt = lambda g, *_: (jnp.minimum(g + 1, n_steps - 1), 0, 0)
    return pl.BlockSpec((1, k, IDX_LANES), first), pl.BlockSpec((1, k, IDX_LANES), nxt)


def _moe_ffn_kernel(ge_ref, idx0_ref, idxn_ref, h_hbm, w1_ref, w3_ref, w2_ref, o_ref,
                    xbuf, idx_smem, sem_i, sem_x, *, ff_chunk):
    del ge_ref
    g = pl.program_id(0)
    slot = lax.rem(g, 2)
    wait_slot = lambda s: pltpu.make_async_copy(xbuf.at[s], xbuf.at[s], sem_x.at[s]).wait()

    @pl.when(g == 0)
    def _():
        _issue_row_gather(idx0_ref, idx_smem, sem_i, h_hbm, xbuf.at[0], sem_x.at[0], EXPERT_ROWS)

    wait_slot(slot)
    cp = pltpu.make_async_copy(idxn_ref.at[0], idx_smem, sem_i)
    cp.start()
    xb = xbuf[slot].astype(BF16)
    d_ff = w1_ref.shape[2]
    n_chunks = d_ff // ff_chunk
    n_issue = max(n_chunks - 2, 1)
    per_chunk = -(-EXPERT_ROWS // n_issue)
    acc = jnp.zeros((EXPERT_ROWS, w2_ref.shape[2]), F32)
    for ci in range(n_chunks):
        c = ci * ff_chunk
        a = jnp.dot(xb, w1_ref[0, :, c:c + ff_chunk], preferred_element_type=F32)
        u = jnp.dot(xb, w3_ref[0, :, c:c + ff_chunk], preferred_element_type=F32)
        hid = (a * jax.nn.sigmoid(a) * u).astype(BF16)
        acc = acc + jnp.dot(hid, w2_ref[0, c:c + ff_chunk, :], preferred_element_type=F32)
        if ci == 0:
            cp.wait()
        for r in range(ci * per_chunk, min((ci + 1) * per_chunk, EXPERT_ROWS)):
            row = idx_smem[r // IDX_LANES, r % IDX_LANES]
            pltpu.make_async_copy(h_hbm.at[pl.ds(row, 1)], xbuf.at[1 - slot, pl.ds(r, 1)], sem_x.at[1 - slot]).start()
    o_ref[...] = acc

    @pl.when(g == pl.num_programs(0) - 1)
    def _():
        wait_slot(1 - slot)


def moe_expert_ffn(h, row_tok, grp_e, w1, w3, w2, ff_chunk=512):
    d = h.shape[1]
    d_ff = w1.shape[2]
    n_groups = grp_e.shape[0]
    k = EXPERT_ROWS // IDX_LANES
    idx = row_tok.reshape(n_groups, k, IDX_LANES)
    once = pl.Buffered(1)
    idx0_spec, idxn_spec = _gather_specs(n_groups, k)
    grid_spec = pltpu.PrefetchScalarGridSpec(
        num_scalar_prefetch=1,
        grid=(n_groups,),
        in_specs=[idx0_spec, idxn_spec, pl.BlockSpec(memory_space=pl.ANY),
                  pl.BlockSpec((1, d, d_ff), lambda g, ge: (ge[g], 0, 0), pipeline_mode=once),
                  pl.BlockSpec((1, d, d_ff), lambda g, ge: (ge[g], 0, 0), pipeline_mode=once),
                  pl.BlockSpec((1, d_ff, d), lambda g, ge: (ge[g], 0, 0), pipeline_mode=once)],
        out_specs=pl.BlockSpec((EXPERT_ROWS, d), lambda g, ge: (g, 0)),
        scratch_shapes=[pltpu.VMEM((2, EXPERT_ROWS, d), F32), pltpu.SMEM((k, IDX_LANES), I32),
                        pltpu.SemaphoreType.DMA(()), pltpu.SemaphoreType.DMA((2,))],
    )
    return pl.pallas_call(
        functools.partial(_moe_ffn_kernel, ff_chunk=ff_chunk),
        grid_spec=grid_spec,
        out_shape=jax.ShapeDtypeStruct((n_groups * EXPERT_ROWS, d), F32),
        compiler_params=_cparams(("arbitrary",)),
        name="moe_expert_ffn",
    )(grp_e, idx, idx, h, w1, w3, w2)


COMBINE_TM = 256


def _moe_combine_ln_kernel(idx0_ref, idxn_ref, y_hbm, h_ref, gate_ref, g_ref, b_ref, o_ref,
                           ybuf, idx_smem, sem_i, sem_y):
    tm = COMBINE_TM
    slot = _pipelined_gather(idx0_ref, idxn_ref, idx_smem, sem_i, y_hbm, ybuf, sem_y, TOP_K * tm)
    y = gate_ref[:, 0:1] * ybuf[slot, 0:tm, :]
    for j in range(1, TOP_K):
        y = y + gate_ref[:, j:j + 1] * ybuf[slot, j * tm:(j + 1) * tm, :]
    o_ref[...] = _layer_norm_rows(ALPHA * h_ref[...] + y, g_ref[...], b_ref[...])


def moe_combine_ln(h, y_rows, dest, gate, g, b):
    m, d = h.shape
    tm = COMBINE_TM
    n_tiles = m // tm
    k = TOP_K * tm // IDX_LANES
    idx = dest.reshape(n_tiles, tm, TOP_K).transpose(0, 2, 1).reshape(n_tiles, k, IDX_LANES)
    idx0_spec, idxn_spec = _gather_specs(n_tiles, k)
    row = pl.BlockSpec((tm, d), lambda i: (i, 0))
    vec = pl.BlockSpec((1, d), lambda i: (0, 0))
    return pl.pallas_call(
        _moe_combine_ln_kernel,
        grid=(n_tiles,),
        in_specs=[idx0_spec, idxn_spec, pl.BlockSpec(memory_space=pl.ANY), row,
                  pl.BlockSpec((tm, TOP_K), lambda i: (i, 0)), vec, vec],
        out_specs=row,
        out_shape=jax.ShapeDtypeStruct((m, d), F32),
        scratch_shapes=[pltpu.VMEM((2, TOP_K * tm, d), F32), pltpu.SMEM((k, IDX_LANES), I32),
                        pltpu.SemaphoreType.DMA(()), pltpu.SemaphoreType.DMA((2,))],
        compiler_params=_cparams(("arbitrary",)),
        name="moe_combine_ln",
    )(idx, idx, y_rows, h, gate, g.reshape(1, d), b.reshape(1, d))


def moe_dispatch_plan(logits):
    n_tok = logits.shape[0]
    top_val, top_e = lax.top_k(logits, TOP_K)
    gate = jax.nn.softmax(top_val, axis=-1)
    e_flat = top_e.reshape(-1)
    onehot = (e_flat[:, None] == jnp.arange(N_EXPERTS, dtype=e_flat.dtype)[None, :]).astype(I32)
    rank = jnp.take_along_axis(jnp.cumsum(onehot, axis=0) - onehot, e_flat[:, None], axis=1)[:, 0]
    counts = jnp.sum(onehot, axis=0)
    padded = (counts + EXPERT_ROWS - 1) // EXPERT_ROWS * EXPERT_ROWS
    pend = jnp.cumsum(padded)
    pstart = pend - padded
    dest = pstart[e_flat] + rank
    n_assign = n_tok * TOP_K
    n_rows = -(-n_assign // EXPERT_ROWS) * EXPERT_ROWS + N_EXPERTS * EXPERT_ROWS
    n_groups = n_rows // EXPERT_ROWS
    grp_e = jnp.minimum(jnp.searchsorted(pend, jnp.arange(n_groups, dtype=I32) * EXPERT_ROWS, side='right'),
                        N_EXPERTS - 1).astype(I32)
    by_expert = jnp.argsort(e_flat, stable=True).astype(I32)
    start = jnp.cumsum(counts) - counts
    r = jnp.arange(n_rows, dtype=I32)
    e_r = jnp.repeat(grp_e, EXPERT_ROWS)
    k = r - pstart[e_r].astype(I32)
    src = jnp.clip(start[e_r].astype(I32) + k, 0, n_assign - 1)
    row_tok = jnp.where(k < counts[e_r], by_expert[src] // TOP_K, 0).astype(I32)
    return gate, dest.astype(I32), row_tok, grp_e


def kernel(x, rel_bias, e_w_in, e_q_norm, e_kv_norm, e_w_uq, e_w_uk, e_w_uv, e_w_qidx, e_pos_k, e_pos_v, e_ck1, e_ck2, e_cv1, e_cv2, e_w_out, e_ln1_g, e_ln1_b, e_ffn_w1, e_ffn_w3, e_ffn_w2, e_ln2_g, e_ln2_b, o_w_in, o_w_out, o_ln1_g, o_ln1_b, o_router, o_moe_w1, o_moe_w3, o_moe_w2, o_ln2_g, o_ln2_b):
    bsz, seq, d = x.shape
    m = bsz * seq
    xf = x.reshape(m, d)
    dsa_bias = dsa_bias_tiles(rel_bias)
    nsa_bc, nsa_toe_s, nsa_toe_w = nsa_bias_inputs(rel_bias, seq)
    moba_bias = moba_bias_tiles(rel_bias)
    gd = B_GROUPS * HEAD_DIM
    for layer in range(DEPTH):
        i = layer // 2
        if layer % 2 == 0:
            x3 = xf.reshape(bsz, seq, d)
            (qidxT, qlatT, sT, kidx, ckv, ckvT, qbT, kcmp, vcmp, kslc, kwin, vT) = even_inproj(
                x3, e_w_in[i], e_q_norm[i], e_kv_norm[i], e_w_uq[i], e_w_uk[i], e_w_qidx[i])
            o_aT = dsa_attention(qidxT, sT, qlatT, kidx, ckv, ckvT, e_w_uv[i], dsa_bias)
            kc = nsa_compress(kcmp.reshape(m, gd), e_pos_k[i], e_ck1[i], e_ck2[i], bsz, seq)
            vc = nsa_compress(vcmp.reshape(m, gd), e_pos_v[i], e_cv1[i], e_cv2[i], bsz, seq)
            o_bT = nsa_attention(qbT, kc, vc, kslc, kwin, vT, sT, nsa_bc, nsa_toe_s, nsa_toe_w)
            h = projT_residual_ln([o_aT, o_bT], e_w_out[i].astype(BF16), x3, e_ln1_g[i], e_ln1_b[i]).reshape(m, d)
            tm = 1024
            xf = swiglu_ffn(h, jnp.zeros((m // tm,), I32), e_ffn_w1[i][None].astype(BF16),
                            e_ffn_w3[i][None].astype(BF16), e_ffn_w2[i][None].astype(BF16),
                            e_ln2_g[i], e_ln2_b[i], with_ln=True, out_dtype=F32, tm=tm, ff_chunk=1408)
        else:
            x3 = xf.reshape(bsz, seq, d)
            o_cT = moba_attention(*moba_inproj(x3, o_w_in[i]), moba_bias)
            h = projT_residual_ln([o_cT], o_w_out[i].astype(BF16), x3, o_ln1_g[i], o_ln1_b[i]).reshape(m, d)
            gate, dest, row_tok, grp_e = moe_dispatch_plan(router_logits(h, o_router[i]))
            y_rows = moe_expert_ffn(h, row_tok, grp_e, o_moe_w1[i].astype(BF16), o_moe_w3[i].astype(BF16),
                                    o_moe_w2[i].astype(BF16))
            xf = moe_combine_ln(h, y_rows, dest, gate, o_ln2_g[i], o_ln2_b[i])
    return xf.reshape(bsz, seq, d)
```

```python
import functools
import math

import numpy as np
import jax
import jax.numpy as jnp
from jax import lax
from jax.experimental import pallas as pl
from jax.experimental.pallas import tpu as pltpu

F32 = jnp.float32
BF16 = jnp.bfloat16
I32 = jnp.int32
BF16_ROWS = 16

HEAD_DIM = 64
NUM_BUCKETS = 32
MAX_DISTANCE = 128
N_BIAS_HEADS = 16
A_HEADS = 8
A_Q_RANK = 256
A_KV_RANK = 128
IDX_HEADS = 16
IDX_DIM = 64
DSA_TOPK = 256
B_HEADS = 8
B_GROUPS = 2
B_HPG = B_HEADS // B_GROUPS
CMP_LEN = 32
CMP_STRIDE = 16
SLC_BLOCK = 64
SLC_TOPN = 16
WINDOW = 512
C_HEADS = 16
MOBA_BLOCK = 256
MOBA_TOPK = 3
N_EXPERTS = 8
TOP_K = 2
EXPERT_ROWS = 256
DEPTH = 2
ALPHA = (2 * DEPTH) ** 0.25

LOG2E = 1.4426950408889634
QK_SCALE = HEAD_DIM ** -0.5 * LOG2E
NEG = -1e30
NEG_HALF = -5e29
INT_MIN = -2 ** 31
VMEM_LIMIT = 56 * 1024 * 1024


def _t5_thresholds():
    def bucket(n):
        if n < NUM_BUCKETS // 2:
            return n
        v = np.log(np.float32(n) / np.float32(NUM_BUCKETS // 2)) / np.float32(math.log(MAX_DISTANCE / (NUM_BUCKETS // 2)))
        return min(NUM_BUCKETS // 2 + int(np.float32(v) * (NUM_BUCKETS - NUM_BUCKETS // 2)), NUM_BUCKETS - 1)
    b = [bucket(i) for i in range(4 * MAX_DISTANCE)]
    return [0] + [min(i for i in range(len(b)) if b[i] >= k) for k in range(1, NUM_BUCKETS)]


T5_THR = _t5_thresholds()
T5_FAR = T5_THR[-1]


def _cparams(sem):
    return pltpu.CompilerParams(dimension_semantics=sem, vmem_limit_bytes=VMEM_LIMIT)


def _bias_kernel(tab_ref, off_ref, o_ref, *, c_row, c_col, h0, causal_neg, window):
    v = pl.program_id(0)
    h = pl.program_id(1) + h0
    shape = o_ref.shape[2:]
    dist = (c_col * lax.broadcasted_iota(I32, shape, 1) + c_row * lax.broadcasted_iota(I32, shape, 0) + off_ref[v])
    n = jnp.maximum(dist, 0)
    acc = jnp.full(shape, tab_ref[h] * LOG2E, F32)
    for k in range(1, NUM_BUCKETS):
        acc = jnp.where(n >= T5_THR[k], tab_ref[k * N_BIAS_HEADS + h] * LOG2E, acc)
    if causal_neg:
        acc = jnp.where(dist >= 0, acc, NEG)
    if window:
        acc = jnp.where(dist < window, acc, NEG)
    o_ref[0, 0] = acc


def bias_tiles(rel_bias, offs, n_heads, h0, rows, cols, c_row, c_col, causal_neg, window=0):
    offs = jnp.asarray(offs, I32)
    nv = offs.shape[0]
    return pl.pallas_call(
        functools.partial(_bias_kernel, c_row=c_row, c_col=c_col, h0=h0, causal_neg=causal_neg, window=window),
        grid=(nv, n_heads),
        in_specs=[pl.BlockSpec(memory_space=pltpu.SMEM), pl.BlockSpec(memory_space=pltpu.SMEM)],
        out_specs=pl.BlockSpec((1, 1, rows, cols), lambda v, h: (v, h, 0, 0)),
        out_shape=jax.ShapeDtypeStruct((nv, n_heads, rows, cols), F32),
        compiler_params=_cparams(("arbitrary", "arbitrary")),
        name="t5_bias_tiles",
    )(rel_bias.reshape(-1), offs)


def _layer_norm_rows(z, g, b):
    mu = jnp.mean(z, axis=-1, keepdims=True)
    zc = z - mu
    var = jnp.mean(zc * zc, axis=-1, keepdims=True)
    return zc * lax.rsqrt(var + 1e-5) * g + b


def _ffn_kernel(ge_ref, x_ref, w1_ref, w3_ref, w2_ref, g_ref, b_ref, o_ref, *, ff_chunk, with_ln):
    del ge_ref
    x = x_ref[...]
    xb = x.astype(BF16)
    d_ff = w1_ref.shape[2]
    acc = jnp.zeros((x.shape[0], w2_ref.shape[2]), F32)
    for c in range(0, d_ff, ff_chunk):
        a = jnp.dot(xb, w1_ref[0, :, c:c + ff_chunk], preferred_element_type=F32)
        u = jnp.dot(xb, w3_ref[0, :, c:c + ff_chunk], preferred_element_type=F32)
        hid = (a * jax.nn.sigmoid(a) * u).astype(BF16)
        acc = acc + jnp.dot(hid, w2_ref[0, c:c + ff_chunk, :], preferred_element_type=F32)
    if with_ln:
        o_ref[...] = _layer_norm_rows(ALPHA * x.astype(F32) + acc, g_ref[...], b_ref[...]).astype(o_ref.dtype)
    else:
        o_ref[...] = acc.astype(o_ref.dtype)


def swiglu_ffn(x_rows, grp_e, w1, w3, w2, ln_g, ln_b, *, with_ln, out_dtype, tm, ff_chunk):
    m, d = x_rows.shape
    d_ff = w1.shape[2]
    once = pl.Buffered(1)
    grid_spec = pltpu.PrefetchScalarGridSpec(
        num_scalar_prefetch=1,
        grid=(m // tm,),
        in_specs=[pl.BlockSpec((tm, d), lambda i, ge: (i, 0)),
                  pl.BlockSpec((1, d, d_ff), lambda i, ge: (ge[i], 0, 0), pipeline_mode=once),
                  pl.BlockSpec((1, d, d_ff), lambda i, ge: (ge[i], 0, 0), pipeline_mode=once),
                  pl.BlockSpec((1, d_ff, d), lambda i, ge: (ge[i], 0, 0), pipeline_mode=once),
                  pl.BlockSpec((1, d), lambda i, ge: (0, 0)), pl.BlockSpec((1, d), lambda i, ge: (0, 0))],
        out_specs=pl.BlockSpec((tm, d), lambda i, ge: (i, 0)),
    )
    return pl.pallas_call(
        functools.partial(_ffn_kernel, ff_chunk=ff_chunk, with_ln=with_ln),
        grid_spec=grid_spec,
        out_shape=jax.ShapeDtypeStruct((m, d), out_dtype),
        compiler_params=_cparams(("arbitrary",)),
        name="swiglu_ffn",
    )(grp_e, x_rows, w1, w3, w2, ln_g.reshape(1, d), ln_b.reshape(1, d))


def _flash_probs(s, m, segs=None):
    if segs is None:
        m_new = jnp.maximum(m, jnp.max(s, axis=0, keepdims=True))
        p = jnp.exp2(s - m_new)
    else:
        m_new, r0 = m, 0
        for n, c in segs:
            seg_max = jnp.max(s[r0:r0 + n], axis=0, keepdims=True)
            m_new = jnp.maximum(m_new, jnp.where(c > NEG_HALF, seg_max + c, NEG))
            r0 += n
        parts, r0 = [], 0
        for n, c in segs:
            shift = jnp.where(c > NEG_HALF, m_new - c, -NEG)
            parts.append(jnp.exp2(s[r0:r0 + n] - shift))
            r0 += n
        p = parts[0] if len(parts) == 1 else jnp.concatenate(parts, axis=0)
    return m_new, jnp.exp2(m - m_new), p.astype(BF16)


def _flash_merge(states):
    ms = [st[0][...] for st in states]
    m = functools.reduce(jnp.maximum, ms)
    ws = [jnp.exp2(mi - m) for mi in ms]
    l = sum(w * st[1][...] for w, st in zip(ws, states))
    acc = sum(w * st[2][...] for w, st in zip(ws, states))
    return m, l, acc


def _flash_finish(m, l, acc):
    return jnp.where(m > NEG_HALF, acc / l, 0.0)


def _rms_rows(x, g):
    return x * lax.rsqrt(jnp.mean(x * x, axis=-1, keepdims=True) + 1e-6) * g


EVEN_T = 256
SMALL_ROWS = 48
GATE_ROW0 = IDX_HEADS
NT_DIMS = (((1,), (1,)), ((), ()))


def _even_inproj_kernel(x_ref, wa_ref, wsT_ref, qn_ref, kvn_ref, wuq_ref, wuk_ref, wqiT_ref, wqbT_ref, wk4_ref, wvT_ref,
                        qidxT_ref, qlatT_ref, sT_ref, kidx_ref, ckv_ref, ckvT_ref,
                        qbT_ref, kcmp_ref, vcmp_ref, kslc_ref, kwin_ref, vT_ref):
    xb = x_ref[0].astype(BF16)
    ya = jnp.dot(xb, wa_ref[...], preferred_element_type=F32)
    cqn = _rms_rows(ya[:, :A_Q_RANK], qn_ref[...]).astype(BF16)
    ckvn = _rms_rows(ya[:, A_Q_RANK:A_Q_RANK + A_KV_RANK], kvn_ref[...])
    kidx_ref[0, 0] = ya[:, A_Q_RANK + A_KV_RANK:A_Q_RANK + A_KV_RANK + IDX_DIM].astype(BF16)
    ckv_ref[0, 0] = ckvn.astype(BF16)
    ckvT_ref[0, 0] = ckvn.T.astype(BF16)
    sT_ref[0] = lax.dot_general(wsT_ref[...], xb, NT_DIMS, preferred_element_type=F32)
    q = jnp.dot(cqn, wuq_ref[...], preferred_element_type=F32).astype(BF16)
    for h in range(A_HEADS):
        qlT = lax.dot_general(wuk_ref[h], q[:, h * HEAD_DIM:(h + 1) * HEAD_DIM], NT_DIMS, preferred_element_type=F32)
        qlatT_ref[0, h * A_KV_RANK:(h + 1) * A_KV_RANK, :] = (qlT * QK_SCALE).astype(BF16)
    qidxT_ref[0] = lax.dot_general(wqiT_ref[...], cqn, NT_DIMS, preferred_element_type=F32).astype(BF16)
    qbT_ref[0] = lax.dot_general(wqbT_ref[...], xb, NT_DIMS, preferred_element_type=F32).astype(BF16)
    yk = jnp.dot(xb, wk4_ref[...], preferred_element_type=F32).astype(BF16)
    gd = B_GROUPS * HEAD_DIM
    for j, ref in enumerate((kcmp_ref, vcmp_ref, kslc_ref, kwin_ref)):
        ref[0] = yk[:, j * gd:(j + 1) * gd]
    vT = lax.dot_general(wvT_ref[...], xb, NT_DIMS, preferred_element_type=F32).astype(BF16)
    for j in range(EVEN_T // NSA_KT):
        vT_ref[0, j] = vT[:, j * NSA_KT:(j + 1) * NSA_KT]


def even_inproj(x3, w_in, q_norm, kv_norm, w_uq, w_uk, w_qidx):
    bsz, seq, d = x3.shape
    T = EVEN_T
    nq = seq // T
    gd = B_GROUPS * HEAD_DIM
    n_kt = seq // NSA_KT
    o_kidx = A_Q_RANK + A_KV_RANK
    o_widx = o_kidx + IDX_DIM
    o_qb = o_widx + IDX_HEADS
    o_kv = o_qb + B_HEADS * HEAD_DIM
    o_gate = o_kv + 6 * gd
    kv = lambda j: w_in[:, o_kv + j * gd:o_kv + (j + 1) * gd]
    wa = jnp.pad(w_in[:, :o_widx], ((0, 0), (0, 512 - o_widx))).astype(BF16)
    w_gate = w_in[:, o_gate:].reshape(d, B_GROUPS, B_HPG, 3).transpose(0, 1, 3, 2).reshape(d, 3 * B_HEADS)
    wsT = jnp.concatenate([w_in[:, o_widx:o_qb] * IDX_HEADS ** -0.5, w_gate,
                           jnp.zeros((d, SMALL_ROWS - IDX_HEADS - 3 * B_HEADS), w_in.dtype)], axis=1).T.astype(BF16)
    wuq = w_uq.reshape(A_Q_RANK, A_HEADS * HEAD_DIM).astype(BF16)
    wuk = jnp.transpose(w_uk, (1, 0, 2)).astype(BF16)
    wqiT = w_qidx.reshape(A_Q_RANK, IDX_HEADS * IDX_DIM).T.astype(BF16)
    wqbT = (w_in[:, o_qb:o_kv] * QK_SCALE).T.astype(BF16)
    wk4 = jnp.concatenate([kv(0), kv(1), kv(2), kv(4)], axis=1).astype(BF16)
    wvT = jnp.concatenate([kv(3), kv(5)], axis=1).T.astype(BF16)
    weights = (wa, wsT, q_norm.reshape(1, -1), kv_norm.reshape(1, -1), wuq, wuk, wqiT, wqbT, wk4, wvT)
    once = pl.Buffered(1)
    w_specs = [pl.BlockSpec(w.shape, (lambda b, i, n=w.ndim: (0,) * n), pipeline_mode=once) for w in weights]
    fm = lambda rows: pl.BlockSpec((1, rows, T), lambda b, i: (b, 0, i))
    tok = lambda cols: pl.BlockSpec((1, T, cols), lambda b, i: (b, i, 0))
    blk = lambda r, c: pl.BlockSpec((1, 1, r, c), lambda b, i: (b, i, 0, 0))
    sds = jax.ShapeDtypeStruct
    return pl.pallas_call(
        _even_inproj_kernel,
        grid=(bsz, nq),
        in_specs=[pl.BlockSpec((1, T, d), lambda b, i: (b, i, 0))] + w_specs,
        out_specs=[fm(IDX_HEADS * IDX_DIM), fm(A_HEADS * A_KV_RANK), fm(SMALL_ROWS),
                   blk(T, IDX_DIM), blk(T, A_KV_RANK), blk(A_KV_RANK, T),
                   fm(B_HEADS * HEAD_DIM), tok(gd), tok(gd), tok(gd), tok(gd),
                   pl.BlockSpec((1, T // NSA_KT, 2 * gd, NSA_KT), lambda b, i: (b, i, 0, 0))],
        out_shape=[sds((bsz, IDX_HEADS * IDX_DIM, seq), BF16), sds((bsz, A_HEADS * A_KV_RANK, seq), BF16),
                   sds((bsz, SMALL_ROWS, seq), F32),
                   sds((bsz, nq, T, IDX_DIM), BF16), sds((bsz, nq, T, A_KV_RANK), BF16), sds((bsz, nq, A_KV_RANK, T), BF16),
                   sds((bsz, B_HEADS * HEAD_DIM, seq), BF16),
                   sds((bsz, seq, gd), BF16), sds((bsz, seq, gd), BF16), sds((bsz, seq, gd), BF16), sds((bsz, seq, gd), BF16),
                   sds((bsz, n_kt, 2 * gd, NSA_KT), BF16)],
        compiler_params=_cparams(("arbitrary", "arbitrary")),
        name="even_inproj",
    )(x3, *weights)


DSA_T = 256
SUB = 128


def _dsa_kernel(qidx_ref, wT_ref, qlat_ref, kidx_ref, ckv_ref, ckvT_ref, bias_ref, wuvt_ref, o_ref,
                key_ref, selb0_ref, selb1_ref, *state_refs, n_keep):
    selb_refs = (selb0_ref, selb1_ref)
    m_refs, l_refs, acc_refs = (state_refs[0:A_HEADS], state_refs[A_HEADS:2 * A_HEADS], state_refs[2 * A_HEADS:])
    qi = pl.program_id(1)
    nkb = qi + 1
    T = DSA_T

    def score_block(kb, carry):
        for sub in range(T // SUB):
            k = kidx_ref[0, kb, sub * SUB:(sub + 1) * SUB, :]
            acc = jnp.zeros((SUB, T), F32)
            for h in range(IDX_HEADS):
                d = jnp.dot(k, qidx_ref[0, h * IDX_DIM:(h + 1) * IDX_DIM, :], preferred_element_type=F32)
                acc = acc + jnp.maximum(d, 0.0) * wT_ref[0, h:h + 1, :]
            bits = lax.bitcast_convert_type(acc, I32)
            key = bits ^ (lax.shift_right_arithmetic(bits, 31) & 0x7FFFFFFF)
            s_pos = kb * T + sub * SUB + lax.broadcasted_iota(I32, (SUB, T), 0)
            t_pos = qi * T + lax.broadcasted_iota(I32, (SUB, T), 1)
            key = jnp.where(s_pos <= t_pos, key, INT_MIN)
            key_ref[pl.ds(pl.multiple_of(kb * T + sub * SUB, SUB), SUB), :] = key
        return carry

    lax.fori_loop(0, nkb, score_block, 0)

    def count_ge(cand):
        def body(kb, cnt):
            blk = key_ref[pl.ds(pl.multiple_of(kb * T, T), T), :]
            ge = jnp.where(blk >= cand, 1, 0).astype(I32)
            return cnt + jnp.sum(ge.reshape(T // 8, 8, T), axis=0)
        cnt = lax.fori_loop(0, nkb, body, jnp.zeros((8, T), I32))
        return jnp.sum(cnt, axis=0, keepdims=True)

    def bit_step(i, u):
        cand_u = u | lax.shift_left(jnp.int32(1), 31 - i)
        cnt = count_ge(cand_u ^ INT_MIN)
        return jnp.where(cnt >= n_keep, cand_u, u)

    u = lax.fori_loop(0, 32, bit_step, jnp.zeros((1, T), I32))
    thr = jnp.maximum(u ^ INT_MIN, INT_MIN + 1)

    for h in range(A_HEADS):
        m_refs[h][...] = jnp.full(m_refs[h].shape, NEG, F32)
        l_refs[h][...] = jnp.zeros(l_refs[h].shape, F32)
        acc_refs[h][...] = jnp.zeros(acc_refs[h].shape, F32)

    states = list(zip(m_refs, l_refs, acc_refs))
    far_bias = [bias_ref[2, h, 0:1, :] for h in range(A_HEADS)]

    def masked_scores(kb, selb):
        selb[...] = jnp.where(key_ref[pl.ds(pl.multiple_of(kb * T, T), T), :] >= thr, 0.0, NEG)
        ckv = ckv_ref[0, kb]
        return [jnp.dot(ckv, qlat_ref[0, h * A_KV_RANK:(h + 1) * A_KV_RANK, :], preferred_element_type=F32) + selb[...]
                for h in range(A_HEADS)]

    def far_body(i, carry):
        tiles = []
        for u, selb in enumerate(selb_refs):
            kb_raw = len(selb_refs) * i + u
            live = kb_raw < n_far
            kb = jnp.minimum(kb_raw, n_far - 1)
            segs = [[(T, jnp.where(live, far_bias[h], NEG))] for h in range(A_HEADS)]
            tiles.append((list(zip(masked_scores(kb, selb), segs)), [ckvT_ref[0, kb]] * A_HEADS))
        for s_all, v_all in tiles:
            _flash_step(s_all, v_all, states)
        return carry

    def near_body(kb, carry):
        s_all = [s + bias_ref[qi - kb, h] for h, s in enumerate(masked_scores(kb, selb_refs[0]))]
        _flash_step(s_all, [ckvT_ref[0, kb]] * A_HEADS, states)
        return carry

    n_far = jnp.maximum(qi - 1, 0)
    lax.fori_loop(0, lax.div(n_far + (len(selb_refs) - 1), jnp.int32(len(selb_refs))), far_body, 0)
    lax.fori_loop(n_far, nkb, near_body, 0)

    for h in range(A_HEADS):
        o_lat = _flash_finish(m_refs[h][...], l_refs[h][...], acc_refs[h][...]).astype(BF16)
        o_ref[0, h * HEAD_DIM:(h + 1) * HEAD_DIM, :] = jnp.dot(
            wuvt_ref[h], o_lat, preferred_element_type=F32).astype(o_ref.dtype)


def dsa_attention(qidxT, sT, qlatT, kidx, ckv, ckvT, w_uv, bias3):
    T = DSA_T
    assert T == EVEN_T
    bsz, nq = kidx.shape[0], kidx.shape[1]
    seq = nq * T
    n_keep = min(DSA_TOPK, seq // 4)
    wuvt = jnp.transpose(w_uv, (1, 2, 0)).astype(BF16)
    return pl.pallas_call(
        functools.partial(_dsa_kernel, n_keep=n_keep),
        grid=(bsz, nq),
        in_specs=[pl.BlockSpec((1, IDX_HEADS * IDX_DIM, T), lambda b, i: (b, 0, i)),
                  pl.BlockSpec((1, SMALL_ROWS, T), lambda b, i: (b, 0, i)),
                  pl.BlockSpec((1, A_HEADS * A_KV_RANK, T), lambda b, i: (b, 0, i)),
                  pl.BlockSpec((1, nq, T, IDX_DIM), lambda b, i: (b, 0, 0, 0)),
                  pl.BlockSpec((1, nq, T, A_KV_RANK), lambda b, i: (b, 0, 0, 0)),
                  pl.BlockSpec((1, nq, A_KV_RANK, T), lambda b, i: (b, 0, 0, 0)),
                  pl.BlockSpec((3, A_HEADS, T, T), lambda b, i: (0, 0, 0, 0)),
                  pl.BlockSpec((A_HEADS, HEAD_DIM, A_KV_RANK), lambda b, i: (0, 0, 0))],
        out_specs=pl.BlockSpec((1, A_HEADS * HEAD_DIM, T), lambda b, i: (b, 0, i)),
        out_shape=jax.ShapeDtypeStruct((bsz, A_HEADS * HEAD_DIM, seq), BF16),
        scratch_shapes=([pltpu.VMEM((seq, T), I32), pltpu.VMEM((T, T), F32), pltpu.VMEM((T, T), F32)]
                        + [pltpu.VMEM((1, T), F32)] * (2 * A_HEADS)
                        + [pltpu.VMEM((A_KV_RANK, T), F32)] * A_HEADS),
        compiler_params=_cparams(("arbitrary", "arbitrary")),
        name="dsa_attention",
    )(qidxT, sT, qlatT, kidx, ckv, ckvT, bias3, wuvt)


def dsa_bias_tiles(rel_bias):
    assert DSA_T + 1 >= T5_FAR
    return bias_tiles(rel_bias, [0, DSA_T, 4 * DSA_T], A_HEADS, 0, DSA_T, DSA_T, -1, 1, False)


N_CMP_PAD = 256


def _compress_kernel(blk_ref, pos_ref, w1_ref, w2_ref, o_ref):
    x = (blk_ref[0].astype(F32) + pos_ref[...]).astype(BF16)
    hid = jax.nn.gelu(jnp.dot(x, w1_ref[...], preferred_element_type=F32))
    o_ref[0] = jnp.dot(hid.astype(BF16), w2_ref[...], preferred_element_type=F32).astype(o_ref.dtype)


def nsa_compress(a, pos, w1, w2, bsz, seq):
    n_chunk = seq // CMP_STRIDE
    assert CMP_LEN == 2 * CMP_STRIDE and n_chunk <= N_CMP_PAD
    width = CMP_STRIDE * HEAD_DIM
    chunks = a.reshape(bsz, n_chunk, CMP_STRIDE, B_GROUPS, HEAD_DIM).transpose(0, 3, 1, 2, 4)
    chunks = chunks.reshape(bsz * B_GROUPS, n_chunk, width)
    blocks = jnp.concatenate([chunks[:, :-1], chunks[:, 1:]], axis=-1)
    blocks = jnp.pad(blocks, ((0, 0), (0, N_CMP_PAD - (n_chunk - 1)), (0, 0)))
    out = pl.pallas_call(
        _compress_kernel,
        grid=(bsz * B_GROUPS,),
        in_specs=[pl.BlockSpec((1, N_CMP_PAD, 2 * width), lambda i: (i, 0, 0)),
                  pl.BlockSpec((1, 2 * width), lambda i: (0, 0)),
                  pl.BlockSpec((2 * width, HEAD_DIM), lambda i: (0, 0)),
                  pl.BlockSpec((HEAD_DIM, HEAD_DIM), lambda i: (0, 0))],
        out_specs=pl.BlockSpec((1, N_CMP_PAD, HEAD_DIM), lambda i: (i, 0, 0)),
        out_shape=jax.ShapeDtypeStruct((bsz * B_GROUPS, N_CMP_PAD, HEAD_DIM), BF16),
        compiler_params=_cparams(("arbitrary",)),
        name="nsa_compress",
    )(blocks, pos.reshape(1, 2 * width), w1.reshape(2 * width, HEAD_DIM).astype(BF16), w2.astype(BF16))
    return out.reshape(bsz, B_GROUPS, N_CMP_PAD, HEAD_DIM)


NSA_TQ = 128
NSA_L = B_HPG * NSA_TQ
NSA_KT = 128
NSA_SLC_REL = 3
NSA_WIN_REL = 5
NSA_FAR_SPLIT = 4


def _flash_step(s_all, vT_all, states):
    probs = []
    for item, (m_ref, _, _) in zip(s_all, states):
        s, segs = item if isinstance(item, tuple) else (item, None)
        m_new, alpha, p = _flash_probs(s, m_ref[...], segs)
        m_ref[...] = m_new
        probs.append((alpha, p))
    for (alpha, p), vT, (_, l_ref, acc_ref) in zip(probs, vT_all, states):
        d = vT.shape[0]
        ones = jnp.ones((BF16_ROWS, vT.shape[1]), BF16)
        pv = jnp.dot(jnp.concatenate([vT, ones], axis=0), p, preferred_element_type=F32)
        acc_ref[...] = alpha * acc_ref[...] + pv[:d]
        l_ref[...] = alpha * l_ref[...] + pv[d:d + 1]


def _flash_loop(lo, hi, scores, values, states, segs, unroll=2):
    def body(i, carry):
        tiles = []
        for u in range(unroll):
            j_raw = lo + unroll * i + u
            live = j_raw < hi
            j = jnp.minimum(j_raw, hi - 1)
            sg = [[(n, jnp.where(live, c, NEG)) for n, c in chain] for chain in segs(j)]
            tiles.append((list(zip(scores(j), sg)), values(j)))
        for s_all, v_all in tiles:
            _flash_step(s_all, v_all, states)
        return carry

    lax.fori_loop(0, lax.div(hi - lo + (unroll - 1), jnp.int32(unroll)), body, 0)


def _nsa_kernel(qT_ref, kc_ref, vcT_ref, biasc_ref, ovl_ref, ks_ref, kw_ref, vT_ref,
                toes_ref, toew_ref, sT_ref, o_ref, selb_ref, *st, n_cmp, n_sel, n_slc):
    qi = pl.program_id(1)
    TQ, L = NSA_TQ, NSA_L
    q0 = qi * TQ
    qTs, qTs_pad = [], []
    for g in range(B_GROUPS):
        q = jnp.concatenate([qT_ref[0, (g * B_HPG + n) * HEAD_DIM:(g * B_HPG + n + 1) * HEAD_DIM, :]
                             for n in range(B_HPG)], axis=1)
        parts = [jnp.zeros_like(q)] * B_GROUPS
        parts[g] = q
        qTs.append(q)
        qTs_pad.append(jnp.concatenate(parts, axis=0))
    t_lane = q0 + (lax.broadcasted_iota(I32, (1, L), 1) & (TQ - 1))

    o_cs = []
    for g in range(B_GROUPS):
        s = jnp.dot(kc_ref[0, g], qTs[g], preferred_element_type=F32) + biasc_ref[g, 0]
        i_idx = lax.broadcasted_iota(I32, (N_CMP_PAD, L), 0)
        valid = jnp.where(i_idx < n_cmp, i_idx * CMP_STRIDE + (CMP_LEN - 1), 2 ** 30) <= t_lane
        s = jnp.where(valid, s, NEG)
        m = jnp.max(s, axis=0, keepdims=True)
        p = jnp.where(valid, jnp.exp2(s - m), 0.0)
        l = jnp.sum(p, axis=0, keepdims=True)
        p_c = p / jnp.where(l > 0, l, 1.0)
        o_c = jnp.dot(vcT_ref[0, g], p_c.astype(BF16), preferred_element_type=F32)

        psum = p_c[:, 0:TQ]
        for n in range(1, B_HPG):
            psum = psum + p_c[:, n * TQ:(n + 1) * TQ]
        sc = jnp.dot(ovl_ref[...], psum, preferred_element_type=F32, precision=lax.Precision.HIGHEST)
        j_idx = lax.broadcasted_iota(I32, (n_slc, TQ), 0)
        cur = (q0 + lax.broadcasted_iota(I32, (1, TQ), 1)) // SLC_BLOCK
        adm = j_idx <= cur
        forced = (j_idx == 0) | (j_idx == cur) | (j_idx == cur - 1)
        scv = jnp.where(adm, jnp.where(forced, jnp.inf, sc), -jnp.inf)
        rank = jnp.zeros((n_slc, TQ), I32)
        for jp in range(n_slc):
            row = scv[jp:jp + 1, :]
            beats = jnp.where(row > scv, 1, jnp.where((row == scv) & (jp < j_idx), 1, 0))
            rank = rank + beats
        selb = jnp.where(rank < n_sel, 0.0, NEG).astype(F32)
        selb4 = jnp.concatenate([selb] * B_HPG, axis=1)
        for j in range(n_slc):
            selb_ref[g, j] = selb4[j:j + 1, :]
        o_cs.append(o_c)

    for ref in st[0::3]:
        ref[...] = jnp.full(ref.shape, NEG, F32)
    for ref in st[1::3] + st[2::3]:
        ref[...] = jnp.zeros(ref.shape, F32)
    slc_st = [st[6 * g:6 * g + 3] for g in range(B_GROUPS)]
    win_st = [st[6 * g + 3:6 * g + 6] for g in range(B_GROUPS)]
    n_main = 6 * B_GROUPS
    xtr_st = [[st[n_main + 3 * (g * (NSA_FAR_SPLIT - 1) + r):n_main + 3 * (g * (NSA_FAR_SPLIT - 1) + r) + 3]
               for r in range(NSA_FAR_SPLIT - 1)] for g in range(B_GROUPS)]
    per_kt = NSA_KT // SLC_BLOCK

    groups = range(B_GROUPS)
    far_bias = [toes_ref[g, NSA_SLC_REL - 1, 0:1, :] for g in groups]

    def slc_scores(g, jt, near):
        s = jnp.dot(ks_ref[0, jt], qTs_pad[g], preferred_element_type=F32)
        return s + toes_ref[g, jnp.minimum(qi - jt, NSA_SLC_REL - 1)] if near else s

    def slc_segs(g, jt, near):
        return [(SLC_BLOCK, selb_ref[g, per_kt * jt + r] + (0.0 if near else far_bias[g])) for r in range(per_kt)]

    def win_scores(g, jt):
        rel = jnp.minimum(qi - jt, NSA_WIN_REL - 1)
        return jnp.dot(kw_ref[0, jt], qTs_pad[g], preferred_element_type=F32) + toew_ref[g, rel]

    gd = B_GROUPS * HEAD_DIM
    v_slc = lambda g, jt: vT_ref[0, jt, g * HEAD_DIM:(g + 1) * HEAD_DIM, :]
    v_win = lambda g, jt: vT_ref[0, jt, gd + g * HEAD_DIM:gd + (g + 1) * HEAD_DIM, :]

    assert NSA_WIN_REL >= NSA_SLC_REL
    j_lo = jnp.maximum(qi - (NSA_WIN_REL - 1), 0)

    def far_body(i, carry):
        s_all, v_all, chains = [], [], []
        for r in range(NSA_FAR_SPLIT):
            jt_raw = NSA_FAR_SPLIT * i + r
            live = jt_raw < j_lo
            jt = jnp.minimum(jt_raw, j_lo - 1)
            for g in groups:
                segs = [(n, jnp.where(live, c, NEG)) for n, c in slc_segs(g, jt, False)]
                s_all.append((slc_scores(g, jt, False), segs))
                v_all.append(v_slc(g, jt))
                chains.append(slc_st[g] if r == 0 else xtr_st[g][r - 1])
        _flash_step(s_all, v_all, chains)
        return carry

    zero_row = jnp.zeros((1, L), F32)
    lax.fori_loop(0, lax.div(j_lo + (NSA_FAR_SPLIT - 1), jnp.int32(NSA_FAR_SPLIT)), far_body, 0)
    _flash_loop(j_lo, qi + 1,
                lambda jt: [slc_scores(g, jt, True) for g in groups] + [win_scores(g, jt) for g in groups],
                lambda jt: [v_slc(g, jt) for g in groups] + [v_win(g, jt) for g in groups],
                slc_st + win_st,
                lambda jt: [slc_segs(g, jt, True) for g in groups] + [[(NSA_KT, zero_row)] for g in groups])

    for g in range(B_GROUPS):
        o_s = _flash_finish(*_flash_merge([slc_st[g]] + xtr_st[g]))
        o_w = _flash_finish(*[r[...] for r in win_st[g]])
        row0 = GATE_ROW0 + g * 3 * B_HPG
        gate = [jax.nn.sigmoid(jnp.concatenate([sT_ref[0, row0 + j * B_HPG + n:row0 + j * B_HPG + n + 1, :]
                                                for n in range(B_HPG)], axis=1)) for j in range(3)]
        o = (gate[0] * o_cs[g] + gate[1] * o_s + gate[2] * o_w).astype(o_ref.dtype)
        for n in range(B_HPG):
            o_ref[0, (g * B_HPG + n) * HEAD_DIM:(g * B_HPG + n + 1) * HEAD_DIM, :] = o[:, n * TQ:(n + 1) * TQ]


def nsa_bias_inputs(rel_bias, seq):
    TQ, L, KT = NSA_TQ, NSA_L, NSA_KT
    nq = seq // TQ
    bc = bias_tiles(rel_bias, [-(CMP_LEN - 1)], B_HEADS, A_HEADS, N_CMP_PAD, seq, -CMP_STRIDE, 1, False, 0)
    bc = bc.reshape(B_GROUPS, B_HPG, N_CMP_PAD, nq, TQ).transpose(0, 3, 2, 1, 4).reshape(B_GROUPS, nq, N_CMP_PAD, L)

    def lanes(t):
        v = t.shape[0]
        return t.reshape(v, B_GROUPS, B_HPG, KT, TQ).transpose(1, 0, 3, 2, 4).reshape(B_GROUPS, v, KT, L)

    assert KT == TQ and (NSA_SLC_REL - 1) * KT - (KT - 1) >= T5_FAR
    toe_s = bias_tiles(rel_bias, [v * KT for v in range(NSA_SLC_REL - 1)] + [64 * KT],
                       B_HEADS, A_HEADS, KT, TQ, -1, 1, True, 0)
    assert (NSA_WIN_REL - 1) * KT - (KT - 1) < WINDOW <= NSA_WIN_REL * KT - (KT - 1)
    toe_w = bias_tiles(rel_bias, [v * KT for v in range(NSA_WIN_REL)], B_HEADS, A_HEADS, KT, TQ, -1, 1, True, WINDOW)
    return bc, lanes(toe_s), lanes(toe_w)


def nsa_overlap(seq):
    n_cmp = (seq - CMP_LEN) // CMP_STRIDE + 1
    n_slc = seq // SLC_BLOCK
    cs = np.arange(N_CMP_PAD) * CMP_STRIDE
    ss = np.arange(n_slc) * SLC_BLOCK
    ov = ((cs[None, :] + CMP_LEN - 1 >= ss[:, None]) & (cs[None, :] <= ss[:, None] + SLC_BLOCK - 1)
          & (np.arange(N_CMP_PAD)[None, :] < n_cmp))
    return jnp.asarray(ov.astype(np.float32))


def nsa_attention(qbT, kc, vc, kslc, kwin, vT, sT, biasc, toe_s, toe_w):
    TQ, L, KT, G = NSA_TQ, NSA_L, NSA_KT, B_GROUPS
    bsz, n_kt = vT.shape[0], vT.shape[1]
    seq = n_kt * KT
    nq = seq // TQ
    n_slc = seq // SLC_BLOCK
    n_cmp = (seq - CMP_LEN) // CMP_STRIDE + 1
    n_sel = min(SLC_TOPN, n_slc)
    gd = G * HEAD_DIM
    vcT = vc.transpose(0, 1, 3, 2)
    once = pl.Buffered(1)
    k_spec = pl.BlockSpec((1, n_kt, KT, gd), lambda b, i: (b, 0, 0, 0))
    n_chain = 2 * G + G * (NSA_FAR_SPLIT - 1)
    return pl.pallas_call(
        functools.partial(_nsa_kernel, n_cmp=n_cmp, n_sel=n_sel, n_slc=n_slc),
        grid=(bsz, nq),
        in_specs=[pl.BlockSpec((1, B_HEADS * HEAD_DIM, TQ), lambda b, i: (b, 0, i)),
                  pl.BlockSpec((1, G, N_CMP_PAD, HEAD_DIM), lambda b, i: (b, 0, 0, 0)),
                  pl.BlockSpec((1, G, HEAD_DIM, N_CMP_PAD), lambda b, i: (b, 0, 0, 0)),
                  pl.BlockSpec((G, 1, N_CMP_PAD, L), lambda b, i: (0, i, 0, 0)),
                  pl.BlockSpec((n_slc, N_CMP_PAD), lambda b, i: (0, 0), pipeline_mode=once),
                  k_spec, k_spec,
                  pl.BlockSpec((1, n_kt, 2 * gd, KT), lambda b, i: (b, 0, 0, 0)),
                  pl.BlockSpec((G, NSA_SLC_REL, KT, L), lambda b, i: (0, 0, 0, 0), pipeline_mode=once),
                  pl.BlockSpec((G, NSA_WIN_REL, KT, L), lambda b, i: (0, 0, 0, 0), pipeline_mode=once),
                  pl.BlockSpec((1, SMALL_ROWS, TQ), lambda b, i: (b, 0, i))],
        out_specs=pl.BlockSpec((1, B_HEADS * HEAD_DIM, TQ), lambda b, i: (b, 0, i)),
        out_shape=jax.ShapeDtypeStruct((bsz, B_HEADS * HEAD_DIM, seq), BF16),
        scratch_shapes=([pltpu.VMEM((G, n_slc, 1, L), F32)]
                        + [pltpu.VMEM((1, L), F32), pltpu.VMEM((1, L), F32), pltpu.VMEM((HEAD_DIM, L), F32)] * n_chain),
        compiler_params=_cparams(("arbitrary", "arbitrary")),
        name="nsa_attention",
    )(qbT, kc, vcT, biasc, nsa_overlap(seq), kslc.reshape(bsz, n_kt, KT, gd), kwin.reshape(bsz, n_kt, KT, gd),
      vT, toe_s, toe_w, sT)


MOBA_T = MOBA_BLOCK


MOBA_HB = 16


PAIR = 2 * HEAD_DIM


def _moba_inproj_kernel(x_ref, wqT_ref, wk_ref, wvT_ref, qT_ref, k_ref, vT_ref):
    xb = x_ref[0].astype(BF16)
    nt = (((1,), (1,)), ((), ()))
    qT_ref[0] = lax.dot_general(wqT_ref[...], xb, nt, preferred_element_type=F32).astype(BF16)
    k_ref[0, 0] = jnp.dot(xb, wk_ref[...], preferred_element_type=F32).astype(BF16)
    vT_ref[0, 0] = lax.dot_general(wvT_ref[...], xb, nt, preferred_element_type=F32).astype(BF16)


def _pair_padded_qT(wq):
    n_heads = wq.shape[1] // HEAD_DIM
    wT = wq.T.reshape(n_heads, HEAD_DIM, wq.shape[0])
    z = jnp.zeros_like(wT)
    even = jnp.concatenate([wT, z], axis=1)
    odd = jnp.concatenate([z, wT], axis=1)
    is_even = (jnp.arange(n_heads) % 2 == 0)[:, None, None]
    return jnp.where(is_even, even, odd).reshape(n_heads * PAIR, wq.shape[0])


def moba_inproj(x3, w_in):
    bsz, seq, d = x3.shape
    T = MOBA_T
    n_blk = seq // T
    hd = C_HEADS * HEAD_DIM
    wqT = _pair_padded_qT(w_in[:, :hd] * QK_SCALE).astype(BF16)
    wk = w_in[:, hd:2 * hd].astype(BF16)
    wvT = w_in[:, 2 * hd:].T.astype(BF16)
    once = pl.Buffered(1)
    return pl.pallas_call(
        _moba_inproj_kernel,
        grid=(bsz, n_blk),
        in_specs=[pl.BlockSpec((1, T, d), lambda b, i: (b, i, 0)),
                  pl.BlockSpec(wqT.shape, lambda b, i: (0, 0), pipeline_mode=once),
                  pl.BlockSpec(wk.shape, lambda b, i: (0, 0), pipeline_mode=once),
                  pl.BlockSpec(wvT.shape, lambda b, i: (0, 0), pipeline_mode=once)],
        out_specs=[pl.BlockSpec((1, C_HEADS * PAIR, T), lambda b, i: (b, 0, i)),
                   pl.BlockSpec((1, 1, T, hd), lambda b, i: (b, i, 0, 0)),
                   pl.BlockSpec((1, 1, hd, T), lambda b, i: (b, i, 0, 0))],
        out_shape=[jax.ShapeDtypeStruct((bsz, C_HEADS * PAIR, seq), BF16),
                   jax.ShapeDtypeStruct((bsz, n_blk, T, hd), BF16),
                   jax.ShapeDtypeStruct((bsz, n_blk, hd, T), BF16)],
        compiler_params=_cparams(("arbitrary", "arbitrary")),
        name="moba_inproj",
    )(x3, wqT, wk, wvT)


def _moba_kernel(qT_ref, k_ref, vT_ref, bias_ref, o_ref, kmean_ref, selb_ref, *st, n_sel):
    qi = pl.program_id(2)
    T = MOBA_T
    n_blk = k_ref.shape[1]
    states = [st[3 * hh:3 * hh + 3] for hh in range(MOBA_HB)]

    @pl.when(qi == 0)
    def _():
        for j in range(n_blk):
            kmean_ref[j:j + 1, :] = jnp.mean(k_ref[0, j].astype(F32), axis=0, keepdims=True)

    qTs = [qT_ref[0, hh * PAIR:(hh + 1) * PAIR, :] for hh in range(MOBA_HB)]
    pair = lambda hh: slice((hh // 2) * PAIR, (hh // 2 + 1) * PAIR)
    j_idx = lax.broadcasted_iota(I32, (n_blk, T), 0)
    for hh in range(MOBA_HB):
        gate = jnp.dot(kmean_ref[:, pair(hh)], qTs[hh].astype(F32), preferred_element_type=F32,
                       precision=lax.Precision.HIGHEST)
        gv = jnp.where(j_idx < qi, gate, -jnp.inf)
        rank = jnp.zeros((n_blk, T), I32)
        for jp in range(n_blk):
            row = gv[jp:jp + 1, :]
            rank = rank + jnp.where(row > gv, 1, jnp.where((row == gv) & (jp < j_idx), 1, 0))
        selb = jnp.where(j_idx < qi, jnp.where(rank < n_sel, 0.0, NEG),
                         jnp.where(j_idx == qi, 0.0, NEG)).astype(F32)
        for j in range(n_blk):
            selb_ref[hh, j] = selb[j:j + 1, :]
        m_ref, l_ref, acc_ref = states[hh]
        m_ref[...] = jnp.full(m_ref.shape, NEG, F32)
        l_ref[...] = jnp.zeros(l_ref.shape, F32)
        acc_ref[...] = jnp.zeros(acc_ref.shape, F32)

    heads = range(MOBA_HB)
    far_bias = [bias_ref[2, hh, 0:1, :] for hh in heads]

    def far_scores(kb):
        return [jnp.dot(k_ref[0, kb, :, pair(hh)], qTs[hh], preferred_element_type=F32) for hh in heads]

    def far_segs(kb):
        return [[(T, selb_ref[hh, kb] + far_bias[hh])] for hh in heads]

    def near_scores(kb):
        return [jnp.dot(k_ref[0, kb, :, pair(hh)], qTs[hh], preferred_element_type=F32) + bias_ref[qi - kb, hh]
                for hh in heads]

    def near_segs(kb):
        return [[(T, selb_ref[hh, kb])] for hh in heads]

    def values(kb):
        return [vT_ref[0, kb, hh * HEAD_DIM:(hh + 1) * HEAD_DIM, :] for hh in heads]

    n_far = jnp.maximum(qi - 1, 0)
    _flash_loop(0, n_far, far_scores, values, states, far_segs)
    _flash_loop(n_far, qi + 1, near_scores, values, states, near_segs)
    for hh in range(MOBA_HB):
        o_ref[0, hh * HEAD_DIM:(hh + 1) * HEAD_DIM, :] = _flash_finish(*[r[...] for r in states[hh]]).astype(o_ref.dtype)


def moba_bias_tiles(rel_bias):
    assert MOBA_T + 1 >= T5_FAR
    t0 = bias_tiles(rel_bias, [0], C_HEADS, 0, MOBA_T, MOBA_T, -1, 1, True)
    t12 = bias_tiles(rel_bias, [MOBA_T, 4 * MOBA_T], C_HEADS, 0, MOBA_T, MOBA_T, -1, 1, False)
    return jnp.concatenate([t0, t12], axis=0)


def moba_attention(qT, k, vT, bias3):
    T = MOBA_T
    bsz, n_blk = k.shape[0], k.shape[1]
    seq = n_blk * T
    n_sel = min(MOBA_TOPK, n_blk - 1)
    HB = MOBA_HB
    assert HB % 2 == 0
    once = pl.Buffered(1)
    out = pl.pallas_call(
        functools.partial(_moba_kernel, n_sel=n_sel),
        grid=(bsz, C_HEADS // HB, n_blk),
        in_specs=[pl.BlockSpec((1, HB * PAIR, T), lambda b, h, i: (b, h, i)),
                  pl.BlockSpec((1, n_blk, T, HB * HEAD_DIM), lambda b, h, i: (b, 0, 0, h), pipeline_mode=once),
                  pl.BlockSpec((1, n_blk, HB * HEAD_DIM, T), lambda b, h, i: (b, 0, h, 0), pipeline_mode=once),
                  pl.BlockSpec((3, HB, T, T), lambda b, h, i: (0, h, 0, 0), pipeline_mode=once)],
        out_specs=pl.BlockSpec((1, HB * HEAD_DIM, T), lambda b, h, i: (b, h, i)),
        out_shape=jax.ShapeDtypeStruct((bsz, C_HEADS * HEAD_DIM, seq), BF16),
        scratch_shapes=([pltpu.VMEM((n_blk, HB * HEAD_DIM), F32), pltpu.VMEM((HB, n_blk, 1, T), F32)]
                        + [pltpu.VMEM((1, T), F32), pltpu.VMEM((1, T), F32), pltpu.VMEM((HEAD_DIM, T), F32)] * HB),
        compiler_params=_cparams(("arbitrary", "arbitrary", "arbitrary")),
        name="moba_attention",
    )(qT, k, vT, bias3)
    return out


ROUTER_LANES = 128


def _projT_ln_kernel(*refs, n_in):
    aT_refs, w_refs = refs[:n_in], refs[n_in:2 * n_in]
    x_ref, g_ref, b_ref, o_ref = refs[2 * n_in:]
    tn = (((0,), (0,)), ((), ()))
    mix = lax.dot_general(aT_refs[0][0], w_refs[0][...], tn, preferred_element_type=F32)
    for aT_ref, w_ref in zip(aT_refs[1:], w_refs[1:]):
        mix = mix + lax.dot_general(aT_ref[0], w_ref[...], tn, preferred_element_type=F32)
    o_ref[0] = _layer_norm_rows(ALPHA * x_ref[0] + mix, g_ref[...], b_ref[...])


def projT_residual_ln(aTs, w, x3, g, b, tm=512):
    bsz, seq, d = x3.shape
    ws, k0 = [], 0
    for aT in aTs:
        ws.append(w[k0:k0 + aT.shape[1]])
        k0 += aT.shape[1]
    n_in = len(aTs)
    once = pl.Buffered(1)
    vec = pl.BlockSpec((1, d), lambda bb, i: (0, 0))
    return pl.pallas_call(
        functools.partial(_projT_ln_kernel, n_in=n_in),
        grid=(bsz, seq // tm),
        in_specs=([pl.BlockSpec((1, aT.shape[1], tm), lambda bb, i: (bb, 0, i)) for aT in aTs]
                  + [pl.BlockSpec(wi.shape, lambda bb, i: (0, 0), pipeline_mode=once) for wi in ws]
                  + [pl.BlockSpec((1, tm, d), lambda bb, i: (bb, i, 0)), vec, vec]),
        out_specs=pl.BlockSpec((1, tm, d), lambda bb, i: (bb, i, 0)),
        out_shape=jax.ShapeDtypeStruct((bsz, seq, d), F32),
        compiler_params=_cparams(("arbitrary", "arbitrary")),
        name="projT_residual_ln",
    )(*aTs, *ws, x3, g.reshape(1, d), b.reshape(1, d))


def _router_kernel(h_ref, w_ref, o_ref):
    o_ref[...] = jnp.dot(h_ref[...], w_ref[...], preferred_element_type=F32, precision=lax.Precision.HIGHEST)


def router_logits(h, router, tm=1024):
    m, d = h.shape
    w = jnp.pad(router, ((0, 0), (0, ROUTER_LANES - N_EXPERTS)))
    out = pl.pallas_call(
        _router_kernel,
        grid=(m // tm,),
        in_specs=[pl.BlockSpec((tm, d), lambda i: (i, 0)), pl.BlockSpec((d, ROUTER_LANES), lambda i: (0, 0))],
        out_specs=pl.BlockSpec((tm, ROUTER_LANES), lambda i: (i, 0)),
        out_shape=jax.ShapeDtypeStruct((m, ROUTER_LANES), F32),
        compiler_params=_cparams(("arbitrary",)),
        name="router_logits",
    )(h, w)
    return out[:, :N_EXPERTS]


IDX_LANES = 128


def _issue_row_gather(idx_vmem_ref, idx_smem, sem_i, src_hbm, dst_slot_ref, sem_slot, n_rows):
    cp = pltpu.make_async_copy(idx_vmem_ref.at[0], idx_smem, sem_i)
    cp.start()
    cp.wait()

    for r in range(n_rows):
        row = idx_smem[r // IDX_LANES, r % IDX_LANES]
        pltpu.make_async_copy(src_hbm.at[pl.ds(row, 1)], dst_slot_ref.at[pl.ds(r, 1)], sem_slot).start(priority=r % 2)


def _pipelined_gather(idx0_ref, idxn_ref, idx_smem, sem_i, src_hbm, buf, sem_buf, n_rows):
    g = pl.program_id(0)
    slot = lax.rem(g, 2)

    @pl.when(g == 0)
    def _():
        _issue_row_gather(idx0_ref, idx_smem, sem_i, src_hbm, buf.at[0], sem_buf.at[0], n_rows)

    @pl.when(g + 1 < pl.num_programs(0))
    def _():
        _issue_row_gather(idxn_ref, idx_smem, sem_i, src_hbm, buf.at[1 - slot], sem_buf.at[1 - slot], n_rows)

    pltpu.make_async_copy(buf.at[slot], buf.at[slot], sem_buf.at[slot]).wait()
    return slot


def _gather_specs(n_steps, k):
    first = lambda g, *_: (0, 0, 0)
    nxt = lambda g, *_: (jnp.minimum(g + 1, n_steps - 1), 0, 0)
    return pl.BlockSpec((1, k, IDX_LANES), first), pl.BlockSpec((1, k, IDX_LANES), nxt)


def _moe_ffn_kernel(ge_ref, idx0_ref, idxn_ref, h_hbm, w1_ref, w3_ref, w2_ref, o_ref,
                    xbuf, idx_smem, sem_i, sem_x, *, ff_chunk):
    del ge_ref
    g = pl.program_id(0)
    slot = lax.rem(g, 2)
    wait_slot = lambda s: pltpu.make_async_copy(xbuf.at[s], xbuf.at[s], sem_x.at[s]).wait()

    @pl.when(g == 0)
    def _():
        _issue_row_gather(idx0_ref, idx_smem, sem_i, h_hbm, xbuf.at[0], sem_x.at[0], EXPERT_ROWS)

    wait_slot(slot)
    cp = pltpu.make_async_copy(idxn_ref.at[0], idx_smem, sem_i)
    cp.start()
    xb = xbuf[slot].astype(BF16)
    d_ff = w1_ref.shape[2]
    n_chunks = d_ff // ff_chunk
    n_issue = max(n_chunks - 2, 1)
    per_chunk = -(-EXPERT_ROWS // n_issue)
    acc = jnp.zeros((EXPERT_ROWS, w2_ref.shape[2]), F32)
    for ci in range(n_chunks):
        c = ci * ff_chunk
        a = jnp.dot(xb, w1_ref[0, :, c:c + ff_chunk], preferred_element_type=F32)
        u = jnp.dot(xb, w3_ref[0, :, c:c + ff_chunk], preferred_element_type=F32)
        hid = (a * jax.nn.sigmoid(a) * u).astype(BF16)
        acc = acc + jnp.dot(hid, w2_ref[0, c:c + ff_chunk, :], preferred_element_type=F32)
        if ci == 0:
            cp.wait()
        for r in range(ci * per_chunk, min((ci + 1) * per_chunk, EXPERT_ROWS)):
            row = idx_smem[r // IDX_LANES, r % IDX_LANES]
            pltpu.make_async_copy(h_hbm.at[pl.ds(row, 1)], xbuf.at[1 - slot, pl.ds(r, 1)],
                                  sem_x.at[1 - slot]).start(priority=r % 2)
    o_ref[...] = acc

    @pl.when(g == pl.num_programs(0) - 1)
    def _():
        wait_slot(1 - slot)


def moe_expert_ffn(h, row_tok, grp_e, w1, w3, w2, ff_chunk=512):
    d = h.shape[1]
    d_ff = w1.shape[2]
    n_groups = grp_e.shape[0]
    k = EXPERT_ROWS // IDX_LANES
    idx = row_tok.reshape(n_groups, k, IDX_LANES)
    once = pl.Buffered(1)
    idx0_spec, idxn_spec = _gather_specs(n_groups, k)
    grid_spec = pltpu.PrefetchScalarGridSpec(
        num_scalar_prefetch=1,
        grid=(n_groups,),
        in_specs=[idx0_spec, idxn_spec, pl.BlockSpec(memory_space=pl.ANY),
                  pl.BlockSpec((1, d, d_ff), lambda g, ge: (ge[g], 0, 0), pipeline_mode=once),
                  pl.BlockSpec((1, d, d_ff), lambda g, ge: (ge[g], 0, 0), pipeline_mode=once),
                  pl.BlockSpec((1, d_ff, d), lambda g, ge: (ge[g], 0, 0), pipeline_mode=once)],
        out_specs=pl.BlockSpec((EXPERT_ROWS, d), lambda g, ge: (g, 0)),
        scratch_shapes=[pltpu.VMEM((2, EXPERT_ROWS, d), F32), pltpu.SMEM((k, IDX_LANES), I32),
                        pltpu.SemaphoreType.DMA(()), pltpu.SemaphoreType.DMA((2,))],
    )
    return pl.pallas_call(
        functools.partial(_moe_ffn_kernel, ff_chunk=ff_chunk),
        grid_spec=grid_spec,
        out_shape=jax.ShapeDtypeStruct((n_groups * EXPERT_ROWS, d), F32),
        compiler_params=_cparams(("arbitrary",)),
        name="moe_expert_ffn",
    )(grp_e, idx, idx, h, w1, w3, w2)


COMBINE_TM = 256


def _moe_combine_ln_kernel(idx0_ref, idxn_ref, y_hbm, h_ref, gate_ref, g_ref, b_ref, o_ref,
                           ybuf, idx_smem, sem_i, sem_y):
    tm = COMBINE_TM
    slot = _pipelined_gather(idx0_ref, idxn_ref, idx_smem, sem_i, y_hbm, ybuf, sem_y, TOP_K * tm)
    y = gate_ref[:, 0:1] * ybuf[slot, 0:tm, :]
    for j in range(1, TOP_K):
        y = y + gate_ref[:, j:j + 1] * ybuf[slot, j * tm:(j + 1) * tm, :]
    o_ref[...] = _layer_norm_rows(ALPHA * h_ref[...] + y, g_ref[...], b_ref[...])


def moe_combine_ln(h, y_rows, dest, gate, g, b):
    m, d = h.shape
    tm = COMBINE_TM
    n_tiles = m // tm
    k = TOP_K * tm // IDX_LANES
    idx = dest.reshape(n_tiles, tm, TOP_K).transpose(0, 2, 1).reshape(n_tiles, k, IDX_LANES)
    idx0_spec, idxn_spec = _gather_specs(n_tiles, k)
    row = pl.BlockSpec((tm, d), lambda i: (i, 0))
    vec = pl.BlockSpec((1, d), lambda i: (0, 0))
    return pl.pallas_call(
        _moe_combine_ln_kernel,
        grid=(n_tiles,),
        in_specs=[idx0_spec, idxn_spec, pl.BlockSpec(memory_space=pl.ANY), row,
                  pl.BlockSpec((tm, TOP_K), lambda i: (i, 0)), vec, vec],
        out_specs=row,
        out_shape=jax.ShapeDtypeStruct((m, d), F32),
        scratch_shapes=[pltpu.VMEM((2, TOP_K * tm, d), F32), pltpu.SMEM((k, IDX_LANES), I32),
                        pltpu.SemaphoreType.DMA(()), pltpu.SemaphoreType.DMA((2,))],
        compiler_params=_cparams(("arbitrary",)),
        name="moe_combine_ln",
    )(idx, idx, y_rows, h, gate, g.reshape(1, d), b.reshape(1, d))


def moe_dispatch_plan(logits):
    n_tok = logits.shape[0]
    top_val, top_e = lax.top_k(logits, TOP_K)
    gate = jax.nn.softmax(top_val, axis=-1)
    e_flat = top_e.reshape(-1)
    onehot = (e_flat[:, None] == jnp.arange(N_EXPERTS, dtype=e_flat.dtype)[None, :]).astype(I32)
    rank = jnp.take_along_axis(jnp.cumsum(onehot, axis=0) - onehot, e_flat[:, None], axis=1)[:, 0]
    counts = jnp.sum(onehot, axis=0)
    padded = (counts + EXPERT_ROWS - 1) // EXPERT_ROWS * EXPERT_ROWS
    pend = jnp.cumsum(padded)
    pstart = pend - padded
    dest = pstart[e_flat] + rank
    n_assign = n_tok * TOP_K
    n_rows = -(-n_assign // EXPERT_ROWS) * EXPERT_ROWS + N_EXPERTS * EXPERT_ROWS
    n_groups = n_rows // EXPERT_ROWS
    grp_e = jnp.minimum(jnp.searchsorted(pend, jnp.arange(n_groups, dtype=I32) * EXPERT_ROWS, side='right'),
                        N_EXPERTS - 1).astype(I32)
    by_expert = jnp.argsort(e_flat, stable=True).astype(I32)
    start = jnp.cumsum(counts) - counts
    r = jnp.arange(n_rows, dtype=I32)
    e_r = jnp.repeat(grp_e, EXPERT_ROWS)
    k = r - pstart[e_r].astype(I32)
    src = jnp.clip(start[e_r].astype(I32) + k, 0, n_assign - 1)
    row_tok = jnp.where(k < counts[e_r], by_expert[src] // TOP_K, 0).astype(I32)
    return gate, dest.astype(I32), row_tok, grp_e


def kernel(x, rel_bias, e_w_in, e_q_norm, e_kv_norm, e_w_uq, e_w_uk, e_w_uv, e_w_qidx, e_pos_k, e_pos_v, e_ck1, e_ck2, e_cv1, e_cv2, e_w_out, e_ln1_g, e_ln1_b, e_ffn_w1, e_ffn_w3, e_ffn_w2, e_ln2_g, e_ln2_b, o_w_in, o_w_out, o_ln1_g, o_ln1_b, o_router, o_moe_w1, o_moe_w3, o_moe_w2, o_ln2_g, o_ln2_b):
    bsz, seq, d = x.shape
    m = bsz * seq
    xf = x.reshape(m, d)
    dsa_bias = dsa_bias_tiles(rel_bias)
    nsa_bc, nsa_toe_s, nsa_toe_w = nsa_bias_inputs(rel_bias, seq)
    moba_bias = moba_bias_tiles(rel_bias)
    gd = B_GROUPS * HEAD_DIM
    for layer in range(DEPTH):
        i = layer // 2
        if layer % 2 == 0:
            x3 = xf.reshape(bsz, seq, d)
            (qidxT, qlatT, sT, kidx, ckv, ckvT, qbT, kcmp, vcmp, kslc, kwin, vT) = even_inproj(
                x3, e_w_in[i], e_q_norm[i], e_kv_norm[i], e_w_uq[i], e_w_uk[i], e_w_qidx[i])
            o_aT = dsa_attention(qidxT, sT, qlatT, kidx, ckv, ckvT, e_w_uv[i], dsa_bias)
            kc = nsa_compress(kcmp.reshape(m, gd), e_pos_k[i], e_ck1[i], e_ck2[i], bsz, seq)
            vc = nsa_compress(vcmp.reshape(m, gd), e_pos_v[i], e_cv1[i], e_cv2[i], bsz, seq)
            o_bT = nsa_attention(qbT, kc, vc, kslc, kwin, vT, sT, nsa_bc, nsa_toe_s, nsa_toe_w)
            h = projT_residual_ln([o_aT, o_bT], e_w_out[i].astype(BF16), x3, e_ln1_g[i], e_ln1_b[i]).reshape(m, d)
            tm = 1024
            xf = swiglu_ffn(h, jnp.zeros((m // tm,), I32), e_ffn_w1[i][None].astype(BF16),
                            e_ffn_w3[i][None].astype(BF16), e_ffn_w2[i][None].astype(BF16),
                            e_ln2_g[i], e_ln2_b[i], with_ln=True, out_dtype=F32, tm=tm, ff_chunk=1408)
        else:
            x3 = xf.reshape(bsz, seq, d)
            o_cT = moba_attention(*moba_inproj(x3, o_w_in[i]), moba_bias)
            h = projT_residual_ln([o_cT], o_w_out[i].astype(BF16), x3, o_ln1_g[i], o_ln1_b[i]).reshape(m, d)
            gate, dest, row_tok, grp_e = moe_dispatch_plan(router_logits(h, o_router[i]))
            y_rows = moe_expert_ffn(h, row_tok, grp_e, o_moe_w1[i].astype(BF16), o_moe_w3[i].astype(BF16),
                                    o_moe_w2[i].astype(BF16))
            xf = moe_combine_ln(h, y_rows, dest, gate, o_ln2_g[i], o_ln2_b[i])
    return xf.reshape(bsz, seq, d)
```
